```python
import jax, jax.numpy as jnp
from jax import lax
import numpy as np

D_MODEL = 4096
BATCH = 4
SEQ = 4096
DEPTH = 1

MIX_WIDTH = D_MODEL
POOL_WIDTH = MIX_WIDTH // 2
FOURIER_WIDTH = MIX_WIDTH - POOL_WIDTH
POOL_WINDOWS = (2, 4, 8, 16)
POOL_GROUPS = len(POOL_WINDOWS)
POOL_GROUP_DIM = POOL_WIDTH // POOL_GROUPS
FOURIER_HEADS = 4
FOURIER_HEAD_DIM = FOURIER_WIDTH // FOURIER_HEADS

N_EXPERTS = 32
TOP_K = 4
D_FF = D_MODEL // 8
SWIGLU_LIMIT = 7.0
SWIGLU_ALPHA = 1.702

RMS_EPS = 1e-5

kernel_name = "hybrid_pool_fourier_moe_encoder"


def rmsnorm(x, g):
    xf = x.astype(jnp.float32)
    y = xf * lax.rsqrt(jnp.mean(xf * xf, axis=-1, keepdims=True) + RMS_EPS)
    return (y * g.astype(jnp.float32)).astype(x.dtype)


def pool_mixer(u, pool_w, pool_scale):
    B, S, _ = u.shape
    uf = u.astype(jnp.float32)
    cs = jnp.concatenate([jnp.zeros((B, 1, POOL_WIDTH), jnp.float32), jnp.cumsum(uf, axis=1)], axis=1)
    t = jnp.arange(S)
    outs = []
    for g, w in enumerate(POOL_WINDOWS):
        lo = jnp.maximum(t - w // 2, 0)
        hi = jnp.minimum(t + w // 2 - 1, S - 1)
        csg = cs[:, :, g * POOL_GROUP_DIM:(g + 1) * POOL_GROUP_DIM]
        win_sum = jnp.take(csg, hi + 1, axis=1) - jnp.take(csg, lo, axis=1)
        cnt = (hi - lo + 1).astype(jnp.float32)[None, :, None]
        outs.append(win_sum / cnt - uf[:, :, g * POOL_GROUP_DIM:(g + 1) * POOL_GROUP_DIM])
    p = jnp.stack(outs, axis=2)
    y = jnp.einsum('bsgc,gcd->bsgd', p, pool_w.astype(jnp.float32)).reshape(B, S, POOL_WIDTH)
    return (y * pool_scale.astype(jnp.float32)).astype(u.dtype)


def fourier_mixer(u, fourier_w):
    B, S, _ = u.shape
    v = u.astype(jnp.float32).reshape(B, S, FOURIER_HEADS, FOURIER_HEAD_DIM)
    f = jnp.fft.fft2(v, axes=(1, 3), norm="ortho").real
    y = jnp.einsum('bshc,hcd->bshd', f, fourier_w.astype(jnp.float32)).reshape(B, S, FOURIER_WIDTH)
    return y.astype(u.dtype)


def moe_ffn(z, router_w, router_b, w_gate, b_gate, w_up, b_up, w_down, b_down):
    B, S, D = z.shape
    N = B * S
    zf = z.reshape(N, D)
    logits = jnp.dot(zf.astype(jnp.float32), router_w.astype(jnp.float32)) + router_b.astype(jnp.float32)
    top_v, top_i = lax.top_k(logits, TOP_K)
    gates = jax.nn.softmax(top_v, axis=-1)
    gate_dense = jnp.sum(jax.nn.one_hot(top_i, N_EXPERTS, dtype=jnp.float32) * gates[..., None], axis=1)
    hg = jnp.einsum('nd,edf->nef', zf, w_gate) + b_gate[None]
    hu = jnp.einsum('nd,edf->nef', zf, w_up) + b_up[None]
    hg = jnp.minimum(hg, SWIGLU_LIMIT)
    hu = jnp.clip(hu, -SWIGLU_LIMIT, SWIGLU_LIMIT)
    act = hg * jax.nn.sigmoid(SWIGLU_ALPHA * hg) * (hu + 1.0)
    act = act * gate_dense.astype(act.dtype)[:, :, None]
    y = jnp.einsum('nef,efd->nd', act, w_down) + jnp.dot(gate_dense.astype(b_down.dtype), b_down)
    return y.reshape(B, S, D).astype(z.dtype)


def setup_inputs(seed: int = 0) -> dict:
    key = jax.random.key(seed)
    ks = jax.random.split(key, 20)
    f32 = jnp.float32
    nrm = lambda k, shape, scale: jax.random.normal(k, shape, f32) * scale
    L = DEPTH
    return {
        "x": jax.random.normal(ks[0], (BATCH, SEQ, D_MODEL), f32),
        "norm1_g": 1.0 + nrm(ks[1], (L, D_MODEL), 0.05),
        "w_in": nrm(ks[2], (L, D_MODEL, MIX_WIDTH), D_MODEL ** -0.5),
        "pool_w": nrm(ks[3], (L, POOL_GROUPS, POOL_GROUP_DIM, POOL_GROUP_DIM), POOL_GROUP_DIM ** -0.5),
        "pool_scale": 1.0 + nrm(ks[4], (L, POOL_WIDTH), 0.1),
        "fourier_w": nrm(ks[5], (L, FOURIER_HEADS, FOURIER_HEAD_DIM, FOURIER_HEAD_DIM), FOURIER_HEAD_DIM ** -0.5),
        "w_out": nrm(ks[6], (L, MIX_WIDTH, D_MODEL), MIX_WIDTH ** -0.5),
        "norm2_g": 1.0 + nrm(ks[7], (L, D_MODEL), 0.05),
        "router_w": nrm(ks[8], (L, D_MODEL, N_EXPERTS), D_MODEL ** -0.5),
        "router_b": nrm(ks[9], (L, N_EXPERTS), 0.01),
        "w_gate": nrm(ks[10], (L, N_EXPERTS, D_MODEL, D_FF), D_MODEL ** -0.5),
        "b_gate": nrm(ks[11], (L, N_EXPERTS, D_FF), 0.01),
        "w_up": nrm(ks[12], (L, N_EXPERTS, D_MODEL, D_FF), D_MODEL ** -0.5),
        "b_up": nrm(ks[13], (L, N_EXPERTS, D_FF), 0.01),
        "w_down": nrm(ks[14], (L, N_EXPERTS, D_FF, D_MODEL), D_FF ** -0.5),
        "b_down": nrm(ks[15], (L, N_EXPERTS, D_MODEL), 0.01),
        "final_g": 1.0 + nrm(ks[16], (D_MODEL,), 0.05),
    }


def reference(x, norm1_g, w_in, pool_w, pool_scale, fourier_w, w_out, norm2_g,
              router_w, router_b, w_gate, b_gate, w_up, b_up, w_down, b_down, final_g):
    h = x
    for l in range(DEPTH):
        z = rmsnorm(h, norm1_g[l])
        u = jnp.einsum('bsd,dm->bsm', z, w_in[l])
        a = pool_mixer(u[..., :POOL_WIDTH], pool_w[l], pool_scale[l])
        f = fourier_mixer(u[..., POOL_WIDTH:], fourier_w[l])
        mixed = jnp.concatenate([a, f], axis=-1)
        h = h + jnp.einsum('bsm,md->bsd', mixed, w_out[l])
        z2 = rmsnorm(h, norm2_g[l])
        h = h + moe_ffn(z2, router_w[l], router_b[l], w_gate[l], b_gate[l],
                        w_up[l], b_up[l], w_down[l], b_down[l])
    return rmsnorm(h, final_g)
```

```python
import functools
import math

import jax
import jax.numpy as jnp
from jax import lax
from jax.experimental import pallas as pl
from jax.experimental.pallas import tpu as pltpu

F32 = jnp.float32
BF16 = jnp.bfloat16
I32 = jnp.int32
U32 = jnp.uint32

RMS_EPS = 1e-5
POOL_WINDOWS = (2, 4, 8, 16)
TOP_K = 4
SWIGLU_LIMIT = 7.0
SWIGLU_ALPHA = 1.702

LANES = 128
BF16_SUBLANES = 16
HALO = 16
VMEM_LIMIT = 56 * 1024 * 1024
HI_MASK = 0xFFFF0000


def _cparams(sem, vmem=VMEM_LIMIT):
    return pltpu.CompilerParams(dimension_semantics=sem, vmem_limit_bytes=vmem)


def _pack_halves(lo_f32, hi_f32):
    lo = lax.bitcast_convert_type(lo_f32, U32) >> 16
    hi = lax.bitcast_convert_type(hi_f32, U32) & jnp.uint32(HI_MASK)
    return lo | hi


def _unpack_halves(w):
    lo = lax.bitcast_convert_type(w << 16, F32)
    hi = lax.bitcast_convert_type(w & jnp.uint32(HI_MASK), F32)
    return lo, hi


def _norm_mm_kernel(x_ref, g_ref, w_ref, o_ref, z_ref, *, rows):
    @pl.when(pl.program_id(1) == 0)
    def _():
        def body(c, carry):
            r = pl.multiple_of(c * rows, rows)
            xc = x_ref[pl.ds(r, rows), :]
            ms = jnp.mean(xc * xc, axis=-1, keepdims=True)
            z_ref[pl.ds(r, rows), :] = (xc * lax.rsqrt(ms + RMS_EPS) * g_ref[...]).astype(BF16)
            return carry
        lax.fori_loop(0, x_ref.shape[0] // rows, body, 0)

    o_ref[...] = jnp.dot(z_ref[...], w_ref[...], preferred_element_type=F32).astype(o_ref.dtype)


def _norm_matmul(x, g, w, *, bm, bn):
    n, d = x.shape
    m = w.shape[1]
    bm, bn = min(bm, n), min(bn, m)
    return pl.pallas_call(
        functools.partial(_norm_mm_kernel, rows=min(32, bm)),
        grid=(n // bm, m // bn),
        in_specs=[pl.BlockSpec((bm, d), lambda i, j: (i, 0)),
                  pl.BlockSpec((1, d), lambda i, j: (0, 0)),
                  pl.BlockSpec((d, bn), lambda i, j: (0, j))],
        out_specs=pl.BlockSpec((bm, bn), lambda i, j: (i, j)),
        out_shape=jax.ShapeDtypeStruct((n, m), BF16),
        scratch_shapes=[pltpu.VMEM((bm, d), BF16)],
        compiler_params=_cparams(("parallel", "arbitrary")),
        name="norm1_w_in",
    )(x, g, w)


def _pool_kernel(cur_ref, prev_ref, next_ref, pw_ref, sc_ref, o_ref, *, seq, ts, c):
    t = pl.program_id(1)
    has_prev = (t > 0).astype(F32)
    has_next = (t < pl.num_programs(1) - 1).astype(F32)
    n_ext = ts + 2 * HALO
    tok = t * ts + lax.broadcasted_iota(I32, (ts, 1), 0)
    for g, w in enumerate(POOL_WINDOWS):
        sl = slice(g * c, (g + 1) * c)
        cur = cur_ref[0, :, sl].astype(F32)
        prev = prev_ref[0, :, sl].astype(F32) * has_prev
        nxt = next_ref[0, :, sl].astype(F32) * has_next
        ext = jnp.concatenate([prev, cur, nxt], axis=0)
        s = ext + pltpu.roll(ext, 1, 0)
        h = 1
        while 2 * h < w:
            s = pltpu.roll(s, h, 0) + pltpu.roll(s, n_ext - h, 0)
            h *= 2
        win = s[HALO:HALO + ts]
        lo = jnp.maximum(tok - w // 2, 0)
        hi = jnp.minimum(tok + w // 2 - 1, seq - 1)
        cnt = (hi - lo + 1).astype(F32)
        p = win / cnt - cur
        y = jnp.dot(p.astype(BF16), pw_ref[g], preferred_element_type=F32)
        o_ref[0, :, sl] = (y * sc_ref[:, sl]).astype(o_ref.dtype)


def _pool_mixer(u3, pool_w, pool_scale, *, ts):
    b, seq, _ = u3.shape
    g, c, _ = pool_w.shape
    pw = g * c
    ts = min(ts, seq)
    nh = seq // HALO
    per = ts // HALO
    return pl.pallas_call(
        functools.partial(_pool_kernel, seq=seq, ts=ts, c=c),
        grid=(b, seq // ts),
        in_specs=[pl.BlockSpec((1, ts, pw), lambda i, t: (i, t, 0)),
                  pl.BlockSpec((1, HALO, pw), lambda i, t: (i, jnp.maximum(t * per - 1, 0), 0)),
                  pl.BlockSpec((1, HALO, pw), lambda i, t: (i, jnp.minimum((t + 1) * per, nh - 1), 0)),
                  pl.BlockSpec((g, c, c), lambda i, t: (0, 0, 0)),
                  pl.BlockSpec((1, pw), lambda i, t: (0, 0))],
        out_specs=pl.BlockSpec((1, ts, pw), lambda i, t: (i, t, 0)),
        out_shape=jax.ShapeDtypeStruct((b, seq, pw), BF16),
        compiler_params=_cparams(("parallel", "parallel")),
        name="pool_mixer",
    )(u3, u3, u3, pool_w, pool_scale)


def _dft_mats(n, scale, dtype):
    j = jnp.arange(n, dtype=I32)
    ang = ((j[:, None] * j[None, :]) % n).astype(F32) * (2.0 * math.pi / n)
    return (jnp.cos(ang) * scale).astype(dtype), (jnp.sin(ang) * scale).astype(dtype)


def _fourier_prep_kernel(cc_ref, sc_ref, w_ref, o_ref, *, c):
    w = w_ref[0]
    o_ref[0, :, :c] = jnp.dot(cc_ref[...], w, preferred_element_type=F32,
                              precision=lax.Precision.HIGHEST).astype(o_ref.dtype)
    o_ref[0, :, c:] = jnp.dot(sc_ref[...], w, preferred_element_type=F32,
                              precision=lax.Precision.HIGHEST).astype(o_ref.dtype)


def _fourier_prep(fourier_w):
    h, c, _ = fourier_w.shape
    cc, sc = _dft_mats(c, c ** -0.5, F32)
    return pl.pallas_call(
        functools.partial(_fourier_prep_kernel, c=c),
        grid=(h,),
        in_specs=[pl.BlockSpec((c, c), lambda i: (0, 0)),
                  pl.BlockSpec((c, c), lambda i: (0, 0)),
                  pl.BlockSpec((1, c, c), lambda i: (i, 0, 0))],
        out_specs=pl.BlockSpec((1, c, 2 * c), lambda i: (i, 0, 0)),
        out_shape=jax.ShapeDtypeStruct((h, c, 2 * c), BF16),
        compiler_params=_cparams(("parallel",)),
        name="fourier_prep",
    )(cc, sc, fourier_w)


def _chan_kernel(v_ref, ab_ref, p_ref, q_ref, *, c):
    pq = jnp.dot(v_ref[0], ab_ref[0], preferred_element_type=F32)
    p_ref[0] = pq[:, :c].astype(p_ref.dtype)
    q_ref[0] = pq[:, c:].astype(q_ref.dtype)


def _fourier_chan(u3, ab, *, col0, ts):
    b, seq, _ = u3.shape
    h, c, _ = ab.shape
    ts = min(ts, seq)
    cb0 = col0 // c
    out = jax.ShapeDtypeStruct((b, seq, h * c), BF16)
    return pl.pallas_call(
        functools.partial(_chan_kernel, c=c),
        grid=(b, seq // ts, h),
        in_specs=[pl.BlockSpec((1, ts, c), lambda i, t, k: (i, t, cb0 + k)),
                  pl.BlockSpec((1, c, 2 * c), lambda i, t, k: (k, 0, 0))],
        out_specs=[pl.BlockSpec((1, ts, c), lambda i, t, k: (i, t, k)),
                   pl.BlockSpec((1, ts, c), lambda i, t, k: (i, t, k))],
        out_shape=[out, out],
        compiler_params=_cparams(("parallel", "parallel", "parallel")),
        name="fourier_chan",
    )(u3, ab)


def _seq_dft_kernel(cs_ref, ss_ref, p_ref, q_ref, o_ref):
    y = jnp.dot(cs_ref[...], p_ref[0], preferred_element_type=F32)
    y = y - jnp.dot(ss_ref[...], q_ref[0], preferred_element_type=F32)
    o_ref[0] = y.astype(o_ref.dtype)


def _seq_dft(p, q, *, bm, bn):
    b, seq, m = p.shape
    cs, ss = _dft_mats(seq, seq ** -0.5, BF16)
    bm, bn = min(bm, seq), min(bn, m)
    return pl.pallas_call(
        _seq_dft_kernel,
        grid=(b, seq // bm, m // bn),
        in_specs=[pl.BlockSpec((bm, seq), lambda i, r, j: (r, 0)),
                  pl.BlockSpec((bm, seq), lambda i, r, j: (r, 0)),
                  pl.BlockSpec((1, seq, bn), lambda i, r, j: (i, 0, j)),
                  pl.BlockSpec((1, seq, bn), lambda i, r, j: (i, 0, j))],
        out_specs=pl.BlockSpec((1, bm, bn), lambda i, r, j: (i, r, j)),
        out_shape=jax.ShapeDtypeStruct((b, seq, m), BF16),
        compiler_params=_cparams(("parallel", "parallel", "parallel")),
        name="fourier_seq_dft",
    )(cs, ss, p, q)


def _out_proj_kernel(a_ref, f_ref, w_ref, x_ref, o_ref, *, ka):
    y = jnp.dot(a_ref[...], w_ref[:ka, :], preferred_element_type=F32)
    y = y + jnp.dot(f_ref[...], w_ref[ka:, :], preferred_element_type=F32)
    o_ref[...] = x_ref[...] + y


def _out_proj(a, f, w, x, *, bm, bn):
    n, ka = a.shape
    kf = f.shape[1]
    d = w.shape[1]
    bm, bn = min(bm, n), min(bn, d)
    return pl.pallas_call(
        functools.partial(_out_proj_kernel, ka=ka),
        grid=(n // bm, d // bn),
        in_specs=[pl.BlockSpec((bm, ka), lambda i, j: (i, 0)),
                  pl.BlockSpec((bm, kf), lambda i, j: (i, 0)),
                  pl.BlockSpec((ka + kf, bn), lambda i, j: (0, j)),
                  pl.BlockSpec((bm, bn), lambda i, j: (i, j))],
        out_specs=pl.BlockSpec((bm, bn), lambda i, j: (i, j)),
        out_shape=jax.ShapeDtypeStruct((n, d), F32),
        compiler_params=_cparams(("parallel", "parallel")),
        name="w_out_residual",
    )(a, f, w, x)


def _norm_router_kernel(h_ref, g_ref, rw_ref, rb_ref, zp_ref, lg_ref, z_ref, *, rows):
    dh = h_ref.shape[1] // 2

    def body(c, carry):
        r = pl.multiple_of(c * rows, rows)
        xc = h_ref[pl.ds(r, rows), :]
        ms = jnp.mean(xc * xc, axis=-1, keepdims=True)
        zb = (xc * lax.rsqrt(ms + RMS_EPS) * g_ref[...]).astype(BF16)
        z_ref[pl.ds(r, rows), :] = zb
        zr = zb.astype(F32)
        zp_ref[pl.ds(r, rows), :] = _pack_halves(zr[:, :dh], zr[:, dh:])
        return carry
    lax.fori_loop(0, h_ref.shape[0] // rows, body, 0)
    lg_ref[...] = jnp.dot(z_ref[...], rw_ref[...], preferred_element_type=F32) + rb_ref[...]


def _norm_router(h, g, rw, rb, *, bm):
    n, d = h.shape
    bm = min(bm, n)
    return pl.pallas_call(
        functools.partial(_norm_router_kernel, rows=min(32, bm)),
        grid=(n // bm,),
        in_specs=[pl.BlockSpec((bm, d), lambda i: (i, 0)),
                  pl.BlockSpec((1, d), lambda i: (0, 0)),
                  pl.BlockSpec((d, LANES), lambda i: (0, 0)),
                  pl.BlockSpec((1, LANES), lambda i: (0, 0))],
        out_specs=[pl.BlockSpec((bm, d // 2), lambda i: (i, 0)),
                   pl.BlockSpec((bm, LANES), lambda i: (i, 0))],
        out_shape=[jax.ShapeDtypeStruct((n, d // 2), U32),
                   jax.ShapeDtypeStruct((n, LANES), F32)],
        scratch_shapes=[pltpu.VMEM((bm, d), BF16)],
        compiler_params=_cparams(("parallel",)),
        name="norm2_router",
    )(h, g, rw, rb)


def _route_kernel(lg_ref, pos_ref, gate_ref, texp_ref, ntl_ref, last_ref, idx_s, rank_s,
                  *, n_tok, n_exp, tm, tb, tmax_pad):
    shift = tm.bit_length() - 1
    iota_e = lax.broadcasted_iota(I32, (n_exp, tb), 0)
    before = (lax.broadcasted_iota(I32, (tb, tb), 0)
              < lax.broadcasted_iota(I32, (tb, tb), 1)).astype(BF16)

    def pass1(i, counts):
        off = pl.multiple_of(i * tb, tb)
        l = lg_ref[pl.ds(off, tb), :].T[:n_exp, :]
        vals, hots = [], []
        for k in range(TOP_K):
            m = jnp.max(l, axis=0, keepdims=True)
            idx = jnp.min(jnp.where(l == m, iota_e, n_exp), axis=0, keepdims=True)
            hot = iota_e == idx
            l = jnp.where(hot, -jnp.inf, l)
            vals.append(m)
            hots.append(hot)
            idx_s[k:k + 1, pl.ds(off, tb)] = idx
        exps = [jnp.exp(v - vals[0]) for v in vals]
        tot = exps[0]
        for e in exps[1:]:
            tot = tot + e
        sel = hots[0].astype(F32)
        for hot in hots[1:]:
            sel = sel + hot.astype(F32)
        rank = jnp.dot(sel.astype(BF16), before, preferred_element_type=F32) + counts
        for k in range(TOP_K):
            gate_ref[k:k + 1, pl.ds(off, tb)] = exps[k] / tot
            rank_s[k:k + 1, pl.ds(off, tb)] = jnp.sum(jnp.where(hots[k], rank, 0.0), axis=0, keepdims=True)
        return counts + jnp.sum(sel, axis=1, keepdims=True)

    counts = lax.fori_loop(0, n_tok // tb, pass1, jnp.zeros((n_exp, 1), F32))
    ntile = (counts.astype(I32) + (tm - 1)) >> shift
    sub = lax.broadcasted_iota(I32, (n_exp, LANES), 0)
    lane = lax.broadcasted_iota(I32, (n_exp, LANES), 1)
    ntile_row = jnp.sum(jnp.where(sub == lane, ntile, 0), axis=0, keepdims=True)
    start = jnp.sum(jnp.where(lane < sub, ntile_row, 0), axis=1, keepdims=True)
    t_lane = lax.broadcasted_iota(I32, (n_exp, tmax_pad), 1)
    texp_ref[...] = jnp.sum((start <= t_lane).astype(I32), axis=0, keepdims=True) - 1
    ntl_ref[...] = jnp.sum(ntile, axis=0, keepdims=True) + jnp.zeros((1, LANES), I32)
    last = jnp.where(ntile > 0, (start + ntile - 1) << shift, -1)
    last_ref[...] = jnp.sum(jnp.where(sub == lane, last, 0), axis=0, keepdims=True)
    start_rows = start << shift

    def pass2(i, carry):
        off = pl.multiple_of(i * tb, tb)
        for k in range(TOP_K):
            hot = iota_e == idx_s[k:k + 1, pl.ds(off, tb)]
            base = jnp.sum(jnp.where(hot, start_rows, 0), axis=0, keepdims=True)
            pos_ref[k:k + 1, pl.ds(off, tb)] = rank_s[k:k + 1, pl.ds(off, tb)].astype(I32) + base
        return carry
    lax.fori_loop(0, n_tok // tb, pass2, 0)


def _route(logits, *, n_exp, tm, tmax):
    n_tok = logits.shape[0]
    tb = min(256, n_tok)
    tmax_pad = -(-tmax // LANES) * LANES
    return pl.pallas_call(
        functools.partial(_route_kernel, n_tok=n_tok, n_exp=n_exp, tm=tm, tb=tb, tmax_pad=tmax_pad),
        out_shape=[jax.ShapeDtypeStruct((TOP_K, n_tok), I32),
                   jax.ShapeDtypeStruct((TOP_K, n_tok), F32),
                   jax.ShapeDtypeStruct((1, tmax_pad), I32),
                   jax.ShapeDtypeStruct((1, LANES), I32),
                   jax.ShapeDtypeStruct((1, LANES), I32)],
        scratch_shapes=[pltpu.VMEM((TOP_K, n_tok), I32), pltpu.VMEM((TOP_K, n_tok), F32)],
        compiler_params=_cparams(None),
        name="route",
    )(logits)


def _dispatch_kernel(last_ref, pos_ref, z_hbm, xs_hbm, zero_v, zsem, sem, *, n_exp, tm, chunk):
    c = pl.program_id(0)

    @pl.when(c == 0)
    def _():
        zero_v[...] = jnp.zeros_like(zero_v)
        for e in range(n_exp):
            r = last_ref[e]

            @pl.when(r >= 0)
            def _():
                cp = pltpu.make_async_copy(zero_v, xs_hbm.at[pl.ds(pl.multiple_of(r, tm), tm)], zsem)
                cp.start()
                cp.wait()

    def row_copy(src, dst):
        return pltpu.make_async_copy(z_hbm.at[pl.ds(src, 1)], xs_hbm.at[pl.ds(dst, 1)], sem)

    def issue(i, carry):
        for k in range(TOP_K):
            row_copy(c * chunk + i, pos_ref[0, k, i]).start()
        return carry
    lax.fori_loop(0, chunk, issue, 0)

    def drain(i, carry):
        row_copy(0, 0).wait()
        return carry
    lax.fori_loop(0, chunk * TOP_K, drain, 0)


def _dispatch(last, pos3, z2p, *, n_rows, tm):
    n_tok, dh = z2p.shape
    nchunk, _, chunk = pos3.shape
    n_exp = last.shape[0]
    grid_spec = pltpu.PrefetchScalarGridSpec(
        num_scalar_prefetch=1,
        grid=(nchunk,),
        in_specs=[pl.BlockSpec((1, TOP_K, chunk), lambda c, last: (c, 0, 0), memory_space=pltpu.SMEM),
                  pl.BlockSpec(memory_space=pl.ANY)],
        out_specs=pl.BlockSpec(memory_space=pl.ANY),
        scratch_shapes=[pltpu.VMEM((tm, dh), U32), pltpu.SemaphoreType.DMA(()), pltpu.SemaphoreType.DMA(())],
    )
    return pl.pallas_call(
        functools.partial(_dispatch_kernel, n_exp=n_exp, tm=tm, chunk=chunk),
        grid_spec=grid_spec,
        out_shape=jax.ShapeDtypeStruct((n_rows, dh), U32),
        compiler_params=_cparams(("arbitrary",)),
        name="dispatch",
    )(last, pos3, z2p)


def _ffn_kernel(texp_ref, ntl_ref, x_ref, wgu_ref, bgu_ref, wd_ref, bd_ref, o_ref, *, f):
    dh = x_ref.shape[1]

    @pl.when(pl.program_id(0) < ntl_ref[0])
    def _():
        lo, hi = _unpack_halves(x_ref[...])
        hgu = jnp.dot(lo.astype(BF16), wgu_ref[0, :dh, :], preferred_element_type=F32)
        hgu = hgu + jnp.dot(hi.astype(BF16), wgu_ref[0, dh:, :], preferred_element_type=F32)
        hgu = hgu + bgu_ref[0]
        hg = jnp.minimum(hgu[:, :f], SWIGLU_LIMIT)
        hu = jnp.clip(hgu[:, f:], -SWIGLU_LIMIT, SWIGLU_LIMIT)
        act = hg * (1.0 / (1.0 + jnp.exp(-SWIGLU_ALPHA * hg))) * (hu + 1.0)
        y = jnp.dot(act.astype(BF16), wd_ref[0], preferred_element_type=F32) + bd_ref[0]
        yr = y.astype(BF16).astype(F32)
        o_ref[...] = _pack_halves(yr[:, :dh], yr[:, dh:])


def _expert_ffn(texp, ntl, xs, wgu, bgu, wd, bd, *, tm):
    n_rows, dh = xs.shape
    n_exp, d, f2 = wgu.shape
    f = f2 // 2
    tmax = n_rows // tm

    def x_map(t, te, nt):
        return (jnp.minimum(t, nt[0] - 1), 0)

    def w_map(t, te, nt):
        return (te[t], 0, 0)

    grid_spec = pltpu.PrefetchScalarGridSpec(
        num_scalar_prefetch=2,
        grid=(tmax,),
        in_specs=[pl.BlockSpec((tm, dh), x_map),
                  pl.BlockSpec((1, d, f2), w_map),
                  pl.BlockSpec((1, 1, f2), w_map),
                  pl.BlockSpec((1, f, d), w_map),
                  pl.BlockSpec((1, 1, d), w_map)],
        out_specs=pl.BlockSpec((tm, dh), lambda t, te, nt: (t, 0)),
    )
    return pl.pallas_call(
        functools.partial(_ffn_kernel, f=f),
        grid_spec=grid_spec,
        out_shape=jax.ShapeDtypeStruct((n_rows, dh), U32),
        compiler_params=_cparams(("arbitrary",)),
        name="expert_ffn",
    )(texp, ntl, xs, wgu, bgu, wd, bd)


def _combine_kernel(pos_c, pos_n, ys_hbm, h_ref, g_ref, fg_ref, o_ref, buf, sem, *, bt):
    i = pl.program_id(0)
    n = pl.num_programs(0)
    slot = i % 2
    dh = buf.shape[-1]

    def row_copy(src, s, k, r):
        return pltpu.make_async_copy(ys_hbm.at[pl.ds(src, 1)], buf.at[s, k, pl.ds(r, 1)], sem.at[s])

    def issue(pos_ref, s):
        def body(r, carry):
            for k in range(TOP_K):
                row_copy(pos_ref[0, k, r], s, k, r).start()
            return carry
        lax.fori_loop(0, bt, body, 0)

    @pl.when(i == 0)
    def _():
        issue(pos_c, 0)

    @pl.when(i + 1 < n)
    def _():
        issue(pos_n, 1 - slot)

    def drain(r, carry):
        row_copy(0, slot, 0, 0).wait()
        return carry
    lax.fori_loop(0, bt * TOP_K, drain, 0)

    acc_lo = h_ref[:, :dh]
    acc_hi = h_ref[:, dh:]
    for k in range(TOP_K):
        lo, hi = _unpack_halves(buf[slot, k])
        gk = g_ref[:, k:k + 1]
        acc_lo = acc_lo + gk * lo
        acc_hi = acc_hi + gk * hi
    ms = (jnp.sum(acc_lo * acc_lo, axis=-1, keepdims=True)
          + jnp.sum(acc_hi * acc_hi, axis=-1, keepdims=True)) / (2 * dh)
    r = lax.rsqrt(ms + RMS_EPS)
    o_ref[:, :dh] = acc_lo * r * fg_ref[:, :dh]
    o_ref[:, dh:] = acc_hi * r * fg_ref[:, dh:]


def _combine(pos3, ys, h, gates_t, fg, *, bt):
    n, d = h.shape
    nblk = n // bt
    return pl.pallas_call(
        functools.partial(_combine_kernel, bt=bt),
        grid=(nblk,),
        in_specs=[pl.BlockSpec((1, TOP_K, bt), lambda i: (i, 0, 0), memory_space=pltpu.SMEM),
                  pl.BlockSpec((1, TOP_K, bt), lambda i: (jnp.minimum(i + 1, nblk - 1), 0, 0),
                               memory_space=pltpu.SMEM),
                  pl.BlockSpec(memory_space=pl.ANY),
                  pl.BlockSpec((bt, d), lambda i: (i, 0)),
                  pl.BlockSpec((bt, TOP_K), lambda i: (i, 0)),
                  pl.BlockSpec((1, d), lambda i: (0, 0))],
        out_specs=pl.BlockSpec((bt, d), lambda i: (i, 0)),
        out_shape=jax.ShapeDtypeStruct((n, d), F32),
        scratch_shapes=[pltpu.VMEM((2, TOP_K, bt, d // 2), U32), pltpu.SemaphoreType.DMA((2,))],
        compiler_params=_cparams(("arbitrary",)),
        name="combine_final_norm",
    )(pos3, pos3, ys, h, gates_t, fg)


def _layer(x2, b, seq, norm1_g, w_in, pool_w, pool_scale, fourier_w, w_out, norm2_g,
           router_w, router_b, w_gate, b_gate, w_up, b_up, w_down, b_down):
    n, d = x2.shape
    mix = w_in.shape[1]
    pw = pool_scale.shape[0]
    n_exp = router_w.shape[1]
    f = w_gate.shape[2]
    tm = min(256, n)
    tmax = (n * TOP_K) // tm + n_exp

    u = _norm_matmul(x2, norm1_g.reshape(1, d), w_in.astype(BF16), bm=512, bn=1024)
    u3 = u.reshape(b, seq, mix)
    a = _pool_mixer(u3, pool_w.astype(BF16), pool_scale.reshape(1, pw), ts=256)
    ab = _fourier_prep(fourier_w)
    p, q = _fourier_chan(u3, ab, col0=pw, ts=512)
    yf = _seq_dft(p, q, bm=512, bn=512)
    h = _out_proj(a.reshape(n, pw), yf.reshape(n, mix - pw), w_out.astype(BF16), x2, bm=1024, bn=512)

    rw = jnp.zeros((d, LANES), BF16).at[:, :n_exp].set(router_w.astype(BF16))
    rb = jnp.zeros((1, LANES), F32).at[0, :n_exp].set(router_b)
    z2p, logits = _norm_router(h, norm2_g.reshape(1, d), rw, rb, bm=256)

    pos, gates, texp, ntl, last = _route(logits, n_exp=n_exp, tm=tm, tmax=tmax)
    chunk = min(2048, n)
    pos_c = pos.reshape(TOP_K, n // chunk, chunk).transpose(1, 0, 2)
    xs = _dispatch(last[0, :n_exp], pos_c, z2p, n_rows=tmax * tm, tm=tm)

    wgu = jnp.concatenate([w_gate, w_up], axis=-1).astype(BF16)
    bgu = jnp.concatenate([b_gate, b_up], axis=-1).reshape(n_exp, 1, 2 * f)
    ys = _expert_ffn(texp[0, :tmax], ntl[0, :1], xs, wgu, bgu, w_down.astype(BF16),
                     b_down.reshape(n_exp, 1, d), tm=tm)

    bt = min(128, n)
    pos_b = pos.reshape(TOP_K, n // bt, bt).transpose(1, 0, 2)
    return pos_b, ys, h, gates.T


def kernel(x, norm1_g, w_in, pool_w, pool_scale, fourier_w, w_out, norm2_g, router_w, router_b,
           w_gate, b_gate, w_up, b_up, w_down, b_down, final_g):
    b, seq, d = x.shape
    assert w_in.shape[0] == 1, "only a single layer is supported"
    pos_b, ys, h, gates_t = _layer(
        x.reshape(b * seq, d), b, seq, norm1_g[0], w_in[0], pool_w[0], pool_scale[0], fourier_w[0], w_out[0],
        norm2_g[0], router_w[0], router_b[0], w_gate[0], b_gate[0], w_up[0], b_up[0], w_down[0], b_down[0])
    out = _combine(pos_b, ys, h, gates_t, final_g.reshape(1, d), bt=min(128, b * seq))
    return out.reshape(b, seq, d)
```

```python
import functools
import math

import jax
import jax.numpy as jnp
from jax import lax
from jax.experimental import pallas as pl
from jax.experimental.pallas import tpu as pltpu

F32 = jnp.float32
BF16 = jnp.bfloat16
I32 = jnp.int32
U32 = jnp.uint32

RMS_EPS = 1e-5
POOL_WINDOWS = (2, 4, 8, 16)
TOP_K = 4
SWIGLU_LIMIT = 7.0
SWIGLU_ALPHA = 1.702

LANES = 128
BF16_SUBLANES = 16
HALO = 16
VMEM_LIMIT = 56 * 1024 * 1024
HI_MASK = 0xFFFF0000


def _cparams(sem, vmem=VMEM_LIMIT):
    return pltpu.CompilerParams(dimension_semantics=sem, vmem_limit_bytes=vmem)


def _pack_halves(lo_f32, hi_f32):
    lo = lax.bitcast_convert_type(lo_f32, U32) >> 16
    hi = lax.bitcast_convert_type(hi_f32, U32) & jnp.uint32(HI_MASK)
    return lo | hi


def _unpack_halves(w):
    lo = lax.bitcast_convert_type(w << 16, F32)
    hi = lax.bitcast_convert_type(w & jnp.uint32(HI_MASK), F32)
    return lo, hi


def _norm_mm_kernel(x_ref, g_ref, w_ref, o_ref, z_ref, *, rows):
    @pl.when(pl.program_id(1) == 0)
    def _():
        def body(c, carry):
            r = pl.multiple_of(c * rows, rows)
            xc = x_ref[pl.ds(r, rows), :]
            ms = jnp.mean(xc * xc, axis=-1, keepdims=True)
            z_ref[pl.ds(r, rows), :] = (xc * lax.rsqrt(ms + RMS_EPS) * g_ref[...]).astype(BF16)
            return carry
        lax.fori_loop(0, x_ref.shape[0] // rows, body, 0)

    o_ref[...] = jnp.dot(z_ref[...], w_ref[...], preferred_element_type=F32).astype(o_ref.dtype)


def _norm_matmul(x, g, w, *, bm, bn):
    n, d = x.shape
    m = w.shape[1]
    bm, bn = min(bm, n), min(bn, m)
    return pl.pallas_call(
        functools.partial(_norm_mm_kernel, rows=min(32, bm)),
        grid=(n // bm, m // bn),
        in_specs=[pl.BlockSpec((bm, d), lambda i, j: (i, 0)),
                  pl.BlockSpec((1, d), lambda i, j: (0, 0)),
                  pl.BlockSpec((d, bn), lambda i, j: (0, j))],
        out_specs=pl.BlockSpec((bm, bn), lambda i, j: (i, j)),
        out_shape=jax.ShapeDtypeStruct((n, m), BF16),
        scratch_shapes=[pltpu.VMEM((bm, d), BF16)],
        compiler_params=_cparams(("parallel", "arbitrary")),
        name="norm1_w_in",
    )(x, g, w)


def _pool_kernel(cur_ref, prev_ref, next_ref, pw_ref, sc_ref, o_ref, *, seq, ts, c):
    t = pl.program_id(1)
    has_prev = (t > 0).astype(F32)
    has_next = (t < pl.num_programs(1) - 1).astype(F32)
    n_ext = ts + 2 * HALO
    tok = t * ts + lax.broadcasted_iota(I32, (ts, 1), 0)
    for g, w in enumerate(POOL_WINDOWS):
        sl = slice(g * c, (g + 1) * c)
        cur = cur_ref[0, :, sl].astype(F32)
        prev = prev_ref[0, :, sl].astype(F32) * has_prev
        nxt = next_ref[0, :, sl].astype(F32) * has_next
        ext = jnp.concatenate([prev, cur, nxt], axis=0)
        s = ext + pltpu.roll(ext, 1, 0)
        h = 1
        while 2 * h < w:
            s = pltpu.roll(s, h, 0) + pltpu.roll(s, n_ext - h, 0)
            h *= 2
        win = s[HALO:HALO + ts]
        lo = jnp.maximum(tok - w // 2, 0)
        hi = jnp.minimum(tok + w // 2 - 1, seq - 1)
        cnt = (hi - lo + 1).astype(F32)
        p = win / cnt - cur
        y = jnp.dot(p.astype(BF16), pw_ref[g], preferred_element_type=F32)
        o_ref[0, :, sl] = (y * sc_ref[:, sl]).astype(o_ref.dtype)


def _pool_mixer(u3, pool_w, pool_scale, *, ts):
    b, seq, _ = u3.shape
    g, c, _ = pool_w.shape
    pw = g * c
    ts = min(ts, seq)
    nh = seq // HALO
    per = ts // HALO
    return pl.pallas_call(
        functools.partial(_pool_kernel, seq=seq, ts=ts, c=c),
        grid=(b, seq // ts),
        in_specs=[pl.BlockSpec((1, ts, pw), lambda i, t: (i, t, 0)),
                  pl.BlockSpec((1, HALO, pw), lambda i, t: (i, jnp.maximum(t * per - 1, 0), 0)),
                  pl.BlockSpec((1, HALO, pw), lambda i, t: (i, jnp.minimum((t + 1) * per, nh - 1), 0)),
                  pl.BlockSpec((g, c, c), lambda i, t: (0, 0, 0)),
                  pl.BlockSpec((1, pw), lambda i, t: (0, 0))],
        out_specs=pl.BlockSpec((1, ts, pw), lambda i, t: (i, t, 0)),
        out_shape=jax.ShapeDtypeStruct((b, seq, pw), BF16),
        compiler_params=_cparams(("parallel", "parallel")),
        name="pool_mixer",
    )(u3, u3, u3, pool_w, pool_scale)


def _dft_mats(n, scale, dtype):
    j = jnp.arange(n, dtype=I32)
    ang = ((j[:, None] * j[None, :]) % n).astype(F32) * (2.0 * math.pi / n)
    return (jnp.cos(ang) * scale).astype(dtype), (jnp.sin(ang) * scale).astype(dtype)


def _fourier_prep_kernel(cc_ref, sc_ref, w_ref, o_ref, *, c):
    w = w_ref[0]
    o_ref[0, :, :c] = jnp.dot(cc_ref[...], w, preferred_element_type=F32,
                              precision=lax.Precision.HIGHEST).astype(o_ref.dtype)
    o_ref[0, :, c:] = jnp.dot(sc_ref[...], w, preferred_element_type=F32,
                              precision=lax.Precision.HIGHEST).astype(o_ref.dtype)


def _fourier_prep(fourier_w):
    h, c, _ = fourier_w.shape
    cc, sc = _dft_mats(c, c ** -0.5, F32)
    return pl.pallas_call(
        functools.partial(_fourier_prep_kernel, c=c),
        grid=(h,),
        in_specs=[pl.BlockSpec((c, c), lambda i: (0, 0)),
                  pl.BlockSpec((c, c), lambda i: (0, 0)),
                  pl.BlockSpec((1, c, c), lambda i: (i, 0, 0))],
        out_specs=pl.BlockSpec((1, c, 2 * c), lambda i: (i, 0, 0)),
        out_shape=jax.ShapeDtypeStruct((h, c, 2 * c), BF16),
        compiler_params=_cparams(("parallel",)),
        name="fourier_prep",
    )(cc, sc, fourier_w)


def _chan_kernel(v_ref, ab_ref, p_ref, q_ref, *, c):
    pq = jnp.dot(v_ref[0], ab_ref[0], preferred_element_type=F32)
    p_ref[0] = pq[:, :c].astype(p_ref.dtype)
    q_ref[0] = pq[:, c:].astype(q_ref.dtype)


def _fourier_chan(u3, ab, *, col0, ts):
    b, seq, _ = u3.shape
    h, c, _ = ab.shape
    ts = min(ts, seq)
    cb0 = col0 // c
    out = jax.ShapeDtypeStruct((b, seq, h * c), BF16)
    return pl.pallas_call(
        functools.partial(_chan_kernel, c=c),
        grid=(b, seq // ts, h),
        in_specs=[pl.BlockSpec((1, ts, c), lambda i, t, k: (i, t, cb0 + k)),
                  pl.BlockSpec((1, c, 2 * c), lambda i, t, k: (k, 0, 0))],
        out_specs=[pl.BlockSpec((1, ts, c), lambda i, t, k: (i, t, k)),
                   pl.BlockSpec((1, ts, c), lambda i, t, k: (i, t, k))],
        out_shape=[out, out],
        compiler_params=_cparams(("parallel", "parallel", "parallel")),
        name="fourier_chan",
    )(u3, ab)


def _seq_dft_kernel(cs_ref, ss_ref, p_ref, q_ref, o_ref):
    y = jnp.dot(cs_ref[...], p_ref[0], preferred_element_type=F32)
    y = y - jnp.dot(ss_ref[...], q_ref[0], preferred_element_type=F32)
    o_ref[0] = y.astype(o_ref.dtype)


def _seq_dft(p, q, *, bm, bn):
    b, seq, m = p.shape
    cs, ss = _dft_mats(seq, seq ** -0.5, BF16)
    bm, bn = min(bm, seq), min(bn, m)
    return pl.pallas_call(
        _seq_dft_kernel,
        grid=(b, seq // bm, m // bn),
        in_specs=[pl.BlockSpec((bm, seq), lambda i, r, j: (r, 0)),
                  pl.BlockSpec((bm, seq), lambda i, r, j: (r, 0)),
                  pl.BlockSpec((1, seq, bn), lambda i, r, j: (i, 0, j)),
                  pl.BlockSpec((1, seq, bn), lambda i, r, j: (i, 0, j))],
        out_specs=pl.BlockSpec((1, bm, bn), lambda i, r, j: (i, r, j)),
        out_shape=jax.ShapeDtypeStruct((b, seq, m), BF16),
        compiler_params=_cparams(("parallel", "parallel", "parallel")),
        name="fourier_seq_dft",
    )(cs, ss, p, q)


def _out_proj_kernel(a_ref, f_ref, w_ref, x_ref, o_ref, *, ka):
    y = jnp.dot(a_ref[...], w_ref[:ka, :], preferred_element_type=F32)
    y = y + jnp.dot(f_ref[...], w_ref[ka:, :], preferred_element_type=F32)
    o_ref[...] = x_ref[...] + y


def _out_proj(a, f, w, x, *, bm, bn):
    n, ka = a.shape
    kf = f.shape[1]
    d = w.shape[1]
    bm, bn = min(bm, n), min(bn, d)
    return pl.pallas_call(
        functools.partial(_out_proj_kernel, ka=ka),
        grid=(n // bm, d // bn),
        in_specs=[pl.BlockSpec((bm, ka), lambda i, j: (i, 0)),
                  pl.BlockSpec((bm, kf), lambda i, j: (i, 0)),
                  pl.BlockSpec((ka + kf, bn), lambda i, j: (0, j)),
                  pl.BlockSpec((bm, bn), lambda i, j: (i, j))],
        out_specs=pl.BlockSpec((bm, bn), lambda i, j: (i, j)),
        out_shape=jax.ShapeDtypeStruct((n, d), F32),
        compiler_params=_cparams(("parallel", "parallel")),
        name="w_out_residual",
    )(a, f, w, x)


def _norm_router_kernel(h_ref, g_ref, rw_ref, rb_ref, zp_ref, lg_ref, z_ref, *, rows):
    dh = h_ref.shape[1] // 2

    def body(c, carry):
        r = pl.multiple_of(c * rows, rows)
        xc = h_ref[pl.ds(r, rows), :]
        ms = jnp.mean(xc * xc, axis=-1, keepdims=True)
        zb = (xc * lax.rsqrt(ms + RMS_EPS) * g_ref[...]).astype(BF16)
        z_ref[pl.ds(r, rows), :] = zb
        zr = zb.astype(F32)
        zp_ref[pl.ds(r, rows), :] = _pack_halves(zr[:, :dh], zr[:, dh:])
        return carry
    lax.fori_loop(0, h_ref.shape[0] // rows, body, 0)
    lg_ref[...] = jnp.dot(z_ref[...], rw_ref[...], preferred_element_type=F32) + rb_ref[...]


def _norm_router(h, g, rw, rb, *, bm):
    n, d = h.shape
    bm = min(bm, n)
    return pl.pallas_call(
        functools.partial(_norm_router_kernel, rows=min(32, bm)),
        grid=(n // bm,),
        in_specs=[pl.BlockSpec((bm, d), lambda i: (i, 0)),
                  pl.BlockSpec((1, d), lambda i: (0, 0)),
                  pl.BlockSpec((d, LANES), lambda i: (0, 0)),
                  pl.BlockSpec((1, LANES), lambda i: (0, 0))],
        out_specs=[pl.BlockSpec((bm, d // 2), lambda i: (i, 0)),
                   pl.BlockSpec((bm, LANES), lambda i: (i, 0))],
        out_shape=[jax.ShapeDtypeStruct((n, d // 2), U32),
                   jax.ShapeDtypeStruct((n, LANES), F32)],
        scratch_shapes=[pltpu.VMEM((bm, d), BF16)],
        compiler_params=_cparams(("parallel",)),
        name="norm2_router",
    )(h, g, rw, rb)


def _route_kernel(lg_ref, pos_ref, gate_ref, texp_ref, ntl_ref, last_ref, idx_s, rank_s,
                  *, n_tok, n_exp, tm, tb, tmax_pad):
    shift = tm.bit_length() - 1
    iota_e = lax.broadcasted_iota(I32, (n_exp, tb), 0)
    before = (lax.broadcasted_iota(I32, (tb, tb), 0)
              < lax.broadcasted_iota(I32, (tb, tb), 1)).astype(BF16)

    def pass1(i, counts):
        off = pl.multiple_of(i * tb, tb)
        l = lg_ref[pl.ds(off, tb), :].T[:n_exp, :]
        vals, hots = [], []
        for k in range(TOP_K):
            m = jnp.max(l, axis=0, keepdims=True)
            idx = jnp.min(jnp.where(l == m, iota_e, n_exp), axis=0, keepdims=True)
            hot = iota_e == idx
            l = jnp.where(hot, -jnp.inf, l)
            vals.append(m)
            hots.append(hot)
            idx_s[k:k + 1, pl.ds(off, tb)] = idx
        exps = [jnp.exp(v - vals[0]) for v in vals]
        tot = exps[0]
        for e in exps[1:]:
            tot = tot + e
        sel = hots[0].astype(F32)
        for hot in hots[1:]:
            sel = sel + hot.astype(F32)
        rank = jnp.dot(sel.astype(BF16), before, preferred_element_type=F32) + counts
        for k in range(TOP_K):
            gate_ref[k:k + 1, pl.ds(off, tb)] = exps[k] / tot
            rank_s[k:k + 1, pl.ds(off, tb)] = jnp.sum(jnp.where(hots[k], rank, 0.0), axis=0, keepdims=True)
        return counts + jnp.sum(sel, axis=1, keepdims=True)

    counts = lax.fori_loop(0, n_tok // tb, pass1, jnp.zeros((n_exp, 1), F32))
    ntile = (counts.astype(I32) + (tm - 1)) >> shift
    sub = lax.broadcasted_iota(I32, (n_exp, LANES), 0)
    lane = lax.broadcasted_iota(I32, (n_exp, LANES), 1)
    ntile_row = jnp.sum(jnp.where(sub == lane, ntile, 0), axis=0, keepdims=True)
    start = jnp.sum(jnp.where(lane < sub, ntile_row, 0), axis=1, keepdims=True)
    t_lane = lax.broadcasted_iota(I32, (n_exp, tmax_pad), 1)
    texp_ref[...] = jnp.sum((start <= t_lane).astype(I32), axis=0, keepdims=True) - 1
    ntl_ref[...] = jnp.sum(ntile, axis=0, keepdims=True) + jnp.zeros((1, LANES), I32)
    last = jnp.where(ntile > 0, (start + ntile - 1) << shift, -1)
    last_ref[...] = jnp.sum(jnp.where(sub == lane, last, 0), axis=0, keepdims=True)
    start_rows = start << shift

    def pass2(i, carry):
        off = pl.multiple_of(i * tb, tb)
        for k in range(TOP_K):
            hot = iota_e == idx_s[k:k + 1, pl.ds(off, tb)]
            base = jnp.sum(jnp.where(hot, start_rows, 0), axis=0, keepdims=True)
            pos_ref[k:k + 1, pl.ds(off, tb)] = rank_s[k:k + 1, pl.ds(off, tb)].astype(I32) + base
        return carry
    lax.fori_loop(0, n_tok // tb, pass2, 0)


def _route(logits, *, n_exp, tm, tmax):
    n_tok = logits.shape[0]
    tb = min(256, n_tok)
    tmax_pad = -(-tmax // LANES) * LANES
    return pl.pallas_call(
        functools.partial(_route_kernel, n_tok=n_tok, n_exp=n_exp, tm=tm, tb=tb, tmax_pad=tmax_pad),
        out_shape=[jax.ShapeDtypeStruct((TOP_K, n_tok), I32),
                   jax.ShapeDtypeStruct((TOP_K, n_tok), F32),
                   jax.ShapeDtypeStruct((1, tmax_pad), I32),
                   jax.ShapeDtypeStruct((1, LANES), I32),
                   jax.ShapeDtypeStruct((1, LANES), I32)],
        scratch_shapes=[pltpu.VMEM((TOP_K, n_tok), I32), pltpu.VMEM((TOP_K, n_tok), F32)],
        compiler_params=_cparams(None),
        name="route",
    )(logits)


def _dispatch_kernel(last_ref, pos_ref, z_ref, xs_hbm, zero_v, zsem, sem, *, n_exp, tm, chunk):
    @pl.when(pl.program_id(0) == 0)
    def _():
        zero_v[...] = jnp.zeros_like(zero_v)
        for e in range(n_exp):
            r = last_ref[e]

            @pl.when(r >= 0)
            def _():
                cp = pltpu.make_async_copy(zero_v, xs_hbm.at[pl.ds(pl.multiple_of(r, tm), tm)], zsem)
                cp.start()
                cp.wait()

    def issue(i, carry):
        for k in range(TOP_K):
            pltpu.make_async_copy(z_ref.at[pl.ds(i, 1)], xs_hbm.at[pl.ds(pos_ref[0, k, i], 1)], sem).start()
        return carry
    lax.fori_loop(0, chunk, issue, 0, unroll=4)
    for k in range(TOP_K):
        pltpu.make_async_copy(z_ref, xs_hbm.at[pl.ds(0, chunk)], sem).wait()


def _dispatch(last, pos3, z2p, *, n_rows, tm):
    n_tok, dh = z2p.shape
    nchunk, _, chunk = pos3.shape
    n_exp = last.shape[0]
    grid_spec = pltpu.PrefetchScalarGridSpec(
        num_scalar_prefetch=1,
        grid=(nchunk,),
        in_specs=[pl.BlockSpec((1, TOP_K, chunk), lambda c, last: (c, 0, 0), memory_space=pltpu.SMEM),
                  pl.BlockSpec((chunk, dh), lambda c, last: (c, 0))],
        out_specs=pl.BlockSpec(memory_space=pl.ANY),
        scratch_shapes=[pltpu.VMEM((tm, dh), U32), pltpu.SemaphoreType.DMA(()), pltpu.SemaphoreType.DMA(())],
    )
    return pl.pallas_call(
        functools.partial(_dispatch_kernel, n_exp=n_exp, tm=tm, chunk=chunk),
        grid_spec=grid_spec,
        out_shape=jax.ShapeDtypeStruct((n_rows, dh), U32),
        compiler_params=_cparams(("arbitrary",)),
        name="dispatch",
    )(last, pos3, z2p)


def _ffn_kernel(texp_ref, ntl_ref, x_ref, wgu_ref, bgu_ref, wd_ref, bd_ref, o_ref, *, f):
    dh = x_ref.shape[1]

    @pl.when(pl.program_id(0) < ntl_ref[0])
    def _():
        lo, hi = _unpack_halves(x_ref[...])
        hgu = jnp.dot(lo.astype(BF16), wgu_ref[0, :dh, :], preferred_element_type=F32)
        hgu = hgu + jnp.dot(hi.astype(BF16), wgu_ref[0, dh:, :], preferred_element_type=F32)
        hgu = hgu + bgu_ref[0]
        hg = jnp.minimum(hgu[:, :f], SWIGLU_LIMIT)
        hu = jnp.clip(hgu[:, f:], -SWIGLU_LIMIT, SWIGLU_LIMIT)
        act = hg * (1.0 / (1.0 + jnp.exp(-SWIGLU_ALPHA * hg))) * (hu + 1.0)
        y = jnp.dot(act.astype(BF16), wd_ref[0], preferred_element_type=F32) + bd_ref[0]
        yr = y.astype(BF16).astype(F32)
        o_ref[...] = _pack_halves(yr[:, :dh], yr[:, dh:])


def _expert_ffn(texp, ntl, xs, wgu, bgu, wd, bd, *, tm):
    n_rows, dh = xs.shape
    n_exp, d, f2 = wgu.shape
    f = f2 // 2
    tmax = n_rows // tm

    def x_map(t, te, nt):
        return (jnp.minimum(t, nt[0] - 1), 0)

    def w_map(t, te, nt):
        return (te[t], 0, 0)

    grid_spec = pltpu.PrefetchScalarGridSpec(
        num_scalar_prefetch=2,
        grid=(tmax,),
        in_specs=[pl.BlockSpec((tm, dh), x_map),
                  pl.BlockSpec((1, d, f2), w_map),
                  pl.BlockSpec((1, 1, f2), w_map),
                  pl.BlockSpec((1, f, d), w_map),
                  pl.BlockSpec((1, 1, d), w_map)],
        out_specs=pl.BlockSpec((tm, dh), lambda t, te, nt: (t, 0)),
    )
    return pl.pallas_call(
        functools.partial(_ffn_kernel, f=f),
        grid_spec=grid_spec,
        out_shape=jax.ShapeDtypeStruct((n_rows, dh), U32),
        compiler_params=_cparams(("arbitrary",)),
        name="expert_ffn",
    )(texp, ntl, xs, wgu, bgu, wd, bd)


def _combine_kernel(pos_c, pos_n, ys_hbm, h_ref, g_ref, fg_ref, o_ref, buf, sem, *, bt):
    i = pl.program_id(0)
    n = pl.num_programs(0)
    slot = i % 2
    dh = buf.shape[-1]

    def row_copy(src, s, k, r):
        return pltpu.make_async_copy(ys_hbm.at[pl.ds(src, 1)], buf.at[s, k, pl.ds(r, 1)], sem.at[s])

    def issue(pos_ref, s):
        def body(r, carry):
            for k in range(TOP_K):
                row_copy(pos_ref[0, k, r], s, k, r).start()
            return carry
        lax.fori_loop(0, bt, body, 0, unroll=4)

    @pl.when(i == 0)
    def _():
        issue(pos_c, 0)

    @pl.when(i + 1 < n)
    def _():
        issue(pos_n, 1 - slot)

    for k in range(TOP_K):
        pltpu.make_async_copy(ys_hbm.at[pl.ds(0, bt)], buf.at[slot, k], sem.at[slot]).wait()

    acc_lo = h_ref[:, :dh]
    acc_hi = h_ref[:, dh:]
    for k in range(TOP_K):
        lo, hi = _unpack_halves(buf[slot, k])
        gk = g_ref[:, k:k + 1]
        acc_lo = acc_lo + gk * lo
        acc_hi = acc_hi + gk * hi
    ms = (jnp.sum(acc_lo * acc_lo, axis=-1, keepdims=True)
          + jnp.sum(acc_hi * acc_hi, axis=-1, keepdims=True)) / (2 * dh)
    r = lax.rsqrt(ms + RMS_EPS)
    o_ref[:, :dh] = acc_lo * r * fg_ref[:, :dh]
    o_ref[:, dh:] = acc_hi * r * fg_ref[:, dh:]


def _combine(pos3, ys, h, gates_t, fg, *, bt):
    n, d = h.shape
    nblk = n // bt
    return pl.pallas_call(
        functools.partial(_combine_kernel, bt=bt),
        grid=(nblk,),
        in_specs=[pl.BlockSpec((1, TOP_K, bt), lambda i: (i, 0, 0), memory_space=pltpu.SMEM),
                  pl.BlockSpec((1, TOP_K, bt), lambda i: (jnp.minimum(i + 1, nblk - 1), 0, 0),
                               memory_space=pltpu.SMEM),
                  pl.BlockSpec(memory_space=pl.ANY),
                  pl.BlockSpec((bt, d), lambda i: (i, 0)),
                  pl.BlockSpec((bt, TOP_K), lambda i: (i, 0)),
                  pl.BlockSpec((1, d), lambda i: (0, 0))],
        out_specs=pl.BlockSpec((bt, d), lambda i: (i, 0)),
        out_shape=jax.ShapeDtypeStruct((n, d), F32),
        scratch_shapes=[pltpu.VMEM((2, TOP_K, bt, d // 2), U32), pltpu.SemaphoreType.DMA((2,))],
        compiler_params=_cparams(("arbitrary",)),
        name="combine_final_norm",
    )(pos3, pos3, ys, h, gates_t, fg)


def _layer(x2, b, seq, norm1_g, w_in, pool_w, pool_scale, fourier_w, w_out, norm2_g,
           router_w, router_b, w_gate, b_gate, w_up, b_up, w_down, b_down):
    n, d = x2.shape
    mix = w_in.shape[1]
    pw = pool_scale.shape[0]
    n_exp = router_w.shape[1]
    f = w_gate.shape[2]
    tm = min(256, n)
    tmax = (n * TOP_K) // tm + n_exp

    u = _norm_matmul(x2, norm1_g.reshape(1, d), w_in.astype(BF16), bm=512, bn=1024)
    u3 = u.reshape(b, seq, mix)
    a = _pool_mixer(u3, pool_w.astype(BF16), pool_scale.reshape(1, pw), ts=256)
    ab = _fourier_prep(fourier_w)
    p, q = _fourier_chan(u3, ab, col0=pw, ts=512)
    yf = _seq_dft(p, q, bm=512, bn=512)
    h = _out_proj(a.reshape(n, pw), yf.reshape(n, mix - pw), w_out.astype(BF16), x2, bm=1024, bn=512)

    rw = jnp.zeros((d, LANES), BF16).at[:, :n_exp].set(router_w.astype(BF16))
    rb = jnp.zeros((1, LANES), F32).at[0, :n_exp].set(router_b)
    z2p, logits = _norm_router(h, norm2_g.reshape(1, d), rw, rb, bm=256)

    pos, gates, texp, ntl, last = _route(logits, n_exp=n_exp, tm=tm, tmax=tmax)
    chunk = min(512, n)
    pos_c = pos.reshape(TOP_K, n // chunk, chunk).transpose(1, 0, 2)
    xs = _dispatch(last[0, :n_exp], pos_c, z2p, n_rows=tmax * tm, tm=tm)

    wgu = jnp.concatenate([w_gate, w_up], axis=-1).astype(BF16)
    bgu = jnp.concatenate([b_gate, b_up], axis=-1).reshape(n_exp, 1, 2 * f)
    ys = _expert_ffn(texp[0, :tmax], ntl[0, :1], xs, wgu, bgu, w_down.astype(BF16),
                     b_down.reshape(n_exp, 1, d), tm=tm)

    bt = min(128, n)
    pos_b = pos.reshape(TOP_K, n // bt, bt).transpose(1, 0, 2)
    return pos_b, ys, h, gates.T


def kernel(x, norm1_g, w_in, pool_w, pool_scale, fourier_w, w_out, norm2_g, router_w, router_b,
           w_gate, b_gate, w_up, b_up, w_down, b_down, final_g):
    b, seq, d = x.shape
    assert w_in.shape[0] == 1, "only a single layer is supported"
    pos_b, ys, h, gates_t = _layer(
        x.reshape(b * seq, d), b, seq, norm1_g[0], w_in[0], pool_w[0], pool_scale[0], fourier_w[0], w_out[0],
        norm2_g[0], router_w[0], router_b[0], w_gate[0], b_gate[0], w_up[0], b_up[0], w_down[0], b_down[0])
    out = _combine(pos_b, ys, h, gates_t, final_g.reshape(1, d), bt=min(128, b * seq))
    return out.reshape(b, seq, d)
```

```python
import functools
import math

import jax
import jax.numpy as jnp
from jax import lax
from jax.experimental import pallas as pl
from jax.experimental.pallas import tpu as pltpu

F32 = jnp.float32
BF16 = jnp.bfloat16
I32 = jnp.int32
U32 = jnp.uint32

RMS_EPS = 1e-5
POOL_WINDOWS = (2, 4, 8, 16)
TOP_K = 4
SWIGLU_LIMIT = 7.0
SWIGLU_ALPHA = 1.702

LANES = 128
BF16_SUBLANES = 16
HALO = 16
VMEM_LIMIT = 56 * 1024 * 1024
HI_MASK = 0xFFFF0000


def _cparams(sem, vmem=VMEM_LIMIT):
    return pltpu.CompilerParams(dimension_semantics=sem, vmem_limit_bytes=vmem)


def _pack_halves(lo_f32, hi_f32):
    lo = lax.bitcast_convert_type(lo_f32, U32) >> 16
    hi = lax.bitcast_convert_type(hi_f32, U32) & jnp.uint32(HI_MASK)
    return lo | hi


def _unpack_halves(w):
    lo = lax.bitcast_convert_type(w << 16, F32)
    hi = lax.bitcast_convert_type(w & jnp.uint32(HI_MASK), F32)
    return lo, hi


def _norm_mm_kernel(x_ref, g_ref, w_ref, o_ref, z_ref, *, rows):
    @pl.when(pl.program_id(1) == 0)
    def _():
        def body(c, carry):
            r = pl.multiple_of(c * rows, rows)
            xc = x_ref[pl.ds(r, rows), :]
            ms = jnp.mean(xc * xc, axis=-1, keepdims=True)
            z_ref[pl.ds(r, rows), :] = (xc * lax.rsqrt(ms + RMS_EPS) * g_ref[...]).astype(BF16)
            return carry
        lax.fori_loop(0, x_ref.shape[0] // rows, body, 0)

    o_ref[...] = jnp.dot(z_ref[...], w_ref[...], preferred_element_type=F32).astype(o_ref.dtype)


def _norm_matmul(x, g, w, *, bm, bn):
    n, d = x.shape
    m = w.shape[1]
    bm, bn = min(bm, n), min(bn, m)
    return pl.pallas_call(
        functools.partial(_norm_mm_kernel, rows=min(32, bm)),
        grid=(n // bm, m // bn),
        in_specs=[pl.BlockSpec((bm, d), lambda i, j: (i, 0)),
                  pl.BlockSpec((1, d), lambda i, j: (0, 0)),
                  pl.BlockSpec((d, bn), lambda i, j: (0, j))],
        out_specs=pl.BlockSpec((bm, bn), lambda i, j: (i, j)),
        out_shape=jax.ShapeDtypeStruct((n, m), BF16),
        scratch_shapes=[pltpu.VMEM((bm, d), BF16)],
        compiler_params=_cparams(("parallel", "arbitrary")),
        name="norm1_w_in",
    )(x, g, w)


def _pool_kernel(cur_ref, prev_ref, next_ref, pw_ref, sc_ref, o_ref, *, seq, ts, c):
    t = pl.program_id(1)
    has_prev = (t > 0).astype(F32)
    has_next = (t < pl.num_programs(1) - 1).astype(F32)
    n_ext = ts + 2 * HALO
    tok = t * ts + lax.broadcasted_iota(I32, (ts, 1), 0)
    for g, w in enumerate(POOL_WINDOWS):
        sl = slice(g * c, (g + 1) * c)
        cur = cur_ref[0, :, sl].astype(F32)
        prev = prev_ref[0, :, sl].astype(F32) * has_prev
        nxt = next_ref[0, :, sl].astype(F32) * has_next
        ext = jnp.concatenate([prev, cur, nxt], axis=0)
        s = ext + pltpu.roll(ext, 1, 0)
        h = 1
        while 2 * h < w:
            s = pltpu.roll(s, h, 0) + pltpu.roll(s, n_ext - h, 0)
            h *= 2
        win = s[HALO:HALO + ts]
        lo = jnp.maximum(tok - w // 2, 0)
        hi = jnp.minimum(tok + w // 2 - 1, seq - 1)
        cnt = (hi - lo + 1).astype(F32)
        p = win / cnt - cur
        y = jnp.dot(p.astype(BF16), pw_ref[g], preferred_element_type=F32)
        o_ref[0, :, sl] = (y * sc_ref[:, sl]).astype(o_ref.dtype)


def _pool_mixer(u3, pool_w, pool_scale, *, ts):
    b, seq, _ = u3.shape
    g, c, _ = pool_w.shape
    pw = g * c
    ts = min(ts, seq)
    nh = seq // HALO
    per = ts // HALO
    return pl.pallas_call(
        functools.partial(_pool_kernel, seq=seq, ts=ts, c=c),
        grid=(b, seq // ts),
        in_specs=[pl.BlockSpec((1, ts, pw), lambda i, t: (i, t, 0)),
                  pl.BlockSpec((1, HALO, pw), lambda i, t: (i, jnp.maximum(t * per - 1, 0), 0)),
                  pl.BlockSpec((1, HALO, pw), lambda i, t: (i, jnp.minimum((t + 1) * per, nh - 1), 0)),
                  pl.BlockSpec((g, c, c), lambda i, t: (0, 0, 0)),
                  pl.BlockSpec((1, pw), lambda i, t: (0, 0))],
        out_specs=pl.BlockSpec((1, ts, pw), lambda i, t: (i, t, 0)),
        out_shape=jax.ShapeDtypeStruct((b, seq, pw), BF16),
        compiler_params=_cparams(("parallel", "parallel")),
        name="pool_mixer",
    )(u3, u3, u3, pool_w, pool_scale)


DFT_N1 = 256
SUBLANES = 8


def _dft_mats(n, scale, dtype):
    j = jnp.arange(n, dtype=I32)
    ang = ((j[:, None] * j[None, :]) % n).astype(F32) * (2.0 * math.pi / n)
    return (jnp.cos(ang) * scale).astype(dtype), (jnp.sin(ang) * scale).astype(dtype)


def _fourier_prep_kernel(cc_ref, sc_ref, w_ref, o_ref, *, c):
    w = w_ref[0]
    o_ref[0, :c, :] = jnp.dot(cc_ref[...], w, preferred_element_type=F32,
                              precision=lax.Precision.HIGHEST).astype(o_ref.dtype)
    o_ref[0, c:, :] = jnp.dot(sc_ref[...], w, preferred_element_type=F32,
                              precision=lax.Precision.HIGHEST).astype(o_ref.dtype)


def _fourier_prep(fourier_w):
    h, c, _ = fourier_w.shape
    cc, sc = _dft_mats(c, c ** -0.5, F32)
    return pl.pallas_call(
        functools.partial(_fourier_prep_kernel, c=c),
        grid=(h,),
        in_specs=[pl.BlockSpec((c, c), lambda i: (0, 0)),
                  pl.BlockSpec((c, c), lambda i: (0, 0)),
                  pl.BlockSpec((1, c, c), lambda i: (i, 0, 0))],
        out_specs=pl.BlockSpec((1, 2 * c, c), lambda i: (i, 0, 0)),
        out_shape=jax.ShapeDtypeStruct((h, 2 * c, c), BF16),
        compiler_params=_cparams(("parallel",)),
        name="fourier_prep",
    )(cc, sc, fourier_w)


def _fft(xs):
    n = len(xs)
    if n == 1:
        return xs
    ev, od = _fft(xs[0::2]), _fft(xs[1::2])
    out = [None] * n
    for k in range(n // 2):
        re, im = od[k]
        if k == 0:
            tr, ti = re, im
        elif 4 * k == n:
            tr, ti = im, -re
        else:
            wr, wi = math.cos(2.0 * math.pi * k / n), -math.sin(2.0 * math.pi * k / n)
            tr, ti = re * wr - im * wi, re * wi + im * wr
        er, ei = ev[k]
        out[k] = (er + tr, ei + ti)
        out[k + n // 2] = (er - tr, ei - ti)
    return out


def _fourier_kernel(*refs, n1, n2, c, rows):
    x_refs = refs[:n2]
    cs_ref, tw_ref, ab_ref, o_ref, y_ref = refs[n2:]
    for j in range(n2):
        y = jnp.dot(cs_ref[...], x_refs[j][0], preferred_element_type=F32)
        y_ref[0, j] = y[:n1]
        y_ref[1, j] = y[n1:]

    nl = c // LANES

    def tile(i, carry):
        r = pl.multiple_of((i // nl) * SUBLANES, SUBLANES)
        l = pl.multiple_of((i % nl) * LANES, LANES)
        zs = []
        for j in range(n2):
            yc = y_ref[0, j, pl.ds(r, SUBLANES), pl.ds(l, LANES)]
            ys = y_ref[1, j, pl.ds(r, SUBLANES), pl.ds(l, LANES)]
            ct = tw_ref[0, j, pl.ds(r, SUBLANES), :]
            st = tw_ref[1, j, pl.ds(r, SUBLANES), :]
            zs.append((yc * ct - ys * st, -(ys * ct + yc * st)))
        gs = _fft(zs)
        for k in range(n2):
            y_ref[0, k, pl.ds(r, SUBLANES), pl.ds(l, LANES)] = gs[k][0]
            y_ref[1, k, pl.ds(r, SUBLANES), pl.ds(l, LANES)] = gs[k][1]
        return carry
    lax.fori_loop(0, (n1 // SUBLANES) * nl, tile, 0)

    per = max(1, min(n2, rows // n1))
    for s in range(0, n2, per):
        gr = y_ref[0, s:s + per].reshape(per * n1, c).astype(BF16)
        gi = y_ref[1, s:s + per].reshape(per * n1, c).astype(BF16)
        y = jnp.dot(gr, ab_ref[0, :c, :], preferred_element_type=F32)
        y = y + jnp.dot(gi, ab_ref[0, c:, :], preferred_element_type=F32)
        o_ref[0, s * n1:(s + per) * n1, :] = y.astype(o_ref.dtype)


def _fourier_mixer(u3, ab, *, col0, rows):
    b, seq, mix = u3.shape
    h, _, c = ab.shape
    n1 = min(DFT_N1, seq)
    n2 = seq // n1
    assert n1 * n2 == seq and n2 & (n2 - 1) == 0, "sequence length must be N1 * 2^m"
    u2 = u3.reshape(b, n1, n2 * mix)
    cb0, cbs = col0 // c, mix // c
    cmat, smat = _dft_mats(n1, 1.0, BF16)
    cs = jnp.concatenate([cmat, smat], axis=0)
    ang = (jnp.arange(n2, dtype=I32)[:, None] * jnp.arange(n1, dtype=I32)[None, :]).astype(F32)
    ang = ang * (2.0 * math.pi / seq)
    tw = jnp.stack([jnp.cos(ang), jnp.sin(ang)]) * (seq ** -0.5)
    tw = jnp.broadcast_to(tw[..., None], (2, n2, n1, LANES))

    def x_spec(j):
        return pl.BlockSpec((1, n1, c), lambda i, k: (i, 0, j * cbs + cb0 + k))

    return pl.pallas_call(
        functools.partial(_fourier_kernel, n1=n1, n2=n2, c=c, rows=rows),
        grid=(b, h),
        in_specs=[x_spec(j) for j in range(n2)] + [
            pl.BlockSpec((2 * n1, n1), lambda i, k: (0, 0), pipeline_mode=pl.Buffered(1)),
            pl.BlockSpec((2, n2, n1, LANES), lambda i, k: (0, 0, 0, 0), pipeline_mode=pl.Buffered(1)),
            pl.BlockSpec((1, 2 * c, c), lambda i, k: (k, 0, 0))],
        out_specs=pl.BlockSpec((1, seq, c), lambda i, k: (i, 0, k)),
        out_shape=jax.ShapeDtypeStruct((b, seq, h * c), BF16),
        scratch_shapes=[pltpu.VMEM((2, n2, n1, c), F32)],
        compiler_params=_cparams(("parallel", "parallel")),
        name="fourier_mixer",
    )(*([u2] * n2), cs, tw, ab)


def _out_proj_kernel(a_ref, f_ref, w_ref, x_ref, o_ref, *, ka):
    y = jnp.dot(a_ref[...], w_ref[:ka, :], preferred_element_type=F32)
    y = y + jnp.dot(f_ref[...], w_ref[ka:, :], preferred_element_type=F32)
    o_ref[...] = x_ref[...] + y


def _out_proj(a, f, w, x, *, bm, bn):
    n, ka = a.shape
    kf = f.shape[1]
    d = w.shape[1]
    bm, bn = min(bm, n), min(bn, d)
    return pl.pallas_call(
        functools.partial(_out_proj_kernel, ka=ka),
        grid=(n // bm, d // bn),
        in_specs=[pl.BlockSpec((bm, ka), lambda i, j: (i, 0)),
                  pl.BlockSpec((bm, kf), lambda i, j: (i, 0)),
                  pl.BlockSpec((ka + kf, bn), lambda i, j: (0, j)),
                  pl.BlockSpec((bm, bn), lambda i, j: (i, j))],
        out_specs=pl.BlockSpec((bm, bn), lambda i, j: (i, j)),
        out_shape=jax.ShapeDtypeStruct((n, d), F32),
        compiler_params=_cparams(("parallel", "parallel")),
        name="w_out_residual",
    )(a, f, w, x)


def _norm_router_kernel(h_ref, g_ref, rw_ref, rb_ref, zp_ref, lg_ref, z_ref, *, rows):
    dh = h_ref.shape[1] // 2

    def body(c, carry):
        r = pl.multiple_of(c * rows, rows)
        xc = h_ref[pl.ds(r, rows), :]
        ms = jnp.mean(xc * xc, axis=-1, keepdims=True)
        zb = (xc * lax.rsqrt(ms + RMS_EPS) * g_ref[...]).astype(BF16)
        z_ref[pl.ds(r, rows), :] = zb
        zr = zb.astype(F32)
        zp_ref[pl.ds(r, rows), :] = _pack_halves(zr[:, :dh], zr[:, dh:])
        return carry
    lax.fori_loop(0, h_ref.shape[0] // rows, body, 0)
    lg_ref[...] = jnp.dot(z_ref[...], rw_ref[...], preferred_element_type=F32) + rb_ref[...]


def _norm_router(h, g, rw, rb, *, bm):
    n, d = h.shape
    bm = min(bm, n)
    return pl.pallas_call(
        functools.partial(_norm_router_kernel, rows=min(32, bm)),
        grid=(n // bm,),
        in_specs=[pl.BlockSpec((bm, d), lambda i: (i, 0)),
                  pl.BlockSpec((1, d), lambda i: (0, 0)),
                  pl.BlockSpec((d, LANES), lambda i: (0, 0)),
                  pl.BlockSpec((1, LANES), lambda i: (0, 0))],
        out_specs=[pl.BlockSpec((bm, d // 2), lambda i: (i, 0)),
                   pl.BlockSpec((bm, LANES), lambda i: (i, 0))],
        out_shape=[jax.ShapeDtypeStruct((n, d // 2), U32),
                   jax.ShapeDtypeStruct((n, LANES), F32)],
        scratch_shapes=[pltpu.VMEM((bm, d), BF16)],
        compiler_params=_cparams(("parallel",)),
        name="norm2_router",
    )(h, g, rw, rb)


def _route_kernel(lg_ref, pos_ref, gate_ref, texp_ref, ntl_ref, last_ref, idx_s, rank_s,
                  *, n_tok, n_exp, tm, tb, tmax_pad):
    shift = tm.bit_length() - 1
    iota_e = lax.broadcasted_iota(I32, (n_exp, tb), 0)
    before = (lax.broadcasted_iota(I32, (tb, tb), 0)
              < lax.broadcasted_iota(I32, (tb, tb), 1)).astype(BF16)

    def pass1(i, counts):
        off = pl.multiple_of(i * tb, tb)
        l = lg_ref[pl.ds(off, tb), :].T[:n_exp, :]
        vals, hots = [], []
        for k in range(TOP_K):
            m = jnp.max(l, axis=0, keepdims=True)
            idx = jnp.min(jnp.where(l == m, iota_e, n_exp), axis=0, keepdims=True)
            hot = iota_e == idx
            l = jnp.where(hot, -jnp.inf, l)
            vals.append(m)
            hots.append(hot)
            idx_s[k:k + 1, pl.ds(off, tb)] = idx
        exps = [jnp.exp(v - vals[0]) for v in vals]
        tot = exps[0]
        for e in exps[1:]:
            tot = tot + e
        sel = hots[0].astype(F32)
        for hot in hots[1:]:
            sel = sel + hot.astype(F32)
        rank = jnp.dot(sel.astype(BF16), before, preferred_element_type=F32) + counts
        for k in range(TOP_K):
            gate_ref[k:k + 1, pl.ds(off, tb)] = exps[k] / tot
            rank_s[k:k + 1, pl.ds(off, tb)] = jnp.sum(jnp.where(hots[k], rank, 0.0), axis=0, keepdims=True)
        return counts + jnp.sum(sel, axis=1, keepdims=True)

    counts = lax.fori_loop(0, n_tok // tb, pass1, jnp.zeros((n_exp, 1), F32))
    ntile = (counts.astype(I32) + (tm - 1)) >> shift
    sub = lax.broadcasted_iota(I32, (n_exp, LANES), 0)
    lane = lax.broadcasted_iota(I32, (n_exp, LANES), 1)
    ntile_row = jnp.sum(jnp.where(sub == lane, ntile, 0), axis=0, keepdims=True)
    start = jnp.sum(jnp.where(lane < sub, ntile_row, 0), axis=1, keepdims=True)
    t_lane = lax.broadcasted_iota(I32, (n_exp, tmax_pad), 1)
    texp_ref[...] = jnp.sum((start <= t_lane).astype(I32), axis=0, keepdims=True) - 1
    ntl_ref[...] = jnp.sum(ntile, axis=0, keepdims=True) + jnp.zeros((1, LANES), I32)
    last = jnp.where(ntile > 0, (start + ntile - 1) << shift, -1)
    last_ref[...] = jnp.sum(jnp.where(sub == lane, last, 0), axis=0, keepdims=True)
    start_rows = start << shift

    def pass2(i, carry):
        off = pl.multiple_of(i * tb, tb)
        for k in range(TOP_K):
            hot = iota_e == idx_s[k:k + 1, pl.ds(off, tb)]
            base = jnp.sum(jnp.where(hot, start_rows, 0), axis=0, keepdims=True)
            pos_ref[k:k + 1, pl.ds(off, tb)] = rank_s[k:k + 1, pl.ds(off, tb)].astype(I32) + base
        return carry
    lax.fori_loop(0, n_tok // tb, pass2, 0)


def _route(logits, *, n_exp, tm, tmax):
    n_tok = logits.shape[0]
    tb = min(256, n_tok)
    tmax_pad = -(-tmax // LANES) * LANES
    return pl.pallas_call(
        functools.partial(_route_kernel, n_tok=n_tok, n_exp=n_exp, tm=tm, tb=tb, tmax_pad=tmax_pad),
        out_shape=[jax.ShapeDtypeStruct((TOP_K, n_tok), I32),
                   jax.ShapeDtypeStruct((TOP_K, n_tok), F32),
                   jax.ShapeDtypeStruct((1, tmax_pad), I32),
                   jax.ShapeDtypeStruct((1, LANES), I32),
                   jax.ShapeDtypeStruct((1, LANES), I32)],
        scratch_shapes=[pltpu.VMEM((TOP_K, n_tok), I32), pltpu.VMEM((TOP_K, n_tok), F32)],
        compiler_params=_cparams(None),
        name="route",
    )(logits)


def _dispatch_kernel(last_ref, pos_ref, z_ref, xs_hbm, zero_v, zsem, sem, *, n_exp, tm, chunk):
    @pl.when(pl.program_id(0) == 0)
    def _():
        zero_v[...] = jnp.zeros_like(zero_v)
        for e in range(n_exp):
            r = last_ref[e]

            @pl.when(r >= 0)
            def _():
                cp = pltpu.make_async_copy(zero_v, xs_hbm.at[pl.ds(pl.multiple_of(r, tm), tm)], zsem)
                cp.start()
                cp.wait()

    def issue(i, carry):
        for k in range(TOP_K):
            pltpu.make_async_copy(z_ref.at[pl.ds(i, 1)], xs_hbm.at[pl.ds(pos_ref[0, k, i], 1)], sem).start()
        return carry
    lax.fori_loop(0, chunk, issue, 0, unroll=4)
    for k in range(TOP_K):
        pltpu.make_async_copy(z_ref, xs_hbm.at[pl.ds(0, chunk)], sem).wait()


def _dispatch(last, pos3, z2p, *, n_rows, tm):
    n_tok, dh = z2p.shape
    nchunk, _, chunk = pos3.shape
    n_exp = last.shape[0]
    grid_spec = pltpu.PrefetchScalarGridSpec(
        num_scalar_prefetch=1,
        grid=(nchunk,),
        in_specs=[pl.BlockSpec((1, TOP_K, chunk), lambda c, last: (c, 0, 0), memory_space=pltpu.SMEM),
                  pl.BlockSpec((chunk, dh), lambda c, last: (c, 0))],
        out_specs=pl.BlockSpec(memory_space=pl.ANY),
        scratch_shapes=[pltpu.VMEM((tm, dh), U32), pltpu.SemaphoreType.DMA(()), pltpu.SemaphoreType.DMA(())],
    )
    return pl.pallas_call(
        functools.partial(_dispatch_kernel, n_exp=n_exp, tm=tm, chunk=chunk),
        grid_spec=grid_spec,
        out_shape=jax.ShapeDtypeStruct((n_rows, dh), U32),
        compiler_params=_cparams(("arbitrary",)),
        name="dispatch",
    )(last, pos3, z2p)


def _ffn_kernel(texp_ref, ntl_ref, x_ref, wgu_ref, bgu_ref, wd_ref, bd_ref, o_ref, *, f):
    dh = x_ref.shape[1]

    @pl.when(pl.program_id(0) < ntl_ref[0])
    def _():
        lo, hi = _unpack_halves(x_ref[...])
        hgu = jnp.dot(lo.astype(BF16), wgu_ref[0, :dh, :], preferred_element_type=F32)
        hgu = hgu + jnp.dot(hi.astype(BF16), wgu_ref[0, dh:, :], preferred_element_type=F32)
        hgu = hgu + bgu_ref[0]
        hg = jnp.minimum(hgu[:, :f], SWIGLU_LIMIT)
        hu = jnp.clip(hgu[:, f:], -SWIGLU_LIMIT, SWIGLU_LIMIT)
        act = hg * (1.0 / (1.0 + jnp.exp(-SWIGLU_ALPHA * hg))) * (hu + 1.0)
        y = jnp.dot(act.astype(BF16), wd_ref[0], preferred_element_type=F32) + bd_ref[0]
        yr = y.astype(BF16).astype(F32)
        o_ref[...] = _pack_halves(yr[:, :dh], yr[:, dh:])


def _expert_ffn(texp, ntl, xs, wgu, bgu, wd, bd, *, tm):
    n_rows, dh = xs.shape
    n_exp, d, f2 = wgu.shape
    f = f2 // 2
    tmax = n_rows // tm

    def x_map(t, te, nt):
        return (jnp.minimum(t, nt[0] - 1), 0)

    def w_map(t, te, nt):
        return (te[t], 0, 0)

    grid_spec = pltpu.PrefetchScalarGridSpec(
        num_scalar_prefetch=2,
        grid=(tmax,),
        in_specs=[pl.BlockSpec((tm, dh), x_map),
                  pl.BlockSpec((1, d, f2), w_map),
                  pl.BlockSpec((1, 1, f2), w_map),
                  pl.BlockSpec((1, f, d), w_map),
                  pl.BlockSpec((1, 1, d), w_map)],
        out_specs=pl.BlockSpec((tm, dh), lambda t, te, nt: (t, 0)),
    )
    return pl.pallas_call(
        functools.partial(_ffn_kernel, f=f),
        grid_spec=grid_spec,
        out_shape=jax.ShapeDtypeStruct((n_rows, dh), U32),
        compiler_params=_cparams(("arbitrary",)),
        name="expert_ffn",
    )(texp, ntl, xs, wgu, bgu, wd, bd)


def _combine_kernel(pos_c, pos_n, ys_hbm, h_ref, g_ref, fg_ref, o_ref, buf, sem, *, bt):
    i = pl.program_id(0)
    n = pl.num_programs(0)
    slot = i % 2
    dh = buf.shape[-1]

    def row_copy(src, s, k, r):
        return pltpu.make_async_copy(ys_hbm.at[pl.ds(src, 1)], buf.at[s, k, pl.ds(r, 1)], sem.at[s])

    def issue(pos_ref, s):
        def body(r, carry):
            for k in range(TOP_K):
                row_copy(pos_ref[0, k, r], s, k, r).start()
            return carry
        lax.fori_loop(0, bt, body, 0, unroll=4)

    @pl.when(i == 0)
    def _():
        issue(pos_c, 0)

    @pl.when(i + 1 < n)
    def _():
        issue(pos_n, 1 - slot)

    for k in range(TOP_K):
        pltpu.make_async_copy(ys_hbm.at[pl.ds(0, bt)], buf.at[slot, k], sem.at[slot]).wait()

    acc_lo = h_ref[:, :dh]
    acc_hi = h_ref[:, dh:]
    for k in range(TOP_K):
        lo, hi = _unpack_halves(buf[slot, k])
        gk = g_ref[:, k:k + 1]
        acc_lo = acc_lo + gk * lo
        acc_hi = acc_hi + gk * hi
    ms = (jnp.sum(acc_lo * acc_lo, axis=-1, keepdims=True)
          + jnp.sum(acc_hi * acc_hi, axis=-1, keepdims=True)) / (2 * dh)
    r = lax.rsqrt(ms + RMS_EPS)
    o_ref[:, :dh] = acc_lo * r * fg_ref[:, :dh]
    o_ref[:, dh:] = acc_hi * r * fg_ref[:, dh:]


def _combine(pos3, ys, h, gates_t, fg, *, bt):
    n, d = h.shape
    nblk = n // bt
    return pl.pallas_call(
        functools.partial(_combine_kernel, bt=bt),
        grid=(nblk,),
        in_specs=[pl.BlockSpec((1, TOP_K, bt), lambda i: (i, 0, 0), memory_space=pltpu.SMEM),
                  pl.BlockSpec((1, TOP_K, bt), lambda i: (jnp.minimum(i + 1, nblk - 1), 0, 0),
                               memory_space=pltpu.SMEM),
                  pl.BlockSpec(memory_space=pl.ANY),
                  pl.BlockSpec((bt, d), lambda i: (i, 0)),
                  pl.BlockSpec((bt, TOP_K), lambda i: (i, 0)),
                  pl.BlockSpec((1, d), lambda i: (0, 0))],
        out_specs=pl.BlockSpec((bt, d), lambda i: (i, 0)),
        out_shape=jax.ShapeDtypeStruct((n, d), F32),
        scratch_shapes=[pltpu.VMEM((2, TOP_K, bt, d // 2), U32), pltpu.SemaphoreType.DMA((2,))],
        compiler_params=_cparams(("arbitrary",)),
        name="combine_final_norm",
    )(pos3, pos3, ys, h, gates_t, fg)


def _layer(x2, b, seq, norm1_g, w_in, pool_w, pool_scale, fourier_w, w_out, norm2_g,
           router_w, router_b, w_gate, b_gate, w_up, b_up, w_down, b_down):
    n, d = x2.shape
    mix = w_in.shape[1]
    pw = pool_scale.shape[0]
    n_exp = router_w.shape[1]
    f = w_gate.shape[2]
    tm = min(256, n)
    tmax = (n * TOP_K) // tm + n_exp

    u = _norm_matmul(x2, norm1_g.reshape(1, d), w_in.astype(BF16), bm=512, bn=1024)
    u3 = u.reshape(b, seq, mix)
    a = _pool_mixer(u3, pool_w.astype(BF16), pool_scale.reshape(1, pw), ts=256)
    yf = _fourier_mixer(u3, _fourier_prep(fourier_w), col0=pw, rows=1024)
    h = _out_proj(a.reshape(n, pw), yf.reshape(n, mix - pw), w_out.astype(BF16), x2, bm=1024, bn=512)

    rw = jnp.zeros((d, LANES), BF16).at[:, :n_exp].set(router_w.astype(BF16))
    rb = jnp.zeros((1, LANES), F32).at[0, :n_exp].set(router_b)
    z2p, logits = _norm_router(h, norm2_g.reshape(1, d), rw, rb, bm=256)

    pos, gates, texp, ntl, last = _route(logits, n_exp=n_exp, tm=tm, tmax=tmax)
    chunk = min(512, n)
    pos_c = pos.reshape(TOP_K, n // chunk, chunk).transpose(1, 0, 2)
    xs = _dispatch(last[0, :n_exp], pos_c, z2p, n_rows=tmax * tm, tm=tm)

    wgu = jnp.concatenate([w_gate, w_up], axis=-1).astype(BF16)
    bgu = jnp.concatenate([b_gate, b_up], axis=-1).reshape(n_exp, 1, 2 * f)
    ys = _expert_ffn(texp[0, :tmax], ntl[0, :1], xs, wgu, bgu, w_down.astype(BF16),
                     b_down.reshape(n_exp, 1, d), tm=tm)

    bt = min(128, n)
    pos_b = pos.reshape(TOP_K, n // bt, bt).transpose(1, 0, 2)
    return pos_b, ys, h, gates.T


def kernel(x, norm1_g, w_in, pool_w, pool_scale, fourier_w, w_out, norm2_g, router_w, router_b,
           w_gate, b_gate, w_up, b_up, w_down, b_down, final_g):
    b, seq, d = x.shape
    assert w_in.shape[0] == 1, "only a single layer is supported"
    pos_b, ys, h, gates_t = _layer(
        x.reshape(b * seq, d), b, seq, norm1_g[0], w_in[0], pool_w[0], pool_scale[0], fourier_w[0], w_out[0],
        norm2_g[0], router_w[0], router_b[0], w_gate[0], b_gate[0], w_up[0], b_up[0], w_down[0], b_down[0])
    out = _combine(pos_b, ys, h, gates_t, final_g.reshape(1, d), bt=min(128, b * seq))
    return out.reshape(b, seq, d)
```

```python
import functools
import math

import jax
import jax.numpy as jnp
from jax import lax
from jax.experimental import pallas as pl
from jax.experimental.pallas import tpu as pltpu

F32 = jnp.float32
BF16 = jnp.bfloat16
I32 = jnp.int32
U32 = jnp.uint32

RMS_EPS = 1e-5
POOL_WINDOWS = (2, 4, 8, 16)
TOP_K = 4
SWIGLU_LIMIT = 7.0
SWIGLU_ALPHA = 1.702

LANES = 128
BF16_SUBLANES = 16
HALO = 16
VMEM_LIMIT = 56 * 1024 * 1024
HI_MASK = 0xFFFF0000


def _cparams(sem, vmem=VMEM_LIMIT):
    return pltpu.CompilerParams(dimension_semantics=sem, vmem_limit_bytes=vmem)


def _pack_halves(lo_f32, hi_f32):
    lo = lax.bitcast_convert_type(lo_f32, U32) >> 16
    hi = lax.bitcast_convert_type(hi_f32, U32) & jnp.uint32(HI_MASK)
    return lo | hi


def _unpack_halves(w):
    lo = lax.bitcast_convert_type(w << 16, F32)
    hi = lax.bitcast_convert_type(w & jnp.uint32(HI_MASK), F32)
    return lo, hi


def _norm_mm_kernel(x_ref, g_ref, w_ref, up_ref, uf_ref, z_ref, *, rows, pool_blocks, c):
    j = pl.program_id(1)

    @pl.when(j == 0)
    def _():
        def body(i, carry):
            r = pl.multiple_of(i * rows, rows)
            xc = x_ref[pl.ds(r, rows), :]
            ms = jnp.mean(xc * xc, axis=-1, keepdims=True)
            z_ref[pl.ds(r, rows), :] = (xc * lax.rsqrt(ms + RMS_EPS) * g_ref[...]).astype(BF16)
            return carry
        lax.fori_loop(0, x_ref.shape[0] // rows, body, 0)

    y = jnp.dot(z_ref[...], w_ref[...], preferred_element_type=F32)

    @pl.when(j < pool_blocks)
    def _():
        up_ref[...] = y.astype(up_ref.dtype)

    @pl.when(j >= pool_blocks)
    def _():
        yr = y.astype(BF16).astype(F32)
        ch = c // 2
        for hd in range(y.shape[1] // c):
            words = _pack_halves(yr[:, hd * c:hd * c + ch], yr[:, hd * c + ch:(hd + 1) * c])
            for s in range(ch // LANES):
                uf_ref[hd * (ch // LANES) + s] = words[:, s * LANES:(s + 1) * LANES]


def _norm_matmul(x, g, w, *, pw, c, bm, bn):
    n, d = x.shape
    m = w.shape[1]
    bm, bn = min(bm, n), min(bn, pw, m - pw)
    assert pw % bn == 0 and (m - pw) % bn == 0 and bn % c == 0 and (c // 2) % LANES == 0
    pool_blocks = pw // bn
    slabs_blk = bn // 2 // LANES
    return pl.pallas_call(
        functools.partial(_norm_mm_kernel, rows=min(32, bm), pool_blocks=pool_blocks, c=c),
        grid=(n // bm, m // bn),
        in_specs=[pl.BlockSpec((bm, d), lambda i, j: (i, 0)),
                  pl.BlockSpec((1, d), lambda i, j: (0, 0)),
                  pl.BlockSpec((d, bn), lambda i, j: (0, j))],
        out_specs=[pl.BlockSpec((bm, bn), lambda i, j: (i, jnp.minimum(j, pool_blocks - 1))),
                   pl.BlockSpec((slabs_blk, bm, LANES), lambda i, j: (jnp.maximum(j - pool_blocks, 0), i, 0))],
        out_shape=[jax.ShapeDtypeStruct((n, pw), BF16),
                   jax.ShapeDtypeStruct(((m - pw) // 2 // LANES, n, LANES), U32)],
        scratch_shapes=[pltpu.VMEM((bm, d), BF16)],
        compiler_params=_cparams(("parallel", "arbitrary")),
        name="norm1_w_in",
    )(x, g, w)


def _pool_kernel(cur_ref, prev_ref, next_ref, pw_ref, sc_ref, o_ref, *, seq, ts, c):
    t = pl.program_id(1)
    has_prev = (t > 0).astype(F32)
    has_next = (t < pl.num_programs(1) - 1).astype(F32)
    n_ext = ts + 2 * HALO
    tok = t * ts + lax.broadcasted_iota(I32, (ts, 1), 0)
    for g, w in enumerate(POOL_WINDOWS):
        sl = slice(g * c, (g + 1) * c)
        cur = cur_ref[0, :, sl].astype(F32)
        prev = prev_ref[0, :, sl].astype(F32) * has_prev
        nxt = next_ref[0, :, sl].astype(F32) * has_next
        ext = jnp.concatenate([prev, cur, nxt], axis=0)
        s = ext + pltpu.roll(ext, 1, 0)
        h = 1
        while 2 * h < w:
            s = pltpu.roll(s, h, 0) + pltpu.roll(s, n_ext - h, 0)
            h *= 2
        win = s[HALO:HALO + ts]
        lo = jnp.maximum(tok - w // 2, 0)
        hi = jnp.minimum(tok + w // 2 - 1, seq - 1)
        cnt = (hi - lo + 1).astype(F32)
        p = win / cnt - cur
        y = jnp.dot(p.astype(BF16), pw_ref[g], preferred_element_type=F32)
        o_ref[0, :, sl] = (y * sc_ref[:, sl]).astype(o_ref.dtype)


def _pool_mixer(u3, pool_w, pool_scale, *, ts):
    b, seq, _ = u3.shape
    g, c, _ = pool_w.shape
    pw = g * c
    ts = min(ts, seq)
    nh = seq // HALO
    per = ts // HALO
    return pl.pallas_call(
        functools.partial(_pool_kernel, seq=seq, ts=ts, c=c),
        grid=(b, seq // ts),
        in_specs=[pl.BlockSpec((1, ts, pw), lambda i, t: (i, t, 0)),
                  pl.BlockSpec((1, HALO, pw), lambda i, t: (i, jnp.maximum(t * per - 1, 0), 0)),
                  pl.BlockSpec((1, HALO, pw), lambda i, t: (i, jnp.minimum((t + 1) * per, nh - 1), 0)),
                  pl.BlockSpec((g, c, c), lambda i, t: (0, 0, 0)),
                  pl.BlockSpec((1, pw), lambda i, t: (0, 0))],
        out_specs=pl.BlockSpec((1, ts, pw), lambda i, t: (i, t, 0)),
        out_shape=jax.ShapeDtypeStruct((b, seq, pw), BF16),
        compiler_params=_cparams(("parallel", "parallel")),
        name="pool_mixer",
    )(u3, u3, u3, pool_w, pool_scale)


DFT_N1 = 256
SUBLANES = 8


def _dft_mats(n, scale, dtype):
    j = jnp.arange(n, dtype=I32)
    ang = ((j[:, None] * j[None, :]) % n).astype(F32) * (2.0 * math.pi / n)
    return (jnp.cos(ang) * scale).astype(dtype), (jnp.sin(ang) * scale).astype(dtype)


def _fourier_prep_kernel(cc_ref, sc_ref, w_ref, o_ref, *, c):
    w = w_ref[0]
    o_ref[0, :c, :] = jnp.dot(cc_ref[...], w, preferred_element_type=F32,
                              precision=lax.Precision.HIGHEST).astype(o_ref.dtype)
    o_ref[0, c:, :] = jnp.dot(sc_ref[...], w, preferred_element_type=F32,
                              precision=lax.Precision.HIGHEST).astype(o_ref.dtype)


def _fourier_prep(fourier_w):
    h, c, _ = fourier_w.shape
    cc, sc = _dft_mats(c, c ** -0.5, F32)
    return pl.pallas_call(
        functools.partial(_fourier_prep_kernel, c=c),
        grid=(h,),
        in_specs=[pl.BlockSpec((c, c), lambda i: (0, 0)),
                  pl.BlockSpec((c, c), lambda i: (0, 0)),
                  pl.BlockSpec((1, c, c), lambda i: (i, 0, 0))],
        out_specs=pl.BlockSpec((1, 2 * c, c), lambda i: (i, 0, 0)),
        out_shape=jax.ShapeDtypeStruct((h, 2 * c, c), BF16),
        compiler_params=_cparams(("parallel",)),
        name="fourier_prep",
    )(cc, sc, fourier_w)


def _fft(xs):
    n = len(xs)
    if n == 1:
        return xs
    ev, od = _fft(xs[0::2]), _fft(xs[1::2])
    out = [None] * n
    for k in range(n // 2):
        re, im = od[k]
        if k == 0:
            tr, ti = re, im
        elif 4 * k == n:
            tr, ti = im, -re
        else:
            wr, wi = math.cos(2.0 * math.pi * k / n), -math.sin(2.0 * math.pi * k / n)
            tr, ti = re * wr - im * wi, re * wi + im * wr
        er, ei = ev[k]
        out[k] = (er + tr, ei + ti)
        out[k + n // 2] = (er - tr, ei - ti)
    return out


def _fourier_kernel(x_ref, cs_ref, tw_ref, ab_ref, o_ref, y_ref, *, n1, n2, c, rows):
    for j in range(n2):
        halves = [_unpack_halves(x_ref[s, pl.ds(j, n1, stride=n2), :]) for s in range(x_ref.shape[0])]
        xj = jnp.concatenate([lo for lo, _ in halves] + [hi for _, hi in halves], axis=1).astype(BF16)
        y = jnp.dot(cs_ref[...], xj, preferred_element_type=F32)
        y_ref[0, j] = y[:n1]
        y_ref[1, j] = y[n1:]

    nl = c // LANES

    def tile(i, carry):
        r = pl.multiple_of((i // nl) * SUBLANES, SUBLANES)
        l = pl.multiple_of((i % nl) * LANES, LANES)
        zs = []
        for j in range(n2):
            yc = y_ref[0, j, pl.ds(r, SUBLANES), pl.ds(l, LANES)]
            ys = y_ref[1, j, pl.ds(r, SUBLANES), pl.ds(l, LANES)]
            ct = tw_ref[0, j, pl.ds(r, SUBLANES), :]
            st = tw_ref[1, j, pl.ds(r, SUBLANES), :]
            zs.append((yc * ct - ys * st, -(ys * ct + yc * st)))
        gs = _fft(zs)
        for k in range(n2):
            y_ref[0, k, pl.ds(r, SUBLANES), pl.ds(l, LANES)] = gs[k][0]
            y_ref[1, k, pl.ds(r, SUBLANES), pl.ds(l, LANES)] = gs[k][1]
        return carry
    lax.fori_loop(0, (n1 // SUBLANES) * nl, tile, 0)

    per = max(1, min(n2, rows // n1))
    for s in range(0, n2, per):
        gr = y_ref[0, s:s + per].reshape(per * n1, c).astype(BF16)
        gi = y_ref[1, s:s + per].reshape(per * n1, c).astype(BF16)
        y = jnp.dot(gr, ab_ref[0, :c, :], preferred_element_type=F32)
        y = y + jnp.dot(gi, ab_ref[0, c:, :], preferred_element_type=F32)
        o_ref[0, s * n1:(s + per) * n1, :] = y.astype(o_ref.dtype)


def _fourier_mixer(uf, ab, *, b, rows):
    slabs, n, _ = uf.shape
    seq = n // b
    h, _, c = ab.shape
    spb = slabs // h
    n1 = min(DFT_N1, seq)
    n2 = seq // n1
    assert n1 * n2 == seq and n2 & (n2 - 1) == 0, "sequence length must be N1 * 2^m"
    cmat, smat = _dft_mats(n1, 1.0, BF16)
    cs = jnp.concatenate([cmat, smat], axis=0)
    ang = (jnp.arange(n2, dtype=I32)[:, None] * jnp.arange(n1, dtype=I32)[None, :]).astype(F32)
    ang = ang * (2.0 * math.pi / seq)
    tw = jnp.stack([jnp.cos(ang), jnp.sin(ang)]) * (seq ** -0.5)
    tw = jnp.broadcast_to(tw[..., None], (2, n2, n1, LANES))

    return pl.pallas_call(
        functools.partial(_fourier_kernel, n1=n1, n2=n2, c=c, rows=rows),
        grid=(b, h),
        in_specs=[
            pl.BlockSpec((spb, None, seq, LANES), lambda i, k: (k, i, 0, 0)),
            pl.BlockSpec((2 * n1, n1), lambda i, k: (0, 0), pipeline_mode=pl.Buffered(1)),
            pl.BlockSpec((2, n2, n1, LANES), lambda i, k: (0, 0, 0, 0), pipeline_mode=pl.Buffered(1)),
            pl.BlockSpec((1, 2 * c, c), lambda i, k: (k, 0, 0))],
        out_specs=pl.BlockSpec((1, seq, c), lambda i, k: (i, 0, k)),
        out_shape=jax.ShapeDtypeStruct((b, seq, h * c), BF16),
        scratch_shapes=[pltpu.VMEM((2, n2, n1, c), F32)],
        compiler_params=_cparams(("parallel", "parallel")),
        name="fourier_mixer",
    )(uf.reshape(slabs, b, seq, LANES), cs, tw, ab)


def _out_proj_kernel(a_ref, f_ref, w_ref, x_ref, o_ref, *, ka):
    y = jnp.dot(a_ref[...], w_ref[:ka, :], preferred_element_type=F32)
    y = y + jnp.dot(f_ref[...], w_ref[ka:, :], preferred_element_type=F32)
    o_ref[...] = x_ref[...] + y


def _out_proj(a, f, w, x, *, bm, bn):
    n, ka = a.shape
    kf = f.shape[1]
    d = w.shape[1]
    bm, bn = min(bm, n), min(bn, d)
    return pl.pallas_call(
        functools.partial(_out_proj_kernel, ka=ka),
        grid=(n // bm, d // bn),
        in_specs=[pl.BlockSpec((bm, ka), lambda i, j: (i, 0)),
                  pl.BlockSpec((bm, kf), lambda i, j: (i, 0)),
                  pl.BlockSpec((ka + kf, bn), lambda i, j: (0, j)),
                  pl.BlockSpec((bm, bn), lambda i, j: (i, j))],
        out_specs=pl.BlockSpec((bm, bn), lambda i, j: (i, j)),
        out_shape=jax.ShapeDtypeStruct((n, d), F32),
        compiler_params=_cparams(("parallel", "parallel")),
        name="w_out_residual",
    )(a, f, w, x)


def _norm_router_kernel(h_ref, g_ref, rw_ref, rb_ref, zp_ref, lg_ref, z_ref, *, rows):
    dh = h_ref.shape[1] // 2

    def body(c, carry):
        r = pl.multiple_of(c * rows, rows)
        xc = h_ref[pl.ds(r, rows), :]
        ms = jnp.mean(xc * xc, axis=-1, keepdims=True)
        zb = (xc * lax.rsqrt(ms + RMS_EPS) * g_ref[...]).astype(BF16)
        z_ref[pl.ds(r, rows), :] = zb
        zr = zb.astype(F32)
        zp_ref[pl.ds(r, rows), :] = _pack_halves(zr[:, :dh], zr[:, dh:])
        return carry
    lax.fori_loop(0, h_ref.shape[0] // rows, body, 0)
    lg_ref[...] = jnp.dot(z_ref[...], rw_ref[...], preferred_element_type=F32) + rb_ref[...]


def _norm_router(h, g, rw, rb, *, bm):
    n, d = h.shape
    bm = min(bm, n)
    return pl.pallas_call(
        functools.partial(_norm_router_kernel, rows=min(32, bm)),
        grid=(n // bm,),
        in_specs=[pl.BlockSpec((bm, d), lambda i: (i, 0)),
                  pl.BlockSpec((1, d), lambda i: (0, 0)),
                  pl.BlockSpec((d, LANES), lambda i: (0, 0)),
                  pl.BlockSpec((1, LANES), lambda i: (0, 0))],
        out_specs=[pl.BlockSpec((bm, d // 2), lambda i: (i, 0)),
                   pl.BlockSpec((bm, LANES), lambda i: (i, 0))],
        out_shape=[jax.ShapeDtypeStruct((n, d // 2), U32),
                   jax.ShapeDtypeStruct((n, LANES), F32)],
        scratch_shapes=[pltpu.VMEM((bm, d), BF16)],
        compiler_params=_cparams(("parallel",)),
        name="norm2_router",
    )(h, g, rw, rb)


def _route_kernel(lg_ref, pos_ref, gate_ref, texp_ref, ntl_ref, last_ref, nxt_ref, idx_s, rank_s,
                  *, n_tok, n_exp, tm, tb, tmax_pad):
    shift = tm.bit_length() - 1
    iota_e = lax.broadcasted_iota(I32, (n_exp, tb), 0)
    before = (lax.broadcasted_iota(I32, (tb, tb), 0)
              < lax.broadcasted_iota(I32, (tb, tb), 1)).astype(BF16)

    def pass1(i, counts):
        off = pl.multiple_of(i * tb, tb)
        l = lg_ref[pl.ds(off, tb), :].T[:n_exp, :]
        vals, hots = [], []
        for k in range(TOP_K):
            m = jnp.max(l, axis=0, keepdims=True)
            idx = jnp.min(jnp.where(l == m, iota_e, n_exp), axis=0, keepdims=True)
            hot = iota_e == idx
            l = jnp.where(hot, -jnp.inf, l)
            vals.append(m)
            hots.append(hot)
            idx_s[k:k + 1, pl.ds(off, tb)] = idx
        exps = [jnp.exp(v - vals[0]) for v in vals]
        tot = exps[0]
        for e in exps[1:]:
            tot = tot + e
        sel = hots[0].astype(F32)
        for hot in hots[1:]:
            sel = sel + hot.astype(F32)
        rank = jnp.dot(sel.astype(BF16), before, preferred_element_type=F32) + counts
        for k in range(TOP_K):
            gate_ref[k:k + 1, pl.ds(off, tb)] = exps[k] / tot
            rank_s[k:k + 1, pl.ds(off, tb)] = jnp.sum(jnp.where(hots[k], rank, 0.0), axis=0, keepdims=True)
        return counts + jnp.sum(sel, axis=1, keepdims=True)

    counts = lax.fori_loop(0, n_tok // tb, pass1, jnp.zeros((n_exp, 1), F32))
    ntile = (counts.astype(I32) + (tm - 1)) >> shift
    sub = lax.broadcasted_iota(I32, (n_exp, LANES), 0)
    lane = lax.broadcasted_iota(I32, (n_exp, LANES), 1)
    ntile_row = jnp.sum(jnp.where(sub == lane, ntile, 0), axis=0, keepdims=True)
    start = jnp.sum(jnp.where(lane < sub, ntile_row, 0), axis=1, keepdims=True)
    t_lane = lax.broadcasted_iota(I32, (n_exp, tmax_pad), 1)
    texp_ref[...] = jnp.sum((start <= t_lane).astype(I32), axis=0, keepdims=True) - 1
    ntl_ref[...] = jnp.sum(ntile, axis=0, keepdims=True) + jnp.zeros((1, LANES), I32)
    last = jnp.where(ntile > 0, (start + ntile - 1) << shift, -1)
    last_ref[...] = jnp.sum(jnp.where(sub == lane, last, 0), axis=0, keepdims=True)
    nxt = jnp.min(jnp.where((lane > sub) & (ntile_row > 0), lane, n_exp), axis=1, keepdims=True)
    nxt_ref[...] = jnp.sum(jnp.where(sub == lane, nxt, 0), axis=0, keepdims=True)
    start_rows = start << shift

    def pass2(i, carry):
        off = pl.multiple_of(i * tb, tb)
        for k in range(TOP_K):
            hot = iota_e == idx_s[k:k + 1, pl.ds(off, tb)]
            base = jnp.sum(jnp.where(hot, start_rows, 0), axis=0, keepdims=True)
            pos_ref[k:k + 1, pl.ds(off, tb)] = rank_s[k:k + 1, pl.ds(off, tb)].astype(I32) + base
        return carry
    lax.fori_loop(0, n_tok // tb, pass2, 0)


def _route(logits, *, n_exp, tm, tmax):
    n_tok = logits.shape[0]
    tb = min(256, n_tok)
    tmax_pad = -(-tmax // LANES) * LANES
    return pl.pallas_call(
        functools.partial(_route_kernel, n_tok=n_tok, n_exp=n_exp, tm=tm, tb=tb, tmax_pad=tmax_pad),
        out_shape=[jax.ShapeDtypeStruct((TOP_K, n_tok), I32),
                   jax.ShapeDtypeStruct((TOP_K, n_tok), F32),
                   jax.ShapeDtypeStruct((1, tmax_pad), I32),
                   jax.ShapeDtypeStruct((1, LANES), I32),
                   jax.ShapeDtypeStruct((1, LANES), I32),
                   jax.ShapeDtypeStruct((1, LANES), I32)],
        scratch_shapes=[pltpu.VMEM((TOP_K, n_tok), I32), pltpu.VMEM((TOP_K, n_tok), F32)],
        compiler_params=_cparams(None),
        name="route",
    )(logits)


def _dispatch_kernel(last_ref, pos_ref, z_ref, xs_hbm, zero_v, zsem, sem, *, n_exp, tm, chunk):
    @pl.when(pl.program_id(0) == 0)
    def _():
        zero_v[...] = jnp.zeros_like(zero_v)
        for e in range(n_exp):
            r = last_ref[e]

            @pl.when(r >= 0)
            def _():
                cp = pltpu.make_async_copy(zero_v, xs_hbm.at[pl.ds(pl.multiple_of(r, tm), tm)], zsem)
                cp.start()
                cp.wait()

    def issue(i, carry):
        for k in range(TOP_K):
            pltpu.make_async_copy(z_ref.at[pl.ds(i, 1)], xs_hbm.at[pl.ds(pos_ref[0, k, i], 1)], sem).start()
        return carry
    lax.fori_loop(0, chunk, issue, 0, unroll=4)
    for k in range(TOP_K):
        pltpu.make_async_copy(z_ref, xs_hbm.at[pl.ds(0, chunk)], sem).wait()


def _dispatch(last, pos3, z2p, *, n_rows, tm):
    n_tok, dh = z2p.shape
    nchunk, _, chunk = pos3.shape
    n_exp = last.shape[0]
    grid_spec = pltpu.PrefetchScalarGridSpec(
        num_scalar_prefetch=1,
        grid=(nchunk,),
        in_specs=[pl.BlockSpec((1, TOP_K, chunk), lambda c, last: (c, 0, 0), memory_space=pltpu.SMEM),
                  pl.BlockSpec((chunk, dh), lambda c, last: (c, 0))],
        out_specs=pl.BlockSpec(memory_space=pl.ANY),
        scratch_shapes=[pltpu.VMEM((tm, dh), U32), pltpu.SemaphoreType.DMA(()), pltpu.SemaphoreType.DMA(())],
    )
    return pl.pallas_call(
        functools.partial(_dispatch_kernel, n_exp=n_exp, tm=tm, chunk=chunk),
        grid_spec=grid_spec,
        out_shape=jax.ShapeDtypeStruct((n_rows, dh), U32),
        compiler_params=_cparams(("arbitrary",)),
        name="dispatch",
    )(last, pos3, z2p)


CAST_ROWS = 256


def _ffn_kernel(texp_ref, ntl_ref, nxt_ref, x_ref, wg_hbm, wu_hbm, wd_hbm, bgu_ref, bd_ref, o_ref,
                stage_g, stage_u, stage_d, wgu, wd, sems, *, f, n_exp, rows):
    t = pl.program_id(0)
    e = texp_ref[t]
    dh = x_ref.shape[1]
    valid = t < ntl_ref[0]
    first = jnp.logical_or(t == 0, e != texp_ref[jnp.maximum(t - 1, 0)])

    def weight_copies(ex):
        return (pltpu.make_async_copy(wg_hbm.at[ex], stage_g, sems.at[0]),
                pltpu.make_async_copy(wu_hbm.at[ex], stage_u, sems.at[1]),
                pltpu.make_async_copy(wd_hbm.at[ex], stage_d, sems.at[2]))

    @pl.when(t == 0)
    def _():
        for cp in weight_copies(e):
            cp.start()

    @pl.when(jnp.logical_and(valid, first))
    def _():
        for cp in weight_copies(e):
            cp.wait()

        def cast_up(i, carry):
            r = pl.multiple_of(i * rows, rows)
            wgu[pl.ds(r, rows), :f] = stage_g[pl.ds(r, rows), :].astype(BF16)
            wgu[pl.ds(r, rows), f:] = stage_u[pl.ds(r, rows), :].astype(BF16)
            return carry
        lax.fori_loop(0, stage_g.shape[0] // rows, cast_up, 0)

        def cast_down(i, carry):
            r = pl.multiple_of(i * rows, rows)
            wd[pl.ds(r, rows), :] = stage_d[pl.ds(r, rows), :].astype(BF16)
            return carry
        lax.fori_loop(0, stage_d.shape[0] // rows, cast_down, 0)

        nx = nxt_ref[e]

        @pl.when(nx < n_exp)
        def _():
            for cp in weight_copies(nx):
                cp.start()

    @pl.when(valid)
    def _():
        lo, hi = _unpack_halves(x_ref[...])
        hgu = jnp.dot(lo.astype(BF16), wgu[:dh, :], preferred_element_type=F32)
        hgu = hgu + jnp.dot(hi.astype(BF16), wgu[dh:, :], preferred_element_type=F32)
        hgu = hgu + bgu_ref[0]
        hg = jnp.minimum(hgu[:, :f], SWIGLU_LIMIT)
        hu = jnp.clip(hgu[:, f:], -SWIGLU_LIMIT, SWIGLU_LIMIT)
        act = hg * (1.0 / (1.0 + jnp.exp(-SWIGLU_ALPHA * hg))) * (hu + 1.0)
        y = jnp.dot(act.astype(BF16), wd[...], preferred_element_type=F32) + bd_ref[0]
        yr = y.astype(BF16).astype(F32)
        o_ref[...] = _pack_halves(yr[:, :dh], yr[:, dh:])


def _expert_ffn(texp, ntl, nxt, xs, w_gate, w_up, w_down, bgu, bd, *, tm):
    n_rows, dh = xs.shape
    n_exp, d, f = w_gate.shape
    tmax = n_rows // tm
    rows = min(CAST_ROWS, f)
    assert d % rows == 0 and f % rows == 0

    def x_map(t, te, nt, nx):
        return (jnp.minimum(t, nt[0] - 1), 0)

    def b_map(t, te, nt, nx):
        return (te[t], 0, 0)

    grid_spec = pltpu.PrefetchScalarGridSpec(
        num_scalar_prefetch=3,
        grid=(tmax,),
        in_specs=[pl.BlockSpec((tm, dh), x_map),
                  pl.BlockSpec(memory_space=pl.ANY),
                  pl.BlockSpec(memory_space=pl.ANY),
                  pl.BlockSpec(memory_space=pl.ANY),
                  pl.BlockSpec((1, 1, 2 * f), b_map),
                  pl.BlockSpec((1, 1, d), b_map)],
        out_specs=pl.BlockSpec((tm, dh), lambda t, te, nt, nx: (t, 0)),
        scratch_shapes=[pltpu.VMEM((d, f), F32), pltpu.VMEM((d, f), F32), pltpu.VMEM((f, d), F32),
                        pltpu.VMEM((d, 2 * f), BF16), pltpu.VMEM((f, d), BF16),
                        pltpu.SemaphoreType.DMA((3,))],
    )
    return pl.pallas_call(
        functools.partial(_ffn_kernel, f=f, n_exp=n_exp, rows=rows),
        grid_spec=grid_spec,
        out_shape=jax.ShapeDtypeStruct((n_rows, dh), U32),
        compiler_params=_cparams(("arbitrary",)),
        name="expert_ffn",
    )(texp, ntl, nxt, xs, w_gate, w_up, w_down, bgu, bd)


def _combine_kernel(pos_c, pos_n, ys_hbm, h_ref, g_ref, fg_ref, o_ref, buf, sem, *, bt):
    i = pl.program_id(0)
    n = pl.num_programs(0)
    slot = i % 2
    dh = buf.shape[-1]

    def row_copy(src, s, k, r):
        return pltpu.make_async_copy(ys_hbm.at[pl.ds(src, 1)], buf.at[s, k, pl.ds(r, 1)], sem.at[s])

    def issue(pos_ref, s):
        def body(r, carry):
            for k in range(TOP_K):
                row_copy(pos_ref[0, k, r], s, k, r).start()
            return carry
        lax.fori_loop(0, bt, body, 0, unroll=4)

    @pl.when(i == 0)
    def _():
        issue(pos_c, 0)

    @pl.when(i + 1 < n)
    def _():
        issue(pos_n, 1 - slot)

    for k in range(TOP_K):
        pltpu.make_async_copy(ys_hbm.at[pl.ds(0, bt)], buf.at[slot, k], sem.at[slot]).wait()

    acc_lo = h_ref[:, :dh]
    acc_hi = h_ref[:, dh:]
    for k in range(TOP_K):
        lo, hi = _unpack_halves(buf[slot, k])
        gk = g_ref[:, k:k + 1]
        acc_lo = acc_lo + gk * lo
        acc_hi = acc_hi + gk * hi
    ms = (jnp.sum(acc_lo * acc_lo, axis=-1, keepdims=True)
          + jnp.sum(acc_hi * acc_hi, axis=-1, keepdims=True)) / (2 * dh)
    r = lax.rsqrt(ms + RMS_EPS)
    o_ref[:, :dh] = acc_lo * r * fg_ref[:, :dh]
    o_ref[:, dh:] = acc_hi * r * fg_ref[:, dh:]


def _combine(pos3, ys, h, gates_t, fg, *, bt):
    n, d = h.shape
    nblk = n // bt
    return pl.pallas_call(
        functools.partial(_combine_kernel, bt=bt),
        grid=(nblk,),
        in_specs=[pl.BlockSpec((1, TOP_K, bt), lambda i: (i, 0, 0), memory_space=pltpu.SMEM),
                  pl.BlockSpec((1, TOP_K, bt), lambda i: (jnp.minimum(i + 1, nblk - 1), 0, 0),
                               memory_space=pltpu.SMEM),
                  pl.BlockSpec(memory_space=pl.ANY),
                  pl.BlockSpec((bt, d), lambda i: (i, 0)),
                  pl.BlockSpec((bt, TOP_K), lambda i: (i, 0)),
                  pl.BlockSpec((1, d), lambda i: (0, 0))],
        out_specs=pl.BlockSpec((bt, d), lambda i: (i, 0)),
        out_shape=jax.ShapeDtypeStruct((n, d), F32),
        scratch_shapes=[pltpu.VMEM((2, TOP_K, bt, d // 2), U32), pltpu.SemaphoreType.DMA((2,))],
        compiler_params=_cparams(("arbitrary",)),
        name="combine_final_norm",
    )(pos3, pos3, ys, h, gates_t, fg)


def _layer(x2, b, seq, norm1_g, w_in, pool_w, pool_scale, fourier_w, w_out, norm2_g,
           router_w, router_b, w_gate, b_gate, w_up, b_up, w_down, b_down):
    n, d = x2.shape
    mix = w_in.shape[1]
    pw = pool_scale.shape[0]
    n_exp = router_w.shape[1]
    f = w_gate.shape[2]
    tm = min(256, n)
    tmax = (n * TOP_K) // tm + n_exp

    up, uf = _norm_matmul(x2, norm1_g.reshape(1, d), w_in.astype(BF16), pw=pw, c=fourier_w.shape[1],
                          bm=512, bn=1024)
    a = _pool_mixer(up.reshape(b, seq, pw), pool_w.astype(BF16), pool_scale.reshape(1, pw), ts=256)
    yf = _fourier_mixer(uf, _fourier_prep(fourier_w), b=b, rows=1024)
    h = _out_proj(a.reshape(n, pw), yf.reshape(n, mix - pw), w_out.astype(BF16), x2, bm=1024, bn=512)

    rw = jnp.zeros((d, LANES), BF16).at[:, :n_exp].set(router_w.astype(BF16))
    rb = jnp.zeros((1, LANES), F32).at[0, :n_exp].set(router_b)
    z2p, logits = _norm_router(h, norm2_g.reshape(1, d), rw, rb, bm=256)

    pos, gates, texp, ntl, last, nxt = _route(logits, n_exp=n_exp, tm=tm, tmax=tmax)
    chunk = min(512, n)
    pos_c = pos.reshape(TOP_K, n // chunk, chunk).transpose(1, 0, 2)
    xs = _dispatch(last[0, :n_exp], pos_c, z2p, n_rows=tmax * tm, tm=tm)

    bgu = jnp.concatenate([b_gate, b_up], axis=-1).reshape(n_exp, 1, 2 * f)
    ys = _expert_ffn(texp[0, :tmax], ntl[0, :1], nxt[0, :n_exp], xs, w_gate, w_up, w_down, bgu,
                     b_down.reshape(n_exp, 1, d), tm=tm)

    bt = min(128, n)
    pos_b = pos.reshape(TOP_K, n // bt, bt).transpose(1, 0, 2)
    return pos_b, ys, h, gates.T


def kernel(x, norm1_g, w_in, pool_w, pool_scale, fourier_w, w_out, norm2_g, router_w, router_b,
           w_gate, b_gate, w_up, b_up, w_down, b_down, final_g):
    b, seq, d = x.shape
    assert w_in.shape[0] == 1, "only a single layer is supported"
    pos_b, ys, h, gates_t = _layer(
        x.reshape(b * seq, d), b, seq, norm1_g[0], w_in[0], pool_w[0], pool_scale[0], fourier_w[0], w_out[0],
        norm2_g[0], router_w[0], router_b[0], w_gate[0], b_gate[0], w_up[0], b_up[0], w_down[0], b_down[0])
    out = _combine(pos_b, ys, h, gates_t, final_g.reshape(1, d), bt=min(128, b * seq))
    return out.reshape(b, seq, d)
```

```python
import functools
import math

import jax
import jax.numpy as jnp
from jax import lax
from jax.experimental import pallas as pl
from jax.experimental.pallas import tpu as pltpu

F32 = jnp.float32
BF16 = jnp.bfloat16
I32 = jnp.int32
U32 = jnp.uint32

RMS_EPS = 1e-5
POOL_WINDOWS = (2, 4, 8, 16)
TOP_K = 4
SWIGLU_LIMIT = 7.0
SWIGLU_ALPHA = 1.702

LANES = 128
BF16_SUBLANES = 16
HALO = 16
NORM_ROWS = 16
NORM_UNROLL = 8
SUBLANES = 8
VMEM_LIMIT = 56 * 1024 * 1024
HI_MASK = 0xFFFF0000


def _cparams(sem, vmem=VMEM_LIMIT):
    return pltpu.CompilerParams(dimension_semantics=sem, vmem_limit_bytes=vmem)


def _pack_halves(lo_f32, hi_f32):
    lo = lax.bitcast_convert_type(lo_f32, U32) >> 16
    hi = lax.bitcast_convert_type(hi_f32, U32) & jnp.uint32(HI_MASK)
    return lo | hi


def _unpack_halves(w):
    lo = lax.bitcast_convert_type(w << 16, F32)
    hi = lax.bitcast_convert_type(w & jnp.uint32(HI_MASK), F32)
    return lo, hi


def _inv_rms_rows(x_ref, r_ref):
    d = x_ref.shape[1]

    def body(i, carry):
        r = pl.multiple_of(i * SUBLANES, SUBLANES)
        xc = x_ref[pl.ds(r, SUBLANES), :]
        sq = xc * xc
        acc = sq[:, :LANES]
        for c0 in range(LANES, d, LANES):
            acc = acc + sq[:, c0:c0 + LANES]
        r_ref[pl.ds(r, SUBLANES), :] = acc
        return carry
    steps = x_ref.shape[0] // SUBLANES
    lax.fori_loop(0, steps, body, 0, unroll=min(NORM_UNROLL, steps))
    ms = jnp.sum(r_ref[...], axis=-1, keepdims=True) / d
    r_ref[...] = jnp.broadcast_to(lax.rsqrt(ms + RMS_EPS), r_ref.shape)


def _norm_mm_kernel(x_ref, g_ref, w_ref, up_ref, uf_ref, z_ref, r_ref, *, rows, pool_blocks, c):
    j = pl.program_id(1)

    @pl.when(j == 0)
    def _():
        _inv_rms_rows(x_ref, r_ref)

        def body(i, carry):
            r = pl.multiple_of(i * rows, rows)
            inv = r_ref[pl.ds(r, rows), :]
            for c0 in range(0, x_ref.shape[1], LANES):
                z = x_ref[pl.ds(r, rows), c0:c0 + LANES] * inv * g_ref[:, c0:c0 + LANES]
                z_ref[pl.ds(r, rows), c0:c0 + LANES] = z.astype(BF16)
            return carry
        lax.fori_loop(0, x_ref.shape[0] // rows, body, 0, unroll=2)

    y = jnp.dot(z_ref[...], w_ref[...], preferred_element_type=F32)

    @pl.when(j < pool_blocks)
    def _():
        up_ref[...] = y.astype(up_ref.dtype)

    @pl.when(j >= pool_blocks)
    def _():
        yr = y.astype(BF16).astype(F32)
        ch = c // 2
        for hd in range(y.shape[1] // c):
            words = _pack_halves(yr[:, hd * c:hd * c + ch], yr[:, hd * c + ch:(hd + 1) * c])
            for s in range(ch // LANES):
                uf_ref[hd * (ch // LANES) + s] = words[:, s * LANES:(s + 1) * LANES]


def _norm_matmul(x, g, w, *, pw, c, bm, bn):
    n, d = x.shape
    m = w.shape[1]
    bm, bn = min(bm, n), min(bn, pw, m - pw)
    assert pw % bn == 0 and (m - pw) % bn == 0 and bn % c == 0 and (c // 2) % LANES == 0
    pool_blocks = pw // bn
    slabs_blk = bn // 2 // LANES
    return pl.pallas_call(
        functools.partial(_norm_mm_kernel, rows=min(NORM_ROWS, bm), pool_blocks=pool_blocks, c=c),
        grid=(n // bm, m // bn),
        in_specs=[pl.BlockSpec((bm, d), lambda i, j: (i, 0)),
                  pl.BlockSpec((1, d), lambda i, j: (0, 0)),
                  pl.BlockSpec((d, bn), lambda i, j: (0, j))],
        out_specs=[pl.BlockSpec((bm, bn), lambda i, j: (i, jnp.minimum(j, pool_blocks - 1))),
                   pl.BlockSpec((slabs_blk, bm, LANES), lambda i, j: (jnp.maximum(j - pool_blocks, 0), i, 0))],
        out_shape=[jax.ShapeDtypeStruct((n, pw), BF16),
                   jax.ShapeDtypeStruct(((m - pw) // 2 // LANES, n, LANES), U32)],
        scratch_shapes=[pltpu.VMEM((bm, d), BF16), pltpu.VMEM((bm, LANES), F32)],
        compiler_params=_cparams(("parallel", "arbitrary")),
        name="norm1_w_in",
    )(x, g, w)


def _pool_kernel(cur_ref, prev_ref, next_ref, pw_ref, sc_ref, o_ref, *, seq, ts, c):
    t = pl.program_id(1)
    has_prev = (t > 0).astype(F32)
    has_next = (t < pl.num_programs(1) - 1).astype(F32)
    n_ext = ts + 2 * HALO
    tok = t * ts + lax.broadcasted_iota(I32, (ts, 1), 0)
    for g, w in enumerate(POOL_WINDOWS):
        sl = slice(g * c, (g + 1) * c)
        cur = cur_ref[0, :, sl].astype(F32)
        prev = prev_ref[0, :, sl].astype(F32) * has_prev
        nxt = next_ref[0, :, sl].astype(F32) * has_next
        ext = jnp.concatenate([prev, cur, nxt], axis=0)
        s = ext + pltpu.roll(ext, 1, 0)
        h = 1
        while 2 * h < w:
            s = pltpu.roll(s, h, 0) + pltpu.roll(s, n_ext - h, 0)
            h *= 2
        win = s[HALO:HALO + ts]
        lo = jnp.maximum(tok - w // 2, 0)
        hi = jnp.minimum(tok + w // 2 - 1, seq - 1)
        cnt = (hi - lo + 1).astype(F32)
        p = win / cnt - cur
        y = jnp.dot(p.astype(BF16), pw_ref[g], preferred_element_type=F32)
        o_ref[0, :, sl] = (y * sc_ref[:, sl]).astype(o_ref.dtype)


def _pool_mixer(u3, pool_w, pool_scale, *, ts):
    b, seq, _ = u3.shape
    g, c, _ = pool_w.shape
    pw = g * c
    ts = min(ts, seq)
    nh = seq // HALO
    per = ts // HALO
    return pl.pallas_call(
        functools.partial(_pool_kernel, seq=seq, ts=ts, c=c),
        grid=(b, seq // ts),
        in_specs=[pl.BlockSpec((1, ts, pw), lambda i, t: (i, t, 0)),
                  pl.BlockSpec((1, HALO, pw), lambda i, t: (i, jnp.maximum(t * per - 1, 0), 0)),
                  pl.BlockSpec((1, HALO, pw), lambda i, t: (i, jnp.minimum((t + 1) * per, nh - 1), 0)),
                  pl.BlockSpec((g, c, c), lambda i, t: (0, 0, 0)),
                  pl.BlockSpec((1, pw), lambda i, t: (0, 0))],
        out_specs=pl.BlockSpec((1, ts, pw), lambda i, t: (i, t, 0)),
        out_shape=jax.ShapeDtypeStruct((b, seq, pw), BF16),
        compiler_params=_cparams(("parallel", "parallel")),
        name="pool_mixer",
    )(u3, u3, u3, pool_w, pool_scale)


DFT_N1 = 256


def _dft_mats(n, scale, dtype):
    j = jnp.arange(n, dtype=I32)
    ang = ((j[:, None] * j[None, :]) % n).astype(F32) * (2.0 * math.pi / n)
    return (jnp.cos(ang) * scale).astype(dtype), (jnp.sin(ang) * scale).astype(dtype)


def _fourier_prep_kernel(cc_ref, sc_ref, w_ref, o_ref, *, c):
    w = w_ref[0]
    o_ref[0, :c, :] = jnp.dot(cc_ref[...], w, preferred_element_type=F32,
                              precision=lax.Precision.HIGHEST).astype(o_ref.dtype)
    o_ref[0, c:, :] = jnp.dot(sc_ref[...], w, preferred_element_type=F32,
                              precision=lax.Precision.HIGHEST).astype(o_ref.dtype)


def _fourier_prep(fourier_w):
    h, c, _ = fourier_w.shape
    cc, sc = _dft_mats(c, c ** -0.5, F32)
    return pl.pallas_call(
        functools.partial(_fourier_prep_kernel, c=c),
        grid=(h,),
        in_specs=[pl.BlockSpec((c, c), lambda i: (0, 0)),
                  pl.BlockSpec((c, c), lambda i: (0, 0)),
                  pl.BlockSpec((1, c, c), lambda i: (i, 0, 0))],
        out_specs=pl.BlockSpec((1, 2 * c, c), lambda i: (i, 0, 0)),
        out_shape=jax.ShapeDtypeStruct((h, 2 * c, c), BF16),
        compiler_params=_cparams(("parallel",)),
        name="fourier_prep",
    )(cc, sc, fourier_w)


def _fft(xs):
    n = len(xs)
    if n == 1:
        return xs
    ev, od = _fft(xs[0::2]), _fft(xs[1::2])
    out = [None] * n
    for k in range(n // 2):
        re, im = od[k]
        if k == 0:
            tr, ti = re, im
        elif 4 * k == n:
            tr, ti = im, -re
        else:
            wr, wi = math.cos(2.0 * math.pi * k / n), -math.sin(2.0 * math.pi * k / n)
            tr, ti = re * wr - im * wi, re * wi + im * wr
        er, ei = ev[k]
        out[k] = (er + tr, ei + ti)
        out[k + n // 2] = (er - tr, ei - ti)
    return out


def _fourier_kernel(x_ref, cs_ref, tw_ref, ab_ref, o_ref, y_ref, *, n1, n2, c, rows):
    for j in range(n2):
        halves = [_unpack_halves(x_ref[s, pl.ds(j, n1, stride=n2), :]) for s in range(x_ref.shape[0])]
        xj = jnp.concatenate([lo for lo, _ in halves] + [hi for _, hi in halves], axis=1).astype(BF16)
        y = jnp.dot(cs_ref[...], xj, preferred_element_type=F32)
        y_ref[0, j] = y[:n1]
        y_ref[1, j] = y[n1:]

    nl = c // LANES

    def tile(i, carry):
        r = pl.multiple_of((i // nl) * SUBLANES, SUBLANES)
        l = pl.multiple_of((i % nl) * LANES, LANES)
        zs = []
        for j in range(n2):
            yc = y_ref[0, j, pl.ds(r, SUBLANES), pl.ds(l, LANES)]
            ys = y_ref[1, j, pl.ds(r, SUBLANES), pl.ds(l, LANES)]
            ct = tw_ref[0, j, pl.ds(r, SUBLANES), :]
            st = tw_ref[1, j, pl.ds(r, SUBLANES), :]
            zs.append((yc * ct - ys * st, -(ys * ct + yc * st)))
        gs = _fft(zs)
        for k in range(n2):
            y_ref[0, k, pl.ds(r, SUBLANES), pl.ds(l, LANES)] = gs[k][0]
            y_ref[1, k, pl.ds(r, SUBLANES), pl.ds(l, LANES)] = gs[k][1]
        return carry
    lax.fori_loop(0, (n1 // SUBLANES) * nl, tile, 0)

    per = max(1, min(n2, rows // n1))
    for s in range(0, n2, per):
        gr = y_ref[0, s:s + per].reshape(per * n1, c).astype(BF16)
        gi = y_ref[1, s:s + per].reshape(per * n1, c).astype(BF16)
        y = jnp.dot(gr, ab_ref[0, :c, :], preferred_element_type=F32)
        y = y + jnp.dot(gi, ab_ref[0, c:, :], preferred_element_type=F32)
        o_ref[0, s * n1:(s + per) * n1, :] = y.astype(o_ref.dtype)


def _fourier_mixer(uf, ab, *, b, rows):
    slabs, n, _ = uf.shape
    seq = n // b
    h, _, c = ab.shape
    spb = slabs // h
    n1 = min(DFT_N1, seq)
    n2 = seq // n1
    assert n1 * n2 == seq and n2 & (n2 - 1) == 0, "sequence length must be N1 * 2^m"
    cmat, smat = _dft_mats(n1, 1.0, BF16)
    cs = jnp.concatenate([cmat, smat], axis=0)
    ang = (jnp.arange(n2, dtype=I32)[:, None] * jnp.arange(n1, dtype=I32)[None, :]).astype(F32)
    ang = ang * (2.0 * math.pi / seq)
    tw = jnp.stack([jnp.cos(ang), jnp.sin(ang)]) * (seq ** -0.5)
    tw = jnp.broadcast_to(tw[..., None], (2, n2, n1, LANES))

    return pl.pallas_call(
        functools.partial(_fourier_kernel, n1=n1, n2=n2, c=c, rows=rows),
        grid=(b, h),
        in_specs=[
            pl.BlockSpec((spb, None, seq, LANES), lambda i, k: (k, i, 0, 0)),
            pl.BlockSpec((2 * n1, n1), lambda i, k: (0, 0), pipeline_mode=pl.Buffered(1)),
            pl.BlockSpec((2, n2, n1, LANES), lambda i, k: (0, 0, 0, 0), pipeline_mode=pl.Buffered(1)),
            pl.BlockSpec((1, 2 * c, c), lambda i, k: (k, 0, 0))],
        out_specs=pl.BlockSpec((1, seq, c), lambda i, k: (i, 0, k)),
        out_shape=jax.ShapeDtypeStruct((b, seq, h * c), BF16),
        scratch_shapes=[pltpu.VMEM((2, n2, n1, c), F32)],
        compiler_params=_cparams(("parallel", "parallel")),
        name="fourier_mixer",
    )(uf.reshape(slabs, b, seq, LANES), cs, tw, ab)


def _out_proj_kernel(a_ref, f_ref, w_ref, x_ref, o_ref, *, ka):
    y = jnp.dot(a_ref[...], w_ref[:ka, :], preferred_element_type=F32)
    y = y + jnp.dot(f_ref[...], w_ref[ka:, :], preferred_element_type=F32)
    o_ref[...] = x_ref[...] + y


def _out_proj(a, f, w, x, *, bm, bn):
    n, ka = a.shape
    kf = f.shape[1]
    d = w.shape[1]
    bm, bn = min(bm, n), min(bn, d)
    return pl.pallas_call(
        functools.partial(_out_proj_kernel, ka=ka),
        grid=(n // bm, d // bn),
        in_specs=[pl.BlockSpec((bm, ka), lambda i, j: (i, 0)),
                  pl.BlockSpec((bm, kf), lambda i, j: (i, 0)),
                  pl.BlockSpec((ka + kf, bn), lambda i, j: (0, j)),
                  pl.BlockSpec((bm, bn), lambda i, j: (i, j))],
        out_specs=pl.BlockSpec((bm, bn), lambda i, j: (i, j)),
        out_shape=jax.ShapeDtypeStruct((n, d), F32),
        compiler_params=_cparams(("parallel", "parallel")),
        name="w_out_residual",
    )(a, f, w, x)


def _norm_router_kernel(h_ref, g_ref, rw_ref, rb_ref, zp_ref, lg_ref, z_ref, r_ref, *, rows):
    dh = h_ref.shape[1] // 2
    _inv_rms_rows(h_ref, r_ref)

    def body(i, carry):
        r = pl.multiple_of(i * rows, rows)
        inv = r_ref[pl.ds(r, rows), :]
        for c0 in range(0, dh, LANES):
            lo = (h_ref[pl.ds(r, rows), c0:c0 + LANES] * inv * g_ref[:, c0:c0 + LANES]).astype(BF16)
            hi = (h_ref[pl.ds(r, rows), dh + c0:dh + c0 + LANES] * inv
                  * g_ref[:, dh + c0:dh + c0 + LANES]).astype(BF16)
            z_ref[pl.ds(r, rows), c0:c0 + LANES] = lo
            z_ref[pl.ds(r, rows), dh + c0:dh + c0 + LANES] = hi
            zp_ref[pl.ds(r, rows), c0:c0 + LANES] = _pack_halves(lo.astype(F32), hi.astype(F32))
        return carry
    lax.fori_loop(0, h_ref.shape[0] // rows, body, 0, unroll=2)
    lg_ref[...] = jnp.dot(z_ref[...], rw_ref[...], preferred_element_type=F32) + rb_ref[...]


def _norm_router(h, g, rw, rb, *, bm):
    n, d = h.shape
    bm = min(bm, n)
    return pl.pallas_call(
        functools.partial(_norm_router_kernel, rows=min(NORM_ROWS, bm)),
        grid=(n // bm,),
        in_specs=[pl.BlockSpec((bm, d), lambda i: (i, 0)),
                  pl.BlockSpec((1, d), lambda i: (0, 0)),
                  pl.BlockSpec((d, LANES), lambda i: (0, 0)),
                  pl.BlockSpec((1, LANES), lambda i: (0, 0))],
        out_specs=[pl.BlockSpec((bm, d // 2), lambda i: (i, 0)),
                   pl.BlockSpec((bm, LANES), lambda i: (i, 0))],
        out_shape=[jax.ShapeDtypeStruct((n, d // 2), U32),
                   jax.ShapeDtypeStruct((n, LANES), F32)],
        scratch_shapes=[pltpu.VMEM((bm, d), BF16), pltpu.VMEM((bm, LANES), F32)],
        compiler_params=_cparams(("parallel",)),
        name="norm2_router",
    )(h, g, rw, rb)


def _route_kernel(lg_ref, pos_ref, gate_ref, texp_ref, ntl_ref, last_ref, nxt_ref, idx_s, rank_s,
                  *, n_tok, n_exp, tm, tb, tmax_pad):
    shift = tm.bit_length() - 1
    iota_e = lax.broadcasted_iota(I32, (n_exp, tb), 0)
    before = (lax.broadcasted_iota(I32, (tb, tb), 0)
              < lax.broadcasted_iota(I32, (tb, tb), 1)).astype(BF16)

    def pass1(i, counts):
        off = pl.multiple_of(i * tb, tb)
        l = lg_ref[pl.ds(off, tb), :].T[:n_exp, :]
        vals, hots = [], []
        for k in range(TOP_K):
            m = jnp.max(l, axis=0, keepdims=True)
            idx = jnp.min(jnp.where(l == m, iota_e, n_exp), axis=0, keepdims=True)
            hot = iota_e == idx
            l = jnp.where(hot, -jnp.inf, l)
            vals.append(m)
            hots.append(hot)
            idx_s[k:k + 1, pl.ds(off, tb)] = idx
        exps = [jnp.exp(v - vals[0]) for v in vals]
        tot = exps[0]
        for e in exps[1:]:
            tot = tot + e
        sel = hots[0].astype(F32)
        for hot in hots[1:]:
            sel = sel + hot.astype(F32)
        rank = jnp.dot(sel.astype(BF16), before, preferred_element_type=F32) + counts
        for k in range(TOP_K):
            gate_ref[k:k + 1, pl.ds(off, tb)] = exps[k] / tot
            rank_s[k:k + 1, pl.ds(off, tb)] = jnp.sum(jnp.where(hots[k], rank, 0.0), axis=0, keepdims=True)
        return counts + jnp.sum(sel, axis=1, keepdims=True)

    counts = lax.fori_loop(0, n_tok // tb, pass1, jnp.zeros((n_exp, 1), F32))
    ntile = (counts.astype(I32) + (tm - 1)) >> shift
    sub = lax.broadcasted_iota(I32, (n_exp, LANES), 0)
    lane = lax.broadcasted_iota(I32, (n_exp, LANES), 1)
    ntile_row = jnp.sum(jnp.where(sub == lane, ntile, 0), axis=0, keepdims=True)
    start = jnp.sum(jnp.where(lane < sub, ntile_row, 0), axis=1, keepdims=True)
    t_lane = lax.broadcasted_iota(I32, (n_exp, tmax_pad), 1)
    texp_ref[...] = jnp.sum((start <= t_lane).astype(I32), axis=0, keepdims=True) - 1
    ntl_ref[...] = jnp.sum(ntile, axis=0, keepdims=True) + jnp.zeros((1, LANES), I32)
    last = jnp.where(ntile > 0, (start + ntile - 1) << shift, -1)
    last_ref[...] = jnp.sum(jnp.where(sub == lane, last, 0), axis=0, keepdims=True)
    nxt = jnp.min(jnp.where((lane > sub) & (ntile_row > 0), lane, n_exp), axis=1, keepdims=True)
    nxt_ref[...] = jnp.sum(jnp.where(sub == lane, nxt, 0), axis=0, keepdims=True)
    start_rows = start << shift

    def pass2(i, carry):
        off = pl.multiple_of(i * tb, tb)
        for k in range(TOP_K):
            hot = iota_e == idx_s[k:k + 1, pl.ds(off, tb)]
            base = jnp.sum(jnp.where(hot, start_rows, 0), axis=0, keepdims=True)
            pos_ref[k:k + 1, pl.ds(off, tb)] = rank_s[k:k + 1, pl.ds(off, tb)].astype(I32) + base
        return carry
    lax.fori_loop(0, n_tok // tb, pass2, 0)


def _route(logits, *, n_exp, tm, tmax):
    n_tok = logits.shape[0]
    tb = min(256, n_tok)
    tmax_pad = -(-tmax // LANES) * LANES
    return pl.pallas_call(
        functools.partial(_route_kernel, n_tok=n_tok, n_exp=n_exp, tm=tm, tb=tb, tmax_pad=tmax_pad),
        out_shape=[jax.ShapeDtypeStruct((TOP_K, n_tok), I32),
                   jax.ShapeDtypeStruct((TOP_K, n_tok), F32),
                   jax.ShapeDtypeStruct((1, tmax_pad), I32),
                   jax.ShapeDtypeStruct((1, LANES), I32),
                   jax.ShapeDtypeStruct((1, LANES), I32),
                   jax.ShapeDtypeStruct((1, LANES), I32)],
        scratch_shapes=[pltpu.VMEM((TOP_K, n_tok), I32), pltpu.VMEM((TOP_K, n_tok), F32)],
        compiler_params=_cparams(None),
        name="route",
    )(logits)


def _dispatch_kernel(last_ref, pos_ref, z_ref, xs_hbm, zero_v, zsem, sem, *, n_exp, tm, chunk):
    @pl.when(pl.program_id(0) == 0)
    def _():
        zero_v[...] = jnp.zeros_like(zero_v)
        for e in range(n_exp):
            r = last_ref[e]

            @pl.when(r >= 0)
            def _():
                cp = pltpu.make_async_copy(zero_v, xs_hbm.at[pl.ds(pl.multiple_of(r, tm), tm)], zsem)
                cp.start()
                cp.wait()

    for i in range(chunk):
        for k in range(TOP_K):
            pltpu.make_async_copy(z_ref.at[pl.ds(i, 1)], xs_hbm.at[pl.ds(pos_ref[0, k, i], 1)], sem).start()
    for k in range(TOP_K):
        pltpu.make_async_copy(z_ref, xs_hbm.at[pl.ds(0, chunk)], sem).wait()


def _dispatch(last, pos3, z2p, *, n_rows, tm):
    n_tok, dh = z2p.shape
    nchunk, _, chunk = pos3.shape
    n_exp = last.shape[0]
    grid_spec = pltpu.PrefetchScalarGridSpec(
        num_scalar_prefetch=1,
        grid=(nchunk,),
        in_specs=[pl.BlockSpec((1, TOP_K, chunk), lambda c, last: (c, 0, 0), memory_space=pltpu.SMEM),
                  pl.BlockSpec((chunk, dh), lambda c, last: (c, 0))],
        out_specs=pl.BlockSpec(memory_space=pl.ANY),
        scratch_shapes=[pltpu.VMEM((tm, dh), U32), pltpu.SemaphoreType.DMA(()), pltpu.SemaphoreType.DMA(())],
    )
    return pl.pallas_call(
        functools.partial(_dispatch_kernel, n_exp=n_exp, tm=tm, chunk=chunk),
        grid_spec=grid_spec,
        out_shape=jax.ShapeDtypeStruct((n_rows, dh), U32),
        compiler_params=_cparams(("arbitrary",)),
        name="dispatch",
    )(last, pos3, z2p)


CAST_ROWS = 256


def _ffn_kernel(texp_ref, ntl_ref, nxt_ref, x_ref, wg_hbm, wu_hbm, wd_hbm, bgu_ref, bd_ref, o_ref,
                stage_g, stage_u, stage_d, wgu, wd, sems, *, f, n_exp, rows):
    t = pl.program_id(0)
    e = texp_ref[t]
    dh = x_ref.shape[1]
    valid = t < ntl_ref[0]
    first = jnp.logical_or(t == 0, e != texp_ref[jnp.maximum(t - 1, 0)])

    def weight_copies(ex):
        return (pltpu.make_async_copy(wg_hbm.at[ex], stage_g, sems.at[0]),
                pltpu.make_async_copy(wu_hbm.at[ex], stage_u, sems.at[1]),
                pltpu.make_async_copy(wd_hbm.at[ex], stage_d, sems.at[2]))

    @pl.when(t == 0)
    def _():
        for cp in weight_copies(e):
            cp.start()

    @pl.when(jnp.logical_and(valid, first))
    def _():
        for cp in weight_copies(e):
            cp.wait()

        def cast_up(i, carry):
            r = pl.multiple_of(i * rows, rows)
            wgu[pl.ds(r, rows), :f] = stage_g[pl.ds(r, rows), :].astype(BF16)
            wgu[pl.ds(r, rows), f:] = stage_u[pl.ds(r, rows), :].astype(BF16)
            return carry
        lax.fori_loop(0, stage_g.shape[0] // rows, cast_up, 0)

        def cast_down(i, carry):
            r = pl.multiple_of(i * rows, rows)
            wd[pl.ds(r, rows), :] = stage_d[pl.ds(r, rows), :].astype(BF16)
            return carry
        lax.fori_loop(0, stage_d.shape[0] // rows, cast_down, 0)

        nx = nxt_ref[e]

        @pl.when(nx < n_exp)
        def _():
            for cp in weight_copies(nx):
                cp.start()

    @pl.when(valid)
    def _():
        lo, hi = _unpack_halves(x_ref[...])
        hgu = jnp.dot(lo.astype(BF16), wgu[:dh, :], preferred_element_type=F32)
        hgu = hgu + jnp.dot(hi.astype(BF16), wgu[dh:, :], preferred_element_type=F32)
        hgu = hgu + bgu_ref[0]
        hg = jnp.minimum(hgu[:, :f], SWIGLU_LIMIT)
        hu = jnp.clip(hgu[:, f:], -SWIGLU_LIMIT, SWIGLU_LIMIT)
        act = hg * (1.0 / (1.0 + jnp.exp(-SWIGLU_ALPHA * hg))) * (hu + 1.0)
        y = jnp.dot(act.astype(BF16), wd[...], preferred_element_type=F32) + bd_ref[0]
        yr = y.astype(BF16).astype(F32)
        o_ref[...] = _pack_halves(yr[:, :dh], yr[:, dh:])


def _expert_ffn(texp, ntl, nxt, xs, w_gate, w_up, w_down, bgu, bd, *, tm):
    n_rows, dh = xs.shape
    n_exp, d, f = w_gate.shape
    tmax = n_rows // tm
    rows = min(CAST_ROWS, f)
    assert d % rows == 0 and f % rows == 0

    def x_map(t, te, nt, nx):
        return (jnp.minimum(t, nt[0] - 1), 0)

    def b_map(t, te, nt, nx):
        return (te[t], 0, 0)

    grid_spec = pltpu.PrefetchScalarGridSpec(
        num_scalar_prefetch=3,
        grid=(tmax,),
        in_specs=[pl.BlockSpec((tm, dh), x_map),
                  pl.BlockSpec(memory_space=pl.ANY),
                  pl.BlockSpec(memory_space=pl.ANY),
                  pl.BlockSpec(memory_space=pl.ANY),
                  pl.BlockSpec((1, 1, 2 * f), b_map),
                  pl.BlockSpec((1, 1, d), b_map)],
        out_specs=pl.BlockSpec((tm, dh), lambda t, te, nt, nx: (t, 0)),
        scratch_shapes=[pltpu.VMEM((d, f), F32), pltpu.VMEM((d, f), F32), pltpu.VMEM((f, d), F32),
                        pltpu.VMEM((d, 2 * f), BF16), pltpu.VMEM((f, d), BF16),
                        pltpu.SemaphoreType.DMA((3,))],
    )
    return pl.pallas_call(
        functools.partial(_ffn_kernel, f=f, n_exp=n_exp, rows=rows),
        grid_spec=grid_spec,
        out_shape=jax.ShapeDtypeStruct((n_rows, dh), U32),
        compiler_params=_cparams(("arbitrary",)),
        name="expert_ffn",
    )(texp, ntl, nxt, xs, w_gate, w_up, w_down, bgu, bd)


COMBINE_ROWS = 16


def _combine_kernel(pos_c, pos_n, ys_hbm, h_ref, g_ref, fg_ref, o_ref, buf_a, buf_b, sem, *, bt, rows):
    i = pl.program_id(0)
    dh = buf_a.shape[-1]

    def start_rows(pos_ref, dst, s, r0):
        for r in range(rows):
            for k in range(TOP_K):
                pltpu.make_async_copy(ys_hbm.at[pl.ds(pos_ref[0, k, r0 + r], 1)],
                                      dst.at[k, pl.ds(r0 + r, 1)], sem.at[s]).start()

    def wait_block(dst, s):
        for k in range(TOP_K):
            pltpu.make_async_copy(ys_hbm.at[pl.ds(0, bt)], dst.at[k], sem.at[s]).wait()

    @pl.when(i == 0)
    def _():
        def first(c, carry):
            start_rows(pos_c, buf_a, 0, pl.multiple_of(c * rows, rows))
            return carry
        lax.fori_loop(0, bt // rows, first, 0)

    def step(cur, s_cur, nxt, s_nxt):
        wait_block(cur, s_cur)
        for r0 in range(0, bt, rows):
            start_rows(pos_n, nxt, s_nxt, r0)
            acc_lo = h_ref[r0:r0 + rows, :dh]
            acc_hi = h_ref[r0:r0 + rows, dh:]
            for k in range(TOP_K):
                lo, hi = _unpack_halves(cur[k, r0:r0 + rows, :])
                gk = g_ref[r0:r0 + rows, k:k + 1]
                acc_lo = acc_lo + gk * lo
                acc_hi = acc_hi + gk * hi
            ms = (jnp.sum(acc_lo * acc_lo, axis=-1, keepdims=True)
                  + jnp.sum(acc_hi * acc_hi, axis=-1, keepdims=True)) / (2 * dh)
            inv = lax.rsqrt(ms + RMS_EPS)
            o_ref[r0:r0 + rows, :dh] = acc_lo * inv * fg_ref[:, :dh]
            o_ref[r0:r0 + rows, dh:] = acc_hi * inv * fg_ref[:, dh:]

        @pl.when(i == pl.num_programs(0) - 1)
        def _():
            wait_block(nxt, s_nxt)

    @pl.when(i % 2 == 0)
    def _():
        step(buf_a, 0, buf_b, 1)

    @pl.when(i % 2 == 1)
    def _():
        step(buf_b, 1, buf_a, 0)


def _combine(pos3, ys, h, gates_t, fg, *, bt):
    n, d = h.shape
    nblk = n // bt
    return pl.pallas_call(
        functools.partial(_combine_kernel, bt=bt, rows=min(COMBINE_ROWS, bt)),
        grid=(nblk,),
        in_specs=[pl.BlockSpec((1, TOP_K, bt), lambda i: (i, 0, 0), memory_space=pltpu.SMEM),
                  pl.BlockSpec((1, TOP_K, bt), lambda i: (jnp.minimum(i + 1, nblk - 1), 0, 0),
                               memory_space=pltpu.SMEM),
                  pl.BlockSpec(memory_space=pl.ANY),
                  pl.BlockSpec((bt, d), lambda i: (i, 0)),
                  pl.BlockSpec((bt, TOP_K), lambda i: (i, 0)),
                  pl.BlockSpec((1, d), lambda i: (0, 0))],
        out_specs=pl.BlockSpec((bt, d), lambda i: (i, 0)),
        out_shape=jax.ShapeDtypeStruct((n, d), F32),
        scratch_shapes=[pltpu.VMEM((TOP_K, bt, d // 2), U32), pltpu.VMEM((TOP_K, bt, d // 2), U32),
                        pltpu.SemaphoreType.DMA((2,))],
        compiler_params=_cparams(("arbitrary",)),
        name="combine_final_norm",
    )(pos3, pos3, ys, h, gates_t, fg)


def _layer(x2, b, seq, norm1_g, w_in, pool_w, pool_scale, fourier_w, w_out, norm2_g,
           router_w, router_b, w_gate, b_gate, w_up, b_up, w_down, b_down):
    n, d = x2.shape
    mix = w_in.shape[1]
    pw = pool_scale.shape[0]
    n_exp = router_w.shape[1]
    f = w_gate.shape[2]
    tm = min(256, n)
    tmax = (n * TOP_K) // tm + n_exp

    up, uf = _norm_matmul(x2, norm1_g.reshape(1, d), w_in.astype(BF16), pw=pw, c=fourier_w.shape[1],
                          bm=512, bn=1024)
    a = _pool_mixer(up.reshape(b, seq, pw), pool_w.astype(BF16), pool_scale.reshape(1, pw), ts=256)
    yf = _fourier_mixer(uf, _fourier_prep(fourier_w), b=b, rows=1024)
    h = _out_proj(a.reshape(n, pw), yf.reshape(n, mix - pw), w_out.astype(BF16), x2, bm=1024, bn=512)

    rw = jnp.zeros((d, LANES), BF16).at[:, :n_exp].set(router_w.astype(BF16))
    rb = jnp.zeros((1, LANES), F32).at[0, :n_exp].set(router_b)
    z2p, logits = _norm_router(h, norm2_g.reshape(1, d), rw, rb, bm=256)

    pos, gates, texp, ntl, last, nxt = _route(logits, n_exp=n_exp, tm=tm, tmax=tmax)
    chunk = min(512, n)
    pos_c = pos.reshape(TOP_K, n // chunk, chunk).transpose(1, 0, 2)
    xs = _dispatch(last[0, :n_exp], pos_c, z2p, n_rows=tmax * tm, tm=tm)

    bgu = jnp.concatenate([b_gate, b_up], axis=-1).reshape(n_exp, 1, 2 * f)
    ys = _expert_ffn(texp[0, :tmax], ntl[0, :1], nxt[0, :n_exp], xs, w_gate, w_up, w_down, bgu,
                     b_down.reshape(n_exp, 1, d), tm=tm)

    bt = min(128, n)
    pos_b = pos.reshape(TOP_K, n // bt, bt).transpose(1, 0, 2)
    return pos_b, ys, h, gates.T


def kernel(x, norm1_g, w_in, pool_w, pool_scale, fourier_w, w_out, norm2_g, router_w, router_b,
           w_gate, b_gate, w_up, b_up, w_down, b_down, final_g):
    b, seq, d = x.shape
    assert w_in.shape[0] == 1, "only a single layer is supported"
    pos_b, ys, h, gates_t = _layer(
        x.reshape(b * seq, d), b, seq, norm1_g[0], w_in[0], pool_w[0], pool_scale[0], fourier_w[0], w_out[0],
        norm2_g[0], router_w[0], router_b[0], w_gate[0], b_gate[0], w_up[0], b_up[0], w_down[0], b_down[0])
    out = _combine(pos_b, ys, h, gates_t, final_g.reshape(1, d), bt=min(128, b * seq))
    return out.reshape(b, seq, d)
```

```python
import functools
import math

import jax
import jax.numpy as jnp
from jax import lax
from jax.experimental import pallas as pl
from jax.experimental.pallas import tpu as pltpu

F32 = jnp.float32
BF16 = jnp.bfloat16
I32 = jnp.int32
U32 = jnp.uint32

RMS_EPS = 1e-5
POOL_WINDOWS = (2, 4, 8, 16)
TOP_K = 4
SWIGLU_LIMIT = 7.0
SWIGLU_ALPHA = 1.702

LANES = 128
BF16_SUBLANES = 16
HALO = 16
NORM_ROWS = 16
NORM_UNROLL = 8
SUBLANES = 8
VMEM_LIMIT = 56 * 1024 * 1024
HI_MASK = 0xFFFF0000


def _cparams(sem, vmem=VMEM_LIMIT):
    return pltpu.CompilerParams(dimension_semantics=sem, vmem_limit_bytes=vmem)


def _pack_halves(lo_f32, hi_f32):
    lo = lax.bitcast_convert_type(lo_f32, U32) >> 16
    hi = lax.bitcast_convert_type(hi_f32, U32) & jnp.uint32(HI_MASK)
    return lo | hi


def _unpack_halves(w):
    lo = lax.bitcast_convert_type(w << 16, F32)
    hi = lax.bitcast_convert_type(w & jnp.uint32(HI_MASK), F32)
    return lo, hi


def _inv_rms_rows(x_ref, r_ref):
    d = x_ref.shape[1]

    def body(i, carry):
        r = pl.multiple_of(i * SUBLANES, SUBLANES)
        xc = x_ref[pl.ds(r, SUBLANES), :]
        sq = xc * xc
        acc = sq[:, :LANES]
        for c0 in range(LANES, d, LANES):
            acc = acc + sq[:, c0:c0 + LANES]
        r_ref[pl.ds(r, SUBLANES), :] = acc
        return carry
    steps = x_ref.shape[0] // SUBLANES
    lax.fori_loop(0, steps, body, 0, unroll=min(NORM_UNROLL, steps))
    ms = jnp.sum(r_ref[...], axis=-1, keepdims=True) / d
    r_ref[...] = jnp.broadcast_to(lax.rsqrt(ms + RMS_EPS), r_ref.shape)


def _norm_mm_kernel(x_ref, g_ref, w_ref, up_ref, uf_ref, z_ref, r_ref, *, rows, pool_blocks, c):
    j = pl.program_id(1)

    @pl.when(j == 0)
    def _():
        _inv_rms_rows(x_ref, r_ref)

        def body(i, carry):
            r = pl.multiple_of(i * rows, rows)
            inv = r_ref[pl.ds(r, rows), :]
            for c0 in range(0, x_ref.shape[1], LANES):
                z = x_ref[pl.ds(r, rows), c0:c0 + LANES] * inv * g_ref[:, c0:c0 + LANES]
                z_ref[pl.ds(r, rows), c0:c0 + LANES] = z.astype(BF16)
            return carry
        lax.fori_loop(0, x_ref.shape[0] // rows, body, 0, unroll=2)

    y = jnp.dot(z_ref[...], w_ref[...], preferred_element_type=F32)

    @pl.when(j < pool_blocks)
    def _():
        up_ref[...] = y.astype(up_ref.dtype)

    @pl.when(j >= pool_blocks)
    def _():
        yr = y.astype(BF16).astype(F32)
        ch = c // 2
        for hd in range(y.shape[1] // c):
            words = _pack_halves(yr[:, hd * c:hd * c + ch], yr[:, hd * c + ch:(hd + 1) * c])
            for s in range(ch // LANES):
                uf_ref[hd * (ch // LANES) + s] = words[:, s * LANES:(s + 1) * LANES]


def _norm_matmul(x, g, w, *, pw, c, bm, bn):
    n, d = x.shape
    m = w.shape[1]
    bm, bn = min(bm, n), min(bn, pw, m - pw)
    assert pw % bn == 0 and (m - pw) % bn == 0 and bn % c == 0 and (c // 2) % LANES == 0
    pool_blocks = pw // bn
    slabs_blk = bn // 2 // LANES
    return pl.pallas_call(
        functools.partial(_norm_mm_kernel, rows=min(NORM_ROWS, bm), pool_blocks=pool_blocks, c=c),
        grid=(n // bm, m // bn),
        in_specs=[pl.BlockSpec((bm, d), lambda i, j: (i, 0)),
                  pl.BlockSpec((1, d), lambda i, j: (0, 0)),
                  pl.BlockSpec((d, bn), lambda i, j: (0, j))],
        out_specs=[pl.BlockSpec((bm, bn), lambda i, j: (i, jnp.minimum(j, pool_blocks - 1))),
                   pl.BlockSpec((slabs_blk, bm, LANES), lambda i, j: (jnp.maximum(j - pool_blocks, 0), i, 0))],
        out_shape=[jax.ShapeDtypeStruct((n, pw), BF16),
                   jax.ShapeDtypeStruct(((m - pw) // 2 // LANES, n, LANES), U32)],
        scratch_shapes=[pltpu.VMEM((bm, d), BF16), pltpu.VMEM((bm, LANES), F32)],
        compiler_params=_cparams(("parallel", "arbitrary")),
        name="norm1_w_in",
    )(x, g, w)


def _pool_kernel(cur_ref, prev_ref, next_ref, pw_ref, sc_ref, o_ref, *, seq, ts, c):
    t = pl.program_id(1)
    has_prev = (t > 0).astype(F32)
    has_next = (t < pl.num_programs(1) - 1).astype(F32)
    n_ext = ts + 2 * HALO
    tok = t * ts + lax.broadcasted_iota(I32, (ts, 1), 0)
    for g, w in enumerate(POOL_WINDOWS):
        sl = slice(g * c, (g + 1) * c)
        cur = cur_ref[0, :, sl].astype(F32)
        prev = prev_ref[0, :, sl].astype(F32) * has_prev
        nxt = next_ref[0, :, sl].astype(F32) * has_next
        ext = jnp.concatenate([prev, cur, nxt], axis=0)
        s = ext + pltpu.roll(ext, 1, 0)
        h = 1
        while 2 * h < w:
            s = pltpu.roll(s, h, 0) + pltpu.roll(s, n_ext - h, 0)
            h *= 2
        win = s[HALO:HALO + ts]
        lo = jnp.maximum(tok - w // 2, 0)
        hi = jnp.minimum(tok + w // 2 - 1, seq - 1)
        cnt = (hi - lo + 1).astype(F32)
        p = win / cnt - cur
        y = jnp.dot(p.astype(BF16), pw_ref[g], preferred_element_type=F32)
        o_ref[0, :, sl] = (y * sc_ref[:, sl]).astype(o_ref.dtype)


def _pool_mixer(u3, pool_w, pool_scale, *, ts):
    b, seq, _ = u3.shape
    g, c, _ = pool_w.shape
    pw = g * c
    ts = min(ts, seq)
    nh = seq // HALO
    per = ts // HALO
    return pl.pallas_call(
        functools.partial(_pool_kernel, seq=seq, ts=ts, c=c),
        grid=(b, seq // ts),
        in_specs=[pl.BlockSpec((1, ts, pw), lambda i, t: (i, t, 0)),
                  pl.BlockSpec((1, HALO, pw), lambda i, t: (i, jnp.maximum(t * per - 1, 0), 0)),
                  pl.BlockSpec((1, HALO, pw), lambda i, t: (i, jnp.minimum((t + 1) * per, nh - 1), 0)),
                  pl.BlockSpec((g, c, c), lambda i, t: (0, 0, 0)),
                  pl.BlockSpec((1, pw), lambda i, t: (0, 0))],
        out_specs=pl.BlockSpec((1, ts, pw), lambda i, t: (i, t, 0)),
        out_shape=jax.ShapeDtypeStruct((b, seq, pw), BF16),
        compiler_params=_cparams(("parallel", "parallel")),
        name="pool_mixer",
    )(u3, u3, u3, pool_w, pool_scale)


DFT_N1 = 256


def _dft_mats(n, scale, dtype):
    j = jnp.arange(n, dtype=I32)
    ang = ((j[:, None] * j[None, :]) % n).astype(F32) * (2.0 * math.pi / n)
    return (jnp.cos(ang) * scale).astype(dtype), (jnp.sin(ang) * scale).astype(dtype)


def _fourier_prep_kernel(cc_ref, sc_ref, w_ref, o_ref, *, c):
    w = w_ref[0]
    o_ref[0, :c, :] = jnp.dot(cc_ref[...], w, preferred_element_type=F32,
                              precision=lax.Precision.HIGHEST).astype(o_ref.dtype)
    o_ref[0, c:, :] = jnp.dot(sc_ref[...], w, preferred_element_type=F32,
                              precision=lax.Precision.HIGHEST).astype(o_ref.dtype)


def _fourier_prep(fourier_w):
    h, c, _ = fourier_w.shape
    cc, sc = _dft_mats(c, c ** -0.5, F32)
    return pl.pallas_call(
        functools.partial(_fourier_prep_kernel, c=c),
        grid=(h,),
        in_specs=[pl.BlockSpec((c, c), lambda i: (0, 0)),
                  pl.BlockSpec((c, c), lambda i: (0, 0)),
                  pl.BlockSpec((1, c, c), lambda i: (i, 0, 0))],
        out_specs=pl.BlockSpec((1, 2 * c, c), lambda i: (i, 0, 0)),
        out_shape=jax.ShapeDtypeStruct((h, 2 * c, c), BF16),
        compiler_params=_cparams(("parallel",)),
        name="fourier_prep",
    )(cc, sc, fourier_w)


def _fft(xs):
    n = len(xs)
    if n == 1:
        return xs
    ev, od = _fft(xs[0::2]), _fft(xs[1::2])
    out = [None] * n
    for k in range(n // 2):
        re, im = od[k]
        if k == 0:
            tr, ti = re, im
        elif 4 * k == n:
            tr, ti = im, -re
        else:
            wr, wi = math.cos(2.0 * math.pi * k / n), -math.sin(2.0 * math.pi * k / n)
            tr, ti = re * wr - im * wi, re * wi + im * wr
        er, ei = ev[k]
        out[k] = (er + tr, ei + ti)
        out[k + n // 2] = (er - tr, ei - ti)
    return out


def _fourier_kernel(x_ref, cs_ref, tw_ref, ab_ref, o_ref, y_ref, *, n1, n2, c, rows):
    for j in range(n2):
        halves = [_unpack_halves(x_ref[s, pl.ds(j, n1, stride=n2), :]) for s in range(x_ref.shape[0])]
        xj = jnp.concatenate([lo for lo, _ in halves] + [hi for _, hi in halves], axis=1).astype(BF16)
        y = jnp.dot(cs_ref[...], xj, preferred_element_type=F32)
        y_ref[0, j] = y[:n1]
        y_ref[1, j] = y[n1:]

    nl = c // LANES

    def tile(i, carry):
        r = pl.multiple_of((i // nl) * SUBLANES, SUBLANES)
        l = pl.multiple_of((i % nl) * LANES, LANES)
        zs = []
        for j in range(n2):
            yc = y_ref[0, j, pl.ds(r, SUBLANES), pl.ds(l, LANES)]
            ys = y_ref[1, j, pl.ds(r, SUBLANES), pl.ds(l, LANES)]
            ct = tw_ref[0, j, pl.ds(r, SUBLANES), :]
            st = tw_ref[1, j, pl.ds(r, SUBLANES), :]
            zs.append((yc * ct - ys * st, -(ys * ct + yc * st)))
        gs = _fft(zs)
        for k in range(n2):
            y_ref[0, k, pl.ds(r, SUBLANES), pl.ds(l, LANES)] = gs[k][0]
            y_ref[1, k, pl.ds(r, SUBLANES), pl.ds(l, LANES)] = gs[k][1]
        return carry
    lax.fori_loop(0, (n1 // SUBLANES) * nl, tile, 0)

    per = max(1, min(n2, rows // n1))
    for s in range(0, n2, per):
        gr = y_ref[0, s:s + per].reshape(per * n1, c).astype(BF16)
        gi = y_ref[1, s:s + per].reshape(per * n1, c).astype(BF16)
        y = jnp.dot(gr, ab_ref[0, :c, :], preferred_element_type=F32)
        y = y + jnp.dot(gi, ab_ref[0, c:, :], preferred_element_type=F32)
        o_ref[0, s * n1:(s + per) * n1, :] = y.astype(o_ref.dtype)


def _fourier_mixer(uf, ab, *, b, rows):
    slabs, n, _ = uf.shape
    seq = n // b
    h, _, c = ab.shape
    spb = slabs // h
    n1 = min(DFT_N1, seq)
    n2 = seq // n1
    assert n1 * n2 == seq and n2 & (n2 - 1) == 0, "sequence length must be N1 * 2^m"
    cmat, smat = _dft_mats(n1, 1.0, BF16)
    cs = jnp.concatenate([cmat, smat], axis=0)
    ang = (jnp.arange(n2, dtype=I32)[:, None] * jnp.arange(n1, dtype=I32)[None, :]).astype(F32)
    ang = ang * (2.0 * math.pi / seq)
    tw = jnp.stack([jnp.cos(ang), jnp.sin(ang)]) * (seq ** -0.5)
    tw = jnp.broadcast_to(tw[..., None], (2, n2, n1, LANES))

    return pl.pallas_call(
        functools.partial(_fourier_kernel, n1=n1, n2=n2, c=c, rows=rows),
        grid=(b, h),
        in_specs=[
            pl.BlockSpec((spb, None, seq, LANES), lambda i, k: (k, i, 0, 0)),
            pl.BlockSpec((2 * n1, n1), lambda i, k: (0, 0), pipeline_mode=pl.Buffered(1)),
            pl.BlockSpec((2, n2, n1, LANES), lambda i, k: (0, 0, 0, 0), pipeline_mode=pl.Buffered(1)),
            pl.BlockSpec((1, 2 * c, c), lambda i, k: (k, 0, 0))],
        out_specs=pl.BlockSpec((1, seq, c), lambda i, k: (i, 0, k)),
        out_shape=jax.ShapeDtypeStruct((b, seq, h * c), BF16),
        scratch_shapes=[pltpu.VMEM((2, n2, n1, c), F32)],
        compiler_params=_cparams(("parallel", "parallel")),
        name="fourier_mixer",
    )(uf.reshape(slabs, b, seq, LANES), cs, tw, ab)


def _out_proj_kernel(a_ref, f_ref, w_ref, x_ref, o_ref, *, ka):
    y = jnp.dot(a_ref[...], w_ref[:ka, :], preferred_element_type=F32)
    y = y + jnp.dot(f_ref[...], w_ref[ka:, :], preferred_element_type=F32)
    o_ref[...] = x_ref[...] + y


def _out_proj(a, f, w, x, *, bm, bn):
    n, ka = a.shape
    kf = f.shape[1]
    d = w.shape[1]
    bm, bn = min(bm, n), min(bn, d)
    return pl.pallas_call(
        functools.partial(_out_proj_kernel, ka=ka),
        grid=(n // bm, d // bn),
        in_specs=[pl.BlockSpec((bm, ka), lambda i, j: (i, 0)),
                  pl.BlockSpec((bm, kf), lambda i, j: (i, 0)),
                  pl.BlockSpec((ka + kf, bn), lambda i, j: (0, j)),
                  pl.BlockSpec((bm, bn), lambda i, j: (i, j))],
        out_specs=pl.BlockSpec((bm, bn), lambda i, j: (i, j)),
        out_shape=jax.ShapeDtypeStruct((n, d), F32),
        compiler_params=_cparams(("parallel", "parallel")),
        name="w_out_residual",
    )(a, f, w, x)


def _norm_router_kernel(h_ref, g_ref, rw_ref, rb_ref, zp_ref, lg_ref, z_ref, r_ref, *, rows):
    dh = h_ref.shape[1] // 2
    _inv_rms_rows(h_ref, r_ref)

    def body(i, carry):
        r = pl.multiple_of(i * rows, rows)
        inv = r_ref[pl.ds(r, rows), :]
        for c0 in range(0, dh, LANES):
            lo = (h_ref[pl.ds(r, rows), c0:c0 + LANES] * inv * g_ref[:, c0:c0 + LANES]).astype(BF16)
            hi = (h_ref[pl.ds(r, rows), dh + c0:dh + c0 + LANES] * inv
                  * g_ref[:, dh + c0:dh + c0 + LANES]).astype(BF16)
            z_ref[pl.ds(r, rows), c0:c0 + LANES] = lo
            z_ref[pl.ds(r, rows), dh + c0:dh + c0 + LANES] = hi
            zp_ref[pl.ds(r, rows), c0:c0 + LANES] = _pack_halves(lo.astype(F32), hi.astype(F32))
        return carry
    lax.fori_loop(0, h_ref.shape[0] // rows, body, 0, unroll=2)
    lg_ref[...] = jnp.dot(z_ref[...], rw_ref[...], preferred_element_type=F32) + rb_ref[...]


def _norm_router(h, g, rw, rb, *, bm):
    n, d = h.shape
    bm = min(bm, n)
    return pl.pallas_call(
        functools.partial(_norm_router_kernel, rows=min(NORM_ROWS, bm)),
        grid=(n // bm,),
        in_specs=[pl.BlockSpec((bm, d), lambda i: (i, 0)),
                  pl.BlockSpec((1, d), lambda i: (0, 0)),
                  pl.BlockSpec((d, LANES), lambda i: (0, 0)),
                  pl.BlockSpec((1, LANES), lambda i: (0, 0))],
        out_specs=[pl.BlockSpec((bm, d // 2), lambda i: (i, 0)),
                   pl.BlockSpec((bm, LANES), lambda i: (i, 0))],
        out_shape=[jax.ShapeDtypeStruct((n, d // 2), U32),
                   jax.ShapeDtypeStruct((n, LANES), F32)],
        scratch_shapes=[pltpu.VMEM((bm, d), BF16), pltpu.VMEM((bm, LANES), F32)],
        compiler_params=_cparams(("parallel",)),
        name="norm2_router",
    )(h, g, rw, rb)


def _route_kernel(lg_ref, pos_ref, gate_ref, texp_ref, ntl_ref, last_ref, nxt_ref, idx_s, rank_s,
                  *, n_tok, n_exp, tm, tb, tmax_pad):
    shift = tm.bit_length() - 1
    iota_e = lax.broadcasted_iota(I32, (n_exp, tb), 0)
    before = (lax.broadcasted_iota(I32, (tb, tb), 0)
              < lax.broadcasted_iota(I32, (tb, tb), 1)).astype(BF16)

    def pass1(i, counts):
        off = pl.multiple_of(i * tb, tb)
        l = lg_ref[pl.ds(off, tb), :].T[:n_exp, :]
        vals, hots = [], []
        for k in range(TOP_K):
            m = jnp.max(l, axis=0, keepdims=True)
            idx = jnp.min(jnp.where(l == m, iota_e, n_exp), axis=0, keepdims=True)
            hot = iota_e == idx
            l = jnp.where(hot, -jnp.inf, l)
            vals.append(m)
            hots.append(hot)
            idx_s[k:k + 1, pl.ds(off, tb)] = idx
        exps = [jnp.exp(v - vals[0]) for v in vals]
        tot = exps[0]
        for e in exps[1:]:
            tot = tot + e
        sel = hots[0].astype(F32)
        for hot in hots[1:]:
            sel = sel + hot.astype(F32)
        rank = jnp.dot(sel.astype(BF16), before, preferred_element_type=F32) + counts
        for k in range(TOP_K):
            gate_ref[k:k + 1, pl.ds(off, tb)] = exps[k] / tot
            rank_s[k:k + 1, pl.ds(off, tb)] = jnp.sum(jnp.where(hots[k], rank, 0.0), axis=0, keepdims=True)
        return counts + jnp.sum(sel, axis=1, keepdims=True)

    counts = lax.fori_loop(0, n_tok // tb, pass1, jnp.zeros((n_exp, 1), F32))
    ntile = (counts.astype(I32) + (tm - 1)) >> shift
    sub = lax.broadcasted_iota(I32, (n_exp, LANES), 0)
    lane = lax.broadcasted_iota(I32, (n_exp, LANES), 1)
    ntile_row = jnp.sum(jnp.where(sub == lane, ntile, 0), axis=0, keepdims=True)
    start = jnp.sum(jnp.where(lane < sub, ntile_row, 0), axis=1, keepdims=True)
    t_lane = lax.broadcasted_iota(I32, (n_exp, tmax_pad), 1)
    texp_ref[...] = jnp.sum((start <= t_lane).astype(I32), axis=0, keepdims=True) - 1
    ntl_ref[...] = jnp.sum(ntile, axis=0, keepdims=True) + jnp.zeros((1, LANES), I32)
    last = jnp.where(ntile > 0, (start + ntile - 1) << shift, -1)
    last_ref[...] = jnp.sum(jnp.where(sub == lane, last, 0), axis=0, keepdims=True)
    nxt = jnp.min(jnp.where((lane > sub) & (ntile_row > 0), lane, n_exp), axis=1, keepdims=True)
    nxt_ref[...] = jnp.sum(jnp.where(sub == lane, nxt, 0), axis=0, keepdims=True)
    start_rows = start << shift

    def pass2(i, carry):
        off = pl.multiple_of(i * tb, tb)
        for k in range(TOP_K):
            hot = iota_e == idx_s[k:k + 1, pl.ds(off, tb)]
            base = jnp.sum(jnp.where(hot, start_rows, 0), axis=0, keepdims=True)
            pos_ref[k:k + 1, pl.ds(off, tb)] = rank_s[k:k + 1, pl.ds(off, tb)].astype(I32) + base
        return carry
    lax.fori_loop(0, n_tok // tb, pass2, 0)


def _route(logits, *, n_exp, tm, tmax):
    n_tok = logits.shape[0]
    tb = min(256, n_tok)
    tmax_pad = -(-tmax // LANES) * LANES
    return pl.pallas_call(
        functools.partial(_route_kernel, n_tok=n_tok, n_exp=n_exp, tm=tm, tb=tb, tmax_pad=tmax_pad),
        out_shape=[jax.ShapeDtypeStruct((TOP_K, n_tok), I32),
                   jax.ShapeDtypeStruct((TOP_K, n_tok), F32),
                   jax.ShapeDtypeStruct((1, tmax_pad), I32),
                   jax.ShapeDtypeStruct((1, LANES), I32),
                   jax.ShapeDtypeStruct((1, LANES), I32),
                   jax.ShapeDtypeStruct((1, LANES), I32)],
        scratch_shapes=[pltpu.VMEM((TOP_K, n_tok), I32), pltpu.VMEM((TOP_K, n_tok), F32)],
        compiler_params=_cparams(None),
        name="route",
    )(logits)


def _dispatch_kernel(last_ref, pos_ref, z_ref, xs_hbm, zero_v, zsem, sem, *, n_exp, tm, chunk):
    @pl.when(pl.program_id(0) == 0)
    def _():
        zero_v[...] = jnp.zeros_like(zero_v)
        for e in range(n_exp):
            r = last_ref[e]

            @pl.when(r >= 0)
            def _():
                cp = pltpu.make_async_copy(zero_v, xs_hbm.at[pl.ds(pl.multiple_of(r, tm), tm)], zsem)
                cp.start()
                cp.wait()

    for i in range(chunk):
        for k in range(TOP_K):
            pltpu.make_async_copy(z_ref.at[pl.ds(i, 1)], xs_hbm.at[pl.ds(pos_ref[0, k, i], 1)],
                                  sem).start(priority=k % 2)
    for k in range(TOP_K):
        pltpu.make_async_copy(z_ref, xs_hbm.at[pl.ds(0, chunk)], sem).wait()


def _dispatch(last, pos3, z2p, *, n_rows, tm):
    n_tok, dh = z2p.shape
    nchunk, _, chunk = pos3.shape
    n_exp = last.shape[0]
    grid_spec = pltpu.PrefetchScalarGridSpec(
        num_scalar_prefetch=1,
        grid=(nchunk,),
        in_specs=[pl.BlockSpec((1, TOP_K, chunk), lambda c, last: (c, 0, 0), memory_space=pltpu.SMEM),
                  pl.BlockSpec((chunk, dh), lambda c, last: (c, 0))],
        out_specs=pl.BlockSpec(memory_space=pl.ANY),
        scratch_shapes=[pltpu.VMEM((tm, dh), U32), pltpu.SemaphoreType.DMA(()), pltpu.SemaphoreType.DMA(())],
    )
    return pl.pallas_call(
        functools.partial(_dispatch_kernel, n_exp=n_exp, tm=tm, chunk=chunk),
        grid_spec=grid_spec,
        out_shape=jax.ShapeDtypeStruct((n_rows, dh), U32),
        compiler_params=_cparams(("arbitrary",)),
        name="dispatch",
    )(last, pos3, z2p)


CAST_ROWS = 256


def _ffn_kernel(texp_ref, ntl_ref, nxt_ref, x_ref, wg_hbm, wu_hbm, wd_hbm, bgu_ref, bd_ref, o_ref,
                stage_g, stage_u, stage_d, wgu, wd, sems, *, f, n_exp, rows):
    t = pl.program_id(0)
    e = texp_ref[t]
    dh = x_ref.shape[1]
    valid = t < ntl_ref[0]
    first = jnp.logical_or(t == 0, e != texp_ref[jnp.maximum(t - 1, 0)])

    def weight_copies(ex):
        return (pltpu.make_async_copy(wg_hbm.at[ex], stage_g, sems.at[0]),
                pltpu.make_async_copy(wu_hbm.at[ex], stage_u, sems.at[1]),
                pltpu.make_async_copy(wd_hbm.at[ex], stage_d, sems.at[2]))

    @pl.when(t == 0)
    def _():
        for cp in weight_copies(e):
            cp.start()

    @pl.when(jnp.logical_and(valid, first))
    def _():
        for cp in weight_copies(e):
            cp.wait()

        def cast_up(i, carry):
            r = pl.multiple_of(i * rows, rows)
            wgu[pl.ds(r, rows), :f] = stage_g[pl.ds(r, rows), :].astype(BF16)
            wgu[pl.ds(r, rows), f:] = stage_u[pl.ds(r, rows), :].astype(BF16)
            return carry
        lax.fori_loop(0, stage_g.shape[0] // rows, cast_up, 0)

        def cast_down(i, carry):
            r = pl.multiple_of(i * rows, rows)
            wd[pl.ds(r, rows), :] = stage_d[pl.ds(r, rows), :].astype(BF16)
            return carry
        lax.fori_loop(0, stage_d.shape[0] // rows, cast_down, 0)

        nx = nxt_ref[e]

        @pl.when(nx < n_exp)
        def _():
            for cp in weight_copies(nx):
                cp.start()

    @pl.when(valid)
    def _():
        lo, hi = _unpack_halves(x_ref[...])
        hgu = jnp.dot(lo.astype(BF16), wgu[:dh, :], preferred_element_type=F32)
        hgu = hgu + jnp.dot(hi.astype(BF16), wgu[dh:, :], preferred_element_type=F32)
        hgu = hgu + bgu_ref[0]
        hg = jnp.minimum(hgu[:, :f], SWIGLU_LIMIT)
        hu = jnp.clip(hgu[:, f:], -SWIGLU_LIMIT, SWIGLU_LIMIT)
        act = hg * (1.0 / (1.0 + jnp.exp(-SWIGLU_ALPHA * hg))) * (hu + 1.0)
        y = jnp.dot(act.astype(BF16), wd[...], preferred_element_type=F32) + bd_ref[0]
        yr = y.astype(BF16).astype(F32)
        o_ref[...] = _pack_halves(yr[:, :dh], yr[:, dh:])


def _expert_ffn(texp, ntl, nxt, xs, w_gate, w_up, w_down, bgu, bd, *, tm):
    n_rows, dh = xs.shape
    n_exp, d, f = w_gate.shape
    tmax = n_rows // tm
    rows = min(CAST_ROWS, f)
    assert d % rows == 0 and f % rows == 0

    def x_map(t, te, nt, nx):
        return (jnp.minimum(t, nt[0] - 1), 0)

    def b_map(t, te, nt, nx):
        return (te[t], 0, 0)

    grid_spec = pltpu.PrefetchScalarGridSpec(
        num_scalar_prefetch=3,
        grid=(tmax,),
        in_specs=[pl.BlockSpec((tm, dh), x_map),
                  pl.BlockSpec(memory_space=pl.ANY),
                  pl.BlockSpec(memory_space=pl.ANY),
                  pl.BlockSpec(memory_space=pl.ANY),
                  pl.BlockSpec((1, 1, 2 * f), b_map),
                  pl.BlockSpec((1, 1, d), b_map)],
        out_specs=pl.BlockSpec((tm, dh), lambda t, te, nt, nx: (t, 0)),
        scratch_shapes=[pltpu.VMEM((d, f), F32), pltpu.VMEM((d, f), F32), pltpu.VMEM((f, d), F32),
                        pltpu.VMEM((d, 2 * f), BF16), pltpu.VMEM((f, d), BF16),
                        pltpu.SemaphoreType.DMA((3,))],
    )
    return pl.pallas_call(
        functools.partial(_ffn_kernel, f=f, n_exp=n_exp, rows=rows),
        grid_spec=grid_spec,
        out_shape=jax.ShapeDtypeStruct((n_rows, dh), U32),
        compiler_params=_cparams(("arbitrary",)),
        name="expert_ffn",
    )(texp, ntl, nxt, xs, w_gate, w_up, w_down, bgu, bd)


COMBINE_ROWS = 16


def _combine_kernel(pos_c, pos_n, ys_hbm, h_ref, g_ref, fg_ref, o_ref, buf_a, buf_b, sem, *, bt, rows):
    i = pl.program_id(0)
    dh = buf_a.shape[-1]

    def start_rows(pos_ref, dst, s, r0):
        for r in range(rows):
            for k in range(TOP_K):
                pltpu.make_async_copy(ys_hbm.at[pl.ds(pos_ref[0, k, r0 + r], 1)],
                                      dst.at[k, pl.ds(r0 + r, 1)], sem.at[s]).start(priority=k % 2)

    def wait_block(dst, s):
        for k in range(TOP_K):
            pltpu.make_async_copy(ys_hbm.at[pl.ds(0, bt)], dst.at[k], sem.at[s]).wait()

    @pl.when(i == 0)
    def _():
        def first(c, carry):
            start_rows(pos_c, buf_a, 0, pl.multiple_of(c * rows, rows))
            return carry
        lax.fori_loop(0, bt // rows, first, 0)

    def step(cur, s_cur, nxt, s_nxt):
        wait_block(cur, s_cur)
        for r0 in range(0, bt, rows):
            start_rows(pos_n, nxt, s_nxt, r0)
            acc_lo = h_ref[r0:r0 + rows, :dh]
            acc_hi = h_ref[r0:r0 + rows, dh:]
            for k in range(TOP_K):
                lo, hi = _unpack_halves(cur[k, r0:r0 + rows, :])
                gk = g_ref[r0:r0 + rows, k:k + 1]
                acc_lo = acc_lo + gk * lo
                acc_hi = acc_hi + gk * hi
            ms = (jnp.sum(acc_lo * acc_lo, axis=-1, keepdims=True)
                  + jnp.sum(acc_hi * acc_hi, axis=-1, keepdims=True)) / (2 * dh)
            inv = lax.rsqrt(ms + RMS_EPS)
            o_ref[r0:r0 + rows, :dh] = acc_lo * inv * fg_ref[:, :dh]
            o_ref[r0:r0 + rows, dh:] = acc_hi * inv * fg_ref[:, dh:]

        @pl.when(i == pl.num_programs(0) - 1)
        def _():
            wait_block(nxt, s_nxt)

    @pl.when(i % 2 == 0)
    def _():
        step(buf_a, 0, buf_b, 1)

    @pl.when(i % 2 == 1)
    def _():
        step(buf_b, 1, buf_a, 0)


def _combine(pos3, ys, h, gates_t, fg, *, bt):
    n, d = h.shape
    nblk = n // bt
    return pl.pallas_call(
        functools.partial(_combine_kernel, bt=bt, rows=min(COMBINE_ROWS, bt)),
        grid=(nblk,),
        in_specs=[pl.BlockSpec((1, TOP_K, bt), lambda i: (i, 0, 0), memory_space=pltpu.SMEM),
                  pl.BlockSpec((1, TOP_K, bt), lambda i: (jnp.minimum(i + 1, nblk - 1), 0, 0),
                               memory_space=pltpu.SMEM),
                  pl.BlockSpec(memory_space=pl.ANY),
                  pl.BlockSpec((bt, d), lambda i: (i, 0)),
                  pl.BlockSpec((bt, TOP_K), lambda i: (i, 0)),
                  pl.BlockSpec((1, d), lambda i: (0, 0))],
        out_specs=pl.BlockSpec((bt, d), lambda i: (i, 0)),
        out_shape=jax.ShapeDtypeStruct((n, d), F32),
        scratch_shapes=[pltpu.VMEM((TOP_K, bt, d // 2), U32), pltpu.VMEM((TOP_K, bt, d // 2), U32),
                        pltpu.SemaphoreType.DMA((2,))],
        compiler_params=_cparams(("arbitrary",)),
        name="combine_final_norm",
    )(pos3, pos3, ys, h, gates_t, fg)


def _layer(x2, b, seq, norm1_g, w_in, pool_w, pool_scale, fourier_w, w_out, norm2_g,
           router_w, router_b, w_gate, b_gate, w_up, b_up, w_down, b_down):
    n, d = x2.shape
    mix = w_in.shape[1]
    pw = pool_scale.shape[0]
    n_exp = router_w.shape[1]
    f = w_gate.shape[2]
    tm = min(256, n)
    tmax = (n * TOP_K) // tm + n_exp

    up, uf = _norm_matmul(x2, norm1_g.reshape(1, d), w_in.astype(BF16), pw=pw, c=fourier_w.shape[1],
                          bm=512, bn=1024)
    a = _pool_mixer(up.reshape(b, seq, pw), pool_w.astype(BF16), pool_scale.reshape(1, pw), ts=256)
    yf = _fourier_mixer(uf, _fourier_prep(fourier_w), b=b, rows=1024)
    h = _out_proj(a.reshape(n, pw), yf.reshape(n, mix - pw), w_out.astype(BF16), x2, bm=1024, bn=512)

    rw = jnp.zeros((d, LANES), BF16).at[:, :n_exp].set(router_w.astype(BF16))
    rb = jnp.zeros((1, LANES), F32).at[0, :n_exp].set(router_b)
    z2p, logits = _norm_router(h, norm2_g.reshape(1, d), rw, rb, bm=256)

    pos, gates, texp, ntl, last, nxt = _route(logits, n_exp=n_exp, tm=tm, tmax=tmax)
    chunk = min(512, n)
    pos_c = pos.reshape(TOP_K, n // chunk, chunk).transpose(1, 0, 2)
    xs = _dispatch(last[0, :n_exp], pos_c, z2p, n_rows=tmax * tm, tm=tm)

    bgu = jnp.concatenate([b_gate, b_up], axis=-1).reshape(n_exp, 1, 2 * f)
    ys = _expert_ffn(texp[0, :tmax], ntl[0, :1], nxt[0, :n_exp], xs, w_gate, w_up, w_down, bgu,
                     b_down.reshape(n_exp, 1, d), tm=tm)

    bt = min(128, n)
    pos_b = pos.reshape(TOP_K, n // bt, bt).transpose(1, 0, 2)
    return pos_b, ys, h, gates.T


def kernel(x, norm1_g, w_in, pool_w, pool_scale, fourier_w, w_out, norm2_g, router_w, router_b,
           w_gate, b_gate, w_up, b_up, w_down, b_down, final_g):
    b, seq, d = x.shape
    assert w_in.shape[0] == 1, "only a single layer is supported"
    pos_b, ys, h, gates_t = _layer(
        x.reshape(b * seq, d), b, seq, norm1_g[0], w_in[0], pool_w[0], pool_scale[0], fourier_w[0], w_out[0],
        norm2_g[0], router_w[0], router_b[0], w_gate[0], b_gate[0], w_up[0], b_up[0], w_down[0], b_down[0])
    out = _combine(pos_b, ys, h, gates_t, final_g.reshape(1, d), bt=min(128, b * seq))
    return out.reshape(b, seq, d)
```

```python
import functools
import math

import jax
import jax.numpy as jnp
from jax import lax
from jax.experimental import pallas as pl
from jax.experimental.pallas import tpu as pltpu

F32 = jnp.float32
BF16 = jnp.bfloat16
I32 = jnp.int32
U32 = jnp.uint32

RMS_EPS = 1e-5
POOL_WINDOWS = (2, 4, 8, 16)
TOP_K = 4
SWIGLU_LIMIT = 7.0
SWIGLU_ALPHA = 1.702

LANES = 128
BF16_SUBLANES = 16
HALO = 16
NORM_ROWS = 16
NORM_UNROLL = 8
SUBLANES = 8
VMEM_LIMIT = 56 * 1024 * 1024
HI_MASK = 0xFFFF0000


def _cparams(sem, vmem=VMEM_LIMIT):
    return pltpu.CompilerParams(dimension_semantics=sem, vmem_limit_bytes=vmem)


def _pack_halves(lo_f32, hi_f32):
    lo = lax.bitcast_convert_type(lo_f32, U32) >> 16
    hi = lax.bitcast_convert_type(hi_f32, U32) & jnp.uint32(HI_MASK)
    return lo | hi


def _unpack_halves(w):
    lo = lax.bitcast_convert_type(w << 16, F32)
    hi = lax.bitcast_convert_type(w & jnp.uint32(HI_MASK), F32)
    return lo, hi


def _inv_rms_rows(x_ref, r_ref):
    d = x_ref.shape[1]

    def body(i, carry):
        r = pl.multiple_of(i * SUBLANES, SUBLANES)
        xc = x_ref[pl.ds(r, SUBLANES), :]
        sq = xc * xc
        acc = sq[:, :LANES]
        for c0 in range(LANES, d, LANES):
            acc = acc + sq[:, c0:c0 + LANES]
        r_ref[pl.ds(r, SUBLANES), :] = acc
        return carry
    steps = x_ref.shape[0] // SUBLANES
    lax.fori_loop(0, steps, body, 0, unroll=min(NORM_UNROLL, steps))
    ms = jnp.sum(r_ref[...], axis=-1, keepdims=True) / d
    r_ref[...] = jnp.broadcast_to(lax.rsqrt(ms + RMS_EPS), r_ref.shape)


def _norm_mm_kernel(x_ref, g_ref, w_ref, up_ref, uf_ref, z_ref, r_ref, *, rows, pool_blocks, c):
    j = pl.program_id(1)

    @pl.when(j == 0)
    def _():
        _inv_rms_rows(x_ref, r_ref)

        def body(i, carry):
            r = pl.multiple_of(i * rows, rows)
            inv = r_ref[pl.ds(r, rows), :]
            for c0 in range(0, x_ref.shape[1], LANES):
                z = x_ref[pl.ds(r, rows), c0:c0 + LANES] * inv * g_ref[:, c0:c0 + LANES]
                z_ref[pl.ds(r, rows), c0:c0 + LANES] = z.astype(BF16)
            return carry
        lax.fori_loop(0, x_ref.shape[0] // rows, body, 0, unroll=2)

    y = jnp.dot(z_ref[...], w_ref[...], preferred_element_type=F32)

    @pl.when(j < pool_blocks)
    def _():
        up_ref[...] = y.astype(up_ref.dtype)

    @pl.when(j >= pool_blocks)
    def _():
        yr = y.astype(BF16).astype(F32)
        ch = c // 2
        for hd in range(y.shape[1] // c):
            words = _pack_halves(yr[:, hd * c:hd * c + ch], yr[:, hd * c + ch:(hd + 1) * c])
            for s in range(ch // LANES):
                uf_ref[hd * (ch // LANES) + s] = words[:, s * LANES:(s + 1) * LANES]


def _norm_matmul(x, g, w, *, pw, c, bm, bn):
    n, d = x.shape
    m = w.shape[1]
    bm, bn = min(bm, n), min(bn, pw, m - pw)
    assert pw % bn == 0 and (m - pw) % bn == 0 and bn % c == 0 and (c // 2) % LANES == 0
    pool_blocks = pw // bn
    slabs_blk = bn // 2 // LANES
    return pl.pallas_call(
        functools.partial(_norm_mm_kernel, rows=min(NORM_ROWS, bm), pool_blocks=pool_blocks, c=c),
        grid=(n // bm, m // bn),
        in_specs=[pl.BlockSpec((bm, d), lambda i, j: (i, 0)),
                  pl.BlockSpec((1, d), lambda i, j: (0, 0)),
                  pl.BlockSpec((d, bn), lambda i, j: (0, j))],
        out_specs=[pl.BlockSpec((bm, bn), lambda i, j: (i, jnp.minimum(j, pool_blocks - 1))),
                   pl.BlockSpec((slabs_blk, bm, LANES), lambda i, j: (jnp.maximum(j - pool_blocks, 0), i, 0))],
        out_shape=[jax.ShapeDtypeStruct((n, pw), BF16),
                   jax.ShapeDtypeStruct(((m - pw) // 2 // LANES, n, LANES), U32)],
        scratch_shapes=[pltpu.VMEM((bm, d), BF16), pltpu.VMEM((bm, LANES), F32)],
        compiler_params=_cparams(("parallel", "arbitrary")),
        name="norm1_w_in",
    )(x, g, w)


def _pool_kernel(cur_ref, prev_ref, next_ref, pw_ref, sc_ref, o_ref, *, seq, ts, c):
    t = pl.program_id(1)
    has_prev = (t > 0).astype(F32)
    has_next = (t < pl.num_programs(1) - 1).astype(F32)
    n_ext = ts + 2 * HALO
    tok = t * ts + lax.broadcasted_iota(I32, (ts, 1), 0)
    for g, w in enumerate(POOL_WINDOWS):
        sl = slice(g * c, (g + 1) * c)
        cur = cur_ref[0, :, sl].astype(F32)
        prev = prev_ref[0, :, sl].astype(F32) * has_prev
        nxt = next_ref[0, :, sl].astype(F32) * has_next
        ext = jnp.concatenate([prev, cur, nxt], axis=0)
        s = ext + pltpu.roll(ext, 1, 0)
        h = 1
        while 2 * h < w:
            s = pltpu.roll(s, h, 0) + pltpu.roll(s, n_ext - h, 0)
            h *= 2
        win = s[HALO:HALO + ts]
        lo = jnp.maximum(tok - w // 2, 0)
        hi = jnp.minimum(tok + w // 2 - 1, seq - 1)
        cnt = (hi - lo + 1).astype(F32)
        p = win / cnt - cur
        y = jnp.dot(p.astype(BF16), pw_ref[g], preferred_element_type=F32)
        o_ref[0, :, sl] = (y * sc_ref[:, sl]).astype(o_ref.dtype)


def _pool_mixer(u3, pool_w, pool_scale, *, ts):
    b, seq, _ = u3.shape
    g, c, _ = pool_w.shape
    pw = g * c
    ts = min(ts, seq)
    nh = seq // HALO
    per = ts // HALO
    return pl.pallas_call(
        functools.partial(_pool_kernel, seq=seq, ts=ts, c=c),
        grid=(b, seq // ts),
        in_specs=[pl.BlockSpec((1, ts, pw), lambda i, t: (i, t, 0)),
                  pl.BlockSpec((1, HALO, pw), lambda i, t: (i, jnp.maximum(t * per - 1, 0), 0)),
                  pl.BlockSpec((1, HALO, pw), lambda i, t: (i, jnp.minimum((t + 1) * per, nh - 1), 0)),
                  pl.BlockSpec((g, c, c), lambda i, t: (0, 0, 0)),
                  pl.BlockSpec((1, pw), lambda i, t: (0, 0))],
        out_specs=pl.BlockSpec((1, ts, pw), lambda i, t: (i, t, 0)),
        out_shape=jax.ShapeDtypeStruct((b, seq, pw), BF16),
        compiler_params=_cparams(("parallel", "parallel")),
        name="pool_mixer",
    )(u3, u3, u3, pool_w, pool_scale)


DFT_N1 = 256


def _dft_mats(n, scale, dtype):
    j = jnp.arange(n, dtype=I32)
    ang = ((j[:, None] * j[None, :]) % n).astype(F32) * (2.0 * math.pi / n)
    return (jnp.cos(ang) * scale).astype(dtype), (jnp.sin(ang) * scale).astype(dtype)


def _fourier_prep_kernel(cc_ref, sc_ref, w_ref, o_ref, *, c):
    w = w_ref[0]
    o_ref[0, :c, :] = jnp.dot(cc_ref[...], w, preferred_element_type=F32,
                              precision=lax.Precision.HIGHEST).astype(o_ref.dtype)
    o_ref[0, c:, :] = jnp.dot(sc_ref[...], w, preferred_element_type=F32,
                              precision=lax.Precision.HIGHEST).astype(o_ref.dtype)


def _fourier_prep(fourier_w):
    h, c, _ = fourier_w.shape
    cc, sc = _dft_mats(c, c ** -0.5, F32)
    return pl.pallas_call(
        functools.partial(_fourier_prep_kernel, c=c),
        grid=(h,),
        in_specs=[pl.BlockSpec((c, c), lambda i: (0, 0)),
                  pl.BlockSpec((c, c), lambda i: (0, 0)),
                  pl.BlockSpec((1, c, c), lambda i: (i, 0, 0))],
        out_specs=pl.BlockSpec((1, 2 * c, c), lambda i: (i, 0, 0)),
        out_shape=jax.ShapeDtypeStruct((h, 2 * c, c), BF16),
        compiler_params=_cparams(("parallel",)),
        name="fourier_prep",
    )(cc, sc, fourier_w)


def _fft(xs):
    n = len(xs)
    if n == 1:
        return xs
    ev, od = _fft(xs[0::2]), _fft(xs[1::2])
    out = [None] * n
    for k in range(n // 2):
        re, im = od[k]
        if k == 0:
            tr, ti = re, im
        elif 4 * k == n:
            tr, ti = im, -re
        else:
            wr, wi = math.cos(2.0 * math.pi * k / n), -math.sin(2.0 * math.pi * k / n)
            tr, ti = re * wr - im * wi, re * wi + im * wr
        er, ei = ev[k]
        out[k] = (er + tr, ei + ti)
        out[k + n // 2] = (er - tr, ei - ti)
    return out


def _fourier_kernel(x_ref, cs_ref, tw_ref, ab_ref, o_ref, y_ref, *, n1, n2, c, rows):
    for j in range(n2):
        halves = [_unpack_halves(x_ref[s, pl.ds(j, n1, stride=n2), :]) for s in range(x_ref.shape[0])]
        xj = jnp.concatenate([lo for lo, _ in halves] + [hi for _, hi in halves], axis=1).astype(BF16)
        y = jnp.dot(cs_ref[...], xj, preferred_element_type=F32)
        y_ref[0, j] = y[:n1]
        y_ref[1, j] = y[n1:]

    nl = c // LANES

    def tile(i, carry):
        r = pl.multiple_of((i // nl) * SUBLANES, SUBLANES)
        l = pl.multiple_of((i % nl) * LANES, LANES)
        zs = []
        for j in range(n2):
            yc = y_ref[0, j, pl.ds(r, SUBLANES), pl.ds(l, LANES)]
            ys = y_ref[1, j, pl.ds(r, SUBLANES), pl.ds(l, LANES)]
            ct = tw_ref[0, j, pl.ds(r, SUBLANES), :]
            st = tw_ref[1, j, pl.ds(r, SUBLANES), :]
            zs.append((yc * ct - ys * st, -(ys * ct + yc * st)))
        gs = _fft(zs)
        for k in range(n2):
            y_ref[0, k, pl.ds(r, SUBLANES), pl.ds(l, LANES)] = gs[k][0]
            y_ref[1, k, pl.ds(r, SUBLANES), pl.ds(l, LANES)] = gs[k][1]
        return carry
    lax.fori_loop(0, (n1 // SUBLANES) * nl, tile, 0, unroll=2)

    per = max(1, min(n2, rows // n1))
    for s in range(0, n2, per):
        gr = y_ref[0, s:s + per].reshape(per * n1, c).astype(BF16)
        gi = y_ref[1, s:s + per].reshape(per * n1, c).astype(BF16)
        y = jnp.dot(gr, ab_ref[0, :c, :], preferred_element_type=F32)
        y = y + jnp.dot(gi, ab_ref[0, c:, :], preferred_element_type=F32)
        o_ref[0, s * n1:(s + per) * n1, :] = y.astype(o_ref.dtype)


def _fourier_mixer(uf, ab, *, b, rows):
    slabs, n, _ = uf.shape
    seq = n // b
    h, _, c = ab.shape
    spb = slabs // h
    n1 = min(DFT_N1, seq)
    n2 = seq // n1
    assert n1 * n2 == seq and n2 & (n2 - 1) == 0, "sequence length must be N1 * 2^m"
    cmat, smat = _dft_mats(n1, 1.0, BF16)
    cs = jnp.concatenate([cmat, smat], axis=0)
    ang = (jnp.arange(n2, dtype=I32)[:, None] * jnp.arange(n1, dtype=I32)[None, :]).astype(F32)
    ang = ang * (2.0 * math.pi / seq)
    tw = jnp.stack([jnp.cos(ang), jnp.sin(ang)]) * (seq ** -0.5)
    tw = jnp.broadcast_to(tw[..., None], (2, n2, n1, LANES))

    return pl.pallas_call(
        functools.partial(_fourier_kernel, n1=n1, n2=n2, c=c, rows=rows),
        grid=(b, h),
        in_specs=[
            pl.BlockSpec((spb, None, seq, LANES), lambda i, k: (k, i, 0, 0)),
            pl.BlockSpec((2 * n1, n1), lambda i, k: (0, 0), pipeline_mode=pl.Buffered(1)),
            pl.BlockSpec((2, n2, n1, LANES), lambda i, k: (0, 0, 0, 0), pipeline_mode=pl.Buffered(1)),
            pl.BlockSpec((1, 2 * c, c), lambda i, k: (k, 0, 0))],
        out_specs=pl.BlockSpec((1, seq, c), lambda i, k: (i, 0, k)),
        out_shape=jax.ShapeDtypeStruct((b, seq, h * c), BF16),
        scratch_shapes=[pltpu.VMEM((2, n2, n1, c), F32)],
        compiler_params=_cparams(("parallel", "parallel")),
        name="fourier_mixer",
    )(uf.reshape(slabs, b, seq, LANES), cs, tw, ab)


def _out_proj_kernel(a_ref, f_ref, w_ref, x_ref, o_ref, *, ka):
    y = jnp.dot(a_ref[...], w_ref[:ka, :], preferred_element_type=F32)
    y = y + jnp.dot(f_ref[...], w_ref[ka:, :], preferred_element_type=F32)
    o_ref[...] = x_ref[...] + y


def _out_proj(a, f, w, x, *, bm, bn):
    n, ka = a.shape
    kf = f.shape[1]
    d = w.shape[1]
    bm, bn = min(bm, n), min(bn, d)
    return pl.pallas_call(
        functools.partial(_out_proj_kernel, ka=ka),
        grid=(n // bm, d // bn),
        in_specs=[pl.BlockSpec((bm, ka), lambda i, j: (i, 0)),
                  pl.BlockSpec((bm, kf), lambda i, j: (i, 0)),
                  pl.BlockSpec((ka + kf, bn), lambda i, j: (0, j)),
                  pl.BlockSpec((bm, bn), lambda i, j: (i, j))],
        out_specs=pl.BlockSpec((bm, bn), lambda i, j: (i, j)),
        out_shape=jax.ShapeDtypeStruct((n, d), F32),
        compiler_params=_cparams(("parallel", "parallel")),
        name="w_out_residual",
    )(a, f, w, x)


def _norm_router_kernel(h_ref, g_ref, rw_ref, rb_ref, zp_ref, lg_ref, z_ref, r_ref, *, rows):
    dh = h_ref.shape[1] // 2
    _inv_rms_rows(h_ref, r_ref)

    def body(i, carry):
        r = pl.multiple_of(i * rows, rows)
        inv = r_ref[pl.ds(r, rows), :]
        for c0 in range(0, dh, LANES):
            lo = (h_ref[pl.ds(r, rows), c0:c0 + LANES] * inv * g_ref[:, c0:c0 + LANES]).astype(BF16)
            hi = (h_ref[pl.ds(r, rows), dh + c0:dh + c0 + LANES] * inv
                  * g_ref[:, dh + c0:dh + c0 + LANES]).astype(BF16)
            z_ref[pl.ds(r, rows), c0:c0 + LANES] = lo
            z_ref[pl.ds(r, rows), dh + c0:dh + c0 + LANES] = hi
            zp_ref[pl.ds(r, rows), c0:c0 + LANES] = _pack_halves(lo.astype(F32), hi.astype(F32))
        return carry
    lax.fori_loop(0, h_ref.shape[0] // rows, body, 0, unroll=2)
    lg_ref[...] = jnp.dot(z_ref[...], rw_ref[...], preferred_element_type=F32) + rb_ref[...]


def _norm_router(h, g, rw, rb, *, bm):
    n, d = h.shape
    bm = min(bm, n)
    return pl.pallas_call(
        functools.partial(_norm_router_kernel, rows=min(NORM_ROWS, bm)),
        grid=(n // bm,),
        in_specs=[pl.BlockSpec((bm, d), lambda i: (i, 0)),
                  pl.BlockSpec((1, d), lambda i: (0, 0)),
                  pl.BlockSpec((d, LANES), lambda i: (0, 0)),
                  pl.BlockSpec((1, LANES), lambda i: (0, 0))],
        out_specs=[pl.BlockSpec((bm, d // 2), lambda i: (i, 0)),
                   pl.BlockSpec((bm, LANES), lambda i: (i, 0))],
        out_shape=[jax.ShapeDtypeStruct((n, d // 2), U32),
                   jax.ShapeDtypeStruct((n, LANES), F32)],
        scratch_shapes=[pltpu.VMEM((bm, d), BF16), pltpu.VMEM((bm, LANES), F32)],
        compiler_params=_cparams(("parallel",)),
        name="norm2_router",
    )(h, g, rw, rb)


def _route_kernel(lg_ref, pos_ref, gate_ref, texp_ref, ntl_ref, last_ref, nxt_ref, idx_s, rank_s,
                  *, n_tok, n_exp, tm, tb, tmax_pad):
    shift = tm.bit_length() - 1
    iota_e = lax.broadcasted_iota(I32, (n_exp, tb), 0)
    before = (lax.broadcasted_iota(I32, (tb, tb), 0)
              < lax.broadcasted_iota(I32, (tb, tb), 1)).astype(BF16)

    def pass1(i, counts):
        off = pl.multiple_of(i * tb, tb)
        l = lg_ref[pl.ds(off, tb), :].T[:n_exp, :]
        vals, hots = [], []
        for k in range(TOP_K):
            m = jnp.max(l, axis=0, keepdims=True)
            idx = jnp.min(jnp.where(l == m, iota_e, n_exp), axis=0, keepdims=True)
            hot = iota_e == idx
            l = jnp.where(hot, -jnp.inf, l)
            vals.append(m)
            hots.append(hot)
            idx_s[k:k + 1, pl.ds(off, tb)] = idx
        exps = [jnp.exp(v - vals[0]) for v in vals]
        tot = exps[0]
        for e in exps[1:]:
            tot = tot + e
        sel = hots[0].astype(F32)
        for hot in hots[1:]:
            sel = sel + hot.astype(F32)
        rank = jnp.dot(sel.astype(BF16), before, preferred_element_type=F32) + counts
        for k in range(TOP_K):
            gate_ref[k:k + 1, pl.ds(off, tb)] = exps[k] / tot
            rank_s[k:k + 1, pl.ds(off, tb)] = jnp.sum(jnp.where(hots[k], rank, 0.0), axis=0, keepdims=True)
        return counts + jnp.sum(sel, axis=1, keepdims=True)

    counts = lax.fori_loop(0, n_tok // tb, pass1, jnp.zeros((n_exp, 1), F32))
    ntile = (counts.astype(I32) + (tm - 1)) >> shift
    sub = lax.broadcasted_iota(I32, (n_exp, LANES), 0)
    lane = lax.broadcasted_iota(I32, (n_exp, LANES), 1)
    ntile_row = jnp.sum(jnp.where(sub == lane, ntile, 0), axis=0, keepdims=True)
    start = jnp.sum(jnp.where(lane < sub, ntile_row, 0), axis=1, keepdims=True)
    t_lane = lax.broadcasted_iota(I32, (n_exp, tmax_pad), 1)
    texp_ref[...] = jnp.sum((start <= t_lane).astype(I32), axis=0, keepdims=True) - 1
    ntl_ref[...] = jnp.sum(ntile, axis=0, keepdims=True) + jnp.zeros((1, LANES), I32)
    last = jnp.where(ntile > 0, (start + ntile - 1) << shift, -1)
    last_ref[...] = jnp.sum(jnp.where(sub == lane, last, 0), axis=0, keepdims=True)
    nxt = jnp.min(jnp.where((lane > sub) & (ntile_row > 0), lane, n_exp), axis=1, keepdims=True)
    nxt_ref[...] = jnp.sum(jnp.where(sub == lane, nxt, 0), axis=0, keepdims=True)
    start_rows = start << shift

    def pass2(i, carry):
        off = pl.multiple_of(i * tb, tb)
        for k in range(TOP_K):
            hot = iota_e == idx_s[k:k + 1, pl.ds(off, tb)]
            base = jnp.sum(jnp.where(hot, start_rows, 0), axis=0, keepdims=True)
            pos_ref[k:k + 1, pl.ds(off, tb)] = rank_s[k:k + 1, pl.ds(off, tb)].astype(I32) + base
        return carry
    lax.fori_loop(0, n_tok // tb, pass2, 0)


def _route(logits, *, n_exp, tm, tmax):
    n_tok = logits.shape[0]
    tb = min(256, n_tok)
    tmax_pad = -(-tmax // LANES) * LANES
    return pl.pallas_call(
        functools.partial(_route_kernel, n_tok=n_tok, n_exp=n_exp, tm=tm, tb=tb, tmax_pad=tmax_pad),
        out_shape=[jax.ShapeDtypeStruct((TOP_K, n_tok), I32),
                   jax.ShapeDtypeStruct((TOP_K, n_tok), F32),
                   jax.ShapeDtypeStruct((1, tmax_pad), I32),
                   jax.ShapeDtypeStruct((1, LANES), I32),
                   jax.ShapeDtypeStruct((1, LANES), I32),
                   jax.ShapeDtypeStruct((1, LANES), I32)],
        scratch_shapes=[pltpu.VMEM((TOP_K, n_tok), I32), pltpu.VMEM((TOP_K, n_tok), F32)],
        compiler_params=_cparams(None),
        name="route",
    )(logits)


def _dispatch_kernel(last_ref, pos_ref, z_ref, xs_hbm, zero_v, zsem, sem, *, n_exp, tm, chunk):
    @pl.when(pl.program_id(0) == 0)
    def _():
        zero_v[...] = jnp.zeros_like(zero_v)
        for e in range(n_exp):
            r = last_ref[e]

            @pl.when(r >= 0)
            def _():
                cp = pltpu.make_async_copy(zero_v, xs_hbm.at[pl.ds(pl.multiple_of(r, tm), tm)], zsem)
                cp.start()
                cp.wait()

    for i in range(chunk):
        for k in range(TOP_K):
            pltpu.make_async_copy(z_ref.at[pl.ds(i, 1)], xs_hbm.at[pl.ds(pos_ref[0, k, i], 1)],
                                  sem).start(priority=k % 2)
    for k in range(TOP_K):
        pltpu.make_async_copy(z_ref, xs_hbm.at[pl.ds(0, chunk)], sem).wait()


def _dispatch(last, pos3, z2p, *, n_rows, tm):
    n_tok, dh = z2p.shape
    nchunk, _, chunk = pos3.shape
    n_exp = last.shape[0]
    grid_spec = pltpu.PrefetchScalarGridSpec(
        num_scalar_prefetch=1,
        grid=(nchunk,),
        in_specs=[pl.BlockSpec((1, TOP_K, chunk), lambda c, last: (c, 0, 0), memory_space=pltpu.SMEM),
                  pl.BlockSpec((chunk, dh), lambda c, last: (c, 0))],
        out_specs=pl.BlockSpec(memory_space=pl.ANY),
        scratch_shapes=[pltpu.VMEM((tm, dh), U32), pltpu.SemaphoreType.DMA(()), pltpu.SemaphoreType.DMA(())],
    )
    return pl.pallas_call(
        functools.partial(_dispatch_kernel, n_exp=n_exp, tm=tm, chunk=chunk),
        grid_spec=grid_spec,
        out_shape=jax.ShapeDtypeStruct((n_rows, dh), U32),
        compiler_params=_cparams(("arbitrary",)),
        name="dispatch",
    )(last, pos3, z2p)


CAST_ROWS = 256


def _ffn_kernel(texp_ref, ntl_ref, nxt_ref, x_ref, wg_hbm, wu_hbm, wd_hbm, bgu_ref, bd_ref, o_ref,
                stage_g, stage_u, stage_d, wgu, wd, sems, *, f, n_exp, rows):
    t = pl.program_id(0)
    e = texp_ref[t]
    dh = x_ref.shape[1]
    valid = t < ntl_ref[0]
    first = jnp.logical_or(t == 0, e != texp_ref[jnp.maximum(t - 1, 0)])

    def weight_copies(ex):
        return (pltpu.make_async_copy(wg_hbm.at[ex], stage_g, sems.at[0]),
                pltpu.make_async_copy(wu_hbm.at[ex], stage_u, sems.at[1]),
                pltpu.make_async_copy(wd_hbm.at[ex], stage_d, sems.at[2]))

    @pl.when(t == 0)
    def _():
        for cp in weight_copies(e):
            cp.start()

    @pl.when(jnp.logical_and(valid, first))
    def _():
        for cp in weight_copies(e):
            cp.wait()

        def cast_up(i, carry):
            r = pl.multiple_of(i * rows, rows)
            wgu[pl.ds(r, rows), :f] = stage_g[pl.ds(r, rows), :].astype(BF16)
            wgu[pl.ds(r, rows), f:] = stage_u[pl.ds(r, rows), :].astype(BF16)
            return carry
        lax.fori_loop(0, stage_g.shape[0] // rows, cast_up, 0)

        def cast_down(i, carry):
            r = pl.multiple_of(i * rows, rows)
            wd[pl.ds(r, rows), :] = stage_d[pl.ds(r, rows), :].astype(BF16)
            return carry
        lax.fori_loop(0, stage_d.shape[0] // rows, cast_down, 0)

        nx = nxt_ref[e]

        @pl.when(nx < n_exp)
        def _():
            for cp in weight_copies(nx):
                cp.start(priority=1)

    @pl.when(valid)
    def _():
        lo, hi = _unpack_halves(x_ref[...])
        hgu = jnp.dot(lo.astype(BF16), wgu[:dh, :], preferred_element_type=F32)
        hgu = hgu + jnp.dot(hi.astype(BF16), wgu[dh:, :], preferred_element_type=F32)
        hgu = hgu + bgu_ref[0]
        hg = jnp.minimum(hgu[:, :f], SWIGLU_LIMIT)
        hu = jnp.clip(hgu[:, f:], -SWIGLU_LIMIT, SWIGLU_LIMIT)
        act = hg * (1.0 / (1.0 + jnp.exp(-SWIGLU_ALPHA * hg))) * (hu + 1.0)
        y = jnp.dot(act.astype(BF16), wd[...], preferred_element_type=F32) + bd_ref[0]
        yr = y.astype(BF16).astype(F32)
        o_ref[...] = _pack_halves(yr[:, :dh], yr[:, dh:])


def _expert_ffn(texp, ntl, nxt, xs, w_gate, w_up, w_down, bgu, bd, *, tm):
    n_rows, dh = xs.shape
    n_exp, d, f = w_gate.shape
    tmax = n_rows // tm
    rows = min(CAST_ROWS, f)
    assert d % rows == 0 and f % rows == 0

    def x_map(t, te, nt, nx):
        return (jnp.minimum(t, nt[0] - 1), 0)

    def b_map(t, te, nt, nx):
        return (te[t], 0, 0)

    grid_spec = pltpu.PrefetchScalarGridSpec(
        num_scalar_prefetch=3,
        grid=(tmax,),
        in_specs=[pl.BlockSpec((tm, dh), x_map),
                  pl.BlockSpec(memory_space=pl.ANY),
                  pl.BlockSpec(memory_space=pl.ANY),
                  pl.BlockSpec(memory_space=pl.ANY),
                  pl.BlockSpec((1, 1, 2 * f), b_map),
                  pl.BlockSpec((1, 1, d), b_map)],
        out_specs=pl.BlockSpec((tm, dh), lambda t, te, nt, nx: (t, 0)),
        scratch_shapes=[pltpu.VMEM((d, f), F32), pltpu.VMEM((d, f), F32), pltpu.VMEM((f, d), F32),
                        pltpu.VMEM((d, 2 * f), BF16), pltpu.VMEM((f, d), BF16),
                        pltpu.SemaphoreType.DMA((3,))],
    )
    return pl.pallas_call(
        functools.partial(_ffn_kernel, f=f, n_exp=n_exp, rows=rows),
        grid_spec=grid_spec,
        out_shape=jax.ShapeDtypeStruct((n_rows, dh), U32),
        compiler_params=_cparams(("arbitrary",)),
        name="expert_ffn",
    )(texp, ntl, nxt, xs, w_gate, w_up, w_down, bgu, bd)


COMBINE_ROWS = 16


def _combine_kernel(pos_c, pos_n, ys_hbm, h_ref, g_ref, fg_ref, o_ref, buf_a, buf_b, sem, *, bt, rows):
    i = pl.program_id(0)
    dh = buf_a.shape[-1]

    def start_rows(pos_ref, dst, s, r0):
        for r in range(rows):
            for k in range(TOP_K):
                pltpu.make_async_copy(ys_hbm.at[pl.ds(pos_ref[0, k, r0 + r], 1)],
                                      dst.at[k, pl.ds(r0 + r, 1)], sem.at[s]).start(priority=k % 2)

    def wait_block(dst, s):
        for k in range(TOP_K):
            pltpu.make_async_copy(ys_hbm.at[pl.ds(0, bt)], dst.at[k], sem.at[s]).wait()

    @pl.when(i == 0)
    def _():
        def first(c, carry):
            start_rows(pos_c, buf_a, 0, pl.multiple_of(c * rows, rows))
            return carry
        lax.fori_loop(0, bt // rows, first, 0)

    def step(cur, s_cur, nxt, s_nxt):
        wait_block(cur, s_cur)
        for r0 in range(0, bt, rows):
            start_rows(pos_n, nxt, s_nxt, r0)
            acc_lo = h_ref[r0:r0 + rows, :dh]
            acc_hi = h_ref[r0:r0 + rows, dh:]
            for k in range(TOP_K):
                lo, hi = _unpack_halves(cur[k, r0:r0 + rows, :])
                gk = g_ref[r0:r0 + rows, k:k + 1]
                acc_lo = acc_lo + gk * lo
                acc_hi = acc_hi + gk * hi
            ms = (jnp.sum(acc_lo * acc_lo, axis=-1, keepdims=True)
                  + jnp.sum(acc_hi * acc_hi, axis=-1, keepdims=True)) / (2 * dh)
            inv = lax.rsqrt(ms + RMS_EPS)
            o_ref[r0:r0 + rows, :dh] = acc_lo * inv * fg_ref[:, :dh]
            o_ref[r0:r0 + rows, dh:] = acc_hi * inv * fg_ref[:, dh:]

        @pl.when(i == pl.num_programs(0) - 1)
        def _():
            wait_block(nxt, s_nxt)

    @pl.when(i % 2 == 0)
    def _():
        step(buf_a, 0, buf_b, 1)

    @pl.when(i % 2 == 1)
    def _():
        step(buf_b, 1, buf_a, 0)


def _combine(pos3, ys, h, gates_t, fg, *, bt):
    n, d = h.shape
    nblk = n // bt
    return pl.pallas_call(
        functools.partial(_combine_kernel, bt=bt, rows=min(COMBINE_ROWS, bt)),
        grid=(nblk,),
        in_specs=[pl.BlockSpec((1, TOP_K, bt), lambda i: (i, 0, 0), memory_space=pltpu.SMEM),
                  pl.BlockSpec((1, TOP_K, bt), lambda i: (jnp.minimum(i + 1, nblk - 1), 0, 0),
                               memory_space=pltpu.SMEM),
                  pl.BlockSpec(memory_space=pl.ANY),
                  pl.BlockSpec((bt, d), lambda i: (i, 0)),
                  pl.BlockSpec((bt, TOP_K), lambda i: (i, 0)),
                  pl.BlockSpec((1, d), lambda i: (0, 0))],
        out_specs=pl.BlockSpec((bt, d), lambda i: (i, 0)),
        out_shape=jax.ShapeDtypeStruct((n, d), F32),
        scratch_shapes=[pltpu.VMEM((TOP_K, bt, d // 2), U32), pltpu.VMEM((TOP_K, bt, d // 2), U32),
                        pltpu.SemaphoreType.DMA((2,))],
        compiler_params=_cparams(("arbitrary",)),
        name="combine_final_norm",
    )(pos3, pos3, ys, h, gates_t, fg)


def _layer(x2, b, seq, norm1_g, w_in, pool_w, pool_scale, fourier_w, w_out, norm2_g,
           router_w, router_b, w_gate, b_gate, w_up, b_up, w_down, b_down):
    n, d = x2.shape
    mix = w_in.shape[1]
    pw = pool_scale.shape[0]
    n_exp = router_w.shape[1]
    f = w_gate.shape[2]
    tm = min(256, n)
    tmax = (n * TOP_K) // tm + n_exp

    up, uf = _norm_matmul(x2, norm1_g.reshape(1, d), w_in.astype(BF16), pw=pw, c=fourier_w.shape[1],
                          bm=512, bn=1024)
    a = _pool_mixer(up.reshape(b, seq, pw), pool_w.astype(BF16), pool_scale.reshape(1, pw), ts=256)
    yf = _fourier_mixer(uf, _fourier_prep(fourier_w), b=b, rows=1024)
    h = _out_proj(a.reshape(n, pw), yf.reshape(n, mix - pw), w_out.astype(BF16), x2, bm=1024, bn=1024)

    rw = jnp.zeros((d, LANES), BF16).at[:, :n_exp].set(router_w.astype(BF16))
    rb = jnp.zeros((1, LANES), F32).at[0, :n_exp].set(router_b)
    z2p, logits = _norm_router(h, norm2_g.reshape(1, d), rw, rb, bm=256)

    pos, gates, texp, ntl, last, nxt = _route(logits, n_exp=n_exp, tm=tm, tmax=tmax)
    chunk = min(512, n)
    pos_c = pos.reshape(TOP_K, n // chunk, chunk).transpose(1, 0, 2)
    xs = _dispatch(last[0, :n_exp], pos_c, z2p, n_rows=tmax * tm, tm=tm)

    bgu = jnp.concatenate([b_gate, b_up], axis=-1).reshape(n_exp, 1, 2 * f)
    ys = _expert_ffn(texp[0, :tmax], ntl[0, :1], nxt[0, :n_exp], xs, w_gate, w_up, w_down, bgu,
                     b_down.reshape(n_exp, 1, d), tm=tm)

    bt = min(128, n)
    pos_b = pos.reshape(TOP_K, n // bt, bt).transpose(1, 0, 2)
    return pos_b, ys, h, gates.T


def kernel(x, norm1_g, w_in, pool_w, pool_scale, fourier_w, w_out, norm2_g, router_w, router_b,
           w_gate, b_gate, w_up, b_up, w_down, b_down, final_g):
    b, seq, d = x.shape
    assert w_in.shape[0] == 1, "only a single layer is supported"
    pos_b, ys, h, gates_t = _layer(
        x.reshape(b * seq, d), b, seq, norm1_g[0], w_in[0], pool_w[0], pool_scale[0], fourier_w[0], w_out[0],
        norm2_g[0], router_w[0], router_b[0], w_gate[0], b_gate[0], w_up[0], b_up[0], w_down[0], b_down[0])
    out = _combine(pos_b, ys, h, gates_t, final_g.reshape(1, d), bt=min(128, b * seq))
    return out.reshape(b, seq, d)
```

```python
import functools
import math

import jax
import jax.numpy as jnp
from jax import lax
from jax.experimental import pallas as pl
from jax.experimental.pallas import tpu as pltpu

F32 = jnp.float32
BF16 = jnp.bfloat16
I32 = jnp.int32
U32 = jnp.uint32

RMS_EPS = 1e-5
POOL_WINDOWS = (2, 4, 8, 16)
TOP_K = 4
SWIGLU_LIMIT = 7.0
SWIGLU_ALPHA = 1.702

LANES = 128
BF16_SUBLANES = 16
HALO = 16
NORM_ROWS = 16
NORM_UNROLL = 8
SUBLANES = 8
VMEM_LIMIT = 56 * 1024 * 1024
VMEM_LIMIT_HIGH = 60 * 1024 * 1024
HI_MASK = 0xFFFF0000


def _cparams(sem, vmem=VMEM_LIMIT):
    return pltpu.CompilerParams(dimension_semantics=sem, vmem_limit_bytes=vmem)


def _pack_halves(lo_f32, hi_f32):
    lo = lax.bitcast_convert_type(lo_f32, U32) >> 16
    hi = lax.bitcast_convert_type(hi_f32, U32) & jnp.uint32(HI_MASK)
    return lo | hi


def _unpack_halves(w):
    lo = lax.bitcast_convert_type(w << 16, F32)
    hi = lax.bitcast_convert_type(w & jnp.uint32(HI_MASK), F32)
    return lo, hi


def _inv_rms_rows(x_ref, r_ref):
    d = x_ref.shape[1]

    def body(i, carry):
        r = pl.multiple_of(i * SUBLANES, SUBLANES)
        xc = x_ref[pl.ds(r, SUBLANES), :]
        sq = xc * xc
        acc = sq[:, :LANES]
        for c0 in range(LANES, d, LANES):
            acc = acc + sq[:, c0:c0 + LANES]
        r_ref[pl.ds(r, SUBLANES), :] = acc
        return carry
    steps = x_ref.shape[0] // SUBLANES
    lax.fori_loop(0, steps, body, 0, unroll=min(NORM_UNROLL, steps))
    ms = jnp.sum(r_ref[...], axis=-1, keepdims=True) / d
    r_ref[...] = jnp.broadcast_to(lax.rsqrt(ms + RMS_EPS), r_ref.shape)


def _norm_mm_kernel(x_hbm, g_ref, w_ref, up_ref, uf_ref, x_buf, z_ref, r_ref, xsem, *, rows, pool_blocks, c):
    i = pl.program_id(0)
    j = pl.program_id(1)
    bm = x_buf.shape[0]

    def x_copy(blk):
        return pltpu.make_async_copy(x_hbm.at[pl.ds(pl.multiple_of(blk * bm, bm), bm)], x_buf, xsem)

    @pl.when(j == 0)
    def _():
        @pl.when(i == 0)
        def _():
            x_copy(0).start()
        x_copy(i).wait()
        _inv_rms_rows(x_buf, r_ref)

        def body(t, carry):
            r = pl.multiple_of(t * rows, rows)
            inv = r_ref[pl.ds(r, rows), :]
            for c0 in range(0, x_buf.shape[1], LANES):
                z = x_buf[pl.ds(r, rows), c0:c0 + LANES] * inv * g_ref[:, c0:c0 + LANES]
                z_ref[pl.ds(r, rows), c0:c0 + LANES] = z.astype(BF16)
            return carry
        lax.fori_loop(0, bm // rows, body, 0, unroll=2)

        @pl.when(i + 1 < pl.num_programs(0))
        def _():
            x_copy(i + 1).start(priority=1)

    y = jnp.dot(z_ref[...], w_ref[...], preferred_element_type=F32)

    @pl.when(j < pool_blocks)
    def _():
        up_ref[...] = y.astype(up_ref.dtype)

    @pl.when(j >= pool_blocks)
    def _():
        yr = y.astype(BF16).astype(F32)
        ch = c // 2
        for hd in range(y.shape[1] // c):
            words = _pack_halves(yr[:, hd * c:hd * c + ch], yr[:, hd * c + ch:(hd + 1) * c])
            for s in range(ch // LANES):
                uf_ref[hd * (ch // LANES) + s] = words[:, s * LANES:(s + 1) * LANES]


def _norm_matmul(x, g, w, *, pw, c, bm, bn):
    n, d = x.shape
    m = w.shape[1]
    bm, bn = min(bm, n), min(bn, pw, m - pw)
    assert pw % bn == 0 and (m - pw) % bn == 0 and bn % c == 0 and (c // 2) % LANES == 0
    pool_blocks = pw // bn
    slabs_blk = bn // 2 // LANES
    return pl.pallas_call(
        functools.partial(_norm_mm_kernel, rows=min(NORM_ROWS, bm), pool_blocks=pool_blocks, c=c),
        grid=(n // bm, m // bn),
        in_specs=[pl.BlockSpec(memory_space=pl.ANY),
                  pl.BlockSpec((1, d), lambda i, j: (0, 0)),
                  pl.BlockSpec((d, bn), lambda i, j: (0, j))],
        out_specs=[pl.BlockSpec((bm, bn), lambda i, j: (i, jnp.minimum(j, pool_blocks - 1))),
                   pl.BlockSpec((slabs_blk, bm, LANES), lambda i, j: (jnp.maximum(j - pool_blocks, 0), i, 0))],
        out_shape=[jax.ShapeDtypeStruct((n, pw), BF16),
                   jax.ShapeDtypeStruct(((m - pw) // 2 // LANES, n, LANES), U32)],
        scratch_shapes=[pltpu.VMEM((bm, d), F32), pltpu.VMEM((bm, d), BF16), pltpu.VMEM((bm, LANES), F32),
                        pltpu.SemaphoreType.DMA(())],
        compiler_params=_cparams(("arbitrary", "arbitrary"), vmem=VMEM_LIMIT_HIGH),
        name="norm1_w_in",
    )(x, g, w)


def _pool_kernel(cur_ref, prev_ref, next_ref, pw_ref, sc_ref, o_ref, *, seq, ts, c):
    t = pl.program_id(1)
    has_prev = (t > 0).astype(F32)
    has_next = (t < pl.num_programs(1) - 1).astype(F32)
    n_ext = ts + 2 * HALO
    tok = t * ts + lax.broadcasted_iota(I32, (ts, 1), 0)
    for g, w in enumerate(POOL_WINDOWS):
        sl = slice(g * c, (g + 1) * c)
        cur = cur_ref[0, :, sl].astype(F32)
        prev = prev_ref[0, :, sl].astype(F32) * has_prev
        nxt = next_ref[0, :, sl].astype(F32) * has_next
        ext = jnp.concatenate([prev, cur, nxt], axis=0)
        s = ext + pltpu.roll(ext, 1, 0)
        h = 1
        while 2 * h < w:
            s = pltpu.roll(s, h, 0) + pltpu.roll(s, n_ext - h, 0)
            h *= 2
        win = s[HALO:HALO + ts]
        lo = jnp.maximum(tok - w // 2, 0)
        hi = jnp.minimum(tok + w // 2 - 1, seq - 1)
        cnt = (hi - lo + 1).astype(F32)
        p = win / cnt - cur
        y = jnp.dot(p.astype(BF16), pw_ref[g], preferred_element_type=F32)
        o_ref[0, :, sl] = (y * sc_ref[:, sl]).astype(o_ref.dtype)


def _pool_mixer(u3, pool_w, pool_scale, *, ts):
    b, seq, _ = u3.shape
    g, c, _ = pool_w.shape
    pw = g * c
    ts = min(ts, seq)
    nh = seq // HALO
    per = ts // HALO
    return pl.pallas_call(
        functools.partial(_pool_kernel, seq=seq, ts=ts, c=c),
        grid=(b, seq // ts),
        in_specs=[pl.BlockSpec((1, ts, pw), lambda i, t: (i, t, 0)),
                  pl.BlockSpec((1, HALO, pw), lambda i, t: (i, jnp.maximum(t * per - 1, 0), 0)),
                  pl.BlockSpec((1, HALO, pw), lambda i, t: (i, jnp.minimum((t + 1) * per, nh - 1), 0)),
                  pl.BlockSpec((g, c, c), lambda i, t: (0, 0, 0)),
                  pl.BlockSpec((1, pw), lambda i, t: (0, 0))],
        out_specs=pl.BlockSpec((1, ts, pw), lambda i, t: (i, t, 0)),
        out_shape=jax.ShapeDtypeStruct((b, seq, pw), BF16),
        compiler_params=_cparams(("parallel", "parallel")),
        name="pool_mixer",
    )(u3, u3, u3, pool_w, pool_scale)


DFT_N1 = 256


def _dft_mats(n, scale, dtype):
    j = jnp.arange(n, dtype=I32)
    ang = ((j[:, None] * j[None, :]) % n).astype(F32) * (2.0 * math.pi / n)
    return (jnp.cos(ang) * scale).astype(dtype), (jnp.sin(ang) * scale).astype(dtype)


def _fourier_prep_kernel(cc_ref, sc_ref, w_ref, o_ref, *, c):
    w = w_ref[0]
    o_ref[0, :c, :] = jnp.dot(cc_ref[...], w, preferred_element_type=F32,
                              precision=lax.Precision.HIGHEST).astype(o_ref.dtype)
    o_ref[0, c:, :] = jnp.dot(sc_ref[...], w, preferred_element_type=F32,
                              precision=lax.Precision.HIGHEST).astype(o_ref.dtype)


def _fourier_prep(fourier_w):
    h, c, _ = fourier_w.shape
    cc, sc = _dft_mats(c, c ** -0.5, F32)
    return pl.pallas_call(
        functools.partial(_fourier_prep_kernel, c=c),
        grid=(h,),
        in_specs=[pl.BlockSpec((c, c), lambda i: (0, 0)),
                  pl.BlockSpec((c, c), lambda i: (0, 0)),
                  pl.BlockSpec((1, c, c), lambda i: (i, 0, 0))],
        out_specs=pl.BlockSpec((1, 2 * c, c), lambda i: (i, 0, 0)),
        out_shape=jax.ShapeDtypeStruct((h, 2 * c, c), BF16),
        compiler_params=_cparams(("parallel",)),
        name="fourier_prep",
    )(cc, sc, fourier_w)


def _fft(xs):
    n = len(xs)
    if n == 1:
        return xs
    ev, od = _fft(xs[0::2]), _fft(xs[1::2])
    out = [None] * n
    for k in range(n // 2):
        re, im = od[k]
        if k == 0:
            tr, ti = re, im
        elif 4 * k == n:
            tr, ti = im, -re
        else:
            wr, wi = math.cos(2.0 * math.pi * k / n), -math.sin(2.0 * math.pi * k / n)
            tr, ti = re * wr - im * wi, re * wi + im * wr
        er, ei = ev[k]
        out[k] = (er + tr, ei + ti)
        out[k + n // 2] = (er - tr, ei - ti)
    return out


def _fourier_kernel(x_ref, cs_ref, tw_ref, ab_ref, o_ref, y_ref, *, n1, n2, c, rows):
    for j in range(n2):
        halves = [_unpack_halves(x_ref[s, pl.ds(j, n1, stride=n2), :]) for s in range(x_ref.shape[0])]
        xj = jnp.concatenate([lo for lo, _ in halves] + [hi for _, hi in halves], axis=1).astype(BF16)
        y = jnp.dot(cs_ref[...], xj, preferred_element_type=F32)
        y_ref[0, j] = y[:n1]
        y_ref[1, j] = y[n1:]

    nl = c // LANES

    def tile(i, carry):
        r = pl.multiple_of((i // nl) * SUBLANES, SUBLANES)
        l = pl.multiple_of((i % nl) * LANES, LANES)
        zs = []
        for j in range(n2):
            yc = y_ref[0, j, pl.ds(r, SUBLANES), pl.ds(l, LANES)]
            ys = y_ref[1, j, pl.ds(r, SUBLANES), pl.ds(l, LANES)]
            ct = tw_ref[0, j, pl.ds(r, SUBLANES), :]
            st = tw_ref[1, j, pl.ds(r, SUBLANES), :]
            zs.append((yc * ct - ys * st, -(ys * ct + yc * st)))
        gs = _fft(zs)
        for k in range(n2):
            y_ref[0, k, pl.ds(r, SUBLANES), pl.ds(l, LANES)] = gs[k][0]
            y_ref[1, k, pl.ds(r, SUBLANES), pl.ds(l, LANES)] = gs[k][1]
        return carry
    lax.fori_loop(0, (n1 // SUBLANES) * nl, tile, 0, unroll=2)

    per = max(1, min(n2, rows // n1))
    for s in range(0, n2, per):
        gr = y_ref[0, s:s + per].reshape(per * n1, c).astype(BF16)
        gi = y_ref[1, s:s + per].reshape(per * n1, c).astype(BF16)
        y = jnp.dot(gr, ab_ref[0, :c, :], preferred_element_type=F32)
        y = y + jnp.dot(gi, ab_ref[0, c:, :], preferred_element_type=F32)
        o_ref[0, s * n1:(s + per) * n1, :] = y.astype(o_ref.dtype)


def _fourier_mixer(uf, ab, *, b, rows):
    slabs, n, _ = uf.shape
    seq = n // b
    h, _, c = ab.shape
    spb = slabs // h
    n1 = min(DFT_N1, seq)
    n2 = seq // n1
    assert n1 * n2 == seq and n2 & (n2 - 1) == 0, "sequence length must be N1 * 2^m"
    cmat, smat = _dft_mats(n1, 1.0, BF16)
    cs = jnp.concatenate([cmat, smat], axis=0)
    ang = (jnp.arange(n2, dtype=I32)[:, None] * jnp.arange(n1, dtype=I32)[None, :]).astype(F32)
    ang = ang * (2.0 * math.pi / seq)
    tw = jnp.stack([jnp.cos(ang), jnp.sin(ang)]) * (seq ** -0.5)
    tw = jnp.broadcast_to(tw[..., None], (2, n2, n1, LANES))

    return pl.pallas_call(
        functools.partial(_fourier_kernel, n1=n1, n2=n2, c=c, rows=rows),
        grid=(b, h),
        in_specs=[
            pl.BlockSpec((spb, None, seq, LANES), lambda i, k: (k, i, 0, 0)),
            pl.BlockSpec((2 * n1, n1), lambda i, k: (0, 0), pipeline_mode=pl.Buffered(1)),
            pl.BlockSpec((2, n2, n1, LANES), lambda i, k: (0, 0, 0, 0), pipeline_mode=pl.Buffered(1)),
            pl.BlockSpec((1, 2 * c, c), lambda i, k: (k, 0, 0))],
        out_specs=pl.BlockSpec((1, seq, c), lambda i, k: (i, 0, k)),
        out_shape=jax.ShapeDtypeStruct((b, seq, h * c), BF16),
        scratch_shapes=[pltpu.VMEM((2, n2, n1, c), F32)],
        compiler_params=_cparams(("parallel", "parallel")),
        name="fourier_mixer",
    )(uf.reshape(slabs, b, seq, LANES), cs, tw, ab)


def _out_proj_kernel(a_ref, f_ref, w_ref, x_ref, o_ref, *, ka):
    y = jnp.dot(a_ref[...], w_ref[:ka, :], preferred_element_type=F32)
    y = y + jnp.dot(f_ref[...], w_ref[ka:, :], preferred_element_type=F32)
    o_ref[...] = x_ref[...] + y


def _out_proj(a, f, w, x, *, bm, bn):
    n, ka = a.shape
    kf = f.shape[1]
    d = w.shape[1]
    bm, bn = min(bm, n), min(bn, d)
    return pl.pallas_call(
        functools.partial(_out_proj_kernel, ka=ka),
        grid=(n // bm, d // bn),
        in_specs=[pl.BlockSpec((bm, ka), lambda i, j: (i, 0)),
                  pl.BlockSpec((bm, kf), lambda i, j: (i, 0)),
                  pl.BlockSpec((ka + kf, bn), lambda i, j: (0, j)),
                  pl.BlockSpec((bm, bn), lambda i, j: (i, j))],
        out_specs=pl.BlockSpec((bm, bn), lambda i, j: (i, j)),
        out_shape=jax.ShapeDtypeStruct((n, d), F32),
        compiler_params=_cparams(("parallel", "parallel")),
        name="w_out_residual",
    )(a, f, w, x)


def _norm_router_kernel(h_ref, g_ref, rw_ref, rb_ref, zp_ref, lg_ref, z_ref, r_ref, *, rows):
    dh = h_ref.shape[1] // 2
    _inv_rms_rows(h_ref, r_ref)

    def body(i, carry):
        r = pl.multiple_of(i * rows, rows)
        inv = r_ref[pl.ds(r, rows), :]
        for c0 in range(0, dh, LANES):
            lo = (h_ref[pl.ds(r, rows), c0:c0 + LANES] * inv * g_ref[:, c0:c0 + LANES]).astype(BF16)
            hi = (h_ref[pl.ds(r, rows), dh + c0:dh + c0 + LANES] * inv
                  * g_ref[:, dh + c0:dh + c0 + LANES]).astype(BF16)
            z_ref[pl.ds(r, rows), c0:c0 + LANES] = lo
            z_ref[pl.ds(r, rows), dh + c0:dh + c0 + LANES] = hi
            zp_ref[pl.ds(r, rows), c0:c0 + LANES] = _pack_halves(lo.astype(F32), hi.astype(F32))
        return carry
    lax.fori_loop(0, h_ref.shape[0] // rows, body, 0, unroll=2)
    lg_ref[...] = jnp.dot(z_ref[...], rw_ref[...], preferred_element_type=F32) + rb_ref[...]


def _norm_router(h, g, rw, rb, *, bm):
    n, d = h.shape
    bm = min(bm, n)
    return pl.pallas_call(
        functools.partial(_norm_router_kernel, rows=min(NORM_ROWS, bm)),
        grid=(n // bm,),
        in_specs=[pl.BlockSpec((bm, d), lambda i: (i, 0)),
                  pl.BlockSpec((1, d), lambda i: (0, 0)),
                  pl.BlockSpec((d, LANES), lambda i: (0, 0)),
                  pl.BlockSpec((1, LANES), lambda i: (0, 0))],
        out_specs=[pl.BlockSpec((bm, d // 2), lambda i: (i, 0)),
                   pl.BlockSpec((bm, LANES), lambda i: (i, 0))],
        out_shape=[jax.ShapeDtypeStruct((n, d // 2), U32),
                   jax.ShapeDtypeStruct((n, LANES), F32)],
        scratch_shapes=[pltpu.VMEM((bm, d), BF16), pltpu.VMEM((bm, LANES), F32)],
        compiler_params=_cparams(("parallel",)),
        name="norm2_router",
    )(h, g, rw, rb)


def _route_kernel(lg_ref, pos_ref, gate_ref, texp_ref, ntl_ref, last_ref, nxt_ref, idx_s, rank_s,
                  *, n_tok, n_exp, tm, tb, tmax_pad):
    shift = tm.bit_length() - 1
    iota_e = lax.broadcasted_iota(I32, (n_exp, tb), 0)
    before = (lax.broadcasted_iota(I32, (tb, tb), 0)
              < lax.broadcasted_iota(I32, (tb, tb), 1)).astype(BF16)

    def pass1(i, counts):
        off = pl.multiple_of(i * tb, tb)
        l = lg_ref[pl.ds(off, tb), :].T[:n_exp, :]
        vals, hots = [], []
        for k in range(TOP_K):
            m = jnp.max(l, axis=0, keepdims=True)
            idx = jnp.min(jnp.where(l == m, iota_e, n_exp), axis=0, keepdims=True)
            hot = iota_e == idx
            l = jnp.where(hot, -jnp.inf, l)
            vals.append(m)
            hots.append(hot)
            idx_s[k:k + 1, pl.ds(off, tb)] = idx
        exps = [jnp.exp(v - vals[0]) for v in vals]
        tot = exps[0]
        for e in exps[1:]:
            tot = tot + e
        sel = hots[0].astype(F32)
        for hot in hots[1:]:
            sel = sel + hot.astype(F32)
        rank = jnp.dot(sel.astype(BF16), before, preferred_element_type=F32) + counts
        for k in range(TOP_K):
            gate_ref[k:k + 1, pl.ds(off, tb)] = exps[k] / tot
            rank_s[k:k + 1, pl.ds(off, tb)] = jnp.sum(jnp.where(hots[k], rank, 0.0), axis=0, keepdims=True)
        return counts + jnp.sum(sel, axis=1, keepdims=True)

    counts = lax.fori_loop(0, n_tok // tb, pass1, jnp.zeros((n_exp, 1), F32))
    ntile = (counts.astype(I32) + (tm - 1)) >> shift
    sub = lax.broadcasted_iota(I32, (n_exp, LANES), 0)
    lane = lax.broadcasted_iota(I32, (n_exp, LANES), 1)
    ntile_row = jnp.sum(jnp.where(sub == lane, ntile, 0), axis=0, keepdims=True)
    start = jnp.sum(jnp.where(lane < sub, ntile_row, 0), axis=1, keepdims=True)
    t_lane = lax.broadcasted_iota(I32, (n_exp, tmax_pad), 1)
    texp_ref[...] = jnp.sum((start <= t_lane).astype(I32), axis=0, keepdims=True) - 1
    ntl_ref[...] = jnp.sum(ntile, axis=0, keepdims=True) + jnp.zeros((1, LANES), I32)
    last = jnp.where(ntile > 0, (start + ntile - 1) << shift, -1)
    last_ref[...] = jnp.sum(jnp.where(sub == lane, last, 0), axis=0, keepdims=True)
    nxt = jnp.min(jnp.where((lane > sub) & (ntile_row > 0), lane, n_exp), axis=1, keepdims=True)
    nxt_ref[...] = jnp.sum(jnp.where(sub == lane, nxt, 0), axis=0, keepdims=True)
    start_rows = start << shift

    def pass2(i, carry):
        off = pl.multiple_of(i * tb, tb)
        for k in range(TOP_K):
            hot = iota_e == idx_s[k:k + 1, pl.ds(off, tb)]
            base = jnp.sum(jnp.where(hot, start_rows, 0), axis=0, keepdims=True)
            pos_ref[k:k + 1, pl.ds(off, tb)] = rank_s[k:k + 1, pl.ds(off, tb)].astype(I32) + base
        return carry
    lax.fori_loop(0, n_tok // tb, pass2, 0)


def _route(logits, *, n_exp, tm, tmax):
    n_tok = logits.shape[0]
    tb = min(256, n_tok)
    tmax_pad = -(-tmax // LANES) * LANES
    return pl.pallas_call(
        functools.partial(_route_kernel, n_tok=n_tok, n_exp=n_exp, tm=tm, tb=tb, tmax_pad=tmax_pad),
        out_shape=[jax.ShapeDtypeStruct((TOP_K, n_tok), I32),
                   jax.ShapeDtypeStruct((TOP_K, n_tok), F32),
                   jax.ShapeDtypeStruct((1, tmax_pad), I32),
                   jax.ShapeDtypeStruct((1, LANES), I32),
                   jax.ShapeDtypeStruct((1, LANES), I32),
                   jax.ShapeDtypeStruct((1, LANES), I32)],
        scratch_shapes=[pltpu.VMEM((TOP_K, n_tok), I32), pltpu.VMEM((TOP_K, n_tok), F32)],
        compiler_params=_cparams(None),
        name="route",
    )(logits)


def _dispatch_kernel(last_ref, pos_ref, z_ref, xs_hbm, zero_v, zsem, sem, *, n_exp, tm, chunk):
    @pl.when(pl.program_id(0) == 0)
    def _():
        zero_v[...] = jnp.zeros_like(zero_v)
        for e in range(n_exp):
            r = last_ref[e]

            @pl.when(r >= 0)
            def _():
                cp = pltpu.make_async_copy(zero_v, xs_hbm.at[pl.ds(pl.multiple_of(r, tm), tm)], zsem)
                cp.start()
                cp.wait()

    for i in range(chunk):
        for k in range(TOP_K):
            pltpu.make_async_copy(z_ref.at[pl.ds(i, 1)], xs_hbm.at[pl.ds(pos_ref[0, k, i], 1)],
                                  sem).start(priority=k % 2)
    for k in range(TOP_K):
        pltpu.make_async_copy(z_ref, xs_hbm.at[pl.ds(0, chunk)], sem).wait()


def _dispatch(last, pos3, z2p, *, n_rows, tm):
    n_tok, dh = z2p.shape
    nchunk, _, chunk = pos3.shape
    n_exp = last.shape[0]
    grid_spec = pltpu.PrefetchScalarGridSpec(
        num_scalar_prefetch=1,
        grid=(nchunk,),
        in_specs=[pl.BlockSpec((1, TOP_K, chunk), lambda c, last: (c, 0, 0), memory_space=pltpu.SMEM),
                  pl.BlockSpec((chunk, dh), lambda c, last: (c, 0))],
        out_specs=pl.BlockSpec(memory_space=pl.ANY),
        scratch_shapes=[pltpu.VMEM((tm, dh), U32), pltpu.SemaphoreType.DMA(()), pltpu.SemaphoreType.DMA(())],
    )
    return pl.pallas_call(
        functools.partial(_dispatch_kernel, n_exp=n_exp, tm=tm, chunk=chunk),
        grid_spec=grid_spec,
        out_shape=jax.ShapeDtypeStruct((n_rows, dh), U32),
        compiler_params=_cparams(("arbitrary",)),
        name="dispatch",
    )(last, pos3, z2p)


CAST_ROWS = 256


def _ffn_kernel(texp_ref, ntl_ref, nxt_ref, x_ref, wg_hbm, wu_hbm, wd_hbm, bgu_ref, bd_ref, o_ref,
                stage_g, stage_u, stage_d, wgu, wd, sems, *, f, n_exp, rows):
    t = pl.program_id(0)
    e = texp_ref[t]
    dh = x_ref.shape[1]
    valid = t < ntl_ref[0]
    first = jnp.logical_or(t == 0, e != texp_ref[jnp.maximum(t - 1, 0)])

    def weight_copies(ex):
        return (pltpu.make_async_copy(wg_hbm.at[ex], stage_g, sems.at[0]),
                pltpu.make_async_copy(wu_hbm.at[ex], stage_u, sems.at[1]),
                pltpu.make_async_copy(wd_hbm.at[ex], stage_d, sems.at[2]))

    @pl.when(t == 0)
    def _():
        for cp in weight_copies(e):
            cp.start()

    @pl.when(jnp.logical_and(valid, first))
    def _():
        for cp in weight_copies(e):
            cp.wait()

        def cast_up(i, carry):
            r = pl.multiple_of(i * rows, rows)
            wgu[pl.ds(r, rows), :f] = stage_g[pl.ds(r, rows), :].astype(BF16)
            wgu[pl.ds(r, rows), f:] = stage_u[pl.ds(r, rows), :].astype(BF16)
            return carry
        lax.fori_loop(0, stage_g.shape[0] // rows, cast_up, 0)

        def cast_down(i, carry):
            r = pl.multiple_of(i * rows, rows)
            wd[pl.ds(r, rows), :] = stage_d[pl.ds(r, rows), :].astype(BF16)
            return carry
        lax.fori_loop(0, stage_d.shape[0] // rows, cast_down, 0)

        nx = nxt_ref[e]

        @pl.when(nx < n_exp)
        def _():
            for cp in weight_copies(nx):
                cp.start(priority=1)

    @pl.when(valid)
    def _():
        lo, hi = _unpack_halves(x_ref[...])
        hgu = jnp.dot(lo.astype(BF16), wgu[:dh, :], preferred_element_type=F32)
        hgu = hgu + jnp.dot(hi.astype(BF16), wgu[dh:, :], preferred_element_type=F32)
        hgu = hgu + bgu_ref[0]
        hg = jnp.minimum(hgu[:, :f], SWIGLU_LIMIT)
        hu = jnp.clip(hgu[:, f:], -SWIGLU_LIMIT, SWIGLU_LIMIT)
        act = hg * (1.0 / (1.0 + jnp.exp(-SWIGLU_ALPHA * hg))) * (hu + 1.0)
        y = jnp.dot(act.astype(BF16), wd[...], preferred_element_type=F32) + bd_ref[0]
        yr = y.astype(BF16).astype(F32)
        o_ref[...] = _pack_halves(yr[:, :dh], yr[:, dh:])


def _expert_ffn(texp, ntl, nxt, xs, w_gate, w_up, w_down, bgu, bd, *, tm):
    n_rows, dh = xs.shape
    n_exp, d, f = w_gate.shape
    tmax = n_rows // tm
    rows = min(CAST_ROWS, f)
    assert d % rows == 0 and f % rows == 0

    def x_map(t, te, nt, nx):
        return (jnp.minimum(t, nt[0] - 1), 0)

    def b_map(t, te, nt, nx):
        return (te[t], 0, 0)

    grid_spec = pltpu.PrefetchScalarGridSpec(
        num_scalar_prefetch=3,
        grid=(tmax,),
        in_specs=[pl.BlockSpec((tm, dh), x_map),
                  pl.BlockSpec(memory_space=pl.ANY),
                  pl.BlockSpec(memory_space=pl.ANY),
                  pl.BlockSpec(memory_space=pl.ANY),
                  pl.BlockSpec((1, 1, 2 * f), b_map),
                  pl.BlockSpec((1, 1, d), b_map)],
        out_specs=pl.BlockSpec((tm, dh), lambda t, te, nt, nx: (t, 0)),
        scratch_shapes=[pltpu.VMEM((d, f), F32), pltpu.VMEM((d, f), F32), pltpu.VMEM((f, d), F32),
                        pltpu.VMEM((d, 2 * f), BF16), pltpu.VMEM((f, d), BF16),
                        pltpu.SemaphoreType.DMA((3,))],
    )
    return pl.pallas_call(
        functools.partial(_ffn_kernel, f=f, n_exp=n_exp, rows=rows),
        grid_spec=grid_spec,
        out_shape=jax.ShapeDtypeStruct((n_rows, dh), U32),
        compiler_params=_cparams(("arbitrary",)),
        name="expert_ffn",
    )(texp, ntl, nxt, xs, w_gate, w_up, w_down, bgu, bd)


COMBINE_ROWS = 16


def _combine_kernel(pos_c, pos_n, ys_hbm, h_ref, g_ref, fg_ref, o_ref, buf_a, buf_b, sem, *, bt, rows):
    i = pl.program_id(0)
    dh = buf_a.shape[-1]

    def start_rows(pos_ref, dst, s, r0):
        for r in range(rows):
            for k in range(TOP_K):
                pltpu.make_async_copy(ys_hbm.at[pl.ds(pos_ref[0, k, r0 + r], 1)],
                                      dst.at[k, pl.ds(r0 + r, 1)], sem.at[s]).start(priority=k % 2)

    def wait_block(dst, s):
        for k in range(TOP_K):
            pltpu.make_async_copy(ys_hbm.at[pl.ds(0, bt)], dst.at[k], sem.at[s]).wait()

    @pl.when(i == 0)
    def _():
        def first(c, carry):
            start_rows(pos_c, buf_a, 0, pl.multiple_of(c * rows, rows))
            return carry
        lax.fori_loop(0, bt // rows, first, 0)

    def step(cur, s_cur, nxt, s_nxt):
        wait_block(cur, s_cur)
        for r0 in range(0, bt, rows):
            start_rows(pos_n, nxt, s_nxt, r0)
            acc_lo = h_ref[r0:r0 + rows, :dh]
            acc_hi = h_ref[r0:r0 + rows, dh:]
            for k in range(TOP_K):
                lo, hi = _unpack_halves(cur[k, r0:r0 + rows, :])
                gk = g_ref[r0:r0 + rows, k:k + 1]
                acc_lo = acc_lo + gk * lo
                acc_hi = acc_hi + gk * hi
            ms = (jnp.sum(acc_lo * acc_lo, axis=-1, keepdims=True)
                  + jnp.sum(acc_hi * acc_hi, axis=-1, keepdims=True)) / (2 * dh)
            inv = lax.rsqrt(ms + RMS_EPS)
            o_ref[r0:r0 + rows, :dh] = acc_lo * inv * fg_ref[:, :dh]
            o_ref[r0:r0 + rows, dh:] = acc_hi * inv * fg_ref[:, dh:]

        @pl.when(i == pl.num_programs(0) - 1)
        def _():
            wait_block(nxt, s_nxt)

    @pl.when(i % 2 == 0)
    def _():
        step(buf_a, 0, buf_b, 1)

    @pl.when(i % 2 == 1)
    def _():
        step(buf_b, 1, buf_a, 0)


def _combine(pos3, ys, h, gates_t, fg, *, bt):
    n, d = h.shape
    nblk = n // bt
    return pl.pallas_call(
        functools.partial(_combine_kernel, bt=bt, rows=min(COMBINE_ROWS, bt)),
        grid=(nblk,),
        in_specs=[pl.BlockSpec((1, TOP_K, bt), lambda i: (i, 0, 0), memory_space=pltpu.SMEM),
                  pl.BlockSpec((1, TOP_K, bt), lambda i: (jnp.minimum(i + 1, nblk - 1), 0, 0),
                               memory_space=pltpu.SMEM),
                  pl.BlockSpec(memory_space=pl.ANY),
                  pl.BlockSpec((bt, d), lambda i: (i, 0)),
                  pl.BlockSpec((bt, TOP_K), lambda i: (i, 0)),
                  pl.BlockSpec((1, d), lambda i: (0, 0))],
        out_specs=pl.BlockSpec((bt, d), lambda i: (i, 0)),
        out_shape=jax.ShapeDtypeStruct((n, d), F32),
        scratch_shapes=[pltpu.VMEM((TOP_K, bt, d // 2), U32), pltpu.VMEM((TOP_K, bt, d // 2), U32),
                        pltpu.SemaphoreType.DMA((2,))],
        compiler_params=_cparams(("arbitrary",)),
        name="combine_final_norm",
    )(pos3, pos3, ys, h, gates_t, fg)


def _layer(x2, b, seq, norm1_g, w_in, pool_w, pool_scale, fourier_w, w_out, norm2_g,
           router_w, router_b, w_gate, b_gate, w_up, b_up, w_down, b_down):
    n, d = x2.shape
    mix = w_in.shape[1]
    pw = pool_scale.shape[0]
    n_exp = router_w.shape[1]
    f = w_gate.shape[2]
    tm = min(256, n)
    tmax = (n * TOP_K) // tm + n_exp

    up, uf = _norm_matmul(x2, norm1_g.reshape(1, d), w_in.astype(BF16), pw=pw, c=fourier_w.shape[1],
                          bm=1024, bn=1024)
    a = _pool_mixer(up.reshape(b, seq, pw), pool_w.astype(BF16), pool_scale.reshape(1, pw), ts=256)
    yf = _fourier_mixer(uf, _fourier_prep(fourier_w), b=b, rows=1024)
    h = _out_proj(a.reshape(n, pw), yf.reshape(n, mix - pw), w_out.astype(BF16), x2, bm=1024, bn=1024)

    rw = jnp.zeros((d, LANES), BF16).at[:, :n_exp].set(router_w.astype(BF16))
    rb = jnp.zeros((1, LANES), F32).at[0, :n_exp].set(router_b)
    z2p, logits = _norm_router(h, norm2_g.reshape(1, d), rw, rb, bm=256)

    pos, gates, texp, ntl, last, nxt = _route(logits, n_exp=n_exp, tm=tm, tmax=tmax)
    chunk = min(512, n)
    pos_c = pos.reshape(TOP_K, n // chunk, chunk).transpose(1, 0, 2)
    xs = _dispatch(last[0, :n_exp], pos_c, z2p, n_rows=tmax * tm, tm=tm)

    bgu = jnp.concatenate([b_gate, b_up], axis=-1).reshape(n_exp, 1, 2 * f)
    ys = _expert_ffn(texp[0, :tmax], ntl[0, :1], nxt[0, :n_exp], xs, w_gate, w_up, w_down, bgu,
                     b_down.reshape(n_exp, 1, d), tm=tm)

    bt = min(256, n)
    pos_b = pos.reshape(TOP_K, n // bt, bt).transpose(1, 0, 2)
    return pos_b, ys, h, gates.T


def kernel(x, norm1_g, w_in, pool_w, pool_scale, fourier_w, w_out, norm2_g, router_w, router_b,
           w_gate, b_gate, w_up, b_up, w_down, b_down, final_g):
    b, seq, d = x.shape
    assert w_in.shape[0] == 1, "only a single layer is supported"
    pos_b, ys, h, gates_t = _layer(
        x.reshape(b * seq, d), b, seq, norm1_g[0], w_in[0], pool_w[0], pool_scale[0], fourier_w[0], w_out[0],
        norm2_g[0], router_w[0], router_b[0], w_gate[0], b_gate[0], w_up[0], b_up[0], w_down[0], b_down[0])
    out = _combine(pos_b, ys, h, gates_t, final_g.reshape(1, d), bt=min(256, b * seq))
    return out.reshape(b, seq, d)
```

```python
import functools
import math

import jax
import jax.numpy as jnp
from jax import lax
from jax.experimental import pallas as pl
from jax.experimental.pallas import tpu as pltpu

F32 = jnp.float32
BF16 = jnp.bfloat16
I32 = jnp.int32
U32 = jnp.uint32

RMS_EPS = 1e-5
POOL_WINDOWS = (2, 4, 8, 16)
TOP_K = 4
SWIGLU_LIMIT = 7.0
SWIGLU_ALPHA = 1.702

LANES = 128
BF16_SUBLANES = 16
HALO = 16
NORM_ROWS = 16
NORM_UNROLL = 8
SUBLANES = 8
VMEM_LIMIT = 56 * 1024 * 1024
VMEM_LIMIT_HIGH = 60 * 1024 * 1024
HI_MASK = 0xFFFF0000


def _cparams(sem, vmem=VMEM_LIMIT):
    return pltpu.CompilerParams(dimension_semantics=sem, vmem_limit_bytes=vmem)


def _pack_halves(lo_f32, hi_f32):
    lo = lax.bitcast_convert_type(lo_f32, U32) >> 16
    hi = lax.bitcast_convert_type(hi_f32, U32) & jnp.uint32(HI_MASK)
    return lo | hi


def _unpack_halves(w):
    lo = lax.bitcast_convert_type(w << 16, F32)
    hi = lax.bitcast_convert_type(w & jnp.uint32(HI_MASK), F32)
    return lo, hi


def _inv_rms_rows(x_ref, r_ref):
    d = x_ref.shape[1]

    def body(i, carry):
        r = pl.multiple_of(i * SUBLANES, SUBLANES)
        xc = x_ref[pl.ds(r, SUBLANES), :]
        sq = xc * xc
        acc = sq[:, :LANES]
        for c0 in range(LANES, d, LANES):
            acc = acc + sq[:, c0:c0 + LANES]
        r_ref[pl.ds(r, SUBLANES), :] = acc
        return carry
    steps = x_ref.shape[0] // SUBLANES
    lax.fori_loop(0, steps, body, 0, unroll=min(NORM_UNROLL, steps))
    ms = jnp.sum(r_ref[...], axis=-1, keepdims=True) / d
    r_ref[...] = jnp.broadcast_to(lax.rsqrt(ms + RMS_EPS), r_ref.shape)


def _norm_mm_kernel(x_hbm, g_ref, w_ref, up_ref, uf_ref, x_buf, z_ref, r_ref, xsem, *, rows, pool_blocks, c):
    i = pl.program_id(0)
    j = pl.program_id(1)
    bm = x_buf.shape[0]

    def x_copy(blk):
        return pltpu.make_async_copy(x_hbm.at[pl.ds(pl.multiple_of(blk * bm, bm), bm)], x_buf, xsem)

    @pl.when(j == 0)
    def _():
        @pl.when(i == 0)
        def _():
            x_copy(0).start()
        x_copy(i).wait()
        _inv_rms_rows(x_buf, r_ref)

        def body(t, carry):
            r = pl.multiple_of(t * rows, rows)
            inv = r_ref[pl.ds(r, rows), :]
            for c0 in range(0, x_buf.shape[1], LANES):
                z = x_buf[pl.ds(r, rows), c0:c0 + LANES] * inv * g_ref[:, c0:c0 + LANES]
                z_ref[pl.ds(r, rows), c0:c0 + LANES] = z.astype(BF16)
            return carry
        lax.fori_loop(0, bm // rows, body, 0, unroll=2)

        @pl.when(i + 1 < pl.num_programs(0))
        def _():
            x_copy(i + 1).start(priority=1)

    y = jnp.dot(z_ref[...], w_ref[...], preferred_element_type=F32)

    @pl.when(j < pool_blocks)
    def _():
        up_ref[...] = y.astype(up_ref.dtype)

    @pl.when(j >= pool_blocks)
    def _():
        yr = y.astype(BF16).astype(F32)
        ch = c // 2
        for hd in range(y.shape[1] // c):
            words = _pack_halves(yr[:, hd * c:hd * c + ch], yr[:, hd * c + ch:(hd + 1) * c])
            for s in range(ch // LANES):
                uf_ref[hd * (ch // LANES) + s] = words[:, s * LANES:(s + 1) * LANES]


def _norm_matmul(x, g, w, *, pw, c, bm, bn):
    n, d = x.shape
    m = w.shape[1]
    bm, bn = min(bm, n), min(bn, pw, m - pw)
    assert pw % bn == 0 and (m - pw) % bn == 0 and bn % c == 0 and (c // 2) % LANES == 0
    pool_blocks = pw // bn
    slabs_blk = bn // 2 // LANES
    return pl.pallas_call(
        functools.partial(_norm_mm_kernel, rows=min(NORM_ROWS, bm), pool_blocks=pool_blocks, c=c),
        grid=(n // bm, m // bn),
        in_specs=[pl.BlockSpec(memory_space=pl.ANY),
                  pl.BlockSpec((1, d), lambda i, j: (0, 0)),
                  pl.BlockSpec((d, bn), lambda i, j: (0, j))],
        out_specs=[pl.BlockSpec((bm, bn), lambda i, j: (i, jnp.minimum(j, pool_blocks - 1))),
                   pl.BlockSpec((slabs_blk, bm, LANES), lambda i, j: (jnp.maximum(j - pool_blocks, 0), i, 0))],
        out_shape=[jax.ShapeDtypeStruct((n, pw), BF16),
                   jax.ShapeDtypeStruct(((m - pw) // 2 // LANES, n, LANES), U32)],
        scratch_shapes=[pltpu.VMEM((bm, d), F32), pltpu.VMEM((bm, d), BF16), pltpu.VMEM((bm, LANES), F32),
                        pltpu.SemaphoreType.DMA(())],
        compiler_params=_cparams(("arbitrary", "arbitrary"), vmem=VMEM_LIMIT_HIGH),
        name="norm1_w_in",
    )(x, g, w)


def _pool_kernel(cur_ref, prev_ref, next_ref, pw_ref, sc_ref, o_ref, *, seq, ts, c):
    t = pl.program_id(1)
    has_prev = (t > 0).astype(F32)
    has_next = (t < pl.num_programs(1) - 1).astype(F32)
    n_ext = ts + 2 * HALO
    tok = t * ts + lax.broadcasted_iota(I32, (ts, 1), 0)
    for g, w in enumerate(POOL_WINDOWS):
        sl = slice(g * c, (g + 1) * c)
        cur = cur_ref[0, :, sl].astype(F32)
        prev = prev_ref[0, :, sl].astype(F32) * has_prev
        nxt = next_ref[0, :, sl].astype(F32) * has_next
        ext = jnp.concatenate([prev, cur, nxt], axis=0)
        s = ext + pltpu.roll(ext, 1, 0)
        h = 1
        while 2 * h < w:
            s = pltpu.roll(s, h, 0) + pltpu.roll(s, n_ext - h, 0)
            h *= 2
        win = s[HALO:HALO + ts]
        lo = jnp.maximum(tok - w // 2, 0)
        hi = jnp.minimum(tok + w // 2 - 1, seq - 1)
        cnt = (hi - lo + 1).astype(F32)
        p = win / cnt - cur
        y = jnp.dot(p.astype(BF16), pw_ref[g], preferred_element_type=F32)
        o_ref[0, :, sl] = (y * sc_ref[:, sl]).astype(o_ref.dtype)


def _pool_mixer(u3, pool_w, pool_scale, *, ts):
    b, seq, _ = u3.shape
    g, c, _ = pool_w.shape
    pw = g * c
    ts = min(ts, seq)
    nh = seq // HALO
    per = ts // HALO
    return pl.pallas_call(
        functools.partial(_pool_kernel, seq=seq, ts=ts, c=c),
        grid=(b, seq // ts),
        in_specs=[pl.BlockSpec((1, ts, pw), lambda i, t: (i, t, 0)),
                  pl.BlockSpec((1, HALO, pw), lambda i, t: (i, jnp.maximum(t * per - 1, 0), 0)),
                  pl.BlockSpec((1, HALO, pw), lambda i, t: (i, jnp.minimum((t + 1) * per, nh - 1), 0)),
                  pl.BlockSpec((g, c, c), lambda i, t: (0, 0, 0)),
                  pl.BlockSpec((1, pw), lambda i, t: (0, 0))],
        out_specs=pl.BlockSpec((1, ts, pw), lambda i, t: (i, t, 0)),
        out_shape=jax.ShapeDtypeStruct((b, seq, pw), BF16),
        compiler_params=_cparams(("parallel", "parallel")),
        name="pool_mixer",
    )(u3, u3, u3, pool_w, pool_scale)


DFT_N1 = 256


def _dft_mats(n, scale, dtype):
    j = jnp.arange(n, dtype=I32)
    ang = ((j[:, None] * j[None, :]) % n).astype(F32) * (2.0 * math.pi / n)
    return (jnp.cos(ang) * scale).astype(dtype), (jnp.sin(ang) * scale).astype(dtype)


def _fourier_prep_kernel(cc_ref, sc_ref, w_ref, o_ref, *, c):
    w = w_ref[0]
    o_ref[0, :c, :] = jnp.dot(cc_ref[...], w, preferred_element_type=F32,
                              precision=lax.Precision.HIGHEST).astype(o_ref.dtype)
    o_ref[0, c:, :] = jnp.dot(sc_ref[...], w, preferred_element_type=F32,
                              precision=lax.Precision.HIGHEST).astype(o_ref.dtype)


def _fourier_prep(fourier_w):
    h, c, _ = fourier_w.shape
    cc, sc = _dft_mats(c, c ** -0.5, F32)
    return pl.pallas_call(
        functools.partial(_fourier_prep_kernel, c=c),
        grid=(h,),
        in_specs=[pl.BlockSpec((c, c), lambda i: (0, 0)),
                  pl.BlockSpec((c, c), lambda i: (0, 0)),
                  pl.BlockSpec((1, c, c), lambda i: (i, 0, 0))],
        out_specs=pl.BlockSpec((1, 2 * c, c), lambda i: (i, 0, 0)),
        out_shape=jax.ShapeDtypeStruct((h, 2 * c, c), BF16),
        compiler_params=_cparams(("parallel",)),
        name="fourier_prep",
    )(cc, sc, fourier_w)


def _fft(xs):
    n = len(xs)
    if n == 1:
        return xs
    ev, od = _fft(xs[0::2]), _fft(xs[1::2])
    out = [None] * n
    for k in range(n // 2):
        re, im = od[k]
        if k == 0:
            tr, ti = re, im
        elif 4 * k == n:
            tr, ti = im, -re
        else:
            wr, wi = math.cos(2.0 * math.pi * k / n), -math.sin(2.0 * math.pi * k / n)
            tr, ti = re * wr - im * wi, re * wi + im * wr
        er, ei = ev[k]
        out[k] = (er + tr, ei + ti)
        out[k + n // 2] = (er - tr, ei - ti)
    return out


def _fourier_kernel(x_ref, cs_ref, tw_ref, ab_ref, o_ref, y_ref, *, n1, n2, c, rows):
    for j in range(n2):
        halves = [_unpack_halves(x_ref[s, pl.ds(j, n1, stride=n2), :]) for s in range(x_ref.shape[0])]
        xj = jnp.concatenate([lo for lo, _ in halves] + [hi for _, hi in halves], axis=1).astype(BF16)
        y = jnp.dot(cs_ref[...], xj, preferred_element_type=F32)
        y_ref[0, j] = y[:n1]
        y_ref[1, j] = y[n1:]

    nl = c // LANES

    def tile(i, carry):
        r = pl.multiple_of((i // nl) * SUBLANES, SUBLANES)
        l = pl.multiple_of((i % nl) * LANES, LANES)
        zs = []
        for j in range(n2):
            yc = y_ref[0, j, pl.ds(r, SUBLANES), pl.ds(l, LANES)]
            ys = y_ref[1, j, pl.ds(r, SUBLANES), pl.ds(l, LANES)]
            ct = tw_ref[0, j, pl.ds(r, SUBLANES), :]
            st = tw_ref[1, j, pl.ds(r, SUBLANES), :]
            zs.append((yc * ct - ys * st, -(ys * ct + yc * st)))
        gs = _fft(zs)
        for k in range(n2):
            y_ref[0, k, pl.ds(r, SUBLANES), pl.ds(l, LANES)] = gs[k][0]
            y_ref[1, k, pl.ds(r, SUBLANES), pl.ds(l, LANES)] = gs[k][1]
        return carry
    lax.fori_loop(0, (n1 // SUBLANES) * nl, tile, 0, unroll=2)

    per = max(1, min(n2, rows // n1))
    for s in range(0, n2, per):
        gr = y_ref[0, s:s + per].reshape(per * n1, c).astype(BF16)
        gi = y_ref[1, s:s + per].reshape(per * n1, c).astype(BF16)
        y = jnp.dot(gr, ab_ref[0, :c, :], preferred_element_type=F32)
        y = y + jnp.dot(gi, ab_ref[0, c:, :], preferred_element_type=F32)
        o_ref[0, s * n1:(s + per) * n1, :] = y.astype(o_ref.dtype)


def _fourier_mixer(uf, ab, *, b, rows):
    slabs, n, _ = uf.shape
    seq = n // b
    h, _, c = ab.shape
    spb = slabs // h
    n1 = min(DFT_N1, seq)
    n2 = seq // n1
    assert n1 * n2 == seq and n2 & (n2 - 1) == 0, "sequence length must be N1 * 2^m"
    cmat, smat = _dft_mats(n1, 1.0, BF16)
    cs = jnp.concatenate([cmat, smat], axis=0)
    ang = (jnp.arange(n2, dtype=I32)[:, None] * jnp.arange(n1, dtype=I32)[None, :]).astype(F32)
    ang = ang * (2.0 * math.pi / seq)
    tw = jnp.stack([jnp.cos(ang), jnp.sin(ang)]) * (seq ** -0.5)
    tw = jnp.broadcast_to(tw[..., None], (2, n2, n1, LANES))

    return pl.pallas_call(
        functools.partial(_fourier_kernel, n1=n1, n2=n2, c=c, rows=rows),
        grid=(b, h),
        in_specs=[
            pl.BlockSpec((spb, None, seq, LANES), lambda i, k: (k, i, 0, 0)),
            pl.BlockSpec((2 * n1, n1), lambda i, k: (0, 0), pipeline_mode=pl.Buffered(1)),
            pl.BlockSpec((2, n2, n1, LANES), lambda i, k: (0, 0, 0, 0), pipeline_mode=pl.Buffered(1)),
            pl.BlockSpec((1, 2 * c, c), lambda i, k: (k, 0, 0))],
        out_specs=pl.BlockSpec((1, seq, c), lambda i, k: (i, 0, k)),
        out_shape=jax.ShapeDtypeStruct((b, seq, h * c), BF16),
        scratch_shapes=[pltpu.VMEM((2, n2, n1, c), F32)],
        compiler_params=_cparams(("parallel", "parallel")),
        name="fourier_mixer",
    )(uf.reshape(slabs, b, seq, LANES), cs, tw, ab)


LOGIT_ROWS = 256


def _out_proj_kernel(a_ref, f_ref, w_ref, x_ref, grw_ref, rb_ref, o_ref, lg_ref, ss_acc, pl_acc, *, ka):
    j = pl.program_id(1)
    y = jnp.dot(a_ref[...], w_ref[:ka, :], preferred_element_type=F32)
    y = y + jnp.dot(f_ref[...], w_ref[ka:, :], preferred_element_type=F32)
    o_ref[...] = x_ref[...] + y

    @pl.when(j == 0)
    def _():
        ss_acc[...] = jnp.zeros_like(ss_acc)
        pl_acc[...] = jnp.zeros_like(pl_acc)

    rc = min(LOGIT_ROWS, o_ref.shape[0])
    for r in range(0, o_ref.shape[0], rc):
        hc = o_ref[r:r + rc, :]
        sq = hc * hc
        part = sq[:, :LANES]
        for c0 in range(LANES, hc.shape[1], LANES):
            part = part + sq[:, c0:c0 + LANES]
        ss_acc[r:r + rc, :] += part
        pl_acc[r:r + rc, :] += jnp.dot(hc.astype(BF16), grw_ref[...], preferred_element_type=F32)

    @pl.when(j == pl.num_programs(1) - 1)
    def _():
        d = o_ref.shape[1] * pl.num_programs(1)
        ms = jnp.sum(ss_acc[...], axis=-1, keepdims=True) / d
        lg_ref[...] = pl_acc[...] * lax.rsqrt(ms + RMS_EPS) + rb_ref[...]


def _out_proj(a, f, w, x, grw, rb, *, bm, bn):
    n, ka = a.shape
    kf = f.shape[1]
    d = w.shape[1]
    bm, bn = min(bm, n), min(bn, d)
    return pl.pallas_call(
        functools.partial(_out_proj_kernel, ka=ka),
        grid=(n // bm, d // bn),
        in_specs=[pl.BlockSpec((bm, ka), lambda i, j: (i, 0)),
                  pl.BlockSpec((bm, kf), lambda i, j: (i, 0)),
                  pl.BlockSpec((ka + kf, bn), lambda i, j: (0, j)),
                  pl.BlockSpec((bm, bn), lambda i, j: (i, j)),
                  pl.BlockSpec((bn, LANES), lambda i, j: (j, 0)),
                  pl.BlockSpec((1, LANES), lambda i, j: (0, 0))],
        out_specs=[pl.BlockSpec((bm, bn), lambda i, j: (i, j)),
                   pl.BlockSpec((bm, LANES), lambda i, j: (i, 0))],
        out_shape=[jax.ShapeDtypeStruct((n, d), F32), jax.ShapeDtypeStruct((n, LANES), F32)],
        scratch_shapes=[pltpu.VMEM((bm, LANES), F32), pltpu.VMEM((bm, LANES), F32)],
        compiler_params=_cparams(("parallel", "arbitrary"), vmem=VMEM_LIMIT_HIGH),
        name="w_out_residual",
    )(a, f, w, x, grw, rb)


def _route_kernel(lg_ref, pos_ref, gate_ref, texp_ref, ntl_ref, last_ref, nxt_ref, idx_s, rank_s,
                  *, n_tok, n_exp, tm, tb, tmax_pad):
    shift = tm.bit_length() - 1
    iota_e = lax.broadcasted_iota(I32, (n_exp, tb), 0)
    before = (lax.broadcasted_iota(I32, (tb, tb), 0)
              < lax.broadcasted_iota(I32, (tb, tb), 1)).astype(BF16)

    def pass1(i, counts):
        off = pl.multiple_of(i * tb, tb)
        l = lg_ref[pl.ds(off, tb), :].T[:n_exp, :]
        vals, hots = [], []
        for k in range(TOP_K):
            m = jnp.max(l, axis=0, keepdims=True)
            idx = jnp.min(jnp.where(l == m, iota_e, n_exp), axis=0, keepdims=True)
            hot = iota_e == idx
            l = jnp.where(hot, -jnp.inf, l)
            vals.append(m)
            hots.append(hot)
            idx_s[k:k + 1, pl.ds(off, tb)] = idx
        exps = [jnp.exp(v - vals[0]) for v in vals]
        tot = exps[0]
        for e in exps[1:]:
            tot = tot + e
        sel = hots[0].astype(F32)
        for hot in hots[1:]:
            sel = sel + hot.astype(F32)
        rank = jnp.dot(sel.astype(BF16), before, preferred_element_type=F32) + counts
        for k in range(TOP_K):
            gate_ref[k:k + 1, pl.ds(off, tb)] = exps[k] / tot
            rank_s[k:k + 1, pl.ds(off, tb)] = jnp.sum(jnp.where(hots[k], rank, 0.0), axis=0, keepdims=True)
        return counts + jnp.sum(sel, axis=1, keepdims=True)

    counts = lax.fori_loop(0, n_tok // tb, pass1, jnp.zeros((n_exp, 1), F32))
    ntile = (counts.astype(I32) + (tm - 1)) >> shift
    sub = lax.broadcasted_iota(I32, (n_exp, LANES), 0)
    lane = lax.broadcasted_iota(I32, (n_exp, LANES), 1)
    ntile_row = jnp.sum(jnp.where(sub == lane, ntile, 0), axis=0, keepdims=True)
    start = jnp.sum(jnp.where(lane < sub, ntile_row, 0), axis=1, keepdims=True)
    t_lane = lax.broadcasted_iota(I32, (n_exp, tmax_pad), 1)
    texp_ref[...] = jnp.sum((start <= t_lane).astype(I32), axis=0, keepdims=True) - 1
    ntl_ref[...] = jnp.sum(ntile, axis=0, keepdims=True) + jnp.zeros((1, LANES), I32)
    last = jnp.where(ntile > 0, (start + ntile - 1) << shift, -1)
    last_ref[...] = jnp.sum(jnp.where(sub == lane, last, 0), axis=0, keepdims=True)
    nxt = jnp.min(jnp.where((lane > sub) & (ntile_row > 0), lane, n_exp), axis=1, keepdims=True)
    nxt_ref[...] = jnp.sum(jnp.where(sub == lane, nxt, 0), axis=0, keepdims=True)
    start_rows = start << shift

    def pass2(i, carry):
        off = pl.multiple_of(i * tb, tb)
        for k in range(TOP_K):
            hot = iota_e == idx_s[k:k + 1, pl.ds(off, tb)]
            base = jnp.sum(jnp.where(hot, start_rows, 0), axis=0, keepdims=True)
            pos_ref[k:k + 1, pl.ds(off, tb)] = rank_s[k:k + 1, pl.ds(off, tb)].astype(I32) + base
        return carry
    lax.fori_loop(0, n_tok // tb, pass2, 0)


def _route(logits, *, n_exp, tm, tmax):
    n_tok = logits.shape[0]
    tb = min(256, n_tok)
    tmax_pad = -(-tmax // LANES) * LANES
    return pl.pallas_call(
        functools.partial(_route_kernel, n_tok=n_tok, n_exp=n_exp, tm=tm, tb=tb, tmax_pad=tmax_pad),
        out_shape=[jax.ShapeDtypeStruct((TOP_K, n_tok), I32),
                   jax.ShapeDtypeStruct((TOP_K, n_tok), F32),
                   jax.ShapeDtypeStruct((1, tmax_pad), I32),
                   jax.ShapeDtypeStruct((1, LANES), I32),
                   jax.ShapeDtypeStruct((1, LANES), I32),
                   jax.ShapeDtypeStruct((1, LANES), I32)],
        scratch_shapes=[pltpu.VMEM((TOP_K, n_tok), I32), pltpu.VMEM((TOP_K, n_tok), F32)],
        compiler_params=_cparams(None),
        name="route",
    )(logits)


def _norm_dispatch_kernel(last_ref, pos_ref, h_ref, g_ref, xs_hbm, z_a, z_b, r_ref, zero_v, zsem, sem,
                          *, n_exp, tm, chunk, rows):
    i = pl.program_id(0)
    dh = z_a.shape[1]

    @pl.when(i == 0)
    def _():
        zero_v[...] = jnp.zeros_like(zero_v)
        for e in range(n_exp):
            r = last_ref[e]

            @pl.when(r >= 0)
            def _():
                cp = pltpu.make_async_copy(zero_v, xs_hbm.at[pl.ds(pl.multiple_of(r, tm), tm)], zsem)
                cp.start()
                cp.wait()

    def wait_rows(z_buf, s):
        for k in range(TOP_K):
            pltpu.make_async_copy(z_buf, xs_hbm.at[pl.ds(0, chunk)], sem.at[s]).wait()

    _inv_rms_rows(h_ref, r_ref)

    def step(z_buf, s, other, s_other):
        @pl.when(i >= 2)
        def _():
            wait_rows(z_buf, s)

        def body(t, carry):
            r = pl.multiple_of(t * rows, rows)
            inv = r_ref[pl.ds(r, rows), :]
            for c0 in range(0, dh, LANES):
                lo = (h_ref[pl.ds(r, rows), c0:c0 + LANES] * inv * g_ref[:, c0:c0 + LANES]).astype(BF16)
                hi = (h_ref[pl.ds(r, rows), dh + c0:dh + c0 + LANES] * inv
                      * g_ref[:, dh + c0:dh + c0 + LANES]).astype(BF16)
                z_buf[pl.ds(r, rows), c0:c0 + LANES] = _pack_halves(lo.astype(F32), hi.astype(F32))
            return carry
        lax.fori_loop(0, chunk // rows, body, 0, unroll=2)

        for n in range(chunk):
            for k in range(TOP_K):
                pltpu.make_async_copy(z_buf.at[pl.ds(n, 1)], xs_hbm.at[pl.ds(pos_ref[0, k, n], 1)],
                                      sem.at[s]).start(priority=k % 2)

        @pl.when(i == pl.num_programs(0) - 1)
        def _():
            wait_rows(z_buf, s)

            @pl.when(i >= 1)
            def _():
                wait_rows(other, s_other)

    @pl.when(i % 2 == 0)
    def _():
        step(z_a, 0, z_b, 1)

    @pl.when(i % 2 == 1)
    def _():
        step(z_b, 1, z_a, 0)


def _norm_dispatch(last, pos3, h, g, *, n_rows, tm):
    n_tok, d = h.shape
    dh = d // 2
    nchunk, _, chunk = pos3.shape
    n_exp = last.shape[0]
    grid_spec = pltpu.PrefetchScalarGridSpec(
        num_scalar_prefetch=1,
        grid=(nchunk,),
        in_specs=[pl.BlockSpec((1, TOP_K, chunk), lambda c, last: (c, 0, 0), memory_space=pltpu.SMEM),
                  pl.BlockSpec((chunk, d), lambda c, last: (c, 0)),
                  pl.BlockSpec((1, d), lambda c, last: (0, 0))],
        out_specs=pl.BlockSpec(memory_space=pl.ANY),
        scratch_shapes=[pltpu.VMEM((chunk, dh), U32), pltpu.VMEM((chunk, dh), U32),
                        pltpu.VMEM((chunk, LANES), F32), pltpu.VMEM((tm, dh), U32),
                        pltpu.SemaphoreType.DMA(()), pltpu.SemaphoreType.DMA((2,))],
    )
    return pl.pallas_call(
        functools.partial(_norm_dispatch_kernel, n_exp=n_exp, tm=tm, chunk=chunk, rows=min(NORM_ROWS, chunk)),
        grid_spec=grid_spec,
        out_shape=jax.ShapeDtypeStruct((n_rows, dh), U32),
        compiler_params=_cparams(("arbitrary",)),
        name="norm2_dispatch",
    )(last, pos3, h, g)


CAST_ROWS = 256


def _ffn_kernel(texp_ref, ntl_ref, nxt_ref, x_ref, wg_hbm, wu_hbm, wd_hbm, bgu_ref, bd_ref, o_ref,
                stage_g, stage_u, stage_d, wgu, wd, sems, *, f, n_exp, rows):
    t = pl.program_id(0)
    e = texp_ref[t]
    dh = x_ref.shape[1]
    valid = t < ntl_ref[0]
    first = jnp.logical_or(t == 0, e != texp_ref[jnp.maximum(t - 1, 0)])

    def weight_copies(ex):
        return (pltpu.make_async_copy(wg_hbm.at[ex], stage_g, sems.at[0]),
                pltpu.make_async_copy(wu_hbm.at[ex], stage_u, sems.at[1]),
                pltpu.make_async_copy(wd_hbm.at[ex], stage_d, sems.at[2]))

    @pl.when(t == 0)
    def _():
        for cp in weight_copies(e):
            cp.start()

    @pl.when(jnp.logical_and(valid, first))
    def _():
        for cp in weight_copies(e):
            cp.wait()

        def cast_up(i, carry):
            r = pl.multiple_of(i * rows, rows)
            wgu[pl.ds(r, rows), :f] = stage_g[pl.ds(r, rows), :].astype(BF16)
            wgu[pl.ds(r, rows), f:] = stage_u[pl.ds(r, rows), :].astype(BF16)
            return carry
        lax.fori_loop(0, stage_g.shape[0] // rows, cast_up, 0)

        def cast_down(i, carry):
            r = pl.multiple_of(i * rows, rows)
            wd[pl.ds(r, rows), :] = stage_d[pl.ds(r, rows), :].astype(BF16)
            return carry
        lax.fori_loop(0, stage_d.shape[0] // rows, cast_down, 0)

        nx = nxt_ref[e]

        @pl.when(nx < n_exp)
        def _():
            for cp in weight_copies(nx):
                cp.start(priority=1)

    @pl.when(valid)
    def _():
        lo, hi = _unpack_halves(x_ref[...])
        hgu = jnp.dot(lo.astype(BF16), wgu[:dh, :], preferred_element_type=F32)
        hgu = hgu + jnp.dot(hi.astype(BF16), wgu[dh:, :], preferred_element_type=F32)
        hgu = hgu + bgu_ref[0]
        hg = jnp.minimum(hgu[:, :f], SWIGLU_LIMIT)
        hu = jnp.clip(hgu[:, f:], -SWIGLU_LIMIT, SWIGLU_LIMIT)
        act = hg * (1.0 / (1.0 + jnp.exp(-SWIGLU_ALPHA * hg))) * (hu + 1.0)
        y = jnp.dot(act.astype(BF16), wd[...], preferred_element_type=F32) + bd_ref[0]
        yr = y.astype(BF16).astype(F32)
        o_ref[...] = _pack_halves(yr[:, :dh], yr[:, dh:])


def _expert_ffn(texp, ntl, nxt, xs, w_gate, w_up, w_down, bgu, bd, *, tm):
    n_rows, dh = xs.shape
    n_exp, d, f = w_gate.shape
    tmax = n_rows // tm
    rows = min(CAST_ROWS, f)
    assert d % rows == 0 and f % rows == 0

    def x_map(t, te, nt, nx):
        return (jnp.minimum(t, nt[0] - 1), 0)

    def b_map(t, te, nt, nx):
        return (te[t], 0, 0)

    grid_spec = pltpu.PrefetchScalarGridSpec(
        num_scalar_prefetch=3,
        grid=(tmax,),
        in_specs=[pl.BlockSpec((tm, dh), x_map),
                  pl.BlockSpec(memory_space=pl.ANY),
                  pl.BlockSpec(memory_space=pl.ANY),
                  pl.BlockSpec(memory_space=pl.ANY),
                  pl.BlockSpec((1, 1, 2 * f), b_map),
                  pl.BlockSpec((1, 1, d), b_map)],
        out_specs=pl.BlockSpec((tm, dh), lambda t, te, nt, nx: (t, 0)),
        scratch_shapes=[pltpu.VMEM((d, f), F32), pltpu.VMEM((d, f), F32), pltpu.VMEM((f, d), F32),
                        pltpu.VMEM((d, 2 * f), BF16), pltpu.VMEM((f, d), BF16),
                        pltpu.SemaphoreType.DMA((3,))],
    )
    return pl.pallas_call(
        functools.partial(_ffn_kernel, f=f, n_exp=n_exp, rows=rows),
        grid_spec=grid_spec,
        out_shape=jax.ShapeDtypeStruct((n_rows, dh), U32),
        compiler_params=_cparams(("arbitrary",)),
        name="expert_ffn",
    )(texp, ntl, nxt, xs, w_gate, w_up, w_down, bgu, bd)


COMBINE_ROWS = 16


def _combine_kernel(pos_c, pos_n, ys_hbm, h_ref, g_ref, fg_ref, o_ref, buf_a, buf_b, sem, *, bt, rows):
    i = pl.program_id(0)
    dh = buf_a.shape[-1]

    def start_rows(pos_ref, dst, s, r0):
        for r in range(rows):
            for k in range(TOP_K):
                pltpu.make_async_copy(ys_hbm.at[pl.ds(pos_ref[0, k, r0 + r], 1)],
                                      dst.at[k, pl.ds(r0 + r, 1)], sem.at[s]).start(priority=k % 2)

    def wait_block(dst, s):
        for k in range(TOP_K):
            pltpu.make_async_copy(ys_hbm.at[pl.ds(0, bt)], dst.at[k], sem.at[s]).wait()

    @pl.when(i == 0)
    def _():
        def first(c, carry):
            start_rows(pos_c, buf_a, 0, pl.multiple_of(c * rows, rows))
            return carry
        lax.fori_loop(0, bt // rows, first, 0)

    def step(cur, s_cur, nxt, s_nxt):
        wait_block(cur, s_cur)
        for r0 in range(0, bt, rows):
            start_rows(pos_n, nxt, s_nxt, r0)
            acc_lo = h_ref[r0:r0 + rows, :dh]
            acc_hi = h_ref[r0:r0 + rows, dh:]
            for k in range(TOP_K):
                lo, hi = _unpack_halves(cur[k, r0:r0 + rows, :])
                gk = g_ref[r0:r0 + rows, k:k + 1]
                acc_lo = acc_lo + gk * lo
                acc_hi = acc_hi + gk * hi
            ms = (jnp.sum(acc_lo * acc_lo, axis=-1, keepdims=True)
                  + jnp.sum(acc_hi * acc_hi, axis=-1, keepdims=True)) / (2 * dh)
            inv = lax.rsqrt(ms + RMS_EPS)
            o_ref[r0:r0 + rows, :dh] = acc_lo * inv * fg_ref[:, :dh]
            o_ref[r0:r0 + rows, dh:] = acc_hi * inv * fg_ref[:, dh:]

        @pl.when(i == pl.num_programs(0) - 1)
        def _():
            wait_block(nxt, s_nxt)

    @pl.when(i % 2 == 0)
    def _():
        step(buf_a, 0, buf_b, 1)

    @pl.when(i % 2 == 1)
    def _():
        step(buf_b, 1, buf_a, 0)


def _combine(pos3, ys, h, gates_t, fg, *, bt):
    n, d = h.shape
    nblk = n // bt
    return pl.pallas_call(
        functools.partial(_combine_kernel, bt=bt, rows=min(COMBINE_ROWS, bt)),
        grid=(nblk,),
        in_specs=[pl.BlockSpec((1, TOP_K, bt), lambda i: (i, 0, 0), memory_space=pltpu.SMEM),
                  pl.BlockSpec((1, TOP_K, bt), lambda i: (jnp.minimum(i + 1, nblk - 1), 0, 0),
                               memory_space=pltpu.SMEM),
                  pl.BlockSpec(memory_space=pl.ANY),
                  pl.BlockSpec((bt, d), lambda i: (i, 0)),
                  pl.BlockSpec((bt, TOP_K), lambda i: (i, 0)),
                  pl.BlockSpec((1, d), lambda i: (0, 0))],
        out_specs=pl.BlockSpec((bt, d), lambda i: (i, 0)),
        out_shape=jax.ShapeDtypeStruct((n, d), F32),
        scratch_shapes=[pltpu.VMEM((TOP_K, bt, d // 2), U32), pltpu.VMEM((TOP_K, bt, d // 2), U32),
                        pltpu.SemaphoreType.DMA((2,))],
        compiler_params=_cparams(("arbitrary",)),
        name="combine_final_norm",
    )(pos3, pos3, ys, h, gates_t, fg)


def _layer(x2, b, seq, norm1_g, w_in, pool_w, pool_scale, fourier_w, w_out, norm2_g,
           router_w, router_b, w_gate, b_gate, w_up, b_up, w_down, b_down):
    n, d = x2.shape
    mix = w_in.shape[1]
    pw = pool_scale.shape[0]
    n_exp = router_w.shape[1]
    f = w_gate.shape[2]
    tm = min(256, n)
    tmax = (n * TOP_K) // tm + n_exp

    up, uf = _norm_matmul(x2, norm1_g.reshape(1, d), w_in.astype(BF16), pw=pw, c=fourier_w.shape[1],
                          bm=1024, bn=1024)
    a = _pool_mixer(up.reshape(b, seq, pw), pool_w.astype(BF16), pool_scale.reshape(1, pw), ts=256)
    yf = _fourier_mixer(uf, _fourier_prep(fourier_w), b=b, rows=1024)
    grw = jnp.zeros((d, LANES), F32).at[:, :n_exp].set(norm2_g[:, None] * router_w).astype(BF16)
    rb = jnp.zeros((1, LANES), F32).at[0, :n_exp].set(router_b)
    h, logits = _out_proj(a.reshape(n, pw), yf.reshape(n, mix - pw), w_out.astype(BF16), x2, grw, rb,
                          bm=1024, bn=1024)

    pos, gates, texp, ntl, last, nxt = _route(logits, n_exp=n_exp, tm=tm, tmax=tmax)
    chunk = min(256, n)
    pos_c = pos.reshape(TOP_K, n // chunk, chunk).transpose(1, 0, 2)
    xs = _norm_dispatch(last[0, :n_exp], pos_c, h, norm2_g.reshape(1, d), n_rows=tmax * tm, tm=tm)

    bgu = jnp.concatenate([b_gate, b_up], axis=-1).reshape(n_exp, 1, 2 * f)
    ys = _expert_ffn(texp[0, :tmax], ntl[0, :1], nxt[0, :n_exp], xs, w_gate, w_up, w_down, bgu,
                     b_down.reshape(n_exp, 1, d), tm=tm)

    bt = min(256, n)
    pos_b = pos.reshape(TOP_K, n // bt, bt).transpose(1, 0, 2)
    return pos_b, ys, h, gates.T


def kernel(x, norm1_g, w_in, pool_w, pool_scale, fourier_w, w_out, norm2_g, router_w, router_b,
           w_gate, b_gate, w_up, b_up, w_down, b_down, final_g):
    b, seq, d = x.shape
    assert w_in.shape[0] == 1, "only a single layer is supported"
    pos_b, ys, h, gates_t = _layer(
        x.reshape(b * seq, d), b, seq, norm1_g[0], w_in[0], pool_w[0], pool_scale[0], fourier_w[0], w_out[0],
        norm2_g[0], router_w[0], router_b[0], w_gate[0], b_gate[0], w_up[0], b_up[0], w_down[0], b_down[0])
    out = _combine(pos_b, ys, h, gates_t, final_g.reshape(1, d), bt=min(256, b * seq))
    return out.reshape(b, seq, d)
```

```python
import functools
import math

import jax
import jax.numpy as jnp
from jax import lax
from jax.experimental import pallas as pl
from jax.experimental.pallas import tpu as pltpu

F32 = jnp.float32
BF16 = jnp.bfloat16
I32 = jnp.int32
U32 = jnp.uint32

RMS_EPS = 1e-5
POOL_WINDOWS = (2, 4, 8, 16)
TOP_K = 4
SWIGLU_LIMIT = 7.0
SWIGLU_ALPHA = 1.702

LANES = 128
BF16_SUBLANES = 16
HALO = 16
NORM_ROWS = 16
NORM_UNROLL = 8
SUBLANES = 8
VMEM_LIMIT = 56 * 1024 * 1024
VMEM_LIMIT_HIGH = 60 * 1024 * 1024
HI_MASK = 0xFFFF0000


def _cparams(sem, vmem=VMEM_LIMIT):
    return pltpu.CompilerParams(dimension_semantics=sem, vmem_limit_bytes=vmem)


def _pack_halves(lo_f32, hi_f32):
    lo = lax.bitcast_convert_type(lo_f32, U32) >> 16
    hi = lax.bitcast_convert_type(hi_f32, U32) & jnp.uint32(HI_MASK)
    return lo | hi


def _unpack_halves(w):
    lo = lax.bitcast_convert_type(w << 16, F32)
    hi = lax.bitcast_convert_type(w & jnp.uint32(HI_MASK), F32)
    return lo, hi


def _inv_rms_rows(x_ref, r_ref):
    d = x_ref.shape[1]

    def body(i, carry):
        r = pl.multiple_of(i * SUBLANES, SUBLANES)
        xc = x_ref[pl.ds(r, SUBLANES), :]
        sq = xc * xc
        acc = sq[:, :LANES]
        for c0 in range(LANES, d, LANES):
            acc = acc + sq[:, c0:c0 + LANES]
        r_ref[pl.ds(r, SUBLANES), :] = acc
        return carry
    steps = x_ref.shape[0] // SUBLANES
    lax.fori_loop(0, steps, body, 0, unroll=min(NORM_UNROLL, steps))
    ms = jnp.sum(r_ref[...], axis=-1, keepdims=True) / d
    r_ref[...] = jnp.broadcast_to(lax.rsqrt(ms + RMS_EPS), r_ref.shape)


def _norm_mm_kernel(x_hbm, g_ref, w_ref, up_ref, uf_ref, x_buf, z_ref, r_ref, xsem, *, rows, pool_blocks, c):
    i = pl.program_id(0)
    j = pl.program_id(1)
    bm = x_buf.shape[0]

    def x_copy(blk):
        return pltpu.make_async_copy(x_hbm.at[pl.ds(pl.multiple_of(blk * bm, bm), bm)], x_buf, xsem)

    @pl.when(j == 0)
    def _():
        @pl.when(i == 0)
        def _():
            x_copy(0).start()
        x_copy(i).wait()
        _inv_rms_rows(x_buf, r_ref)

        def body(t, carry):
            r = pl.multiple_of(t * rows, rows)
            inv = r_ref[pl.ds(r, rows), :]
            for c0 in range(0, x_buf.shape[1], LANES):
                z = x_buf[pl.ds(r, rows), c0:c0 + LANES] * inv * g_ref[:, c0:c0 + LANES]
                z_ref[pl.ds(r, rows), c0:c0 + LANES] = z.astype(BF16)
            return carry
        lax.fori_loop(0, bm // rows, body, 0, unroll=2)

        @pl.when(i + 1 < pl.num_programs(0))
        def _():
            x_copy(i + 1).start(priority=1)

    y = jnp.dot(z_ref[...], w_ref[...], preferred_element_type=F32)

    @pl.when(j < pool_blocks)
    def _():
        up_ref[...] = y.astype(up_ref.dtype)

    @pl.when(j >= pool_blocks)
    def _():
        yr = y.astype(BF16).astype(F32)
        ch = c // 2
        for hd in range(y.shape[1] // c):
            words = _pack_halves(yr[:, hd * c:hd * c + ch], yr[:, hd * c + ch:(hd + 1) * c])
            for s in range(ch // LANES):
                uf_ref[hd * (ch // LANES) + s] = words[:, s * LANES:(s + 1) * LANES]


def _norm_matmul(x, g, w, *, pw, c, bm, bn):
    n, d = x.shape
    m = w.shape[1]
    bm, bn = min(bm, n), min(bn, pw, m - pw)
    assert pw % bn == 0 and (m - pw) % bn == 0 and bn % c == 0 and (c // 2) % LANES == 0
    pool_blocks = pw // bn
    slabs_blk = bn // 2 // LANES
    return pl.pallas_call(
        functools.partial(_norm_mm_kernel, rows=min(NORM_ROWS, bm), pool_blocks=pool_blocks, c=c),
        grid=(n // bm, m // bn),
        in_specs=[pl.BlockSpec(memory_space=pl.ANY),
                  pl.BlockSpec((1, d), lambda i, j: (0, 0)),
                  pl.BlockSpec((d, bn), lambda i, j: (0, j))],
        out_specs=[pl.BlockSpec((bm, bn), lambda i, j: (i, jnp.minimum(j, pool_blocks - 1))),
                   pl.BlockSpec((slabs_blk, bm, LANES), lambda i, j: (jnp.maximum(j - pool_blocks, 0), i, 0))],
        out_shape=[jax.ShapeDtypeStruct((n, pw), BF16),
                   jax.ShapeDtypeStruct(((m - pw) // 2 // LANES, n, LANES), U32)],
        scratch_shapes=[pltpu.VMEM((bm, d), F32), pltpu.VMEM((bm, d), BF16), pltpu.VMEM((bm, LANES), F32),
                        pltpu.SemaphoreType.DMA(())],
        compiler_params=_cparams(("arbitrary", "arbitrary"), vmem=VMEM_LIMIT_HIGH),
        name="norm1_w_in",
    )(x, g, w)


def _pool_kernel(cur_ref, prev_ref, next_ref, pw_ref, sc_ref, o_ref, *, seq, ts, c):
    t = pl.program_id(1)
    has_prev = (t > 0).astype(F32)
    has_next = (t < pl.num_programs(1) - 1).astype(F32)
    n_ext = ts + 2 * HALO
    tok = t * ts + lax.broadcasted_iota(I32, (ts, 1), 0)
    for g, w in enumerate(POOL_WINDOWS):
        sl = slice(g * c, (g + 1) * c)
        cur = cur_ref[0, :, sl].astype(F32)
        prev = prev_ref[0, :, sl].astype(F32) * has_prev
        nxt = next_ref[0, :, sl].astype(F32) * has_next
        ext = jnp.concatenate([prev, cur, nxt], axis=0)
        s = ext + pltpu.roll(ext, 1, 0)
        h = 1
        while 2 * h < w:
            s = pltpu.roll(s, h, 0) + pltpu.roll(s, n_ext - h, 0)
            h *= 2
        win = s[HALO:HALO + ts]
        lo = jnp.maximum(tok - w // 2, 0)
        hi = jnp.minimum(tok + w // 2 - 1, seq - 1)
        cnt = (hi - lo + 1).astype(F32)
        p = win / cnt - cur
        y = jnp.dot(p.astype(BF16), pw_ref[g], preferred_element_type=F32)
        o_ref[0, :, sl] = (y * sc_ref[:, sl]).astype(o_ref.dtype)


def _pool_mixer(u3, pool_w, pool_scale, *, ts):
    b, seq, _ = u3.shape
    g, c, _ = pool_w.shape
    pw = g * c
    ts = min(ts, seq)
    nh = seq // HALO
    per = ts // HALO
    return pl.pallas_call(
        functools.partial(_pool_kernel, seq=seq, ts=ts, c=c),
        grid=(b, seq // ts),
        in_specs=[pl.BlockSpec((1, ts, pw), lambda i, t: (i, t, 0)),
                  pl.BlockSpec((1, HALO, pw), lambda i, t: (i, jnp.maximum(t * per - 1, 0), 0)),
                  pl.BlockSpec((1, HALO, pw), lambda i, t: (i, jnp.minimum((t + 1) * per, nh - 1), 0)),
                  pl.BlockSpec((g, c, c), lambda i, t: (0, 0, 0)),
                  pl.BlockSpec((1, pw), lambda i, t: (0, 0))],
        out_specs=pl.BlockSpec((1, ts, pw), lambda i, t: (i, t, 0)),
        out_shape=jax.ShapeDtypeStruct((b, seq, pw), BF16),
        compiler_params=_cparams(("parallel", "parallel")),
        name="pool_mixer",
    )(u3, u3, u3, pool_w, pool_scale)


DFT_N1 = 256


def _dft_mats(n, scale, dtype):
    j = jnp.arange(n, dtype=I32)
    ang = ((j[:, None] * j[None, :]) % n).astype(F32) * (2.0 * math.pi / n)
    return (jnp.cos(ang) * scale).astype(dtype), (jnp.sin(ang) * scale).astype(dtype)


def _fourier_prep_kernel(cc_ref, sc_ref, w_ref, o_ref, *, c):
    w = w_ref[0]
    o_ref[0, :c, :] = jnp.dot(cc_ref[...], w, preferred_element_type=F32,
                              precision=lax.Precision.HIGHEST).astype(o_ref.dtype)
    o_ref[0, c:, :] = jnp.dot(sc_ref[...], w, preferred_element_type=F32,
                              precision=lax.Precision.HIGHEST).astype(o_ref.dtype)


def _fourier_prep(fourier_w):
    h, c, _ = fourier_w.shape
    cc, sc = _dft_mats(c, c ** -0.5, F32)
    return pl.pallas_call(
        functools.partial(_fourier_prep_kernel, c=c),
        grid=(h,),
        in_specs=[pl.BlockSpec((c, c), lambda i: (0, 0)),
                  pl.BlockSpec((c, c), lambda i: (0, 0)),
                  pl.BlockSpec((1, c, c), lambda i: (i, 0, 0))],
        out_specs=pl.BlockSpec((1, 2 * c, c), lambda i: (i, 0, 0)),
        out_shape=jax.ShapeDtypeStruct((h, 2 * c, c), BF16),
        compiler_params=_cparams(("parallel",)),
        name="fourier_prep",
    )(cc, sc, fourier_w)


def _fft(xs):
    n = len(xs)
    if n == 1:
        return xs
    ev, od = _fft(xs[0::2]), _fft(xs[1::2])
    out = [None] * n
    for k in range(n // 2):
        re, im = od[k]
        if k == 0:
            tr, ti = re, im
        elif 4 * k == n:
            tr, ti = im, -re
        else:
            wr, wi = math.cos(2.0 * math.pi * k / n), -math.sin(2.0 * math.pi * k / n)
            tr, ti = re * wr - im * wi, re * wi + im * wr
        er, ei = ev[k]
        out[k] = (er + tr, ei + ti)
        out[k + n // 2] = (er - tr, ei - ti)
    return out


def _fourier_kernel(x_ref, cs_ref, tw_ref, ab_ref, o_ref, y_ref, *, n1, n2, c, rows):
    for j in range(n2):
        halves = [_unpack_halves(x_ref[s, pl.ds(j, n1, stride=n2), :]) for s in range(x_ref.shape[0])]
        xj = jnp.concatenate([lo for lo, _ in halves] + [hi for _, hi in halves], axis=1).astype(BF16)
        y = jnp.dot(cs_ref[...], xj, preferred_element_type=F32)
        y_ref[0, j] = y[:n1]
        y_ref[1, j] = y[n1:]

    nl = c // LANES

    def tile(i, carry):
        r = pl.multiple_of((i // nl) * SUBLANES, SUBLANES)
        l = pl.multiple_of((i % nl) * LANES, LANES)
        zs = []
        for j in range(n2):
            yc = y_ref[0, j, pl.ds(r, SUBLANES), pl.ds(l, LANES)]
            ys = y_ref[1, j, pl.ds(r, SUBLANES), pl.ds(l, LANES)]
            ct = tw_ref[0, j, pl.ds(r, SUBLANES), :]
            st = tw_ref[1, j, pl.ds(r, SUBLANES), :]
            zs.append((yc * ct - ys * st, -(ys * ct + yc * st)))
        gs = _fft(zs)
        for k in range(n2):
            y_ref[0, k, pl.ds(r, SUBLANES), pl.ds(l, LANES)] = gs[k][0]
            y_ref[1, k, pl.ds(r, SUBLANES), pl.ds(l, LANES)] = gs[k][1]
        return carry
    lax.fori_loop(0, (n1 // SUBLANES) * nl, tile, 0, unroll=2)

    per = max(1, min(n2, rows // n1))
    for s in range(0, n2, per):
        gr = y_ref[0, s:s + per].reshape(per * n1, c).astype(BF16)
        gi = y_ref[1, s:s + per].reshape(per * n1, c).astype(BF16)
        y = jnp.dot(gr, ab_ref[0, :c, :], preferred_element_type=F32)
        y = y + jnp.dot(gi, ab_ref[0, c:, :], preferred_element_type=F32)
        o_ref[0, s * n1:(s + per) * n1, :] = y.astype(o_ref.dtype)


def _fourier_mixer(uf, ab, *, b, rows):
    slabs, n, _ = uf.shape
    seq = n // b
    h, _, c = ab.shape
    spb = slabs // h
    n1 = min(DFT_N1, seq)
    n2 = seq // n1
    assert n1 * n2 == seq and n2 & (n2 - 1) == 0, "sequence length must be N1 * 2^m"
    cmat, smat = _dft_mats(n1, 1.0, BF16)
    cs = jnp.concatenate([cmat, smat], axis=0)
    ang = (jnp.arange(n2, dtype=I32)[:, None] * jnp.arange(n1, dtype=I32)[None, :]).astype(F32)
    ang = ang * (2.0 * math.pi / seq)
    tw = jnp.stack([jnp.cos(ang), jnp.sin(ang)]) * (seq ** -0.5)
    tw = jnp.broadcast_to(tw[..., None], (2, n2, n1, LANES))

    return pl.pallas_call(
        functools.partial(_fourier_kernel, n1=n1, n2=n2, c=c, rows=rows),
        grid=(b, h),
        in_specs=[
            pl.BlockSpec((spb, None, seq, LANES), lambda i, k: (k, i, 0, 0)),
            pl.BlockSpec((2 * n1, n1), lambda i, k: (0, 0), pipeline_mode=pl.Buffered(1)),
            pl.BlockSpec((2, n2, n1, LANES), lambda i, k: (0, 0, 0, 0), pipeline_mode=pl.Buffered(1)),
            pl.BlockSpec((1, 2 * c, c), lambda i, k: (k, 0, 0))],
        out_specs=pl.BlockSpec((1, seq, c), lambda i, k: (i, 0, k)),
        out_shape=jax.ShapeDtypeStruct((b, seq, h * c), BF16),
        scratch_shapes=[pltpu.VMEM((2, n2, n1, c), F32)],
        compiler_params=_cparams(("parallel", "parallel")),
        name="fourier_mixer",
    )(uf.reshape(slabs, b, seq, LANES), cs, tw, ab)


LOGIT_ROWS = 256


def _out_proj_kernel(a_ref, f_ref, w_ref, x_ref, grw_ref, rb_ref, o_ref, lg_ref, ss_acc, pl_acc, *, ka):
    j = pl.program_id(1)
    y = jnp.dot(a_ref[...], w_ref[:ka, :], preferred_element_type=F32)
    y = y + jnp.dot(f_ref[...], w_ref[ka:, :], preferred_element_type=F32)
    o_ref[...] = x_ref[...] + y

    @pl.when(j == 0)
    def _():
        ss_acc[...] = jnp.zeros_like(ss_acc)
        pl_acc[...] = jnp.zeros_like(pl_acc)

    rc = min(LOGIT_ROWS, o_ref.shape[0])
    for r in range(0, o_ref.shape[0], rc):
        hc = o_ref[r:r + rc, :]
        sq = hc * hc
        part = sq[:, :LANES]
        for c0 in range(LANES, hc.shape[1], LANES):
            part = part + sq[:, c0:c0 + LANES]
        ss_acc[r:r + rc, :] += part
        pl_acc[r:r + rc, :] += jnp.dot(hc.astype(BF16), grw_ref[...], preferred_element_type=F32)

    @pl.when(j == pl.num_programs(1) - 1)
    def _():
        d = o_ref.shape[1] * pl.num_programs(1)
        ms = jnp.sum(ss_acc[...], axis=-1, keepdims=True) / d
        lg_ref[...] = pl_acc[...] * lax.rsqrt(ms + RMS_EPS) + rb_ref[...]


def _out_proj(a, f, w, x, grw, rb, *, bm, bn):
    n, ka = a.shape
    kf = f.shape[1]
    d = w.shape[1]
    bm, bn = min(bm, n), min(bn, d)
    return pl.pallas_call(
        functools.partial(_out_proj_kernel, ka=ka),
        grid=(n // bm, d // bn),
        in_specs=[pl.BlockSpec((bm, ka), lambda i, j: (i, 0)),
                  pl.BlockSpec((bm, kf), lambda i, j: (i, 0)),
                  pl.BlockSpec((ka + kf, bn), lambda i, j: (0, j)),
                  pl.BlockSpec((bm, bn), lambda i, j: (i, j)),
                  pl.BlockSpec((bn, LANES), lambda i, j: (j, 0)),
                  pl.BlockSpec((1, LANES), lambda i, j: (0, 0))],
        out_specs=[pl.BlockSpec((bm, bn), lambda i, j: (i, j)),
                   pl.BlockSpec((bm, LANES), lambda i, j: (i, 0))],
        out_shape=[jax.ShapeDtypeStruct((n, d), F32), jax.ShapeDtypeStruct((n, LANES), F32)],
        scratch_shapes=[pltpu.VMEM((bm, LANES), F32), pltpu.VMEM((bm, LANES), F32)],
        compiler_params=_cparams(("parallel", "arbitrary"), vmem=VMEM_LIMIT_HIGH),
        name="w_out_residual",
    )(a, f, w, x, grw, rb)


def _route_kernel(lg_ref, pos_ref, gate_ref, texp_ref, ntl_ref, last_ref, nxt_ref, nval_ref, idx_s, rank_s,
                  *, n_tok, n_exp, tm, tb, tmax_pad):
    shift = tm.bit_length() - 1
    iota_e = lax.broadcasted_iota(I32, (n_exp, tb), 0)
    before = (lax.broadcasted_iota(I32, (tb, tb), 0)
              < lax.broadcasted_iota(I32, (tb, tb), 1)).astype(BF16)

    def pass1(i, counts):
        off = pl.multiple_of(i * tb, tb)
        l = lg_ref[pl.ds(off, tb), :].T[:n_exp, :]
        vals, hots = [], []
        for k in range(TOP_K):
            m = jnp.max(l, axis=0, keepdims=True)
            idx = jnp.min(jnp.where(l == m, iota_e, n_exp), axis=0, keepdims=True)
            hot = iota_e == idx
            l = jnp.where(hot, -jnp.inf, l)
            vals.append(m)
            hots.append(hot)
            idx_s[k:k + 1, pl.ds(off, tb)] = idx
        exps = [jnp.exp(v - vals[0]) for v in vals]
        tot = exps[0]
        for e in exps[1:]:
            tot = tot + e
        sel = hots[0].astype(F32)
        for hot in hots[1:]:
            sel = sel + hot.astype(F32)
        rank = jnp.dot(sel.astype(BF16), before, preferred_element_type=F32) + counts
        for k in range(TOP_K):
            gate_ref[k:k + 1, pl.ds(off, tb)] = exps[k] / tot
            rank_s[k:k + 1, pl.ds(off, tb)] = jnp.sum(jnp.where(hots[k], rank, 0.0), axis=0, keepdims=True)
        return counts + jnp.sum(sel, axis=1, keepdims=True)

    counts = lax.fori_loop(0, n_tok // tb, pass1, jnp.zeros((n_exp, 1), F32))
    ntile = (counts.astype(I32) + (tm - 1)) >> shift
    sub = lax.broadcasted_iota(I32, (n_exp, LANES), 0)
    lane = lax.broadcasted_iota(I32, (n_exp, LANES), 1)
    ntile_row = jnp.sum(jnp.where(sub == lane, ntile, 0), axis=0, keepdims=True)
    start = jnp.sum(jnp.where(lane < sub, ntile_row, 0), axis=1, keepdims=True)
    t_lane = lax.broadcasted_iota(I32, (n_exp, tmax_pad), 1)
    texp_ref[...] = jnp.sum((start <= t_lane).astype(I32), axis=0, keepdims=True) - 1
    own = (start <= t_lane) & (t_lane < start + ntile)
    left = counts.astype(I32) - ((t_lane - start) << shift)
    nval_ref[...] = jnp.sum(jnp.where(own, jnp.clip(left, 0, tm), 0), axis=0, keepdims=True)
    ntl_ref[...] = jnp.sum(ntile, axis=0, keepdims=True) + jnp.zeros((1, LANES), I32)
    last = jnp.where(ntile > 0, (start + ntile - 1) << shift, -1)
    last_ref[...] = jnp.sum(jnp.where(sub == lane, last, 0), axis=0, keepdims=True)
    nxt = jnp.min(jnp.where((lane > sub) & (ntile_row > 0), lane, n_exp), axis=1, keepdims=True)
    nxt_ref[...] = jnp.sum(jnp.where(sub == lane, nxt, 0), axis=0, keepdims=True)
    start_rows = start << shift

    def pass2(i, carry):
        off = pl.multiple_of(i * tb, tb)
        for k in range(TOP_K):
            hot = iota_e == idx_s[k:k + 1, pl.ds(off, tb)]
            base = jnp.sum(jnp.where(hot, start_rows, 0), axis=0, keepdims=True)
            pos_ref[k:k + 1, pl.ds(off, tb)] = rank_s[k:k + 1, pl.ds(off, tb)].astype(I32) + base
        return carry
    lax.fori_loop(0, n_tok // tb, pass2, 0)


def _route(logits, *, n_exp, tm, tmax):
    n_tok = logits.shape[0]
    tb = min(256, n_tok)
    tmax_pad = -(-tmax // LANES) * LANES
    return pl.pallas_call(
        functools.partial(_route_kernel, n_tok=n_tok, n_exp=n_exp, tm=tm, tb=tb, tmax_pad=tmax_pad),
        out_shape=[jax.ShapeDtypeStruct((TOP_K, n_tok), I32),
                   jax.ShapeDtypeStruct((TOP_K, n_tok), F32),
                   jax.ShapeDtypeStruct((1, tmax_pad), I32),
                   jax.ShapeDtypeStruct((1, LANES), I32),
                   jax.ShapeDtypeStruct((1, LANES), I32),
                   jax.ShapeDtypeStruct((1, LANES), I32),
                   jax.ShapeDtypeStruct((1, tmax_pad), I32)],
        scratch_shapes=[pltpu.VMEM((TOP_K, n_tok), I32), pltpu.VMEM((TOP_K, n_tok), F32)],
        compiler_params=_cparams(None),
        name="route",
    )(logits)


def _norm_dispatch_kernel(last_ref, pos_ref, h_ref, g_ref, xs_hbm, z_a, z_b, r_ref, zero_v, zsem, sem,
                          *, n_exp, tm, chunk, rows):
    i = pl.program_id(0)
    dh = z_a.shape[1]

    @pl.when(i == 0)
    def _():
        zero_v[...] = jnp.zeros_like(zero_v)
        for e in range(n_exp):
            r = last_ref[e]

            @pl.when(r >= 0)
            def _():
                cp = pltpu.make_async_copy(zero_v, xs_hbm.at[pl.ds(pl.multiple_of(r, tm), tm)], zsem)
                cp.start()
                cp.wait()

    def wait_rows(z_buf, s):
        for k in range(TOP_K):
            pltpu.make_async_copy(z_buf, xs_hbm.at[pl.ds(0, chunk)], sem.at[s]).wait()

    _inv_rms_rows(h_ref, r_ref)

    def step(z_buf, s, other, s_other):
        @pl.when(i >= 2)
        def _():
            wait_rows(z_buf, s)

        def body(t, carry):
            r = pl.multiple_of(t * rows, rows)
            inv = r_ref[pl.ds(r, rows), :]
            for c0 in range(0, dh, LANES):
                lo = (h_ref[pl.ds(r, rows), c0:c0 + LANES] * inv * g_ref[:, c0:c0 + LANES]).astype(BF16)
                hi = (h_ref[pl.ds(r, rows), dh + c0:dh + c0 + LANES] * inv
                      * g_ref[:, dh + c0:dh + c0 + LANES]).astype(BF16)
                z_buf[pl.ds(r, rows), c0:c0 + LANES] = _pack_halves(lo.astype(F32), hi.astype(F32))
            return carry
        lax.fori_loop(0, chunk // rows, body, 0, unroll=2)

        for n in range(chunk):
            for k in range(TOP_K):
                pltpu.make_async_copy(z_buf.at[pl.ds(n, 1)], xs_hbm.at[pl.ds(pos_ref[0, k, n], 1)],
                                      sem.at[s]).start(priority=k % 2)

        @pl.when(i == pl.num_programs(0) - 1)
        def _():
            wait_rows(z_buf, s)

            @pl.when(i >= 1)
            def _():
                wait_rows(other, s_other)

    @pl.when(i % 2 == 0)
    def _():
        step(z_a, 0, z_b, 1)

    @pl.when(i % 2 == 1)
    def _():
        step(z_b, 1, z_a, 0)


def _norm_dispatch(last, pos3, h, g, *, n_rows, tm):
    n_tok, d = h.shape
    dh = d // 2
    nchunk, _, chunk = pos3.shape
    n_exp = last.shape[0]
    grid_spec = pltpu.PrefetchScalarGridSpec(
        num_scalar_prefetch=1,
        grid=(nchunk,),
        in_specs=[pl.BlockSpec((1, TOP_K, chunk), lambda c, last: (c, 0, 0), memory_space=pltpu.SMEM),
                  pl.BlockSpec((chunk, d), lambda c, last: (c, 0)),
                  pl.BlockSpec((1, d), lambda c, last: (0, 0))],
        out_specs=pl.BlockSpec(memory_space=pl.ANY),
        scratch_shapes=[pltpu.VMEM((chunk, dh), U32), pltpu.VMEM((chunk, dh), U32),
                        pltpu.VMEM((chunk, LANES), F32), pltpu.VMEM((tm, dh), U32),
                        pltpu.SemaphoreType.DMA(()), pltpu.SemaphoreType.DMA((2,))],
    )
    return pl.pallas_call(
        functools.partial(_norm_dispatch_kernel, n_exp=n_exp, tm=tm, chunk=chunk, rows=min(NORM_ROWS, chunk)),
        grid_spec=grid_spec,
        out_shape=jax.ShapeDtypeStruct((n_rows, dh), U32),
        compiler_params=_cparams(("arbitrary",)),
        name="norm2_dispatch",
    )(last, pos3, h, g)


CAST_ROWS = 256


FFN_SUB = 256


def _ffn_kernel(texp_ref, ntl_ref, nxt_ref, nval_ref, x_ref, wg_hbm, wu_hbm, wd_hbm, bgu_ref, bd_ref, o_ref,
                stage_g, stage_u, stage_d, wgu, wd, sems, *, f, n_exp, rows, sub):
    t = pl.program_id(0)
    e = texp_ref[t]
    dh = x_ref.shape[1]
    valid = t < ntl_ref[0]
    first = jnp.logical_or(t == 0, e != texp_ref[jnp.maximum(t - 1, 0)])

    def weight_copies(ex):
        return (pltpu.make_async_copy(wg_hbm.at[ex], stage_g, sems.at[0]),
                pltpu.make_async_copy(wu_hbm.at[ex], stage_u, sems.at[1]),
                pltpu.make_async_copy(wd_hbm.at[ex], stage_d, sems.at[2]))

    @pl.when(t == 0)
    def _():
        for cp in weight_copies(e):
            cp.start()

    @pl.when(jnp.logical_and(valid, first))
    def _():
        for cp in weight_copies(e):
            cp.wait()

        def cast_up(i, carry):
            r = pl.multiple_of(i * rows, rows)
            wgu[pl.ds(r, rows), :f] = stage_g[pl.ds(r, rows), :].astype(BF16)
            wgu[pl.ds(r, rows), f:] = stage_u[pl.ds(r, rows), :].astype(BF16)
            return carry
        lax.fori_loop(0, stage_g.shape[0] // rows, cast_up, 0)

        def cast_down(i, carry):
            r = pl.multiple_of(i * rows, rows)
            wd[pl.ds(r, rows), :] = stage_d[pl.ds(r, rows), :].astype(BF16)
            return carry
        lax.fori_loop(0, stage_d.shape[0] // rows, cast_down, 0)

        nx = nxt_ref[e]

        @pl.when(nx < n_exp)
        def _():
            for cp in weight_copies(nx):
                cp.start(priority=1)

    def ffn_rows(r0):
        lo, hi = _unpack_halves(x_ref[r0:r0 + sub, :])
        hgu = jnp.dot(lo.astype(BF16), wgu[:dh, :], preferred_element_type=F32)
        hgu = hgu + jnp.dot(hi.astype(BF16), wgu[dh:, :], preferred_element_type=F32)
        hgu = hgu + bgu_ref[0]
        hg = jnp.minimum(hgu[:, :f], SWIGLU_LIMIT)
        hu = jnp.clip(hgu[:, f:], -SWIGLU_LIMIT, SWIGLU_LIMIT)
        act = hg * (1.0 / (1.0 + jnp.exp(-SWIGLU_ALPHA * hg))) * (hu + 1.0)
        y = jnp.dot(act.astype(BF16), wd[...], preferred_element_type=F32) + bd_ref[0]
        yr = y.astype(BF16).astype(F32)
        o_ref[r0:r0 + sub, :] = _pack_halves(yr[:, :dh], yr[:, dh:])

    for r0 in range(0, x_ref.shape[0], sub):
        @pl.when(nval_ref[t] > r0)
        def _():
            ffn_rows(r0)

        @pl.when(jnp.logical_and(valid, nval_ref[t] <= r0))
        def _():
            o_ref[r0:r0 + sub, :] = jnp.zeros((sub, dh), o_ref.dtype)


def _expert_ffn(texp, ntl, nxt, nval, xs, w_gate, w_up, w_down, bgu, bd, *, tm):
    n_rows, dh = xs.shape
    n_exp, d, f = w_gate.shape
    tmax = n_rows // tm
    rows = min(CAST_ROWS, f)
    assert d % rows == 0 and f % rows == 0

    def x_map(t, te, nt, nx, nv):
        return (jnp.minimum(t, nt[0] - 1), 0)

    def b_map(t, te, nt, nx, nv):
        return (te[t], 0, 0)

    grid_spec = pltpu.PrefetchScalarGridSpec(
        num_scalar_prefetch=4,
        grid=(tmax,),
        in_specs=[pl.BlockSpec((tm, dh), x_map),
                  pl.BlockSpec(memory_space=pl.ANY),
                  pl.BlockSpec(memory_space=pl.ANY),
                  pl.BlockSpec(memory_space=pl.ANY),
                  pl.BlockSpec((1, 1, 2 * f), b_map),
                  pl.BlockSpec((1, 1, d), b_map)],
        out_specs=pl.BlockSpec((tm, dh), lambda t, te, nt, nx, nv: (t, 0)),
        scratch_shapes=[pltpu.VMEM((d, f), F32), pltpu.VMEM((d, f), F32), pltpu.VMEM((f, d), F32),
                        pltpu.VMEM((d, 2 * f), BF16), pltpu.VMEM((f, d), BF16),
                        pltpu.SemaphoreType.DMA((3,))],
    )
    return pl.pallas_call(
        functools.partial(_ffn_kernel, f=f, n_exp=n_exp, rows=rows, sub=min(FFN_SUB, tm)),
        grid_spec=grid_spec,
        out_shape=jax.ShapeDtypeStruct((n_rows, dh), U32),
        compiler_params=_cparams(("arbitrary",), vmem=VMEM_LIMIT_HIGH),
        name="expert_ffn",
    )(texp, ntl, nxt, nval, xs, w_gate, w_up, w_down, bgu, bd)


COMBINE_ROWS = 16


def _combine_kernel(pos_c, pos_n, ys_hbm, h_ref, g_ref, fg_ref, o_ref, buf_a, buf_b, sem, *, bt, rows):
    i = pl.program_id(0)
    dh = buf_a.shape[-1]

    def start_rows(pos_ref, dst, s, r0):
        for r in range(rows):
            for k in range(TOP_K):
                pltpu.make_async_copy(ys_hbm.at[pl.ds(pos_ref[0, k, r0 + r], 1)],
                                      dst.at[k, pl.ds(r0 + r, 1)], sem.at[s]).start(priority=k % 2)

    def wait_block(dst, s):
        for k in range(TOP_K):
            pltpu.make_async_copy(ys_hbm.at[pl.ds(0, bt)], dst.at[k], sem.at[s]).wait()

    @pl.when(i == 0)
    def _():
        def first(c, carry):
            start_rows(pos_c, buf_a, 0, pl.multiple_of(c * rows, rows))
            return carry
        lax.fori_loop(0, bt // rows, first, 0)

    def step(cur, s_cur, nxt, s_nxt):
        wait_block(cur, s_cur)
        for r0 in range(0, bt, rows):
            start_rows(pos_n, nxt, s_nxt, r0)
            acc_lo = h_ref[r0:r0 + rows, :dh]
            acc_hi = h_ref[r0:r0 + rows, dh:]
            for k in range(TOP_K):
                lo, hi = _unpack_halves(cur[k, r0:r0 + rows, :])
                gk = g_ref[r0:r0 + rows, k:k + 1]
                acc_lo = acc_lo + gk * lo
                acc_hi = acc_hi + gk * hi
            ms = (jnp.sum(acc_lo * acc_lo, axis=-1, keepdims=True)
                  + jnp.sum(acc_hi * acc_hi, axis=-1, keepdims=True)) / (2 * dh)
            inv = lax.rsqrt(ms + RMS_EPS)
            o_ref[r0:r0 + rows, :dh] = acc_lo * inv * fg_ref[:, :dh]
            o_ref[r0:r0 + rows, dh:] = acc_hi * inv * fg_ref[:, dh:]

        @pl.when(i == pl.num_programs(0) - 1)
        def _():
            wait_block(nxt, s_nxt)

    @pl.when(i % 2 == 0)
    def _():
        step(buf_a, 0, buf_b, 1)

    @pl.when(i % 2 == 1)
    def _():
        step(buf_b, 1, buf_a, 0)


def _combine(pos3, ys, h, gates_t, fg, *, bt):
    n, d = h.shape
    nblk = n // bt
    return pl.pallas_call(
        functools.partial(_combine_kernel, bt=bt, rows=min(COMBINE_ROWS, bt)),
        grid=(nblk,),
        in_specs=[pl.BlockSpec((1, TOP_K, bt), lambda i: (i, 0, 0), memory_space=pltpu.SMEM),
                  pl.BlockSpec((1, TOP_K, bt), lambda i: (jnp.minimum(i + 1, nblk - 1), 0, 0),
                               memory_space=pltpu.SMEM),
                  pl.BlockSpec(memory_space=pl.ANY),
                  pl.BlockSpec((bt, d), lambda i: (i, 0)),
                  pl.BlockSpec((bt, TOP_K), lambda i: (i, 0)),
                  pl.BlockSpec((1, d), lambda i: (0, 0))],
        out_specs=pl.BlockSpec((bt, d), lambda i: (i, 0)),
        out_shape=jax.ShapeDtypeStruct((n, d), F32),
        scratch_shapes=[pltpu.VMEM((TOP_K, bt, d // 2), U32), pltpu.VMEM((TOP_K, bt, d // 2), U32),
                        pltpu.SemaphoreType.DMA((2,))],
        compiler_params=_cparams(("arbitrary",)),
        name="combine_final_norm",
    )(pos3, pos3, ys, h, gates_t, fg)


def _layer(x2, b, seq, norm1_g, w_in, pool_w, pool_scale, fourier_w, w_out, norm2_g,
           router_w, router_b, w_gate, b_gate, w_up, b_up, w_down, b_down):
    n, d = x2.shape
    mix = w_in.shape[1]
    pw = pool_scale.shape[0]
    n_exp = router_w.shape[1]
    f = w_gate.shape[2]
    tm = min(512, n)
    tmax = (n * TOP_K) // tm + n_exp

    up, uf = _norm_matmul(x2, norm1_g.reshape(1, d), w_in.astype(BF16), pw=pw, c=fourier_w.shape[1],
                          bm=1024, bn=1024)
    a = _pool_mixer(up.reshape(b, seq, pw), pool_w.astype(BF16), pool_scale.reshape(1, pw), ts=256)
    yf = _fourier_mixer(uf, _fourier_prep(fourier_w), b=b, rows=1024)
    grw = jnp.zeros((d, LANES), F32).at[:, :n_exp].set(norm2_g[:, None] * router_w).astype(BF16)
    rb = jnp.zeros((1, LANES), F32).at[0, :n_exp].set(router_b)
    h, logits = _out_proj(a.reshape(n, pw), yf.reshape(n, mix - pw), w_out.astype(BF16), x2, grw, rb,
                          bm=1024, bn=1024)

    pos, gates, texp, ntl, last, nxt, nval = _route(logits, n_exp=n_exp, tm=tm, tmax=tmax)
    chunk = min(256, n)
    pos_c = pos.reshape(TOP_K, n // chunk, chunk).transpose(1, 0, 2)
    xs = _norm_dispatch(last[0, :n_exp], pos_c, h, norm2_g.reshape(1, d), n_rows=tmax * tm, tm=tm)

    bgu = jnp.concatenate([b_gate, b_up], axis=-1).reshape(n_exp, 1, 2 * f)
    ys = _expert_ffn(texp[0, :tmax], ntl[0, :1], nxt[0, :n_exp], nval[0, :tmax], xs, w_gate, w_up, w_down, bgu,
                     b_down.reshape(n_exp, 1, d), tm=tm)

    bt = min(256, n)
    pos_b = pos.reshape(TOP_K, n // bt, bt).transpose(1, 0, 2)
    return pos_b, ys, h, gates.T


def kernel(x, norm1_g, w_in, pool_w, pool_scale, fourier_w, w_out, norm2_g, router_w, router_b,
           w_gate, b_gate, w_up, b_up, w_down, b_down, final_g):
    b, seq, d = x.shape
    assert w_in.shape[0] == 1, "only a single layer is supported"
    pos_b, ys, h, gates_t = _layer(
        x.reshape(b * seq, d), b, seq, norm1_g[0], w_in[0], pool_w[0], pool_scale[0], fourier_w[0], w_out[0],
        norm2_g[0], router_w[0], router_b[0], w_gate[0], b_gate[0], w_up[0], b_up[0], w_down[0], b_down[0])
    out = _combine(pos_b, ys, h, gates_t, final_g.reshape(1, d), bt=min(256, b * seq))
    return out.reshape(b, seq, d)
```

```python
import functools
import math

import jax
import jax.numpy as jnp
from jax import lax
from jax.experimental import pallas as pl
from jax.experimental.pallas import tpu as pltpu

F32 = jnp.float32
BF16 = jnp.bfloat16
I32 = jnp.int32
U32 = jnp.uint32

RMS_EPS = 1e-5
POOL_WINDOWS = (2, 4, 8, 16)
TOP_K = 4
SWIGLU_LIMIT = 7.0
SWIGLU_ALPHA = 1.702

LANES = 128
BF16_SUBLANES = 16
HALO = 16
NORM_ROWS = 16
NORM_UNROLL = 8
SUBLANES = 8
VMEM_LIMIT = 56 * 1024 * 1024
VMEM_LIMIT_HIGH = 60 * 1024 * 1024
HI_MASK = 0xFFFF0000


def _cparams(sem, vmem=VMEM_LIMIT):
    return pltpu.CompilerParams(dimension_semantics=sem, vmem_limit_bytes=vmem)


def _pack_halves(lo_f32, hi_f32):
    lo = lax.bitcast_convert_type(lo_f32, U32) >> 16
    hi = lax.bitcast_convert_type(hi_f32, U32) & jnp.uint32(HI_MASK)
    return lo | hi


def _unpack_halves(w):
    lo = lax.bitcast_convert_type(w << 16, F32)
    hi = lax.bitcast_convert_type(w & jnp.uint32(HI_MASK), F32)
    return lo, hi


def _inv_rms_rows(x_ref, r_ref):
    d = x_ref.shape[1]

    def body(i, carry):
        r = pl.multiple_of(i * SUBLANES, SUBLANES)
        xc = x_ref[pl.ds(r, SUBLANES), :]
        sq = xc * xc
        acc = sq[:, :LANES]
        for c0 in range(LANES, d, LANES):
            acc = acc + sq[:, c0:c0 + LANES]
        r_ref[pl.ds(r, SUBLANES), :] = acc
        return carry
    steps = x_ref.shape[0] // SUBLANES
    lax.fori_loop(0, steps, body, 0, unroll=min(NORM_UNROLL, steps))
    ms = jnp.sum(r_ref[...], axis=-1, keepdims=True) / d
    r_ref[...] = jnp.broadcast_to(lax.rsqrt(ms + RMS_EPS), r_ref.shape)


def _norm_mm_kernel(x_hbm, g_ref, w_ref, up_ref, uf_ref, x_buf, z_ref, r_ref, xsem, *, rows, pool_blocks, c):
    i = pl.program_id(0)
    j = pl.program_id(1)
    bm = x_buf.shape[0]

    def x_copy(blk):
        return pltpu.make_async_copy(x_hbm.at[pl.ds(pl.multiple_of(blk * bm, bm), bm)], x_buf, xsem)

    @pl.when(j == 0)
    def _():
        @pl.when(i == 0)
        def _():
            x_copy(0).start()
        x_copy(i).wait()
        _inv_rms_rows(x_buf, r_ref)

        def body(t, carry):
            r = pl.multiple_of(t * rows, rows)
            inv = r_ref[pl.ds(r, rows), :]
            for c0 in range(0, x_buf.shape[1], LANES):
                z = x_buf[pl.ds(r, rows), c0:c0 + LANES] * inv * g_ref[:, c0:c0 + LANES]
                z_ref[pl.ds(r, rows), c0:c0 + LANES] = z.astype(BF16)
            return carry
        lax.fori_loop(0, bm // rows, body, 0, unroll=2)

        @pl.when(i + 1 < pl.num_programs(0))
        def _():
            x_copy(i + 1).start(priority=1)

    y = jnp.dot(z_ref[...], w_ref[...], preferred_element_type=F32)

    @pl.when(j < pool_blocks)
    def _():
        up_ref[...] = y.astype(up_ref.dtype)

    @pl.when(j >= pool_blocks)
    def _():
        yr = y.astype(BF16).astype(F32)
        ch = c // 2
        for hd in range(y.shape[1] // c):
            words = _pack_halves(yr[:, hd * c:hd * c + ch], yr[:, hd * c + ch:(hd + 1) * c])
            for s in range(ch // LANES):
                uf_ref[hd * (ch // LANES) + s] = words[:, s * LANES:(s + 1) * LANES]


def _norm_matmul(x, g, w, *, pw, c, bm, bn):
    n, d = x.shape
    m = w.shape[1]
    bm, bn = min(bm, n), min(bn, pw, m - pw)
    assert pw % bn == 0 and (m - pw) % bn == 0 and bn % c == 0 and (c // 2) % LANES == 0
    pool_blocks = pw // bn
    slabs_blk = bn // 2 // LANES
    return pl.pallas_call(
        functools.partial(_norm_mm_kernel, rows=min(NORM_ROWS, bm), pool_blocks=pool_blocks, c=c),
        grid=(n // bm, m // bn),
        in_specs=[pl.BlockSpec(memory_space=pl.ANY),
                  pl.BlockSpec((1, d), lambda i, j: (0, 0)),
                  pl.BlockSpec((d, bn), lambda i, j: (0, j))],
        out_specs=[pl.BlockSpec((bm, bn), lambda i, j: (i, jnp.minimum(j, pool_blocks - 1))),
                   pl.BlockSpec((slabs_blk, bm, LANES), lambda i, j: (jnp.maximum(j - pool_blocks, 0), i, 0))],
        out_shape=[jax.ShapeDtypeStruct((n, pw), BF16),
                   jax.ShapeDtypeStruct(((m - pw) // 2 // LANES, n, LANES), U32)],
        scratch_shapes=[pltpu.VMEM((bm, d), F32), pltpu.VMEM((bm, d), BF16), pltpu.VMEM((bm, LANES), F32),
                        pltpu.SemaphoreType.DMA(())],
        compiler_params=_cparams(("arbitrary", "arbitrary"), vmem=VMEM_LIMIT_HIGH),
        name="norm1_w_in",
    )(x, g, w)


def _pool_kernel(cur_ref, prev_ref, next_ref, pw_ref, sc_ref, o_ref, *, seq, ts, c):
    t = pl.program_id(1)
    has_prev = (t > 0).astype(F32)
    has_next = (t < pl.num_programs(1) - 1).astype(F32)
    n_ext = ts + 2 * HALO
    tok = t * ts + lax.broadcasted_iota(I32, (ts, 1), 0)
    for g, w in enumerate(POOL_WINDOWS):
        sl = slice(g * c, (g + 1) * c)
        cur = cur_ref[0, :, sl].astype(F32)
        prev = prev_ref[0, :, sl].astype(F32) * has_prev
        nxt = next_ref[0, :, sl].astype(F32) * has_next
        ext = jnp.concatenate([prev, cur, nxt], axis=0)
        s = ext + pltpu.roll(ext, 1, 0)
        h = 1
        while 2 * h < w:
            s = pltpu.roll(s, h, 0) + pltpu.roll(s, n_ext - h, 0)
            h *= 2
        win = s[HALO:HALO + ts]
        lo = jnp.maximum(tok - w // 2, 0)
        hi = jnp.minimum(tok + w // 2 - 1, seq - 1)
        cnt = (hi - lo + 1).astype(F32)
        p = win / cnt - cur
        y = jnp.dot(p.astype(BF16), pw_ref[g], preferred_element_type=F32)
        o_ref[0, :, sl] = (y * sc_ref[:, sl]).astype(o_ref.dtype)


def _pool_mixer(u3, pool_w, pool_scale, *, ts):
    b, seq, _ = u3.shape
    g, c, _ = pool_w.shape
    pw = g * c
    ts = min(ts, seq)
    nh = seq // HALO
    per = ts // HALO
    return pl.pallas_call(
        functools.partial(_pool_kernel, seq=seq, ts=ts, c=c),
        grid=(b, seq // ts),
        in_specs=[pl.BlockSpec((1, ts, pw), lambda i, t: (i, t, 0)),
                  pl.BlockSpec((1, HALO, pw), lambda i, t: (i, jnp.maximum(t * per - 1, 0), 0)),
                  pl.BlockSpec((1, HALO, pw), lambda i, t: (i, jnp.minimum((t + 1) * per, nh - 1), 0)),
                  pl.BlockSpec((g, c, c), lambda i, t: (0, 0, 0)),
                  pl.BlockSpec((1, pw), lambda i, t: (0, 0))],
        out_specs=pl.BlockSpec((1, ts, pw), lambda i, t: (i, t, 0)),
        out_shape=jax.ShapeDtypeStruct((b, seq, pw), BF16),
        compiler_params=_cparams(("parallel", "parallel")),
        name="pool_mixer",
    )(u3, u3, u3, pool_w, pool_scale)


DFT_N1 = 256


def _dft_mats(n, scale, dtype):
    j = jnp.arange(n, dtype=I32)
    ang = ((j[:, None] * j[None, :]) % n).astype(F32) * (2.0 * math.pi / n)
    return (jnp.cos(ang) * scale).astype(dtype), (jnp.sin(ang) * scale).astype(dtype)


def _fourier_prep_kernel(cc_ref, sc_ref, w_ref, o_ref, *, c):
    w = w_ref[0]
    o_ref[0, :c, :] = jnp.dot(cc_ref[...], w, preferred_element_type=F32,
                              precision=lax.Precision.HIGHEST).astype(o_ref.dtype)
    o_ref[0, c:, :] = jnp.dot(sc_ref[...], w, preferred_element_type=F32,
                              precision=lax.Precision.HIGHEST).astype(o_ref.dtype)


def _fourier_prep(fourier_w):
    h, c, _ = fourier_w.shape
    cc, sc = _dft_mats(c, c ** -0.5, F32)
    return pl.pallas_call(
        functools.partial(_fourier_prep_kernel, c=c),
        grid=(h,),
        in_specs=[pl.BlockSpec((c, c), lambda i: (0, 0)),
                  pl.BlockSpec((c, c), lambda i: (0, 0)),
                  pl.BlockSpec((1, c, c), lambda i: (i, 0, 0))],
        out_specs=pl.BlockSpec((1, 2 * c, c), lambda i: (i, 0, 0)),
        out_shape=jax.ShapeDtypeStruct((h, 2 * c, c), BF16),
        compiler_params=_cparams(("parallel",)),
        name="fourier_prep",
    )(cc, sc, fourier_w)


def _fft(xs):
    n = len(xs)
    if n == 1:
        return xs
    ev, od = _fft(xs[0::2]), _fft(xs[1::2])
    out = [None] * n
    for k in range(n // 2):
        re, im = od[k]
        if k == 0:
            tr, ti = re, im
        elif 4 * k == n:
            tr, ti = im, -re
        else:
            wr, wi = math.cos(2.0 * math.pi * k / n), -math.sin(2.0 * math.pi * k / n)
            tr, ti = re * wr - im * wi, re * wi + im * wr
        er, ei = ev[k]
        out[k] = (er + tr, ei + ti)
        out[k + n // 2] = (er - tr, ei - ti)
    return out


def _fourier_kernel(x_ref, cs_ref, tw_ref, ab_ref, o_ref, y_ref, *, n1, n2, c, rows):
    for j in range(n2):
        halves = [_unpack_halves(x_ref[s, pl.ds(j, n1, stride=n2), :]) for s in range(x_ref.shape[0])]
        xj = jnp.concatenate([lo for lo, _ in halves] + [hi for _, hi in halves], axis=1).astype(BF16)
        y = jnp.dot(cs_ref[...], xj, preferred_element_type=F32)
        y_ref[0, j] = y[:n1]
        y_ref[1, j] = y[n1:]

    nl = c // LANES

    def tile(i, carry):
        r = pl.multiple_of((i // nl) * SUBLANES, SUBLANES)
        l = pl.multiple_of((i % nl) * LANES, LANES)
        zs = []
        for j in range(n2):
            yc = y_ref[0, j, pl.ds(r, SUBLANES), pl.ds(l, LANES)]
            ys = y_ref[1, j, pl.ds(r, SUBLANES), pl.ds(l, LANES)]
            ct = tw_ref[0, j, pl.ds(r, SUBLANES), :]
            st = tw_ref[1, j, pl.ds(r, SUBLANES), :]
            zs.append((yc * ct - ys * st, -(ys * ct + yc * st)))
        gs = _fft(zs)
        for k in range(n2):
            y_ref[0, k, pl.ds(r, SUBLANES), pl.ds(l, LANES)] = gs[k][0]
            y_ref[1, k, pl.ds(r, SUBLANES), pl.ds(l, LANES)] = gs[k][1]
        return carry
    lax.fori_loop(0, (n1 // SUBLANES) * nl, tile, 0, unroll=2)

    per = max(1, min(n2, rows // n1))
    for s in range(0, n2, per):
        gr = y_ref[0, s:s + per].reshape(per * n1, c).astype(BF16)
        gi = y_ref[1, s:s + per].reshape(per * n1, c).astype(BF16)
        y = jnp.dot(gr, ab_ref[0, :c, :], preferred_element_type=F32)
        y = y + jnp.dot(gi, ab_ref[0, c:, :], preferred_element_type=F32)
        o_ref[0, s * n1:(s + per) * n1, :] = y.astype(o_ref.dtype)


def _fourier_mixer(uf, ab, *, b, rows):
    slabs, n, _ = uf.shape
    seq = n // b
    h, _, c = ab.shape
    spb = slabs // h
    n1 = min(DFT_N1, seq)
    n2 = seq // n1
    assert n1 * n2 == seq and n2 & (n2 - 1) == 0, "sequence length must be N1 * 2^m"
    cmat, smat = _dft_mats(n1, 1.0, BF16)
    cs = jnp.concatenate([cmat, smat], axis=0)
    ang = (jnp.arange(n2, dtype=I32)[:, None] * jnp.arange(n1, dtype=I32)[None, :]).astype(F32)
    ang = ang * (2.0 * math.pi / seq)
    tw = jnp.stack([jnp.cos(ang), jnp.sin(ang)]) * (seq ** -0.5)
    tw = jnp.broadcast_to(tw[..., None], (2, n2, n1, LANES))

    return pl.pallas_call(
        functools.partial(_fourier_kernel, n1=n1, n2=n2, c=c, rows=rows),
        grid=(b, h),
        in_specs=[
            pl.BlockSpec((spb, None, seq, LANES), lambda i, k: (k, i, 0, 0)),
            pl.BlockSpec((2 * n1, n1), lambda i, k: (0, 0), pipeline_mode=pl.Buffered(1)),
            pl.BlockSpec((2, n2, n1, LANES), lambda i, k: (0, 0, 0, 0), pipeline_mode=pl.Buffered(1)),
            pl.BlockSpec((1, 2 * c, c), lambda i, k: (k, 0, 0))],
        out_specs=pl.BlockSpec((1, seq, c), lambda i, k: (i, 0, k)),
        out_shape=jax.ShapeDtypeStruct((b, seq, h * c), BF16),
        scratch_shapes=[pltpu.VMEM((2, n2, n1, c), F32)],
        compiler_params=_cparams(("parallel", "parallel")),
        name="fourier_mixer",
    )(uf.reshape(slabs, b, seq, LANES), cs, tw, ab)


LOGIT_ROWS = 256


def _out_proj_kernel(a_ref, f_ref, w_ref, x_ref, grw_ref, rb_ref, o_ref, lg_ref, ss_acc, pl_acc, *, ka):
    j = pl.program_id(1)
    y = jnp.dot(a_ref[...], w_ref[:ka, :], preferred_element_type=F32)
    y = y + jnp.dot(f_ref[...], w_ref[ka:, :], preferred_element_type=F32)
    o_ref[...] = x_ref[...] + y

    @pl.when(j == 0)
    def _():
        ss_acc[...] = jnp.zeros_like(ss_acc)
        pl_acc[...] = jnp.zeros_like(pl_acc)

    rc = min(LOGIT_ROWS, o_ref.shape[0])
    for r in range(0, o_ref.shape[0], rc):
        hc = o_ref[r:r + rc, :]
        sq = hc * hc
        part = sq[:, :LANES]
        for c0 in range(LANES, hc.shape[1], LANES):
            part = part + sq[:, c0:c0 + LANES]
        ss_acc[r:r + rc, :] += part
        pl_acc[r:r + rc, :] += jnp.dot(hc.astype(BF16), grw_ref[...], preferred_element_type=F32)

    @pl.when(j == pl.num_programs(1) - 1)
    def _():
        d = o_ref.shape[1] * pl.num_programs(1)
        ms = jnp.sum(ss_acc[...], axis=-1, keepdims=True) / d
        lg_ref[...] = pl_acc[...] * lax.rsqrt(ms + RMS_EPS) + rb_ref[...]


def _out_proj(a, f, w, x, grw, rb, *, bm, bn):
    n, ka = a.shape
    kf = f.shape[1]
    d = w.shape[1]
    bm, bn = min(bm, n), min(bn, d)
    return pl.pallas_call(
        functools.partial(_out_proj_kernel, ka=ka),
        grid=(n // bm, d // bn),
        in_specs=[pl.BlockSpec((bm, ka), lambda i, j: (i, 0)),
                  pl.BlockSpec((bm, kf), lambda i, j: (i, 0)),
                  pl.BlockSpec((ka + kf, bn), lambda i, j: (0, j)),
                  pl.BlockSpec((bm, bn), lambda i, j: (i, j)),
                  pl.BlockSpec((bn, LANES), lambda i, j: (j, 0)),
                  pl.BlockSpec((1, LANES), lambda i, j: (0, 0))],
        out_specs=[pl.BlockSpec((bm, bn), lambda i, j: (i, j)),
                   pl.BlockSpec((bm, LANES), lambda i, j: (i, 0))],
        out_shape=[jax.ShapeDtypeStruct((n, d), F32), jax.ShapeDtypeStruct((n, LANES), F32)],
        scratch_shapes=[pltpu.VMEM((bm, LANES), F32), pltpu.VMEM((bm, LANES), F32)],
        compiler_params=_cparams(("parallel", "arbitrary"), vmem=VMEM_LIMIT_HIGH),
        name="w_out_residual",
    )(a, f, w, x, grw, rb)


def _route_kernel(lg_ref, pos_ref, gate_ref, texp_ref, ntl_ref, last_ref, nxt_ref, nval_ref, idx_s, rank_s,
                  *, n_tok, n_exp, tm, tb, tmax_pad):
    shift = tm.bit_length() - 1
    iota_e = lax.broadcasted_iota(I32, (n_exp, tb), 0)
    before = (lax.broadcasted_iota(I32, (tb, tb), 0)
              < lax.broadcasted_iota(I32, (tb, tb), 1)).astype(BF16)

    def pass1(i, counts):
        off = pl.multiple_of(i * tb, tb)
        l = lg_ref[pl.ds(off, tb), :].T[:n_exp, :]
        vals, hots = [], []
        for k in range(TOP_K):
            m = jnp.max(l, axis=0, keepdims=True)
            idx = jnp.min(jnp.where(l == m, iota_e, n_exp), axis=0, keepdims=True)
            hot = iota_e == idx
            l = jnp.where(hot, -jnp.inf, l)
            vals.append(m)
            hots.append(hot)
            idx_s[k:k + 1, pl.ds(off, tb)] = idx
        exps = [jnp.exp(v - vals[0]) for v in vals]
        tot = exps[0]
        for e in exps[1:]:
            tot = tot + e
        sel = hots[0].astype(F32)
        for hot in hots[1:]:
            sel = sel + hot.astype(F32)
        rank = jnp.dot(sel.astype(BF16), before, preferred_element_type=F32) + counts
        for k in range(TOP_K):
            gate_ref[k:k + 1, pl.ds(off, tb)] = exps[k] / tot
            rank_s[k:k + 1, pl.ds(off, tb)] = jnp.sum(jnp.where(hots[k], rank, 0.0), axis=0, keepdims=True)
        return counts + jnp.sum(sel, axis=1, keepdims=True)

    counts = lax.fori_loop(0, n_tok // tb, pass1, jnp.zeros((n_exp, 1), F32))
    ntile = (counts.astype(I32) + (tm - 1)) >> shift
    sub = lax.broadcasted_iota(I32, (n_exp, LANES), 0)
    lane = lax.broadcasted_iota(I32, (n_exp, LANES), 1)
    ntile_row = jnp.sum(jnp.where(sub == lane, ntile, 0), axis=0, keepdims=True)
    start = jnp.sum(jnp.where(lane < sub, ntile_row, 0), axis=1, keepdims=True)
    t_lane = lax.broadcasted_iota(I32, (n_exp, tmax_pad), 1)
    texp_ref[...] = jnp.sum((start <= t_lane).astype(I32), axis=0, keepdims=True) - 1
    own = (start <= t_lane) & (t_lane < start + ntile)
    left = counts.astype(I32) - ((t_lane - start) << shift)
    nval_ref[...] = jnp.sum(jnp.where(own, jnp.clip(left, 0, tm), 0), axis=0, keepdims=True)
    ntl_ref[...] = jnp.sum(ntile, axis=0, keepdims=True) + jnp.zeros((1, LANES), I32)
    last = jnp.where(ntile > 0, (start + ntile - 1) << shift, -1)
    last_ref[...] = jnp.sum(jnp.where(sub == lane, last, 0), axis=0, keepdims=True)
    nxt = jnp.min(jnp.where((lane > sub) & (ntile_row > 0), lane, n_exp), axis=1, keepdims=True)
    nxt_ref[...] = jnp.sum(jnp.where(sub == lane, nxt, 0), axis=0, keepdims=True)
    start_rows = start << shift

    def pass2(i, carry):
        off = pl.multiple_of(i * tb, tb)
        for k in range(TOP_K):
            hot = iota_e == idx_s[k:k + 1, pl.ds(off, tb)]
            base = jnp.sum(jnp.where(hot, start_rows, 0), axis=0, keepdims=True)
            pos_ref[k:k + 1, pl.ds(off, tb)] = rank_s[k:k + 1, pl.ds(off, tb)].astype(I32) + base
        return carry
    lax.fori_loop(0, n_tok // tb, pass2, 0)


def _route(logits, *, n_exp, tm, tmax):
    n_tok = logits.shape[0]
    tb = min(256, n_tok)
    tmax_pad = -(-tmax // LANES) * LANES
    return pl.pallas_call(
        functools.partial(_route_kernel, n_tok=n_tok, n_exp=n_exp, tm=tm, tb=tb, tmax_pad=tmax_pad),
        out_shape=[jax.ShapeDtypeStruct((TOP_K, n_tok), I32),
                   jax.ShapeDtypeStruct((TOP_K, n_tok), F32),
                   jax.ShapeDtypeStruct((1, tmax_pad), I32),
                   jax.ShapeDtypeStruct((1, LANES), I32),
                   jax.ShapeDtypeStruct((1, LANES), I32),
                   jax.ShapeDtypeStruct((1, LANES), I32),
                   jax.ShapeDtypeStruct((1, tmax_pad), I32)],
        scratch_shapes=[pltpu.VMEM((TOP_K, n_tok), I32), pltpu.VMEM((TOP_K, n_tok), F32)],
        compiler_params=_cparams(None),
        name="route",
    )(logits)


def _norm_dispatch_kernel(last_ref, pos_ref, h_ref, g_ref, xs_hbm, z_a, z_b, r_ref, zero_v, zsem, sem,
                          *, n_exp, tm, chunk, rows):
    i = pl.program_id(0)
    dh = z_a.shape[1]

    @pl.when(i == 0)
    def _():
        zero_v[...] = jnp.zeros_like(zero_v)

        def zero_tile(e):
            r = jnp.maximum(last_ref[e], 0)
            return pltpu.make_async_copy(zero_v, xs_hbm.at[pl.ds(pl.multiple_of(r, tm), tm)], zsem)
        for e in range(n_exp):
            @pl.when(last_ref[e] >= 0)
            def _():
                zero_tile(e).start()
        for e in range(n_exp):
            @pl.when(last_ref[e] >= 0)
            def _():
                zero_tile(e).wait()

    def wait_rows(z_buf, s):
        for k in range(TOP_K):
            pltpu.make_async_copy(z_buf, xs_hbm.at[pl.ds(0, chunk)], sem.at[s]).wait()

    _inv_rms_rows(h_ref, r_ref)

    def step(z_buf, s, other, s_other):
        @pl.when(i >= 2)
        def _():
            wait_rows(z_buf, s)

        def body(t, carry):
            r = pl.multiple_of(t * rows, rows)
            inv = r_ref[pl.ds(r, rows), :]
            for c0 in range(0, dh, LANES):
                lo = (h_ref[pl.ds(r, rows), c0:c0 + LANES] * inv * g_ref[:, c0:c0 + LANES]).astype(BF16)
                hi = (h_ref[pl.ds(r, rows), dh + c0:dh + c0 + LANES] * inv
                      * g_ref[:, dh + c0:dh + c0 + LANES]).astype(BF16)
                z_buf[pl.ds(r, rows), c0:c0 + LANES] = _pack_halves(lo.astype(F32), hi.astype(F32))
            return carry
        lax.fori_loop(0, chunk // rows, body, 0, unroll=2)

        for n in range(chunk):
            for k in range(TOP_K):
                pltpu.make_async_copy(z_buf.at[pl.ds(n, 1)], xs_hbm.at[pl.ds(pos_ref[0, k, n], 1)],
                                      sem.at[s]).start(priority=k % 2)

        @pl.when(i == pl.num_programs(0) - 1)
        def _():
            wait_rows(z_buf, s)

            @pl.when(i >= 1)
            def _():
                wait_rows(other, s_other)

    @pl.when(i % 2 == 0)
    def _():
        step(z_a, 0, z_b, 1)

    @pl.when(i % 2 == 1)
    def _():
        step(z_b, 1, z_a, 0)


def _norm_dispatch(last, pos3, h, g, *, n_rows, tm):
    n_tok, d = h.shape
    dh = d // 2
    nchunk, _, chunk = pos3.shape
    n_exp = last.shape[0]
    grid_spec = pltpu.PrefetchScalarGridSpec(
        num_scalar_prefetch=1,
        grid=(nchunk,),
        in_specs=[pl.BlockSpec((1, TOP_K, chunk), lambda c, last: (c, 0, 0), memory_space=pltpu.SMEM),
                  pl.BlockSpec((chunk, d), lambda c, last: (c, 0)),
                  pl.BlockSpec((1, d), lambda c, last: (0, 0))],
        out_specs=pl.BlockSpec(memory_space=pl.ANY),
        scratch_shapes=[pltpu.VMEM((chunk, dh), U32), pltpu.VMEM((chunk, dh), U32),
                        pltpu.VMEM((chunk, LANES), F32), pltpu.VMEM((tm, dh), U32),
                        pltpu.SemaphoreType.DMA(()), pltpu.SemaphoreType.DMA((2,))],
    )
    return pl.pallas_call(
        functools.partial(_norm_dispatch_kernel, n_exp=n_exp, tm=tm, chunk=chunk, rows=min(NORM_ROWS, chunk)),
        grid_spec=grid_spec,
        out_shape=jax.ShapeDtypeStruct((n_rows, dh), U32),
        compiler_params=_cparams(("arbitrary",)),
        name="norm2_dispatch",
    )(last, pos3, h, g)


CAST_ROWS = 256


FFN_SUB = 256


def _ffn_kernel(texp_ref, ntl_ref, nxt_ref, nval_ref, x_ref, wg_hbm, wu_hbm, wd_hbm, bgu_ref, bd_ref, o_ref,
                stage_g, stage_u, stage_d, wgu, wd, sems, *, f, n_exp, rows, sub):
    t = pl.program_id(0)
    e = texp_ref[t]
    dh = x_ref.shape[1]
    valid = t < ntl_ref[0]
    first = jnp.logical_or(t == 0, e != texp_ref[jnp.maximum(t - 1, 0)])

    def weight_copies(ex):
        return (pltpu.make_async_copy(wg_hbm.at[ex], stage_g, sems.at[0]),
                pltpu.make_async_copy(wu_hbm.at[ex], stage_u, sems.at[1]),
                pltpu.make_async_copy(wd_hbm.at[ex], stage_d, sems.at[2]))

    @pl.when(t == 0)
    def _():
        for cp in weight_copies(e):
            cp.start()

    @pl.when(jnp.logical_and(valid, first))
    def _():
        for cp in weight_copies(e):
            cp.wait()

        def cast_up(i, carry):
            r = pl.multiple_of(i * rows, rows)
            wgu[pl.ds(r, rows), :f] = stage_g[pl.ds(r, rows), :].astype(BF16)
            wgu[pl.ds(r, rows), f:] = stage_u[pl.ds(r, rows), :].astype(BF16)
            return carry
        lax.fori_loop(0, stage_g.shape[0] // rows, cast_up, 0)

        def cast_down(i, carry):
            r = pl.multiple_of(i * rows, rows)
            wd[pl.ds(r, rows), :] = stage_d[pl.ds(r, rows), :].astype(BF16)
            return carry
        lax.fori_loop(0, stage_d.shape[0] // rows, cast_down, 0)

        nx = nxt_ref[e]

        @pl.when(nx < n_exp)
        def _():
            for cp in weight_copies(nx):
                cp.start(priority=1)

    def ffn_rows(r0, m):
        lo, hi = _unpack_halves(x_ref[r0:r0 + m, :])
        hgu = jnp.dot(lo.astype(BF16), wgu[:dh, :], preferred_element_type=F32)
        hgu = hgu + jnp.dot(hi.astype(BF16), wgu[dh:, :], preferred_element_type=F32)
        hgu = hgu + bgu_ref[0]
        hg = jnp.minimum(hgu[:, :f], SWIGLU_LIMIT)
        hu = jnp.clip(hgu[:, f:], -SWIGLU_LIMIT, SWIGLU_LIMIT)
        act = (hg * (1.0 / (1.0 + jnp.exp(-SWIGLU_ALPHA * hg))) * (hu + 1.0)).astype(BF16)
        ylo = jnp.dot(act, wd[:, :dh], preferred_element_type=F32) + bd_ref[0, :, :dh]
        yhi = jnp.dot(act, wd[:, dh:], preferred_element_type=F32) + bd_ref[0, :, dh:]
        o_ref[r0:r0 + m, :] = _pack_halves(ylo.astype(BF16).astype(F32), yhi.astype(BF16).astype(F32))

    tmr = x_ref.shape[0]
    nv = nval_ref[t]

    @pl.when(nv > sub)
    def _():
        ffn_rows(0, tmr)

    @pl.when(jnp.logical_and(nv > 0, nv <= sub))
    def _():
        ffn_rows(0, sub)
        if tmr > sub:
            o_ref[sub:, :] = jnp.zeros((tmr - sub, dh), o_ref.dtype)


def _expert_ffn(texp, ntl, nxt, nval, xs, w_gate, w_up, w_down, bgu, bd, *, tm):
    n_rows, dh = xs.shape
    n_exp, d, f = w_gate.shape
    tmax = n_rows // tm
    rows = min(CAST_ROWS, f)
    assert d % rows == 0 and f % rows == 0

    def x_map(t, te, nt, nx, nv):
        return (jnp.minimum(t, nt[0] - 1), 0)

    def b_map(t, te, nt, nx, nv):
        return (te[t], 0, 0)

    grid_spec = pltpu.PrefetchScalarGridSpec(
        num_scalar_prefetch=4,
        grid=(tmax,),
        in_specs=[pl.BlockSpec((tm, dh), x_map),
                  pl.BlockSpec(memory_space=pl.ANY),
                  pl.BlockSpec(memory_space=pl.ANY),
                  pl.BlockSpec(memory_space=pl.ANY),
                  pl.BlockSpec((1, 1, 2 * f), b_map),
                  pl.BlockSpec((1, 1, d), b_map)],
        out_specs=pl.BlockSpec((tm, dh), lambda t, te, nt, nx, nv: (t, 0)),
        scratch_shapes=[pltpu.VMEM((d, f), F32), pltpu.VMEM((d, f), F32), pltpu.VMEM((f, d), F32),
                        pltpu.VMEM((d, 2 * f), BF16), pltpu.VMEM((f, d), BF16),
                        pltpu.SemaphoreType.DMA((3,))],
    )
    return pl.pallas_call(
        functools.partial(_ffn_kernel, f=f, n_exp=n_exp, rows=rows, sub=min(FFN_SUB, tm)),
        grid_spec=grid_spec,
        out_shape=jax.ShapeDtypeStruct((n_rows, dh), U32),
        compiler_params=_cparams(("arbitrary",), vmem=VMEM_LIMIT_HIGH),
        name="expert_ffn",
    )(texp, ntl, nxt, nval, xs, w_gate, w_up, w_down, bgu, bd)


COMBINE_ROWS = 16


def _combine_kernel(pos_c, pos_n, ys_hbm, h_ref, g_ref, fg_ref, o_ref, buf_a, buf_b, sem, *, bt, rows):
    i = pl.program_id(0)
    dh = buf_a.shape[-1]

    def start_rows(pos_ref, dst, s, r0):
        for r in range(rows):
            for k in range(TOP_K):
                pltpu.make_async_copy(ys_hbm.at[pl.ds(pos_ref[0, k, r0 + r], 1)],
                                      dst.at[k, pl.ds(r0 + r, 1)], sem.at[s]).start(priority=k % 2)

    def wait_block(dst, s):
        for k in range(TOP_K):
            pltpu.make_async_copy(ys_hbm.at[pl.ds(0, bt)], dst.at[k], sem.at[s]).wait()

    @pl.when(i == 0)
    def _():
        def first(c, carry):
            start_rows(pos_c, buf_a, 0, pl.multiple_of(c * rows, rows))
            return carry
        lax.fori_loop(0, bt // rows, first, 0)

    def step(cur, s_cur, nxt, s_nxt):
        wait_block(cur, s_cur)
        for r0 in range(0, bt, rows):
            start_rows(pos_n, nxt, s_nxt, r0)
            acc_lo = h_ref[r0:r0 + rows, :dh]
            acc_hi = h_ref[r0:r0 + rows, dh:]
            for k in range(TOP_K):
                lo, hi = _unpack_halves(cur[k, r0:r0 + rows, :])
                gk = g_ref[r0:r0 + rows, k:k + 1]
                acc_lo = acc_lo + gk * lo
                acc_hi = acc_hi + gk * hi
            ms = (jnp.sum(acc_lo * acc_lo, axis=-1, keepdims=True)
                  + jnp.sum(acc_hi * acc_hi, axis=-1, keepdims=True)) / (2 * dh)
            inv = lax.rsqrt(ms + RMS_EPS)
            o_ref[r0:r0 + rows, :dh] = acc_lo * inv * fg_ref[:, :dh]
            o_ref[r0:r0 + rows, dh:] = acc_hi * inv * fg_ref[:, dh:]

        @pl.when(i == pl.num_programs(0) - 1)
        def _():
            wait_block(nxt, s_nxt)

    @pl.when(i % 2 == 0)
    def _():
        step(buf_a, 0, buf_b, 1)

    @pl.when(i % 2 == 1)
    def _():
        step(buf_b, 1, buf_a, 0)


def _combine(pos3, ys, h, gates_t, fg, *, bt):
    n, d = h.shape
    nblk = n // bt
    return pl.pallas_call(
        functools.partial(_combine_kernel, bt=bt, rows=min(COMBINE_ROWS, bt)),
        grid=(nblk,),
        in_specs=[pl.BlockSpec((1, TOP_K, bt), lambda i: (i, 0, 0), memory_space=pltpu.SMEM),
                  pl.BlockSpec((1, TOP_K, bt), lambda i: (jnp.minimum(i + 1, nblk - 1), 0, 0),
                               memory_space=pltpu.SMEM),
                  pl.BlockSpec(memory_space=pl.ANY),
                  pl.BlockSpec((bt, d), lambda i: (i, 0)),
                  pl.BlockSpec((bt, TOP_K), lambda i: (i, 0)),
                  pl.BlockSpec((1, d), lambda i: (0, 0))],
        out_specs=pl.BlockSpec((bt, d), lambda i: (i, 0)),
        out_shape=jax.ShapeDtypeStruct((n, d), F32),
        scratch_shapes=[pltpu.VMEM((TOP_K, bt, d // 2), U32), pltpu.VMEM((TOP_K, bt, d // 2), U32),
                        pltpu.SemaphoreType.DMA((2,))],
        compiler_params=_cparams(("arbitrary",)),
        name="combine_final_norm",
    )(pos3, pos3, ys, h, gates_t, fg)


def _layer(x2, b, seq, norm1_g, w_in, pool_w, pool_scale, fourier_w, w_out, norm2_g,
           router_w, router_b, w_gate, b_gate, w_up, b_up, w_down, b_down):
    n, d = x2.shape
    mix = w_in.shape[1]
    pw = pool_scale.shape[0]
    n_exp = router_w.shape[1]
    f = w_gate.shape[2]
    tm = min(512, n)
    tmax = (n * TOP_K) // tm + n_exp

    up, uf = _norm_matmul(x2, norm1_g.reshape(1, d), w_in.astype(BF16), pw=pw, c=fourier_w.shape[1],
                          bm=1024, bn=1024)
    a = _pool_mixer(up.reshape(b, seq, pw), pool_w.astype(BF16), pool_scale.reshape(1, pw), ts=256)
    yf = _fourier_mixer(uf, _fourier_prep(fourier_w), b=b, rows=1024)
    grw = jnp.zeros((d, LANES), F32).at[:, :n_exp].set(norm2_g[:, None] * router_w).astype(BF16)
    rb = jnp.zeros((1, LANES), F32).at[0, :n_exp].set(router_b)
    h, logits = _out_proj(a.reshape(n, pw), yf.reshape(n, mix - pw), w_out.astype(BF16), x2, grw, rb,
                          bm=1024, bn=1024)

    pos, gates, texp, ntl, last, nxt, nval = _route(logits, n_exp=n_exp, tm=tm, tmax=tmax)
    chunk = min(256, n)
    pos_c = pos.reshape(TOP_K, n // chunk, chunk).transpose(1, 0, 2)
    xs = _norm_dispatch(last[0, :n_exp], pos_c, h, norm2_g.reshape(1, d), n_rows=tmax * tm, tm=tm)

    bgu = jnp.concatenate([b_gate, b_up], axis=-1).reshape(n_exp, 1, 2 * f)
    ys = _expert_ffn(texp[0, :tmax], ntl[0, :1], nxt[0, :n_exp], nval[0, :tmax], xs, w_gate, w_up, w_down, bgu,
                     b_down.reshape(n_exp, 1, d), tm=tm)

    bt = min(256, n)
    pos_b = pos.reshape(TOP_K, n // bt, bt).transpose(1, 0, 2)
    return pos_b, ys, h, gates.T


def kernel(x, norm1_g, w_in, pool_w, pool_scale, fourier_w, w_out, norm2_g, router_w, router_b,
           w_gate, b_gate, w_up, b_up, w_down, b_down, final_g):
    b, seq, d = x.shape
    assert w_in.shape[0] == 1, "only a single layer is supported"
    pos_b, ys, h, gates_t = _layer(
        x.reshape(b * seq, d), b, seq, norm1_g[0], w_in[0], pool_w[0], pool_scale[0], fourier_w[0], w_out[0],
        norm2_g[0], router_w[0], router_b[0], w_gate[0], b_gate[0], w_up[0], b_up[0], w_down[0], b_down[0])
    out = _combine(pos_b, ys, h, gates_t, final_g.reshape(1, d), bt=min(256, b * seq))
    return out.reshape(b, seq, d)
```

```python
import functools
import math

import jax
import jax.numpy as jnp
from jax import lax
from jax.experimental import pallas as pl
from jax.experimental.pallas import tpu as pltpu

F32 = jnp.float32
BF16 = jnp.bfloat16
I32 = jnp.int32
U32 = jnp.uint32

RMS_EPS = 1e-5
POOL_WINDOWS = (2, 4, 8, 16)
TOP_K = 4
SWIGLU_LIMIT = 7.0
SWIGLU_ALPHA = 1.702

LANES = 128
HALO = 16
NORM_ROWS = 16
NORM_UNROLL = 8
SUBLANES = 8
VMEM_LIMIT = 56 * 1024 * 1024
VMEM_LIMIT_HIGH = 60 * 1024 * 1024
HI_MASK = 0xFFFF0000


def _cparams(sem, vmem=VMEM_LIMIT):
    return pltpu.CompilerParams(dimension_semantics=sem, vmem_limit_bytes=vmem)


def _pack_halves(lo_f32, hi_f32):
    lo = lax.bitcast_convert_type(lo_f32, U32) >> 16
    hi = lax.bitcast_convert_type(hi_f32, U32) & jnp.uint32(HI_MASK)
    return lo | hi


def _unpack_halves(w):
    lo = lax.bitcast_convert_type(w << 16, F32)
    hi = lax.bitcast_convert_type(w & jnp.uint32(HI_MASK), F32)
    return lo, hi


def _inv_rms_rows(x_ref, r_ref):
    d = x_ref.shape[1]

    def body(i, carry):
        r = pl.multiple_of(i * SUBLANES, SUBLANES)
        xc = x_ref[pl.ds(r, SUBLANES), :]
        sq = xc * xc
        acc = sq[:, :LANES]
        for c0 in range(LANES, d, LANES):
            acc = acc + sq[:, c0:c0 + LANES]
        r_ref[pl.ds(r, SUBLANES), :] = acc
        return carry
    steps = x_ref.shape[0] // SUBLANES
    lax.fori_loop(0, steps, body, 0, unroll=min(NORM_UNROLL, steps))
    ms = jnp.sum(r_ref[...], axis=-1, keepdims=True) / d
    r_ref[...] = jnp.broadcast_to(lax.rsqrt(ms + RMS_EPS), r_ref.shape)


def _norm_mm_kernel(x_hbm, g_ref, w_ref, up_ref, uf_ref, x_buf, z_ref, r_ref, xsem, *, rows, pool_blocks, c):
    i = pl.program_id(0)
    j = pl.program_id(1)
    bm = x_buf.shape[0]

    def x_copy(blk):
        return pltpu.make_async_copy(x_hbm.at[pl.ds(pl.multiple_of(blk * bm, bm), bm)], x_buf, xsem)

    @pl.when(j == 0)
    def _():
        @pl.when(i == 0)
        def _():
            x_copy(0).start()
        x_copy(i).wait()
        _inv_rms_rows(x_buf, r_ref)

        def body(t, carry):
            r = pl.multiple_of(t * rows, rows)
            inv = r_ref[pl.ds(r, rows), :]
            for c0 in range(0, x_buf.shape[1], LANES):
                z = x_buf[pl.ds(r, rows), c0:c0 + LANES] * inv * g_ref[:, c0:c0 + LANES]
                z_ref[pl.ds(r, rows), c0:c0 + LANES] = z.astype(BF16)
            return carry
        lax.fori_loop(0, bm // rows, body, 0, unroll=2)

        @pl.when(i + 1 < pl.num_programs(0))
        def _():
            x_copy(i + 1).start(priority=1)

    y = jnp.dot(z_ref[...], w_ref[...], preferred_element_type=F32)

    @pl.when(j < pool_blocks)
    def _():
        up_ref[...] = y.astype(up_ref.dtype)

    @pl.when(j >= pool_blocks)
    def _():
        yr = y.astype(BF16).astype(F32)
        ch = c // 2
        for hd in range(y.shape[1] // c):
            words = _pack_halves(yr[:, hd * c:hd * c + ch], yr[:, hd * c + ch:(hd + 1) * c])
            for s in range(ch // LANES):
                uf_ref[hd * (ch // LANES) + s] = words[:, s * LANES:(s + 1) * LANES]


def _norm_matmul(x, g, w, *, pw, c, bm, bn):
    n, d = x.shape
    m = w.shape[1]
    bm, bn = min(bm, n), min(bn, pw, m - pw)
    assert pw % bn == 0 and (m - pw) % bn == 0 and bn % c == 0 and (c // 2) % LANES == 0
    pool_blocks = pw // bn
    slabs_blk = bn // 2 // LANES
    return pl.pallas_call(
        functools.partial(_norm_mm_kernel, rows=min(NORM_ROWS, bm), pool_blocks=pool_blocks, c=c),
        grid=(n // bm, m // bn),
        in_specs=[pl.BlockSpec(memory_space=pl.ANY),
                  pl.BlockSpec((1, d), lambda i, j: (0, 0)),
                  pl.BlockSpec((d, bn), lambda i, j: (0, j))],
        out_specs=[pl.BlockSpec((bm, bn), lambda i, j: (i, jnp.minimum(j, pool_blocks - 1))),
                   pl.BlockSpec((slabs_blk, bm, LANES), lambda i, j: (jnp.maximum(j - pool_blocks, 0), i, 0))],
        out_shape=[jax.ShapeDtypeStruct((n, pw), BF16),
                   jax.ShapeDtypeStruct(((m - pw) // 2 // LANES, n, LANES), U32)],
        scratch_shapes=[pltpu.VMEM((bm, d), F32), pltpu.VMEM((bm, d), BF16), pltpu.VMEM((bm, LANES), F32),
                        pltpu.SemaphoreType.DMA(())],
        compiler_params=_cparams(("arbitrary", "arbitrary"), vmem=VMEM_LIMIT_HIGH),
        name="norm1_w_in",
    )(x, g, w)


def _pool_kernel(cur_ref, prev_ref, next_ref, pw_ref, sc_ref, o_ref, *, seq, ts, c):
    t = pl.program_id(1)
    has_prev = (t > 0).astype(F32)
    has_next = (t < pl.num_programs(1) - 1).astype(F32)
    n_ext = ts + 2 * HALO
    tok = t * ts + lax.broadcasted_iota(I32, (ts, 1), 0)
    for g, w in enumerate(POOL_WINDOWS):
        sl = slice(g * c, (g + 1) * c)
        cur = cur_ref[0, :, sl].astype(F32)
        prev = prev_ref[0, :, sl].astype(F32) * has_prev
        nxt = next_ref[0, :, sl].astype(F32) * has_next
        ext = jnp.concatenate([prev, cur, nxt], axis=0)
        s = ext + pltpu.roll(ext, 1, 0)
        h = 1
        while 2 * h < w:
            s = pltpu.roll(s, h, 0) + pltpu.roll(s, n_ext - h, 0)
            h *= 2
        win = s[HALO:HALO + ts]
        lo = jnp.maximum(tok - w // 2, 0)
        hi = jnp.minimum(tok + w // 2 - 1, seq - 1)
        cnt = (hi - lo + 1).astype(F32)
        p = win / cnt - cur
        y = jnp.dot(p.astype(BF16), pw_ref[g], preferred_element_type=F32)
        o_ref[0, :, sl] = (y * sc_ref[:, sl]).astype(o_ref.dtype)


def _pool_mixer(u3, pool_w, pool_scale, *, ts):
    b, seq, _ = u3.shape
    g, c, _ = pool_w.shape
    pw = g * c
    ts = min(ts, seq)
    nh = seq // HALO
    per = ts // HALO
    return pl.pallas_call(
        functools.partial(_pool_kernel, seq=seq, ts=ts, c=c),
        grid=(b, seq // ts),
        in_specs=[pl.BlockSpec((1, ts, pw), lambda i, t: (i, t, 0)),
                  pl.BlockSpec((1, HALO, pw), lambda i, t: (i, jnp.maximum(t * per - 1, 0), 0)),
                  pl.BlockSpec((1, HALO, pw), lambda i, t: (i, jnp.minimum((t + 1) * per, nh - 1), 0)),
                  pl.BlockSpec((g, c, c), lambda i, t: (0, 0, 0)),
                  pl.BlockSpec((1, pw), lambda i, t: (0, 0))],
        out_specs=pl.BlockSpec((1, ts, pw), lambda i, t: (i, t, 0)),
        out_shape=jax.ShapeDtypeStruct((b, seq, pw), BF16),
        compiler_params=_cparams(("parallel", "parallel")),
        name="pool_mixer",
    )(u3, u3, u3, pool_w, pool_scale)


DFT_N1 = 256


def _dft_mats(n, scale, dtype):
    j = jnp.arange(n, dtype=I32)
    ang = ((j[:, None] * j[None, :]) % n).astype(F32) * (2.0 * math.pi / n)
    return (jnp.cos(ang) * scale).astype(dtype), (jnp.sin(ang) * scale).astype(dtype)


def _fourier_prep_kernel(cc_ref, sc_ref, w_ref, o_ref, *, c):
    w = w_ref[0]
    o_ref[0, :c, :] = jnp.dot(cc_ref[...], w, preferred_element_type=F32,
                              precision=lax.Precision.HIGHEST).astype(o_ref.dtype)
    o_ref[0, c:, :] = jnp.dot(sc_ref[...], w, preferred_element_type=F32,
                              precision=lax.Precision.HIGHEST).astype(o_ref.dtype)


def _fourier_prep(fourier_w):
    h, c, _ = fourier_w.shape
    cc, sc = _dft_mats(c, c ** -0.5, F32)
    return pl.pallas_call(
        functools.partial(_fourier_prep_kernel, c=c),
        grid=(h,),
        in_specs=[pl.BlockSpec((c, c), lambda i: (0, 0)),
                  pl.BlockSpec((c, c), lambda i: (0, 0)),
                  pl.BlockSpec((1, c, c), lambda i: (i, 0, 0))],
        out_specs=pl.BlockSpec((1, 2 * c, c), lambda i: (i, 0, 0)),
        out_shape=jax.ShapeDtypeStruct((h, 2 * c, c), BF16),
        compiler_params=_cparams(("parallel",)),
        name="fourier_prep",
    )(cc, sc, fourier_w)


def _fft(xs):
    n = len(xs)
    if n == 1:
        return xs
    ev, od = _fft(xs[0::2]), _fft(xs[1::2])
    out = [None] * n
    for k in range(n // 2):
        re, im = od[k]
        if k == 0:
            tr, ti = re, im
        elif 4 * k == n:
            tr, ti = im, -re
        else:
            wr, wi = math.cos(2.0 * math.pi * k / n), -math.sin(2.0 * math.pi * k / n)
            tr, ti = re * wr - im * wi, re * wi + im * wr
        er, ei = ev[k]
        out[k] = (er + tr, ei + ti)
        out[k + n // 2] = (er - tr, ei - ti)
    return out


def _fourier_kernel(x_ref, cs_ref, tw_ref, ab_ref, o_ref, y_ref, *, n1, n2, c, rows):
    for j in range(n2):
        halves = [_unpack_halves(x_ref[s, pl.ds(j, n1, stride=n2), :]) for s in range(x_ref.shape[0])]
        xj = jnp.concatenate([lo for lo, _ in halves] + [hi for _, hi in halves], axis=1).astype(BF16)
        y = jnp.dot(cs_ref[...], xj, preferred_element_type=F32)
        y_ref[0, j] = y[:n1]
        y_ref[1, j] = y[n1:]

    nl = c // LANES

    def tile(i, carry):
        r = pl.multiple_of((i // nl) * SUBLANES, SUBLANES)
        l = pl.multiple_of((i % nl) * LANES, LANES)
        zs = []
        for j in range(n2):
            yc = y_ref[0, j, pl.ds(r, SUBLANES), pl.ds(l, LANES)]
            ys = y_ref[1, j, pl.ds(r, SUBLANES), pl.ds(l, LANES)]
            ct = tw_ref[0, j, pl.ds(r, SUBLANES), :]
            st = tw_ref[1, j, pl.ds(r, SUBLANES), :]
            zs.append((yc * ct - ys * st, -(ys * ct + yc * st)))
        gs = _fft(zs)
        for k in range(n2):
            y_ref[0, k, pl.ds(r, SUBLANES), pl.ds(l, LANES)] = gs[k][0]
            y_ref[1, k, pl.ds(r, SUBLANES), pl.ds(l, LANES)] = gs[k][1]
        return carry
    lax.fori_loop(0, (n1 // SUBLANES) * nl, tile, 0, unroll=2)

    per = max(1, min(n2, rows // n1))
    for s in range(0, n2, per):
        gr = y_ref[0, s:s + per].reshape(per * n1, c).astype(BF16)
        gi = y_ref[1, s:s + per].reshape(per * n1, c).astype(BF16)
        y = jnp.dot(gr, ab_ref[0, :c, :], preferred_element_type=F32)
        y = y + jnp.dot(gi, ab_ref[0, c:, :], preferred_element_type=F32)
        o_ref[0, s * n1:(s + per) * n1, :] = y.astype(o_ref.dtype)


def _fourier_mixer(uf, ab, *, b, rows):
    slabs, n, _ = uf.shape
    seq = n // b
    h, _, c = ab.shape
    spb = slabs // h
    n1 = min(DFT_N1, seq)
    n2 = seq // n1
    assert n1 * n2 == seq and n2 & (n2 - 1) == 0, "sequence length must be N1 * 2^m"
    cmat, smat = _dft_mats(n1, 1.0, BF16)
    cs = jnp.concatenate([cmat, smat], axis=0)
    ang = (jnp.arange(n2, dtype=I32)[:, None] * jnp.arange(n1, dtype=I32)[None, :]).astype(F32)
    ang = ang * (2.0 * math.pi / seq)
    tw = jnp.stack([jnp.cos(ang), jnp.sin(ang)]) * (seq ** -0.5)
    tw = jnp.broadcast_to(tw[..., None], (2, n2, n1, LANES))

    return pl.pallas_call(
        functools.partial(_fourier_kernel, n1=n1, n2=n2, c=c, rows=rows),
        grid=(b, h),
        in_specs=[
            pl.BlockSpec((spb, None, seq, LANES), lambda i, k: (k, i, 0, 0)),
            pl.BlockSpec((2 * n1, n1), lambda i, k: (0, 0), pipeline_mode=pl.Buffered(1)),
            pl.BlockSpec((2, n2, n1, LANES), lambda i, k: (0, 0, 0, 0), pipeline_mode=pl.Buffered(1)),
            pl.BlockSpec((1, 2 * c, c), lambda i, k: (k, 0, 0))],
        out_specs=pl.BlockSpec((1, seq, c), lambda i, k: (i, 0, k)),
        out_shape=jax.ShapeDtypeStruct((b, seq, h * c), BF16),
        scratch_shapes=[pltpu.VMEM((2, n2, n1, c), F32)],
        compiler_params=_cparams(("parallel", "parallel")),
        name="fourier_mixer",
    )(uf.reshape(slabs, b, seq, LANES), cs, tw, ab)


LOGIT_ROWS = 256


def _out_proj_kernel(a_ref, f_ref, w_ref, x_ref, grw_ref, rb_ref, o_ref, lg_ref, ss_acc, pl_acc, *, ka):
    j = pl.program_id(1)
    y = jnp.dot(a_ref[...], w_ref[:ka, :], preferred_element_type=F32)
    y = y + jnp.dot(f_ref[...], w_ref[ka:, :], preferred_element_type=F32)
    o_ref[...] = x_ref[...] + y

    @pl.when(j == 0)
    def _():
        ss_acc[...] = jnp.zeros_like(ss_acc)
        pl_acc[...] = jnp.zeros_like(pl_acc)

    rc = min(LOGIT_ROWS, o_ref.shape[0])
    for r in range(0, o_ref.shape[0], rc):
        hc = o_ref[r:r + rc, :]
        sq = hc * hc
        part = sq[:, :LANES]
        for c0 in range(LANES, hc.shape[1], LANES):
            part = part + sq[:, c0:c0 + LANES]
        ss_acc[r:r + rc, :] += part
        pl_acc[r:r + rc, :] += jnp.dot(hc.astype(BF16), grw_ref[...], preferred_element_type=F32)

    @pl.when(j == pl.num_programs(1) - 1)
    def _():
        d = o_ref.shape[1] * pl.num_programs(1)
        ms = jnp.sum(ss_acc[...], axis=-1, keepdims=True) / d
        lg_ref[...] = pl_acc[...] * lax.rsqrt(ms + RMS_EPS) + rb_ref[...]


def _out_proj(a, f, w, x, grw, rb, *, bm, bn):
    n, ka = a.shape
    kf = f.shape[1]
    d = w.shape[1]
    bm, bn = min(bm, n), min(bn, d)
    return pl.pallas_call(
        functools.partial(_out_proj_kernel, ka=ka),
        grid=(n // bm, d // bn),
        in_specs=[pl.BlockSpec((bm, ka), lambda i, j: (i, 0)),
                  pl.BlockSpec((bm, kf), lambda i, j: (i, 0)),
                  pl.BlockSpec((ka + kf, bn), lambda i, j: (0, j)),
                  pl.BlockSpec((bm, bn), lambda i, j: (i, j)),
                  pl.BlockSpec((bn, LANES), lambda i, j: (j, 0)),
                  pl.BlockSpec((1, LANES), lambda i, j: (0, 0))],
        out_specs=[pl.BlockSpec((bm, bn), lambda i, j: (i, j)),
                   pl.BlockSpec((bm, LANES), lambda i, j: (i, 0))],
        out_shape=[jax.ShapeDtypeStruct((n, d), F32), jax.ShapeDtypeStruct((n, LANES), F32)],
        scratch_shapes=[pltpu.VMEM((bm, LANES), F32), pltpu.VMEM((bm, LANES), F32)],
        compiler_params=_cparams(("parallel", "arbitrary"), vmem=VMEM_LIMIT_HIGH),
        name="w_out_residual",
    )(a, f, w, x, grw, rb)


def _route_kernel(lg_ref, pos_ref, gate_ref, texp_ref, ntl_ref, last_ref, nxt_ref, nval_ref, idx_s, rank_s,
                  *, n_tok, n_exp, tm, tb, tmax_pad):
    shift = tm.bit_length() - 1
    iota_e = lax.broadcasted_iota(I32, (n_exp, tb), 0)
    before = (lax.broadcasted_iota(I32, (tb, tb), 0)
              < lax.broadcasted_iota(I32, (tb, tb), 1)).astype(BF16)

    def pass1(i, counts):
        off = pl.multiple_of(i * tb, tb)
        l = lg_ref[pl.ds(off, tb), :].T[:n_exp, :]
        vals, hots = [], []
        for k in range(TOP_K):
            m = jnp.max(l, axis=0, keepdims=True)
            idx = jnp.min(jnp.where(l == m, iota_e, n_exp), axis=0, keepdims=True)
            hot = iota_e == idx
            l = jnp.where(hot, -jnp.inf, l)
            vals.append(m)
            hots.append(hot)
            idx_s[k:k + 1, pl.ds(off, tb)] = idx
        exps = [jnp.exp(v - vals[0]) for v in vals]
        tot = exps[0]
        for e in exps[1:]:
            tot = tot + e
        sel = hots[0].astype(F32)
        for hot in hots[1:]:
            sel = sel + hot.astype(F32)
        rank = jnp.dot(sel.astype(BF16), before, preferred_element_type=F32) + counts
        for k in range(TOP_K):
            gate_ref[k:k + 1, pl.ds(off, tb)] = exps[k] / tot
            rank_s[k:k + 1, pl.ds(off, tb)] = jnp.sum(jnp.where(hots[k], rank, 0.0), axis=0, keepdims=True)
        return counts + jnp.sum(sel, axis=1, keepdims=True)

    counts = lax.fori_loop(0, n_tok // tb, pass1, jnp.zeros((n_exp, 1), F32))
    ntile = (counts.astype(I32) + (tm - 1)) >> shift
    sub = lax.broadcasted_iota(I32, (n_exp, LANES), 0)
    lane = lax.broadcasted_iota(I32, (n_exp, LANES), 1)
    ntile_row = jnp.sum(jnp.where(sub == lane, ntile, 0), axis=0, keepdims=True)
    start = jnp.sum(jnp.where(lane < sub, ntile_row, 0), axis=1, keepdims=True)
    t_lane = lax.broadcasted_iota(I32, (n_exp, tmax_pad), 1)
    texp_ref[...] = jnp.sum((start <= t_lane).astype(I32), axis=0, keepdims=True) - 1
    own = (start <= t_lane) & (t_lane < start + ntile)
    left = counts.astype(I32) - ((t_lane - start) << shift)
    nval_ref[...] = jnp.sum(jnp.where(own, jnp.clip(left, 0, tm), 0), axis=0, keepdims=True)
    ntl_ref[...] = jnp.sum(ntile, axis=0, keepdims=True) + jnp.zeros((1, LANES), I32)
    last = jnp.where(ntile > 0, (start + ntile - 1) << shift, -1)
    last_ref[...] = jnp.sum(jnp.where(sub == lane, last, 0), axis=0, keepdims=True)
    nxt = jnp.min(jnp.where((lane > sub) & (ntile_row > 0), lane, n_exp), axis=1, keepdims=True)
    nxt_ref[...] = jnp.sum(jnp.where(sub == lane, nxt, 0), axis=0, keepdims=True)
    start_rows = start << shift

    def pass2(i, carry):
        off = pl.multiple_of(i * tb, tb)
        for k in range(TOP_K):
            hot = iota_e == idx_s[k:k + 1, pl.ds(off, tb)]
            base = jnp.sum(jnp.where(hot, start_rows, 0), axis=0, keepdims=True)
            pos_ref[k:k + 1, pl.ds(off, tb)] = rank_s[k:k + 1, pl.ds(off, tb)].astype(I32) + base
        return carry
    lax.fori_loop(0, n_tok // tb, pass2, 0)


def _route(logits, *, n_exp, tm, tmax):
    n_tok = logits.shape[0]
    tb = min(256, n_tok)
    tmax_pad = -(-tmax // LANES) * LANES
    return pl.pallas_call(
        functools.partial(_route_kernel, n_tok=n_tok, n_exp=n_exp, tm=tm, tb=tb, tmax_pad=tmax_pad),
        out_shape=[jax.ShapeDtypeStruct((TOP_K, n_tok), I32),
                   jax.ShapeDtypeStruct((TOP_K, n_tok), F32),
                   jax.ShapeDtypeStruct((1, tmax_pad), I32),
                   jax.ShapeDtypeStruct((1, LANES), I32),
                   jax.ShapeDtypeStruct((1, LANES), I32),
                   jax.ShapeDtypeStruct((1, LANES), I32),
                   jax.ShapeDtypeStruct((1, tmax_pad), I32)],
        scratch_shapes=[pltpu.VMEM((TOP_K, n_tok), I32), pltpu.VMEM((TOP_K, n_tok), F32)],
        compiler_params=_cparams(None),
        name="route",
    )(logits)


def _norm_dispatch_kernel(last_ref, pos_ref, h_ref, g_ref, xs_hbm, z_a, z_b, r_ref, zero_v, zsem, sem,
                          *, n_exp, tm, chunk, rows):
    i = pl.program_id(0)
    dh = z_a.shape[1]

    @pl.when(i == 0)
    def _():
        zero_v[...] = jnp.zeros_like(zero_v)

        def zero_tile(e):
            r = jnp.maximum(last_ref[e], 0)
            return pltpu.make_async_copy(zero_v, xs_hbm.at[pl.ds(pl.multiple_of(r, tm), tm)], zsem)
        for e in range(n_exp):
            @pl.when(last_ref[e] >= 0)
            def _():
                zero_tile(e).start()
        for e in range(n_exp):
            @pl.when(last_ref[e] >= 0)
            def _():
                zero_tile(e).wait()

    def wait_rows(z_buf, s):
        for k in range(TOP_K):
            pltpu.make_async_copy(z_buf, xs_hbm.at[pl.ds(0, chunk)], sem.at[s]).wait()

    _inv_rms_rows(h_ref, r_ref)

    def step(z_buf, s, other, s_other):
        @pl.when(i >= 2)
        def _():
            wait_rows(z_buf, s)

        def body(t, carry):
            r = pl.multiple_of(t * rows, rows)
            inv = r_ref[pl.ds(r, rows), :]
            for c0 in range(0, dh, LANES):
                lo = (h_ref[pl.ds(r, rows), c0:c0 + LANES] * inv * g_ref[:, c0:c0 + LANES]).astype(BF16)
                hi = (h_ref[pl.ds(r, rows), dh + c0:dh + c0 + LANES] * inv
                      * g_ref[:, dh + c0:dh + c0 + LANES]).astype(BF16)
                z_buf[pl.ds(r, rows), c0:c0 + LANES] = _pack_halves(lo.astype(F32), hi.astype(F32))
            return carry
        lax.fori_loop(0, chunk // rows, body, 0, unroll=2)

        for n in range(chunk):
            for k in range(TOP_K):
                pltpu.make_async_copy(z_buf.at[pl.ds(n, 1)], xs_hbm.at[pl.ds(pos_ref[0, k, n], 1)],
                                      sem.at[s]).start(priority=k % 2)

        @pl.when(i == pl.num_programs(0) - 1)
        def _():
            wait_rows(z_buf, s)

            @pl.when(i >= 1)
            def _():
                wait_rows(other, s_other)

    @pl.when(i % 2 == 0)
    def _():
        step(z_a, 0, z_b, 1)

    @pl.when(i % 2 == 1)
    def _():
        step(z_b, 1, z_a, 0)


def _norm_dispatch(last, pos3, h, g, *, n_rows, tm):
    n_tok, d = h.shape
    dh = d // 2
    nchunk, _, chunk = pos3.shape
    n_exp = last.shape[0]
    grid_spec = pltpu.PrefetchScalarGridSpec(
        num_scalar_prefetch=1,
        grid=(nchunk,),
        in_specs=[pl.BlockSpec((1, TOP_K, chunk), lambda c, last: (c, 0, 0), memory_space=pltpu.SMEM),
                  pl.BlockSpec((chunk, d), lambda c, last: (c, 0)),
                  pl.BlockSpec((1, d), lambda c, last: (0, 0))],
        out_specs=pl.BlockSpec(memory_space=pl.ANY),
        scratch_shapes=[pltpu.VMEM((chunk, dh), U32), pltpu.VMEM((chunk, dh), U32),
                        pltpu.VMEM((chunk, LANES), F32), pltpu.VMEM((tm, dh), U32),
                        pltpu.SemaphoreType.DMA(()), pltpu.SemaphoreType.DMA((2,))],
    )
    return pl.pallas_call(
        functools.partial(_norm_dispatch_kernel, n_exp=n_exp, tm=tm, chunk=chunk, rows=min(NORM_ROWS, chunk)),
        grid_spec=grid_spec,
        out_shape=jax.ShapeDtypeStruct((n_rows, dh), U32),
        compiler_params=_cparams(("arbitrary",)),
        name="norm2_dispatch",
    )(last, pos3, h, g)


CAST_ROWS = 256


FFN_SUB = 256


def _ffn_kernel(texp_ref, ntl_ref, nxt_ref, nval_ref, x_ref, wg_hbm, wu_hbm, wd_hbm, bgu_ref, bd_ref, o_ref,
                stage_g, stage_u, stage_d, wgu, wd, sems, *, f, n_exp, rows, sub):
    t = pl.program_id(0)
    e = texp_ref[t]
    dh = x_ref.shape[1]
    valid = t < ntl_ref[0]
    first = jnp.logical_or(t == 0, e != texp_ref[jnp.maximum(t - 1, 0)])

    def weight_copies(ex):
        return (pltpu.make_async_copy(wg_hbm.at[ex], stage_g, sems.at[0]),
                pltpu.make_async_copy(wu_hbm.at[ex], stage_u, sems.at[1]),
                pltpu.make_async_copy(wd_hbm.at[ex], stage_d, sems.at[2]))

    @pl.when(t == 0)
    def _():
        for cp in weight_copies(e):
            cp.start()

    @pl.when(jnp.logical_and(valid, first))
    def _():
        for cp in weight_copies(e):
            cp.wait()

        def cast_up(i, carry):
            r = pl.multiple_of(i * rows, rows)
            wgu[pl.ds(r, rows), :f] = stage_g[pl.ds(r, rows), :].astype(BF16)
            wgu[pl.ds(r, rows), f:] = stage_u[pl.ds(r, rows), :].astype(BF16)
            return carry
        lax.fori_loop(0, stage_g.shape[0] // rows, cast_up, 0)

        def cast_down(i, carry):
            r = pl.multiple_of(i * rows, rows)
            wd[pl.ds(r, rows), :] = stage_d[pl.ds(r, rows), :].astype(BF16)
            return carry
        lax.fori_loop(0, stage_d.shape[0] // rows, cast_down, 0)

        nx = nxt_ref[e]

        @pl.when(nx < n_exp)
        def _():
            for cp in weight_copies(nx):
                cp.start(priority=1)

    def ffn_rows(r0, m):
        lo, hi = _unpack_halves(x_ref[r0:r0 + m, :])
        hgu = jnp.dot(lo.astype(BF16), wgu[:dh, :], preferred_element_type=F32)
        hgu = hgu + jnp.dot(hi.astype(BF16), wgu[dh:, :], preferred_element_type=F32)
        hgu = hgu + bgu_ref[0]
        hg = jnp.minimum(hgu[:, :f], SWIGLU_LIMIT)
        hu = jnp.clip(hgu[:, f:], -SWIGLU_LIMIT, SWIGLU_LIMIT)
        act = (hg * (1.0 / (1.0 + jnp.exp(-SWIGLU_ALPHA * hg))) * (hu + 1.0)).astype(BF16)
        ylo = jnp.dot(act, wd[:, :dh], preferred_element_type=F32) + bd_ref[0, :, :dh]
        yhi = jnp.dot(act, wd[:, dh:], preferred_element_type=F32) + bd_ref[0, :, dh:]
        o_ref[r0:r0 + m, :] = _pack_halves(ylo.astype(BF16).astype(F32), yhi.astype(BF16).astype(F32))

    tmr = x_ref.shape[0]
    nv = nval_ref[t]

    @pl.when(nv > sub)
    def _():
        ffn_rows(0, tmr)

    @pl.when(jnp.logical_and(nv > 0, nv <= sub))
    def _():
        ffn_rows(0, sub)
        if tmr > sub:
            o_ref[sub:, :] = jnp.zeros((tmr - sub, dh), o_ref.dtype)


def _expert_ffn(texp, ntl, nxt, nval, xs, w_gate, w_up, w_down, bgu, bd, *, tm):
    n_rows, dh = xs.shape
    n_exp, d, f = w_gate.shape
    tmax = n_rows // tm
    rows = min(CAST_ROWS, f)
    assert d % rows == 0 and f % rows == 0

    def x_map(t, te, nt, nx, nv):
        return (jnp.minimum(t, nt[0] - 1), 0)

    def b_map(t, te, nt, nx, nv):
        return (te[t], 0, 0)

    grid_spec = pltpu.PrefetchScalarGridSpec(
        num_scalar_prefetch=4,
        grid=(tmax,),
        in_specs=[pl.BlockSpec((tm, dh), x_map),
                  pl.BlockSpec(memory_space=pl.ANY),
                  pl.BlockSpec(memory_space=pl.ANY),
                  pl.BlockSpec(memory_space=pl.ANY),
                  pl.BlockSpec((1, 1, 2 * f), b_map),
                  pl.BlockSpec((1, 1, d), b_map)],
        out_specs=pl.BlockSpec((tm, dh), lambda t, te, nt, nx, nv: (t, 0)),
        scratch_shapes=[pltpu.VMEM((d, f), F32), pltpu.VMEM((d, f), F32), pltpu.VMEM((f, d), F32),
                        pltpu.VMEM((d, 2 * f), BF16), pltpu.VMEM((f, d), BF16),
                        pltpu.SemaphoreType.DMA((3,))],
    )
    return pl.pallas_call(
        functools.partial(_ffn_kernel, f=f, n_exp=n_exp, rows=rows, sub=min(FFN_SUB, tm)),
        grid_spec=grid_spec,
        out_shape=jax.ShapeDtypeStruct((n_rows, dh), U32),
        compiler_params=_cparams(("arbitrary",), vmem=VMEM_LIMIT_HIGH),
        name="expert_ffn",
    )(texp, ntl, nxt, nval, xs, w_gate, w_up, w_down, bgu, bd)


COMBINE_ROWS = 16


def _combine_kernel(pos_c, pos_n, ys_hbm, h_ref, g_ref, fg_ref, o_ref, buf_a, buf_b, sem, *, bt, rows):
    i = pl.program_id(0)
    dh = buf_a.shape[-1]

    def start_rows(pos_ref, dst, s, r0):
        for r in range(rows):
            for k in range(TOP_K):
                pltpu.make_async_copy(ys_hbm.at[pl.ds(pos_ref[0, k, r0 + r], 1)],
                                      dst.at[k, pl.ds(r0 + r, 1)], sem.at[s]).start(priority=k % 2)

    def wait_block(dst, s):
        for k in range(TOP_K):
            pltpu.make_async_copy(ys_hbm.at[pl.ds(0, bt)], dst.at[k], sem.at[s]).wait()

    @pl.when(i == 0)
    def _():
        def first(c, carry):
            start_rows(pos_c, buf_a, 0, pl.multiple_of(c * rows, rows))
            return carry
        lax.fori_loop(0, bt // rows, first, 0)

    def step(cur, s_cur, nxt, s_nxt):
        wait_block(cur, s_cur)
        for r0 in range(0, bt, rows):
            start_rows(pos_n, nxt, s_nxt, r0)
            acc_lo = h_ref[r0:r0 + rows, :dh]
            acc_hi = h_ref[r0:r0 + rows, dh:]
            for k in range(TOP_K):
                lo, hi = _unpack_halves(cur[k, r0:r0 + rows, :])
                gk = g_ref[r0:r0 + rows, k:k + 1]
                acc_lo = acc_lo + gk * lo
                acc_hi = acc_hi + gk * hi
            ms = (jnp.sum(acc_lo * acc_lo, axis=-1, keepdims=True)
                  + jnp.sum(acc_hi * acc_hi, axis=-1, keepdims=True)) / (2 * dh)
            inv = lax.rsqrt(ms + RMS_EPS)
            o_ref[r0:r0 + rows, :dh] = acc_lo * inv * fg_ref[:, :dh]
            o_ref[r0:r0 + rows, dh:] = acc_hi * inv * fg_ref[:, dh:]

        @pl.when(i == pl.num_programs(0) - 1)
        def _():
            wait_block(nxt, s_nxt)

    @pl.when(i % 2 == 0)
    def _():
        step(buf_a, 0, buf_b, 1)

    @pl.when(i % 2 == 1)
    def _():
        step(buf_b, 1, buf_a, 0)


def _combine(pos3, ys, h, gates_t, fg, *, bt):
    n, d = h.shape
    nblk = n // bt
    return pl.pallas_call(
        functools.partial(_combine_kernel, bt=bt, rows=min(COMBINE_ROWS, bt)),
        grid=(nblk,),
        in_specs=[pl.BlockSpec((1, TOP_K, bt), lambda i: (i, 0, 0), memory_space=pltpu.SMEM),
                  pl.BlockSpec((1, TOP_K, bt), lambda i: (jnp.minimum(i + 1, nblk - 1), 0, 0),
                               memory_space=pltpu.SMEM),
                  pl.BlockSpec(memory_space=pl.ANY),
                  pl.BlockSpec((bt, d), lambda i: (i, 0)),
                  pl.BlockSpec((bt, TOP_K), lambda i: (i, 0)),
                  pl.BlockSpec((1, d), lambda i: (0, 0))],
        out_specs=pl.BlockSpec((bt, d), lambda i: (i, 0)),
        out_shape=jax.ShapeDtypeStruct((n, d), F32),
        scratch_shapes=[pltpu.VMEM((TOP_K, bt, d // 2), U32), pltpu.VMEM((TOP_K, bt, d // 2), U32),
                        pltpu.SemaphoreType.DMA((2,))],
        compiler_params=_cparams(("arbitrary",)),
        name="combine_final_norm",
    )(pos3, pos3, ys, h, gates_t, fg)


def _layer(x2, b, seq, norm1_g, w_in, pool_w, pool_scale, fourier_w, w_out, norm2_g,
           router_w, router_b, w_gate, b_gate, w_up, b_up, w_down, b_down):
    n, d = x2.shape
    mix = w_in.shape[1]
    pw = pool_scale.shape[0]
    n_exp = router_w.shape[1]
    f = w_gate.shape[2]
    tm = min(512, n)
    tmax = (n * TOP_K) // tm + n_exp

    up, uf = _norm_matmul(x2, norm1_g.reshape(1, d), w_in.astype(BF16), pw=pw, c=fourier_w.shape[1],
                          bm=1024, bn=1024)
    a = _pool_mixer(up.reshape(b, seq, pw), pool_w.astype(BF16), pool_scale.reshape(1, pw), ts=512)
    yf = _fourier_mixer(uf, _fourier_prep(fourier_w), b=b, rows=1024)
    grw = jnp.zeros((d, LANES), F32).at[:, :n_exp].set(norm2_g[:, None] * router_w).astype(BF16)
    rb = jnp.zeros((1, LANES), F32).at[0, :n_exp].set(router_b)
    h, logits = _out_proj(a.reshape(n, pw), yf.reshape(n, mix - pw), w_out.astype(BF16), x2, grw, rb,
                          bm=1024, bn=1024)

    pos, gates, texp, ntl, last, nxt, nval = _route(logits, n_exp=n_exp, tm=tm, tmax=tmax)
    chunk = min(256, n)
    pos_c = pos.reshape(TOP_K, n // chunk, chunk).transpose(1, 0, 2)
    xs = _norm_dispatch(last[0, :n_exp], pos_c, h, norm2_g.reshape(1, d), n_rows=tmax * tm, tm=tm)

    bgu = jnp.concatenate([b_gate, b_up], axis=-1).reshape(n_exp, 1, 2 * f)
    ys = _expert_ffn(texp[0, :tmax], ntl[0, :1], nxt[0, :n_exp], nval[0, :tmax], xs, w_gate, w_up, w_down, bgu,
                     b_down.reshape(n_exp, 1, d), tm=tm)

    bt = min(256, n)
    pos_b = pos.reshape(TOP_K, n // bt, bt).transpose(1, 0, 2)
    return pos_b, ys, h, gates.T


def kernel(x, norm1_g, w_in, pool_w, pool_scale, fourier_w, w_out, norm2_g, router_w, router_b,
           w_gate, b_gate, w_up, b_up, w_down, b_down, final_g):
    b, seq, d = x.shape
    assert w_in.shape[0] == 1, "only a single layer is supported"
    pos_b, ys, h, gates_t = _layer(
        x.reshape(b * seq, d), b, seq, norm1_g[0], w_in[0], pool_w[0], pool_scale[0], fourier_w[0], w_out[0],
        norm2_g[0], router_w[0], router_b[0], w_gate[0], b_gate[0], w_up[0], b_up[0], w_down[0], b_down[0])
    out = _combine(pos_b, ys, h, gates_t, final_g.reshape(1, d), bt=min(256, b * seq))
    return out.reshape(b, seq, d)
```

```python
import functools
import math

import jax
import jax.numpy as jnp
from jax import lax
from jax.experimental import pallas as pl
from jax.experimental.pallas import tpu as pltpu

F32 = jnp.float32
BF16 = jnp.bfloat16
I32 = jnp.int32
U32 = jnp.uint32

RMS_EPS = 1e-5
POOL_WINDOWS = (2, 4, 8, 16)
TOP_K = 4
SWIGLU_LIMIT = 7.0
SWIGLU_ALPHA = 1.702

LANES = 128
HALO = 16
NORM_ROWS = 16
NORM_UNROLL = 8
SUBLANES = 8
VMEM_LIMIT = 56 * 1024 * 1024
VMEM_LIMIT_HIGH = 60 * 1024 * 1024
HI_MASK = 0xFFFF0000


def _cparams(sem, vmem=VMEM_LIMIT):
    return pltpu.CompilerParams(dimension_semantics=sem, vmem_limit_bytes=vmem)


def _pack_halves(lo_f32, hi_f32):
    lo = lax.bitcast_convert_type(lo_f32, U32) >> 16
    hi = lax.bitcast_convert_type(hi_f32, U32) & jnp.uint32(HI_MASK)
    return lo | hi


def _unpack_halves(w):
    lo = lax.bitcast_convert_type(w << 16, F32)
    hi = lax.bitcast_convert_type(w & jnp.uint32(HI_MASK), F32)
    return lo, hi


def _inv_rms_rows(x_ref, r_ref):
    d = x_ref.shape[1]

    def body(i, carry):
        r = pl.multiple_of(i * SUBLANES, SUBLANES)
        xc = x_ref[pl.ds(r, SUBLANES), :]
        sq = xc * xc
        acc = sq[:, :LANES]
        for c0 in range(LANES, d, LANES):
            acc = acc + sq[:, c0:c0 + LANES]
        r_ref[pl.ds(r, SUBLANES), :] = acc
        return carry
    steps = x_ref.shape[0] // SUBLANES
    lax.fori_loop(0, steps, body, 0, unroll=min(NORM_UNROLL, steps))
    ms = jnp.sum(r_ref[...], axis=-1, keepdims=True) / d
    r_ref[...] = jnp.broadcast_to(lax.rsqrt(ms + RMS_EPS), r_ref.shape)


def _norm_mm_kernel(x_hbm, g_ref, w_ref, up_ref, uf_ref, x_buf, z_ref, r_ref, xsem, *, rows, pool_blocks, c):
    i = pl.program_id(0)
    j = pl.program_id(1)
    bm = x_buf.shape[0]

    def x_copy(blk):
        return pltpu.make_async_copy(x_hbm.at[pl.ds(pl.multiple_of(blk * bm, bm), bm)], x_buf, xsem)

    @pl.when(j == 0)
    def _():
        @pl.when(i == 0)
        def _():
            x_copy(0).start()
        x_copy(i).wait()
        _inv_rms_rows(x_buf, r_ref)

        def body(t, carry):
            r = pl.multiple_of(t * rows, rows)
            inv = r_ref[pl.ds(r, rows), :]
            for c0 in range(0, x_buf.shape[1], LANES):
                z = x_buf[pl.ds(r, rows), c0:c0 + LANES] * inv * g_ref[:, c0:c0 + LANES]
                z_ref[pl.ds(r, rows), c0:c0 + LANES] = z.astype(BF16)
            return carry
        lax.fori_loop(0, bm // rows, body, 0, unroll=2)

        @pl.when(i + 1 < pl.num_programs(0))
        def _():
            x_copy(i + 1).start(priority=1)

    y = jnp.dot(z_ref[...], w_ref[...], preferred_element_type=F32)

    @pl.when(j < pool_blocks)
    def _():
        up_ref[...] = y.astype(up_ref.dtype)

    @pl.when(j >= pool_blocks)
    def _():
        yr = y.astype(BF16).astype(F32)
        ch = c // 2
        for hd in range(y.shape[1] // c):
            words = _pack_halves(yr[:, hd * c:hd * c + ch], yr[:, hd * c + ch:(hd + 1) * c])
            for s in range(ch // LANES):
                uf_ref[hd * (ch // LANES) + s] = words[:, s * LANES:(s + 1) * LANES]


def _norm_matmul(x, g, w, *, pw, c, bm, bn):
    n, d = x.shape
    m = w.shape[1]
    bm, bn = min(bm, n), min(bn, pw, m - pw)
    assert pw % bn == 0 and (m - pw) % bn == 0 and bn % c == 0 and (c // 2) % LANES == 0
    pool_blocks = pw // bn
    slabs_blk = bn // 2 // LANES
    return pl.pallas_call(
        functools.partial(_norm_mm_kernel, rows=min(NORM_ROWS, bm), pool_blocks=pool_blocks, c=c),
        grid=(n // bm, m // bn),
        in_specs=[pl.BlockSpec(memory_space=pl.ANY),
                  pl.BlockSpec((1, d), lambda i, j: (0, 0)),
                  pl.BlockSpec((d, bn), lambda i, j: (0, j))],
        out_specs=[pl.BlockSpec((bm, bn), lambda i, j: (i, jnp.minimum(j, pool_blocks - 1))),
                   pl.BlockSpec((slabs_blk, bm, LANES), lambda i, j: (jnp.maximum(j - pool_blocks, 0), i, 0))],
        out_shape=[jax.ShapeDtypeStruct((n, pw), BF16),
                   jax.ShapeDtypeStruct(((m - pw) // 2 // LANES, n, LANES), U32)],
        scratch_shapes=[pltpu.VMEM((bm, d), F32), pltpu.VMEM((bm, d), BF16), pltpu.VMEM((bm, LANES), F32),
                        pltpu.SemaphoreType.DMA(())],
        compiler_params=_cparams(("arbitrary", "arbitrary"), vmem=VMEM_LIMIT_HIGH),
        name="norm1_w_in",
    )(x, g, w)


def _pool_kernel(cur_ref, prev_ref, next_ref, pw_ref, sc_ref, o_ref, *, seq, ts, c):
    t = pl.program_id(1)
    has_prev = (t > 0).astype(F32)
    has_next = (t < pl.num_programs(1) - 1).astype(F32)
    n_ext = ts + 2 * HALO
    tok = t * ts + lax.broadcasted_iota(I32, (ts, 1), 0)
    for g, w in enumerate(POOL_WINDOWS):
        sl = slice(g * c, (g + 1) * c)
        cur = cur_ref[0, :, sl].astype(F32)
        prev = prev_ref[0, :, sl].astype(F32) * has_prev
        nxt = next_ref[0, :, sl].astype(F32) * has_next
        ext = jnp.concatenate([prev, cur, nxt], axis=0)
        s = ext + pltpu.roll(ext, 1, 0)
        h = 1
        while 2 * h < w:
            s = pltpu.roll(s, h, 0) + pltpu.roll(s, n_ext - h, 0)
            h *= 2
        win = s[HALO:HALO + ts]
        lo = jnp.maximum(tok - w // 2, 0)
        hi = jnp.minimum(tok + w // 2 - 1, seq - 1)
        cnt = (hi - lo + 1).astype(F32)
        p = win / cnt - cur
        y = jnp.dot(p.astype(BF16), pw_ref[g], preferred_element_type=F32)
        o_ref[0, :, sl] = (y * sc_ref[:, sl]).astype(o_ref.dtype)


def _pool_mixer(u3, pool_w, pool_scale, *, ts):
    b, seq, _ = u3.shape
    g, c, _ = pool_w.shape
    pw = g * c
    ts = min(ts, seq)
    nh = seq // HALO
    per = ts // HALO
    return pl.pallas_call(
        functools.partial(_pool_kernel, seq=seq, ts=ts, c=c),
        grid=(b, seq // ts),
        in_specs=[pl.BlockSpec((1, ts, pw), lambda i, t: (i, t, 0)),
                  pl.BlockSpec((1, HALO, pw), lambda i, t: (i, jnp.maximum(t * per - 1, 0), 0)),
                  pl.BlockSpec((1, HALO, pw), lambda i, t: (i, jnp.minimum((t + 1) * per, nh - 1), 0)),
                  pl.BlockSpec((g, c, c), lambda i, t: (0, 0, 0)),
                  pl.BlockSpec((1, pw), lambda i, t: (0, 0))],
        out_specs=pl.BlockSpec((1, ts, pw), lambda i, t: (i, t, 0)),
        out_shape=jax.ShapeDtypeStruct((b, seq, pw), BF16),
        compiler_params=_cparams(("parallel", "parallel")),
        name="pool_mixer",
    )(u3, u3, u3, pool_w, pool_scale)


DFT_N1 = 256


def _dft_mats(n, scale, dtype):
    j = jnp.arange(n, dtype=I32)
    ang = ((j[:, None] * j[None, :]) % n).astype(F32) * (2.0 * math.pi / n)
    return (jnp.cos(ang) * scale).astype(dtype), (jnp.sin(ang) * scale).astype(dtype)


def _fourier_prep_kernel(cc_ref, sc_ref, w_ref, o_ref, *, c):
    w = w_ref[0]
    o_ref[0, :c, :] = jnp.dot(cc_ref[...], w, preferred_element_type=F32,
                              precision=lax.Precision.HIGHEST).astype(o_ref.dtype)
    o_ref[0, c:, :] = jnp.dot(sc_ref[...], w, preferred_element_type=F32,
                              precision=lax.Precision.HIGHEST).astype(o_ref.dtype)


def _fourier_prep(fourier_w):
    h, c, _ = fourier_w.shape
    cc, sc = _dft_mats(c, c ** -0.5, F32)
    return pl.pallas_call(
        functools.partial(_fourier_prep_kernel, c=c),
        grid=(h,),
        in_specs=[pl.BlockSpec((c, c), lambda i: (0, 0)),
                  pl.BlockSpec((c, c), lambda i: (0, 0)),
                  pl.BlockSpec((1, c, c), lambda i: (i, 0, 0))],
        out_specs=pl.BlockSpec((1, 2 * c, c), lambda i: (i, 0, 0)),
        out_shape=jax.ShapeDtypeStruct((h, 2 * c, c), BF16),
        compiler_params=_cparams(("parallel",)),
        name="fourier_prep",
    )(cc, sc, fourier_w)


def _fft(xs):
    n = len(xs)
    if n == 1:
        return xs
    ev, od = _fft(xs[0::2]), _fft(xs[1::2])
    out = [None] * n
    for k in range(n // 2):
        re, im = od[k]
        if k == 0:
            tr, ti = re, im
        elif 4 * k == n:
            tr, ti = im, -re
        else:
            wr, wi = math.cos(2.0 * math.pi * k / n), -math.sin(2.0 * math.pi * k / n)
            tr, ti = re * wr - im * wi, re * wi + im * wr
        er, ei = ev[k]
        out[k] = (er + tr, ei + ti)
        out[k + n // 2] = (er - tr, ei - ti)
    return out


def _fourier_kernel(x_ref, ts_ref, ab_ref, o_ref, y_ref, *, n1, n2, c, rows):
    for j in range(n2):
        halves = [_unpack_halves(x_ref[s, pl.ds(j, n1, stride=n2), :]) for s in range(x_ref.shape[0])]
        xj = jnp.concatenate([lo for lo, _ in halves] + [hi for _, hi in halves], axis=1).astype(BF16)
        y = jnp.dot(ts_ref[j], xj, preferred_element_type=F32)
        y_ref[0, j] = y[:n1]
        y_ref[1, j] = y[n1:]

    nl = c // LANES

    def tile(i, carry):
        r = pl.multiple_of((i // nl) * SUBLANES, SUBLANES)
        l = pl.multiple_of((i % nl) * LANES, LANES)
        zs = [(y_ref[0, j, pl.ds(r, SUBLANES), pl.ds(l, LANES)], y_ref[1, j, pl.ds(r, SUBLANES), pl.ds(l, LANES)])
              for j in range(n2)]
        gs = _fft(zs)
        for k in range(n2):
            y_ref[0, k, pl.ds(r, SUBLANES), pl.ds(l, LANES)] = gs[k][0]
            y_ref[1, k, pl.ds(r, SUBLANES), pl.ds(l, LANES)] = gs[k][1]
        return carry
    lax.fori_loop(0, (n1 // SUBLANES) * nl, tile, 0, unroll=2)

    per = max(1, min(n2, rows // n1))
    for s in range(0, n2, per):
        gr = y_ref[0, s:s + per].reshape(per * n1, c).astype(BF16)
        gi = y_ref[1, s:s + per].reshape(per * n1, c).astype(BF16)
        y = jnp.dot(gr, ab_ref[0, :c, :], preferred_element_type=F32)
        y = y + jnp.dot(gi, ab_ref[0, c:, :], preferred_element_type=F32)
        o_ref[0, s * n1:(s + per) * n1, :] = y.astype(o_ref.dtype)


def _fourier_mixer(uf, ab, *, b, rows):
    slabs, n, _ = uf.shape
    seq = n // b
    h, _, c = ab.shape
    spb = slabs // h
    n1 = min(DFT_N1, seq)
    n2 = seq // n1
    assert n1 * n2 == seq and n2 & (n2 - 1) == 0, "sequence length must be N1 * 2^m"
    k1 = jnp.arange(n1, dtype=I32)
    pos = n2 * k1[None, None, :] + jnp.arange(n2, dtype=I32)[:, None, None]
    ang = ((k1[None, :, None] * pos) % seq).astype(F32) * (2.0 * math.pi / seq)
    ts = (jnp.concatenate([jnp.cos(ang), -jnp.sin(ang)], axis=1) * (seq ** -0.5)).astype(BF16)

    return pl.pallas_call(
        functools.partial(_fourier_kernel, n1=n1, n2=n2, c=c, rows=rows),
        grid=(b, h),
        in_specs=[
            pl.BlockSpec((spb, None, seq, LANES), lambda i, k: (k, i, 0, 0)),
            pl.BlockSpec((n2, 2 * n1, n1), lambda i, k: (0, 0, 0), pipeline_mode=pl.Buffered(1)),
            pl.BlockSpec((1, 2 * c, c), lambda i, k: (k, 0, 0))],
        out_specs=pl.BlockSpec((1, seq, c), lambda i, k: (i, 0, k)),
        out_shape=jax.ShapeDtypeStruct((b, seq, h * c), BF16),
        scratch_shapes=[pltpu.VMEM((2, n2, n1, c), F32)],
        compiler_params=_cparams(("parallel", "parallel")),
        name="fourier_mixer",
    )(uf.reshape(slabs, b, seq, LANES), ts, ab)


LOGIT_ROWS = 256


def _out_proj_kernel(a_ref, f_ref, w_ref, x_ref, grw_ref, rb_ref, o_ref, lg_ref, ss_acc, pl_acc, *, ka):
    j = pl.program_id(1)
    y = jnp.dot(a_ref[...], w_ref[:ka, :], preferred_element_type=F32)
    y = y + jnp.dot(f_ref[...], w_ref[ka:, :], preferred_element_type=F32)
    o_ref[...] = x_ref[...] + y

    @pl.when(j == 0)
    def _():
        ss_acc[...] = jnp.zeros_like(ss_acc)
        pl_acc[...] = jnp.zeros_like(pl_acc)

    rc = min(LOGIT_ROWS, o_ref.shape[0])
    for r in range(0, o_ref.shape[0], rc):
        hc = o_ref[r:r + rc, :]
        sq = hc * hc
        part = sq[:, :LANES]
        for c0 in range(LANES, hc.shape[1], LANES):
            part = part + sq[:, c0:c0 + LANES]
        ss_acc[r:r + rc, :] += part
        pl_acc[r:r + rc, :] += jnp.dot(hc.astype(BF16), grw_ref[...], preferred_element_type=F32)

    @pl.when(j == pl.num_programs(1) - 1)
    def _():
        d = o_ref.shape[1] * pl.num_programs(1)
        ms = jnp.sum(ss_acc[...], axis=-1, keepdims=True) / d
        lg_ref[...] = pl_acc[...] * lax.rsqrt(ms + RMS_EPS) + rb_ref[...]


def _out_proj(a, f, w, x, grw, rb, *, bm, bn):
    n, ka = a.shape
    kf = f.shape[1]
    d = w.shape[1]
    bm, bn = min(bm, n), min(bn, d)
    return pl.pallas_call(
        functools.partial(_out_proj_kernel, ka=ka),
        grid=(n // bm, d // bn),
        in_specs=[pl.BlockSpec((bm, ka), lambda i, j: (i, 0)),
                  pl.BlockSpec((bm, kf), lambda i, j: (i, 0)),
                  pl.BlockSpec((ka + kf, bn), lambda i, j: (0, j)),
                  pl.BlockSpec((bm, bn), lambda i, j: (i, j)),
                  pl.BlockSpec((bn, LANES), lambda i, j: (j, 0)),
                  pl.BlockSpec((1, LANES), lambda i, j: (0, 0))],
        out_specs=[pl.BlockSpec((bm, bn), lambda i, j: (i, j)),
                   pl.BlockSpec((bm, LANES), lambda i, j: (i, 0))],
        out_shape=[jax.ShapeDtypeStruct((n, d), F32), jax.ShapeDtypeStruct((n, LANES), F32)],
        scratch_shapes=[pltpu.VMEM((bm, LANES), F32), pltpu.VMEM((bm, LANES), F32)],
        compiler_params=_cparams(("parallel", "arbitrary"), vmem=VMEM_LIMIT_HIGH),
        name="w_out_residual",
    )(a, f, w, x, grw, rb)


def _route_kernel(lg_ref, pos_ref, gate_ref, texp_ref, ntl_ref, last_ref, nxt_ref, nval_ref, idx_s, rank_s,
                  *, n_tok, n_exp, tm, tb, tmax_pad):
    shift = tm.bit_length() - 1
    iota_e = lax.broadcasted_iota(I32, (n_exp, tb), 0)
    before = (lax.broadcasted_iota(I32, (tb, tb), 0)
              < lax.broadcasted_iota(I32, (tb, tb), 1)).astype(BF16)

    def pass1(i, counts):
        off = pl.multiple_of(i * tb, tb)
        l = lg_ref[pl.ds(off, tb), :].T[:n_exp, :]
        vals, hots = [], []
        for k in range(TOP_K):
            m = jnp.max(l, axis=0, keepdims=True)
            idx = jnp.min(jnp.where(l == m, iota_e, n_exp), axis=0, keepdims=True)
            hot = iota_e == idx
            l = jnp.where(hot, -jnp.inf, l)
            vals.append(m)
            hots.append(hot)
            idx_s[k:k + 1, pl.ds(off, tb)] = idx
        exps = [jnp.exp(v - vals[0]) for v in vals]
        tot = exps[0]
        for e in exps[1:]:
            tot = tot + e
        sel = hots[0].astype(F32)
        for hot in hots[1:]:
            sel = sel + hot.astype(F32)
        rank = jnp.dot(sel.astype(BF16), before, preferred_element_type=F32) + counts
        for k in range(TOP_K):
            gate_ref[k:k + 1, pl.ds(off, tb)] = exps[k] / tot
            rank_s[k:k + 1, pl.ds(off, tb)] = jnp.sum(jnp.where(hots[k], rank, 0.0), axis=0, keepdims=True)
        return counts + jnp.sum(sel, axis=1, keepdims=True)

    counts = lax.fori_loop(0, n_tok // tb, pass1, jnp.zeros((n_exp, 1), F32))
    ntile = (counts.astype(I32) + (tm - 1)) >> shift
    sub = lax.broadcasted_iota(I32, (n_exp, LANES), 0)
    lane = lax.broadcasted_iota(I32, (n_exp, LANES), 1)
    ntile_row = jnp.sum(jnp.where(sub == lane, ntile, 0), axis=0, keepdims=True)
    start = jnp.sum(jnp.where(lane < sub, ntile_row, 0), axis=1, keepdims=True)
    t_lane = lax.broadcasted_iota(I32, (n_exp, tmax_pad), 1)
    texp_ref[...] = jnp.sum((start <= t_lane).astype(I32), axis=0, keepdims=True) - 1
    own = (start <= t_lane) & (t_lane < start + ntile)
    left = counts.astype(I32) - ((t_lane - start) << shift)
    nval_ref[...] = jnp.sum(jnp.where(own, jnp.clip(left, 0, tm), 0), axis=0, keepdims=True)
    ntl_ref[...] = jnp.sum(ntile, axis=0, keepdims=True) + jnp.zeros((1, LANES), I32)
    last = jnp.where(ntile > 0, (start + ntile - 1) << shift, -1)
    last_ref[...] = jnp.sum(jnp.where(sub == lane, last, 0), axis=0, keepdims=True)
    nxt = jnp.min(jnp.where((lane > sub) & (ntile_row > 0), lane, n_exp), axis=1, keepdims=True)
    nxt_ref[...] = jnp.sum(jnp.where(sub == lane, nxt, 0), axis=0, keepdims=True)
    start_rows = start << shift

    def pass2(i, carry):
        off = pl.multiple_of(i * tb, tb)
        for k in range(TOP_K):
            hot = iota_e == idx_s[k:k + 1, pl.ds(off, tb)]
            base = jnp.sum(jnp.where(hot, start_rows, 0), axis=0, keepdims=True)
            pos_ref[k:k + 1, pl.ds(off, tb)] = rank_s[k:k + 1, pl.ds(off, tb)].astype(I32) + base
        return carry
    lax.fori_loop(0, n_tok // tb, pass2, 0)


def _route(logits, *, n_exp, tm, tmax):
    n_tok = logits.shape[0]
    tb = min(256, n_tok)
    tmax_pad = -(-tmax // LANES) * LANES
    return pl.pallas_call(
        functools.partial(_route_kernel, n_tok=n_tok, n_exp=n_exp, tm=tm, tb=tb, tmax_pad=tmax_pad),
        out_shape=[jax.ShapeDtypeStruct((TOP_K, n_tok), I32),
                   jax.ShapeDtypeStruct((TOP_K, n_tok), F32),
                   jax.ShapeDtypeStruct((1, tmax_pad), I32),
                   jax.ShapeDtypeStruct((1, LANES), I32),
                   jax.ShapeDtypeStruct((1, LANES), I32),
                   jax.ShapeDtypeStruct((1, LANES), I32),
                   jax.ShapeDtypeStruct((1, tmax_pad), I32)],
        scratch_shapes=[pltpu.VMEM((TOP_K, n_tok), I32), pltpu.VMEM((TOP_K, n_tok), F32)],
        compiler_params=_cparams(None),
        name="route",
    )(logits)


def _norm_dispatch_kernel(last_ref, pos_ref, h_ref, g_ref, xs_hbm, z_a, z_b, r_ref, zero_v, zsem, sem,
                          *, n_exp, tm, chunk, rows):
    i = pl.program_id(0)
    dh = z_a.shape[1]

    @pl.when(i == 0)
    def _():
        zero_v[...] = jnp.zeros_like(zero_v)

        def zero_tile(e):
            r = jnp.maximum(last_ref[e], 0)
            return pltpu.make_async_copy(zero_v, xs_hbm.at[pl.ds(pl.multiple_of(r, tm), tm)], zsem)
        for e in range(n_exp):
            @pl.when(last_ref[e] >= 0)
            def _():
                zero_tile(e).start()
        for e in range(n_exp):
            @pl.when(last_ref[e] >= 0)
            def _():
                zero_tile(e).wait()

    def wait_rows(z_buf, s):
        for k in range(TOP_K):
            pltpu.make_async_copy(z_buf, xs_hbm.at[pl.ds(0, chunk)], sem.at[s]).wait()

    _inv_rms_rows(h_ref, r_ref)

    def step(z_buf, s, other, s_other):
        @pl.when(i >= 2)
        def _():
            wait_rows(z_buf, s)

        def body(t, carry):
            r = pl.multiple_of(t * rows, rows)
            inv = r_ref[pl.ds(r, rows), :]
            for c0 in range(0, dh, LANES):
                lo = (h_ref[pl.ds(r, rows), c0:c0 + LANES] * inv * g_ref[:, c0:c0 + LANES]).astype(BF16)
                hi = (h_ref[pl.ds(r, rows), dh + c0:dh + c0 + LANES] * inv
                      * g_ref[:, dh + c0:dh + c0 + LANES]).astype(BF16)
                z_buf[pl.ds(r, rows), c0:c0 + LANES] = _pack_halves(lo.astype(F32), hi.astype(F32))
            return carry
        lax.fori_loop(0, chunk // rows, body, 0, unroll=2)

        for n in range(chunk):
            for k in range(TOP_K):
                pltpu.make_async_copy(z_buf.at[pl.ds(n, 1)], xs_hbm.at[pl.ds(pos_ref[0, k, n], 1)],
                                      sem.at[s]).start(priority=k % 2)

        @pl.when(i == pl.num_programs(0) - 1)
        def _():
            wait_rows(z_buf, s)

            @pl.when(i >= 1)
            def _():
                wait_rows(other, s_other)

    @pl.when(i % 2 == 0)
    def _():
        step(z_a, 0, z_b, 1)

    @pl.when(i % 2 == 1)
    def _():
        step(z_b, 1, z_a, 0)


def _norm_dispatch(last, pos3, h, g, *, n_rows, tm):
    n_tok, d = h.shape
    dh = d // 2
    nchunk, _, chunk = pos3.shape
    n_exp = last.shape[0]
    grid_spec = pltpu.PrefetchScalarGridSpec(
        num_scalar_prefetch=1,
        grid=(nchunk,),
        in_specs=[pl.BlockSpec((1, TOP_K, chunk), lambda c, last: (c, 0, 0), memory_space=pltpu.SMEM),
                  pl.BlockSpec((chunk, d), lambda c, last: (c, 0)),
                  pl.BlockSpec((1, d), lambda c, last: (0, 0))],
        out_specs=pl.BlockSpec(memory_space=pl.ANY),
        scratch_shapes=[pltpu.VMEM((chunk, dh), U32), pltpu.VMEM((chunk, dh), U32),
                        pltpu.VMEM((chunk, LANES), F32), pltpu.VMEM((tm, dh), U32),
                        pltpu.SemaphoreType.DMA(()), pltpu.SemaphoreType.DMA((2,))],
    )
    return pl.pallas_call(
        functools.partial(_norm_dispatch_kernel, n_exp=n_exp, tm=tm, chunk=chunk, rows=min(NORM_ROWS, chunk)),
        grid_spec=grid_spec,
        out_shape=jax.ShapeDtypeStruct((n_rows, dh), U32),
        compiler_params=_cparams(("arbitrary",)),
        name="norm2_dispatch",
    )(last, pos3, h, g)


CAST_ROWS = 256


FFN_SUB = 256


def _ffn_kernel(texp_ref, ntl_ref, nxt_ref, nval_ref, x_ref, wg_hbm, wu_hbm, wd_hbm, bgu_ref, bd_ref, o_ref,
                stage_g, stage_u, stage_d, wgu, wd, sems, *, f, n_exp, rows, sub):
    t = pl.program_id(0)
    e = texp_ref[t]
    dh = x_ref.shape[1]
    valid = t < ntl_ref[0]
    first = jnp.logical_or(t == 0, e != texp_ref[jnp.maximum(t - 1, 0)])

    def weight_copies(ex):
        return (pltpu.make_async_copy(wg_hbm.at[ex], stage_g, sems.at[0]),
                pltpu.make_async_copy(wu_hbm.at[ex], stage_u, sems.at[1]),
                pltpu.make_async_copy(wd_hbm.at[ex], stage_d, sems.at[2]))

    @pl.when(t == 0)
    def _():
        for cp in weight_copies(e):
            cp.start()

    @pl.when(jnp.logical_and(valid, first))
    def _():
        for cp in weight_copies(e):
            cp.wait()

        def cast_up(i, carry):
            r = pl.multiple_of(i * rows, rows)
            wgu[pl.ds(r, rows), :f] = stage_g[pl.ds(r, rows), :].astype(BF16)
            wgu[pl.ds(r, rows), f:] = stage_u[pl.ds(r, rows), :].astype(BF16)
            return carry
        lax.fori_loop(0, stage_g.shape[0] // rows, cast_up, 0)

        def cast_down(i, carry):
            r = pl.multiple_of(i * rows, rows)
            wd[pl.ds(r, rows), :] = stage_d[pl.ds(r, rows), :].astype(BF16)
            return carry
        lax.fori_loop(0, stage_d.shape[0] // rows, cast_down, 0)

        nx = nxt_ref[e]

        @pl.when(nx < n_exp)
        def _():
            for cp in weight_copies(nx):
                cp.start(priority=1)

    def ffn_rows(r0, m):
        lo, hi = _unpack_halves(x_ref[r0:r0 + m, :])
        hgu = jnp.dot(lo.astype(BF16), wgu[:dh, :], preferred_element_type=F32)
        hgu = hgu + jnp.dot(hi.astype(BF16), wgu[dh:, :], preferred_element_type=F32)
        hgu = hgu + bgu_ref[0]
        hg = jnp.minimum(hgu[:, :f], SWIGLU_LIMIT)
        hu = jnp.clip(hgu[:, f:], -SWIGLU_LIMIT, SWIGLU_LIMIT)
        act = (hg * (1.0 / (1.0 + jnp.exp(-SWIGLU_ALPHA * hg))) * (hu + 1.0)).astype(BF16)
        ylo = jnp.dot(act, wd[:, :dh], preferred_element_type=F32) + bd_ref[0, :, :dh]
        yhi = jnp.dot(act, wd[:, dh:], preferred_element_type=F32) + bd_ref[0, :, dh:]
        o_ref[r0:r0 + m, :] = _pack_halves(ylo.astype(BF16).astype(F32), yhi.astype(BF16).astype(F32))

    tmr = x_ref.shape[0]
    nv = nval_ref[t]

    @pl.when(nv > sub)
    def _():
        ffn_rows(0, tmr)

    @pl.when(jnp.logical_and(nv > 0, nv <= sub))
    def _():
        ffn_rows(0, sub)
        if tmr > sub:
            o_ref[sub:, :] = jnp.zeros((tmr - sub, dh), o_ref.dtype)


def _expert_ffn(texp, ntl, nxt, nval, xs, w_gate, w_up, w_down, bgu, bd, *, tm):
    n_rows, dh = xs.shape
    n_exp, d, f = w_gate.shape
    tmax = n_rows // tm
    rows = min(CAST_ROWS, f)
    assert d % rows == 0 and f % rows == 0

    def x_map(t, te, nt, nx, nv):
        return (jnp.minimum(t, nt[0] - 1), 0)

    def b_map(t, te, nt, nx, nv):
        return (te[t], 0, 0)

    grid_spec = pltpu.PrefetchScalarGridSpec(
        num_scalar_prefetch=4,
        grid=(tmax,),
        in_specs=[pl.BlockSpec((tm, dh), x_map),
                  pl.BlockSpec(memory_space=pl.ANY),
                  pl.BlockSpec(memory_space=pl.ANY),
                  pl.BlockSpec(memory_space=pl.ANY),
                  pl.BlockSpec((1, 1, 2 * f), b_map),
                  pl.BlockSpec((1, 1, d), b_map)],
        out_specs=pl.BlockSpec((tm, dh), lambda t, te, nt, nx, nv: (t, 0)),
        scratch_shapes=[pltpu.VMEM((d, f), F32), pltpu.VMEM((d, f), F32), pltpu.VMEM((f, d), F32),
                        pltpu.VMEM((d, 2 * f), BF16), pltpu.VMEM((f, d), BF16),
                        pltpu.SemaphoreType.DMA((3,))],
    )
    return pl.pallas_call(
        functools.partial(_ffn_kernel, f=f, n_exp=n_exp, rows=rows, sub=min(FFN_SUB, tm)),
        grid_spec=grid_spec,
        out_shape=jax.ShapeDtypeStruct((n_rows, dh), U32),
        compiler_params=_cparams(("arbitrary",), vmem=VMEM_LIMIT_HIGH),
        name="expert_ffn",
    )(texp, ntl, nxt, nval, xs, w_gate, w_up, w_down, bgu, bd)


COMBINE_ROWS = 16


def _combine_kernel(pos_c, pos_n, ys_hbm, h_ref, g_ref, fg_ref, o_ref, buf_a, buf_b, sem, *, bt, rows):
    i = pl.program_id(0)
    dh = buf_a.shape[-1]

    def start_rows(pos_ref, dst, s, r0):
        for r in range(rows):
            for k in range(TOP_K):
                pltpu.make_async_copy(ys_hbm.at[pl.ds(pos_ref[0, k, r0 + r], 1)],
                                      dst.at[k, pl.ds(r0 + r, 1)], sem.at[s]).start(priority=k % 2)

    def wait_block(dst, s):
        for k in range(TOP_K):
            pltpu.make_async_copy(ys_hbm.at[pl.ds(0, bt)], dst.at[k], sem.at[s]).wait()

    @pl.when(i == 0)
    def _():
        def first(c, carry):
            start_rows(pos_c, buf_a, 0, pl.multiple_of(c * rows, rows))
            return carry
        lax.fori_loop(0, bt // rows, first, 0)

    def step(cur, s_cur, nxt, s_nxt):
        wait_block(cur, s_cur)
        for r0 in range(0, bt, rows):
            start_rows(pos_n, nxt, s_nxt, r0)
            acc_lo = h_ref[r0:r0 + rows, :dh]
            acc_hi = h_ref[r0:r0 + rows, dh:]
            for k in range(TOP_K):
                lo, hi = _unpack_halves(cur[k, r0:r0 + rows, :])
                gk = g_ref[r0:r0 + rows, k:k + 1]
                acc_lo = acc_lo + gk * lo
                acc_hi = acc_hi + gk * hi
            ms = (jnp.sum(acc_lo * acc_lo, axis=-1, keepdims=True)
                  + jnp.sum(acc_hi * acc_hi, axis=-1, keepdims=True)) / (2 * dh)
            inv = lax.rsqrt(ms + RMS_EPS)
            o_ref[r0:r0 + rows, :dh] = acc_lo * inv * fg_ref[:, :dh]
            o_ref[r0:r0 + rows, dh:] = acc_hi * inv * fg_ref[:, dh:]

        @pl.when(i == pl.num_programs(0) - 1)
        def _():
            wait_block(nxt, s_nxt)

    @pl.when(i % 2 == 0)
    def _():
        step(buf_a, 0, buf_b, 1)

    @pl.when(i % 2 == 1)
    def _():
        step(buf_b, 1, buf_a, 0)


def _combine(pos3, ys, h, gates_t, fg, *, bt):
    n, d = h.shape
    nblk = n // bt
    return pl.pallas_call(
        functools.partial(_combine_kernel, bt=bt, rows=min(COMBINE_ROWS, bt)),
        grid=(nblk,),
        in_specs=[pl.BlockSpec((1, TOP_K, bt), lambda i: (i, 0, 0), memory_space=pltpu.SMEM),
                  pl.BlockSpec((1, TOP_K, bt), lambda i: (jnp.minimum(i + 1, nblk - 1), 0, 0),
                               memory_space=pltpu.SMEM),
                  pl.BlockSpec(memory_space=pl.ANY),
                  pl.BlockSpec((bt, d), lambda i: (i, 0)),
                  pl.BlockSpec((bt, TOP_K), lambda i: (i, 0)),
                  pl.BlockSpec((1, d), lambda i: (0, 0))],
        out_specs=pl.BlockSpec((bt, d), lambda i: (i, 0)),
        out_shape=jax.ShapeDtypeStruct((n, d), F32),
        scratch_shapes=[pltpu.VMEM((TOP_K, bt, d // 2), U32), pltpu.VMEM((TOP_K, bt, d // 2), U32),
                        pltpu.SemaphoreType.DMA((2,))],
        compiler_params=_cparams(("arbitrary",)),
        name="combine_final_norm",
    )(pos3, pos3, ys, h, gates_t, fg)


def _layer(x2, b, seq, norm1_g, w_in, pool_w, pool_scale, fourier_w, w_out, norm2_g,
           router_w, router_b, w_gate, b_gate, w_up, b_up, w_down, b_down):
    n, d = x2.shape
    mix = w_in.shape[1]
    pw = pool_scale.shape[0]
    n_exp = router_w.shape[1]
    f = w_gate.shape[2]
    tm = min(512, n)
    tmax = (n * TOP_K) // tm + n_exp

    up, uf = _norm_matmul(x2, norm1_g.reshape(1, d), w_in.astype(BF16), pw=pw, c=fourier_w.shape[1],
                          bm=1024, bn=1024)
    a = _pool_mixer(up.reshape(b, seq, pw), pool_w.astype(BF16), pool_scale.reshape(1, pw), ts=512)
    yf = _fourier_mixer(uf, _fourier_prep(fourier_w), b=b, rows=1024)
    grw = jnp.zeros((d, LANES), F32).at[:, :n_exp].set(norm2_g[:, None] * router_w).astype(BF16)
    rb = jnp.zeros((1, LANES), F32).at[0, :n_exp].set(router_b)
    h, logits = _out_proj(a.reshape(n, pw), yf.reshape(n, mix - pw), w_out.astype(BF16), x2, grw, rb,
                          bm=1024, bn=1024)

    pos, gates, texp, ntl, last, nxt, nval = _route(logits, n_exp=n_exp, tm=tm, tmax=tmax)
    chunk = min(256, n)
    pos_c = pos.reshape(TOP_K, n // chunk, chunk).transpose(1, 0, 2)
    xs = _norm_dispatch(last[0, :n_exp], pos_c, h, norm2_g.reshape(1, d), n_rows=tmax * tm, tm=tm)

    bgu = jnp.concatenate([b_gate, b_up], axis=-1).reshape(n_exp, 1, 2 * f)
    ys = _expert_ffn(texp[0, :tmax], ntl[0, :1], nxt[0, :n_exp], nval[0, :tmax], xs, w_gate, w_up, w_down, bgu,
                     b_down.reshape(n_exp, 1, d), tm=tm)

    bt = min(256, n)
    pos_b = pos.reshape(TOP_K, n // bt, bt).transpose(1, 0, 2)
    return pos_b, ys, h, gates.T


def kernel(x, norm1_g, w_in, pool_w, pool_scale, fourier_w, w_out, norm2_g, router_w, router_b,
           w_gate, b_gate, w_up, b_up, w_down, b_down, final_g):
    b, seq, d = x.shape
    assert w_in.shape[0] == 1, "only a single layer is supported"
    pos_b, ys, h, gates_t = _layer(
        x.reshape(b * seq, d), b, seq, norm1_g[0], w_in[0], pool_w[0], pool_scale[0], fourier_w[0], w_out[0],
        norm2_g[0], router_w[0], router_b[0], w_gate[0], b_gate[0], w_up[0], b_up[0], w_down[0], b_down[0])
    out = _combine(pos_b, ys, h, gates_t, final_g.reshape(1, d), bt=min(256, b * seq))
    return out.reshape(b, seq, d)
```

```python
import functools
import math

import jax
import jax.numpy as jnp
from jax import lax
from jax.experimental import pallas as pl
from jax.experimental.pallas import tpu as pltpu

F32 = jnp.float32
BF16 = jnp.bfloat16
I32 = jnp.int32
U32 = jnp.uint32

RMS_EPS = 1e-5
POOL_WINDOWS = (2, 4, 8, 16)
TOP_K = 4
SWIGLU_LIMIT = 7.0
SWIGLU_ALPHA = 1.702

LANES = 128
HALO = 16
NORM_ROWS = 16
NORM_UNROLL = 8
SUBLANES = 8
VMEM_LIMIT = 56 * 1024 * 1024
VMEM_LIMIT_HIGH = 60 * 1024 * 1024
HI_MASK = 0xFFFF0000


def _cparams(sem, vmem=VMEM_LIMIT):
    return pltpu.CompilerParams(dimension_semantics=sem, vmem_limit_bytes=vmem)


def _pack_halves(lo_f32, hi_f32):
    lo = lax.bitcast_convert_type(lo_f32, U32) >> 16
    hi = lax.bitcast_convert_type(hi_f32, U32) & jnp.uint32(HI_MASK)
    return lo | hi


def _unpack_halves(w):
    lo = lax.bitcast_convert_type(w << 16, F32)
    hi = lax.bitcast_convert_type(w & jnp.uint32(HI_MASK), F32)
    return lo, hi


def _inv_rms_rows(x_ref, r_ref):
    d = x_ref.shape[1]

    def body(i, carry):
        r = pl.multiple_of(i * SUBLANES, SUBLANES)
        xc = x_ref[pl.ds(r, SUBLANES), :]
        sq = xc * xc
        acc = sq[:, :LANES]
        for c0 in range(LANES, d, LANES):
            acc = acc + sq[:, c0:c0 + LANES]
        r_ref[pl.ds(r, SUBLANES), :] = acc
        return carry
    steps = x_ref.shape[0] // SUBLANES
    lax.fori_loop(0, steps, body, 0, unroll=min(NORM_UNROLL, steps))
    ms = jnp.sum(r_ref[...], axis=-1, keepdims=True) / d
    r_ref[...] = jnp.broadcast_to(lax.rsqrt(ms + RMS_EPS), r_ref.shape)


def _norm_mm_kernel(x_hbm, g_ref, w_ref, up_ref, uf_ref, x_buf, z_ref, r_ref, xsem, *, rows, pool_blocks, c):
    i = pl.program_id(0)
    j = pl.program_id(1)
    bm = x_buf.shape[0]

    def x_copy(blk):
        return pltpu.make_async_copy(x_hbm.at[pl.ds(pl.multiple_of(blk * bm, bm), bm)], x_buf, xsem)

    @pl.when(j == 0)
    def _():
        @pl.when(i == 0)
        def _():
            x_copy(0).start()
        x_copy(i).wait()
        _inv_rms_rows(x_buf, r_ref)

        def body(t, carry):
            r = pl.multiple_of(t * rows, rows)
            inv = r_ref[pl.ds(r, rows), :]
            for c0 in range(0, x_buf.shape[1], LANES):
                z = x_buf[pl.ds(r, rows), c0:c0 + LANES] * inv * g_ref[:, c0:c0 + LANES]
                z_ref[pl.ds(r, rows), c0:c0 + LANES] = z.astype(BF16)
            return carry
        lax.fori_loop(0, bm // rows, body, 0, unroll=2)

        @pl.when(i + 1 < pl.num_programs(0))
        def _():
            x_copy(i + 1).start(priority=1)

    y = jnp.dot(z_ref[...], w_ref[...], preferred_element_type=F32)

    @pl.when(j < pool_blocks)
    def _():
        up_ref[...] = y.astype(up_ref.dtype)

    @pl.when(j >= pool_blocks)
    def _():
        yr = y.astype(BF16).astype(F32)
        ch = c // 2
        for hd in range(y.shape[1] // c):
            words = _pack_halves(yr[:, hd * c:hd * c + ch], yr[:, hd * c + ch:(hd + 1) * c])
            for s in range(ch // LANES):
                uf_ref[hd * (ch // LANES) + s] = words[:, s * LANES:(s + 1) * LANES]


def _norm_matmul(x, g, w, *, pw, c, bm, bn):
    n, d = x.shape
    m = w.shape[1]
    bm, bn = min(bm, n), min(bn, pw, m - pw)
    assert pw % bn == 0 and (m - pw) % bn == 0 and bn % c == 0 and (c // 2) % LANES == 0
    pool_blocks = pw // bn
    slabs_blk = bn // 2 // LANES
    return pl.pallas_call(
        functools.partial(_norm_mm_kernel, rows=min(NORM_ROWS, bm), pool_blocks=pool_blocks, c=c),
        grid=(n // bm, m // bn),
        in_specs=[pl.BlockSpec(memory_space=pl.ANY),
                  pl.BlockSpec((1, d), lambda i, j: (0, 0)),
                  pl.BlockSpec((d, bn), lambda i, j: (0, j))],
        out_specs=[pl.BlockSpec((bm, bn), lambda i, j: (i, jnp.minimum(j, pool_blocks - 1))),
                   pl.BlockSpec((slabs_blk, bm, LANES), lambda i, j: (jnp.maximum(j - pool_blocks, 0), i, 0))],
        out_shape=[jax.ShapeDtypeStruct((n, pw), BF16),
                   jax.ShapeDtypeStruct(((m - pw) // 2 // LANES, n, LANES), U32)],
        scratch_shapes=[pltpu.VMEM((bm, d), F32), pltpu.VMEM((bm, d), BF16), pltpu.VMEM((bm, LANES), F32),
                        pltpu.SemaphoreType.DMA(())],
        compiler_params=_cparams(("arbitrary", "arbitrary"), vmem=VMEM_LIMIT_HIGH),
        name="norm1_w_in",
    )(x, g, w)


def _pool_kernel(cur_ref, prev_ref, next_ref, pw_ref, sc_ref, o_ref, *, seq, ts, c):
    t = pl.program_id(1)
    has_prev = (t > 0).astype(F32)
    has_next = (t < pl.num_programs(1) - 1).astype(F32)
    n_ext = ts + 2 * HALO
    tok = t * ts + lax.broadcasted_iota(I32, (ts, 1), 0)
    for g, w in enumerate(POOL_WINDOWS):
        sl = slice(g * c, (g + 1) * c)
        cur = cur_ref[0, :, sl].astype(F32)
        prev = prev_ref[0, :, sl].astype(F32) * has_prev
        nxt = next_ref[0, :, sl].astype(F32) * has_next
        ext = jnp.concatenate([prev, cur, nxt], axis=0)
        s = ext + pltpu.roll(ext, 1, 0)
        h = 1
        while 2 * h < w:
            s = pltpu.roll(s, h, 0) + pltpu.roll(s, n_ext - h, 0)
            h *= 2
        win = s[HALO:HALO + ts]
        lo = jnp.maximum(tok - w // 2, 0)
        hi = jnp.minimum(tok + w // 2 - 1, seq - 1)
        cnt = (hi - lo + 1).astype(F32)
        p = win / cnt - cur
        y = jnp.dot(p.astype(BF16), pw_ref[g], preferred_element_type=F32)
        o_ref[0, :, sl] = (y * sc_ref[:, sl]).astype(o_ref.dtype)


def _pool_mixer(u3, pool_w, pool_scale, *, ts):
    b, seq, _ = u3.shape
    g, c, _ = pool_w.shape
    pw = g * c
    ts = min(ts, seq)
    nh = seq // HALO
    per = ts // HALO
    return pl.pallas_call(
        functools.partial(_pool_kernel, seq=seq, ts=ts, c=c),
        grid=(b, seq // ts),
        in_specs=[pl.BlockSpec((1, ts, pw), lambda i, t: (i, t, 0)),
                  pl.BlockSpec((1, HALO, pw), lambda i, t: (i, jnp.maximum(t * per - 1, 0), 0)),
                  pl.BlockSpec((1, HALO, pw), lambda i, t: (i, jnp.minimum((t + 1) * per, nh - 1), 0)),
                  pl.BlockSpec((g, c, c), lambda i, t: (0, 0, 0)),
                  pl.BlockSpec((1, pw), lambda i, t: (0, 0))],
        out_specs=pl.BlockSpec((1, ts, pw), lambda i, t: (i, t, 0)),
        out_shape=jax.ShapeDtypeStruct((b, seq, pw), BF16),
        compiler_params=_cparams(("parallel", "parallel")),
        name="pool_mixer",
    )(u3, u3, u3, pool_w, pool_scale)


DFT_N1 = 256


def _dft_mats(n, scale, dtype):
    j = jnp.arange(n, dtype=I32)
    ang = ((j[:, None] * j[None, :]) % n).astype(F32) * (2.0 * math.pi / n)
    return (jnp.cos(ang) * scale).astype(dtype), (jnp.sin(ang) * scale).astype(dtype)


def _fourier_prep_kernel(cc_ref, sc_ref, w_ref, o_ref, *, c):
    w = w_ref[0]
    o_ref[0, :c, :] = jnp.dot(cc_ref[...], w, preferred_element_type=F32,
                              precision=lax.Precision.HIGHEST).astype(o_ref.dtype)
    o_ref[0, c:, :] = jnp.dot(sc_ref[...], w, preferred_element_type=F32,
                              precision=lax.Precision.HIGHEST).astype(o_ref.dtype)


def _fourier_prep(fourier_w):
    h, c, _ = fourier_w.shape
    cc, sc = _dft_mats(c, c ** -0.5, F32)
    return pl.pallas_call(
        functools.partial(_fourier_prep_kernel, c=c),
        grid=(h,),
        in_specs=[pl.BlockSpec((c, c), lambda i: (0, 0)),
                  pl.BlockSpec((c, c), lambda i: (0, 0)),
                  pl.BlockSpec((1, c, c), lambda i: (i, 0, 0))],
        out_specs=pl.BlockSpec((1, 2 * c, c), lambda i: (i, 0, 0)),
        out_shape=jax.ShapeDtypeStruct((h, 2 * c, c), BF16),
        compiler_params=_cparams(("parallel",)),
        name="fourier_prep",
    )(cc, sc, fourier_w)


def _fft(xs):
    n = len(xs)
    if n == 1:
        return xs
    ev, od = _fft(xs[0::2]), _fft(xs[1::2])
    out = [None] * n
    for k in range(n // 2):
        re, im = od[k]
        if k == 0:
            tr, ti = re, im
        elif 4 * k == n:
            tr, ti = im, -re
        else:
            wr, wi = math.cos(2.0 * math.pi * k / n), -math.sin(2.0 * math.pi * k / n)
            tr, ti = re * wr - im * wi, re * wi + im * wr
        er, ei = ev[k]
        out[k] = (er + tr, ei + ti)
        out[k + n // 2] = (er - tr, ei - ti)
    return out


def _fourier_kernel(x_ref, ts_ref, ab_ref, o_ref, y_ref, *, n1, n2, c, rows):
    for j in range(n2):
        halves = [_unpack_halves(x_ref[s, pl.ds(j, n1, stride=n2), :]) for s in range(x_ref.shape[0])]
        xj = jnp.concatenate([lo for lo, _ in halves] + [hi for _, hi in halves], axis=1).astype(BF16)
        y = jnp.dot(ts_ref[j], xj, preferred_element_type=F32)
        y_ref[0, j] = y[:n1]
        y_ref[1, j] = y[n1:]

    nl = c // LANES

    def tile(i, carry):
        r = pl.multiple_of((i // nl) * SUBLANES, SUBLANES)
        l = pl.multiple_of((i % nl) * LANES, LANES)
        zs = [(y_ref[0, j, pl.ds(r, SUBLANES), pl.ds(l, LANES)], y_ref[1, j, pl.ds(r, SUBLANES), pl.ds(l, LANES)])
              for j in range(n2)]
        gs = _fft(zs)
        for k in range(n2):
            y_ref[0, k, pl.ds(r, SUBLANES), pl.ds(l, LANES)] = gs[k][0]
            y_ref[1, k, pl.ds(r, SUBLANES), pl.ds(l, LANES)] = gs[k][1]
        return carry
    lax.fori_loop(0, (n1 // SUBLANES) * nl, tile, 0, unroll=2)

    per = max(1, min(n2, rows // n1))
    for s in range(0, n2, per):
        gr = y_ref[0, s:s + per].reshape(per * n1, c).astype(BF16)
        gi = y_ref[1, s:s + per].reshape(per * n1, c).astype(BF16)
        y = jnp.dot(gr, ab_ref[0, :c, :], preferred_element_type=F32)
        y = y + jnp.dot(gi, ab_ref[0, c:, :], preferred_element_type=F32)
        o_ref[0, s * n1:(s + per) * n1, :] = y.astype(o_ref.dtype)


def _fourier_mixer(uf, ab, *, b, rows):
    slabs, n, _ = uf.shape
    seq = n // b
    h, _, c = ab.shape
    spb = slabs // h
    n1 = min(DFT_N1, seq)
    n2 = seq // n1
    assert n1 * n2 == seq and n2 & (n2 - 1) == 0, "sequence length must be N1 * 2^m"
    k1 = jnp.arange(n1, dtype=I32)
    pos = n2 * k1[None, None, :] + jnp.arange(n2, dtype=I32)[:, None, None]
    ang = ((k1[None, :, None] * pos) % seq).astype(F32) * (2.0 * math.pi / seq)
    ts = (jnp.concatenate([jnp.cos(ang), -jnp.sin(ang)], axis=1) * (seq ** -0.5)).astype(BF16)

    return pl.pallas_call(
        functools.partial(_fourier_kernel, n1=n1, n2=n2, c=c, rows=rows),
        grid=(b, h),
        in_specs=[
            pl.BlockSpec((spb, None, seq, LANES), lambda i, k: (k, i, 0, 0)),
            pl.BlockSpec((n2, 2 * n1, n1), lambda i, k: (0, 0, 0), pipeline_mode=pl.Buffered(1)),
            pl.BlockSpec((1, 2 * c, c), lambda i, k: (k, 0, 0))],
        out_specs=pl.BlockSpec((1, seq, c), lambda i, k: (i, 0, k)),
        out_shape=jax.ShapeDtypeStruct((b, seq, h * c), BF16),
        scratch_shapes=[pltpu.VMEM((2, n2, n1, c), F32)],
        compiler_params=_cparams(("parallel", "parallel")),
        name="fourier_mixer",
    )(uf.reshape(slabs, b, seq, LANES), ts, ab)


LOGIT_ROWS = 256


def _out_proj_kernel(a_ref, f_ref, w_ref, x_ref, grw_ref, rb_ref, o_ref, lg_ref, ss_acc, pl_acc, *, ka):
    j = pl.program_id(1)
    y = jnp.dot(a_ref[...], w_ref[:ka, :], preferred_element_type=F32)
    y = y + jnp.dot(f_ref[...], w_ref[ka:, :], preferred_element_type=F32)
    o_ref[...] = x_ref[...] + y

    @pl.when(j == 0)
    def _():
        ss_acc[...] = jnp.zeros_like(ss_acc)
        pl_acc[...] = jnp.zeros_like(pl_acc)

    rc = min(LOGIT_ROWS, o_ref.shape[0])
    for r in range(0, o_ref.shape[0], rc):
        hc = o_ref[r:r + rc, :]
        sq = hc * hc
        part = sq[:, :LANES]
        for c0 in range(LANES, hc.shape[1], LANES):
            part = part + sq[:, c0:c0 + LANES]
        ss_acc[r:r + rc, :] += part
        pl_acc[r:r + rc, :] += jnp.dot(hc.astype(BF16), grw_ref[...], preferred_element_type=F32)

    @pl.when(j == pl.num_programs(1) - 1)
    def _():
        d = o_ref.shape[1] * pl.num_programs(1)
        ms = jnp.sum(ss_acc[...], axis=-1, keepdims=True) / d
        lg_ref[...] = pl_acc[...] * lax.rsqrt(ms + RMS_EPS) + rb_ref[...]


def _out_proj(a, f, w, x, grw, rb, *, bm, bn):
    n, ka = a.shape
    kf = f.shape[1]
    d = w.shape[1]
    bm, bn = min(bm, n), min(bn, d)
    return pl.pallas_call(
        functools.partial(_out_proj_kernel, ka=ka),
        grid=(n // bm, d // bn),
        in_specs=[pl.BlockSpec((bm, ka), lambda i, j: (i, 0)),
                  pl.BlockSpec((bm, kf), lambda i, j: (i, 0)),
                  pl.BlockSpec((ka + kf, bn), lambda i, j: (0, j)),
                  pl.BlockSpec((bm, bn), lambda i, j: (i, j)),
                  pl.BlockSpec((bn, LANES), lambda i, j: (j, 0)),
                  pl.BlockSpec((1, LANES), lambda i, j: (0, 0))],
        out_specs=[pl.BlockSpec((bm, bn), lambda i, j: (i, j)),
                   pl.BlockSpec((bm, LANES), lambda i, j: (i, 0))],
        out_shape=[jax.ShapeDtypeStruct((n, d), F32), jax.ShapeDtypeStruct((n, LANES), F32)],
        scratch_shapes=[pltpu.VMEM((bm, LANES), F32), pltpu.VMEM((bm, LANES), F32)],
        compiler_params=_cparams(("parallel", "arbitrary"), vmem=VMEM_LIMIT_HIGH),
        name="w_out_residual",
    )(a, f, w, x, grw, rb)


def _route_kernel(lg_ref, pos_ref, gate_ref, texp_ref, ntl_ref, last_ref, nxt_ref, nval_ref, idx_s, rank_s,
                  *, n_tok, n_exp, tm, tb, tmax_pad):
    shift = tm.bit_length() - 1
    iota_e = lax.broadcasted_iota(I32, (n_exp, tb), 0)
    before = (lax.broadcasted_iota(I32, (tb, tb), 0)
              < lax.broadcasted_iota(I32, (tb, tb), 1)).astype(BF16)

    def pass1(i, counts):
        off = pl.multiple_of(i * tb, tb)
        l = lg_ref[pl.ds(off, tb), :].T[:n_exp, :]
        vals, hots = [], []
        for k in range(TOP_K):
            m = jnp.max(l, axis=0, keepdims=True)
            idx = jnp.min(jnp.where(l == m, iota_e, n_exp), axis=0, keepdims=True)
            hot = iota_e == idx
            l = jnp.where(hot, -jnp.inf, l)
            vals.append(m)
            hots.append(hot)
            idx_s[k:k + 1, pl.ds(off, tb)] = idx
        exps = [jnp.exp(v - vals[0]) for v in vals]
        tot = exps[0]
        for e in exps[1:]:
            tot = tot + e
        sel = hots[0].astype(F32)
        for hot in hots[1:]:
            sel = sel + hot.astype(F32)
        rank = jnp.dot(sel.astype(BF16), before, preferred_element_type=F32) + counts
        for k in range(TOP_K):
            gate_ref[k:k + 1, pl.ds(off, tb)] = exps[k] / tot
            rank_s[k:k + 1, pl.ds(off, tb)] = jnp.sum(jnp.where(hots[k], rank, 0.0), axis=0, keepdims=True)
        return counts + jnp.sum(sel, axis=1, keepdims=True)

    counts = lax.fori_loop(0, n_tok // tb, pass1, jnp.zeros((n_exp, 1), F32))
    ntile = (counts.astype(I32) + (tm - 1)) >> shift
    sub = lax.broadcasted_iota(I32, (n_exp, LANES), 0)
    lane = lax.broadcasted_iota(I32, (n_exp, LANES), 1)
    ntile_row = jnp.sum(jnp.where(sub == lane, ntile, 0), axis=0, keepdims=True)
    start = jnp.sum(jnp.where(lane < sub, ntile_row, 0), axis=1, keepdims=True)
    t_lane = lax.broadcasted_iota(I32, (n_exp, tmax_pad), 1)
    texp_ref[...] = jnp.sum((start <= t_lane).astype(I32), axis=0, keepdims=True) - 1
    own = (start <= t_lane) & (t_lane < start + ntile)
    left = counts.astype(I32) - ((t_lane - start) << shift)
    nval_ref[...] = jnp.sum(jnp.where(own, jnp.clip(left, 0, tm), 0), axis=0, keepdims=True)
    ntl_ref[...] = jnp.sum(ntile, axis=0, keepdims=True) + jnp.zeros((1, LANES), I32)
    last = jnp.where(ntile > 0, (start + ntile - 1) << shift, -1)
    last_ref[...] = jnp.sum(jnp.where(sub == lane, last, 0), axis=0, keepdims=True)
    nxt = jnp.min(jnp.where((lane > sub) & (ntile_row > 0), lane, n_exp), axis=1, keepdims=True)
    nxt_ref[...] = jnp.sum(jnp.where(sub == lane, nxt, 0), axis=0, keepdims=True)
    start_rows = start << shift

    def pass2(i, carry):
        off = pl.multiple_of(i * tb, tb)
        for k in range(TOP_K):
            hot = iota_e == idx_s[k:k + 1, pl.ds(off, tb)]
            base = jnp.sum(jnp.where(hot, start_rows, 0), axis=0, keepdims=True)
            pos_ref[k:k + 1, pl.ds(off, tb)] = rank_s[k:k + 1, pl.ds(off, tb)].astype(I32) + base
        return carry
    lax.fori_loop(0, n_tok // tb, pass2, 0)


def _route(logits, *, n_exp, tm, tmax):
    n_tok = logits.shape[0]
    tb = min(256, n_tok)
    tmax_pad = -(-tmax // LANES) * LANES
    return pl.pallas_call(
        functools.partial(_route_kernel, n_tok=n_tok, n_exp=n_exp, tm=tm, tb=tb, tmax_pad=tmax_pad),
        out_shape=[jax.ShapeDtypeStruct((TOP_K, n_tok), I32),
                   jax.ShapeDtypeStruct((TOP_K, n_tok), F32),
                   jax.ShapeDtypeStruct((1, tmax_pad), I32),
                   jax.ShapeDtypeStruct((1, LANES), I32),
                   jax.ShapeDtypeStruct((1, LANES), I32),
                   jax.ShapeDtypeStruct((1, LANES), I32),
                   jax.ShapeDtypeStruct((1, tmax_pad), I32)],
        scratch_shapes=[pltpu.VMEM((TOP_K, n_tok), I32), pltpu.VMEM((TOP_K, n_tok), F32)],
        compiler_params=_cparams(None),
        name="route",
    )(logits)


def _norm_dispatch_kernel(last_ref, pos_ref, h_ref, g_ref, xs_hbm, z_a, z_b, r_ref, zero_v, zsem, sem,
                          *, n_exp, tm, chunk, rows):
    i = pl.program_id(0)
    dh = z_a.shape[1]

    @pl.when(i == 0)
    def _():
        zero_v[...] = jnp.zeros_like(zero_v)

        def zero_tile(e):
            r = jnp.maximum(last_ref[e], 0)
            return pltpu.make_async_copy(zero_v, xs_hbm.at[pl.ds(pl.multiple_of(r, tm), tm)], zsem)
        for e in range(n_exp):
            @pl.when(last_ref[e] >= 0)
            def _():
                zero_tile(e).start()
        for e in range(n_exp):
            @pl.when(last_ref[e] >= 0)
            def _():
                zero_tile(e).wait()

    def wait_rows(z_buf, s):
        for k in range(TOP_K):
            pltpu.make_async_copy(z_buf, xs_hbm.at[pl.ds(0, chunk)], sem.at[s]).wait()

    _inv_rms_rows(h_ref, r_ref)

    def step(z_buf, s, other, s_other):
        @pl.when(i >= 2)
        def _():
            wait_rows(z_buf, s)

        def body(t, carry):
            r = pl.multiple_of(t * rows, rows)
            inv = r_ref[pl.ds(r, rows), :]
            for c0 in range(0, dh, LANES):
                lo = (h_ref[pl.ds(r, rows), c0:c0 + LANES] * inv * g_ref[:, c0:c0 + LANES]).astype(BF16)
                hi = (h_ref[pl.ds(r, rows), dh + c0:dh + c0 + LANES] * inv
                      * g_ref[:, dh + c0:dh + c0 + LANES]).astype(BF16)
                z_buf[pl.ds(r, rows), c0:c0 + LANES] = _pack_halves(lo.astype(F32), hi.astype(F32))
            return carry
        lax.fori_loop(0, chunk // rows, body, 0, unroll=2)

        for n in range(chunk):
            for k in range(TOP_K):
                pltpu.make_async_copy(z_buf.at[pl.ds(n, 1)], xs_hbm.at[pl.ds(pos_ref[0, k, n], 1)],
                                      sem.at[s]).start(priority=k % 2)

        @pl.when(i == pl.num_programs(0) - 1)
        def _():
            wait_rows(z_buf, s)

            @pl.when(i >= 1)
            def _():
                wait_rows(other, s_other)

    @pl.when(i % 2 == 0)
    def _():
        step(z_a, 0, z_b, 1)

    @pl.when(i % 2 == 1)
    def _():
        step(z_b, 1, z_a, 0)


def _norm_dispatch(last, pos3, h, g, *, n_rows, tm):
    n_tok, d = h.shape
    dh = d // 2
    nchunk, _, chunk = pos3.shape
    n_exp = last.shape[0]
    grid_spec = pltpu.PrefetchScalarGridSpec(
        num_scalar_prefetch=1,
        grid=(nchunk,),
        in_specs=[pl.BlockSpec((1, TOP_K, chunk), lambda c, last: (c, 0, 0), memory_space=pltpu.SMEM),
                  pl.BlockSpec((chunk, d), lambda c, last: (c, 0)),
                  pl.BlockSpec((1, d), lambda c, last: (0, 0))],
        out_specs=pl.BlockSpec(memory_space=pl.ANY),
        scratch_shapes=[pltpu.VMEM((chunk, dh), U32), pltpu.VMEM((chunk, dh), U32),
                        pltpu.VMEM((chunk, LANES), F32), pltpu.VMEM((tm, dh), U32),
                        pltpu.SemaphoreType.DMA(()), pltpu.SemaphoreType.DMA((2,))],
    )
    return pl.pallas_call(
        functools.partial(_norm_dispatch_kernel, n_exp=n_exp, tm=tm, chunk=chunk, rows=min(NORM_ROWS, chunk)),
        grid_spec=grid_spec,
        out_shape=jax.ShapeDtypeStruct((n_rows, dh), U32),
        compiler_params=_cparams(("arbitrary",)),
        name="norm2_dispatch",
    )(last, pos3, h, g)


CAST_ROWS = 256


FFN_SUB = 256


def _ffn_kernel(texp_ref, ntl_ref, nxt_ref, nval_ref, x_ref, wg_hbm, wu_hbm, wd_hbm, bgu_ref, bd_ref, o_ref,
                stage_g, stage_u, stage_d, wgu, wd, sems, *, f, n_exp, rows, sub):
    t = pl.program_id(0)
    e = texp_ref[t]
    dh = x_ref.shape[1]
    valid = t < ntl_ref[0]
    first = jnp.logical_or(t == 0, e != texp_ref[jnp.maximum(t - 1, 0)])

    def weight_copies(ex):
        return (pltpu.make_async_copy(wg_hbm.at[ex], stage_g, sems.at[0]),
                pltpu.make_async_copy(wu_hbm.at[ex], stage_u, sems.at[1]),
                pltpu.make_async_copy(wd_hbm.at[ex], stage_d, sems.at[2]))

    @pl.when(t == 0)
    def _():
        for cp in weight_copies(e):
            cp.start()

    @pl.when(jnp.logical_and(valid, first))
    def _():
        for cp in weight_copies(e):
            cp.wait()

        def cast_up(i, carry):
            r = pl.multiple_of(i * rows, rows)
            wgu[pl.ds(r, rows), :f] = stage_g[pl.ds(r, rows), :].astype(BF16)
            wgu[pl.ds(r, rows), f:] = stage_u[pl.ds(r, rows), :].astype(BF16)
            return carry
        lax.fori_loop(0, stage_g.shape[0] // rows, cast_up, 0)

        def cast_down(i, carry):
            r = pl.multiple_of(i * rows, rows)
            wd[pl.ds(r, rows), :] = stage_d[pl.ds(r, rows), :].astype(BF16)
            return carry
        lax.fori_loop(0, stage_d.shape[0] // rows, cast_down, 0)

        nx = nxt_ref[e]

        @pl.when(nx < n_exp)
        def _():
            for cp in weight_copies(nx):
                cp.start(priority=1)

    def ffn_rows(r0, m):
        lo, hi = _unpack_halves(x_ref[r0:r0 + m, :])
        hgu = jnp.dot(lo.astype(BF16), wgu[:dh, :], preferred_element_type=F32)
        hgu = hgu + jnp.dot(hi.astype(BF16), wgu[dh:, :], preferred_element_type=F32)
        hgu = hgu + bgu_ref[0]
        hg = jnp.minimum(hgu[:, :f], SWIGLU_LIMIT)
        hu = jnp.clip(hgu[:, f:], -SWIGLU_LIMIT, SWIGLU_LIMIT)
        act = (hg * (1.0 / (1.0 + jnp.exp(-SWIGLU_ALPHA * hg))) * (hu + 1.0)).astype(BF16)
        ylo = jnp.dot(act, wd[:, :dh], preferred_element_type=F32) + bd_ref[0, :, :dh]
        yhi = jnp.dot(act, wd[:, dh:], preferred_element_type=F32) + bd_ref[0, :, dh:]
        o_ref[r0:r0 + m, :] = _pack_halves(ylo.astype(BF16).astype(F32), yhi.astype(BF16).astype(F32))

    tmr = x_ref.shape[0]
    nv = nval_ref[t]

    @pl.when(nv > sub)
    def _():
        ffn_rows(0, tmr)

    @pl.when(jnp.logical_and(nv > 0, nv <= sub))
    def _():
        ffn_rows(0, sub)
        if tmr > sub:
            o_ref[sub:, :] = jnp.zeros((tmr - sub, dh), o_ref.dtype)


def _expert_ffn(texp, ntl, nxt, nval, xs, w_gate, w_up, w_down, bgu, bd, *, tm):
    n_rows, dh = xs.shape
    n_exp, d, f = w_gate.shape
    tmax = n_rows // tm
    rows = min(CAST_ROWS, f)
    assert d % rows == 0 and f % rows == 0

    def x_map(t, te, nt, nx, nv):
        return (jnp.minimum(t, nt[0] - 1), 0)

    def b_map(t, te, nt, nx, nv):
        return (te[t], 0, 0)

    grid_spec = pltpu.PrefetchScalarGridSpec(
        num_scalar_prefetch=4,
        grid=(tmax,),
        in_specs=[pl.BlockSpec((tm, dh), x_map),
                  pl.BlockSpec(memory_space=pl.ANY),
                  pl.BlockSpec(memory_space=pl.ANY),
                  pl.BlockSpec(memory_space=pl.ANY),
                  pl.BlockSpec((1, 1, 2 * f), b_map),
                  pl.BlockSpec((1, 1, d), b_map)],
        out_specs=pl.BlockSpec((tm, dh), lambda t, te, nt, nx, nv: (t, 0)),
        scratch_shapes=[pltpu.VMEM((d, f), F32), pltpu.VMEM((d, f), F32), pltpu.VMEM((f, d), F32),
                        pltpu.VMEM((d, 2 * f), BF16), pltpu.VMEM((f, d), BF16),
                        pltpu.SemaphoreType.DMA((3,))],
    )
    return pl.pallas_call(
        functools.partial(_ffn_kernel, f=f, n_exp=n_exp, rows=rows, sub=min(FFN_SUB, tm)),
        grid_spec=grid_spec,
        out_shape=jax.ShapeDtypeStruct((n_rows, dh), U32),
        compiler_params=_cparams(("arbitrary",), vmem=VMEM_LIMIT_HIGH),
        name="expert_ffn",
    )(texp, ntl, nxt, nval, xs, w_gate, w_up, w_down, bgu, bd)


COMBINE_ROWS = 16


def _combine_kernel(pos_c, pos_n, ys_hbm, h_ref, g_ref, fg_ref, o_ref, buf_a, buf_b, sem, *, bt, rows):
    i = pl.program_id(0)
    dh = buf_a.shape[-1]

    def start_rows(pos_ref, dst, s, r0):
        for r in range(rows):
            for k in range(TOP_K):
                pltpu.make_async_copy(ys_hbm.at[pl.ds(pos_ref[0, k, r0 + r], 1)],
                                      dst.at[k, pl.ds(r0 + r, 1)], sem.at[s]).start(priority=k % 2)

    def wait_block(dst, s):
        for k in range(TOP_K):
            pltpu.make_async_copy(ys_hbm.at[pl.ds(0, bt)], dst.at[k], sem.at[s]).wait()

    @pl.when(i == 0)
    def _():
        def first(c, carry):
            start_rows(pos_c, buf_a, 0, pl.multiple_of(c * rows, rows))
            return carry
        lax.fori_loop(0, bt // rows, first, 0)

    def step(cur, s_cur, nxt, s_nxt):
        wait_block(cur, s_cur)
        for r0 in range(0, bt, rows):
            start_rows(pos_n, nxt, s_nxt, r0)
            acc_lo = h_ref[r0:r0 + rows, :dh]
            acc_hi = h_ref[r0:r0 + rows, dh:]
            for k in range(TOP_K):
                lo, hi = _unpack_halves(cur[k, r0:r0 + rows, :])
                gk = g_ref[r0:r0 + rows, k:k + 1]
                acc_lo = acc_lo + gk * lo
                acc_hi = acc_hi + gk * hi
            ms = (jnp.sum(acc_lo * acc_lo, axis=-1, keepdims=True)
                  + jnp.sum(acc_hi * acc_hi, axis=-1, keepdims=True)) / (2 * dh)
            inv = lax.rsqrt(ms + RMS_EPS)
            o_ref[r0:r0 + rows, :dh] = acc_lo * inv * fg_ref[:, :dh]
            o_ref[r0:r0 + rows, dh:] = acc_hi * inv * fg_ref[:, dh:]

        @pl.when(i == pl.num_programs(0) - 1)
        def _():
            wait_block(nxt, s_nxt)

    @pl.when(i % 2 == 0)
    def _():
        step(buf_a, 0, buf_b, 1)

    @pl.when(i % 2 == 1)
    def _():
        step(buf_b, 1, buf_a, 0)


def _combine(pos3, ys, h, gates_t, fg, *, bt):
    n, d = h.shape
    nblk = n // bt
    return pl.pallas_call(
        functools.partial(_combine_kernel, bt=bt, rows=min(COMBINE_ROWS, bt)),
        grid=(nblk,),
        in_specs=[pl.BlockSpec((1, TOP_K, bt), lambda i: (i, 0, 0), memory_space=pltpu.SMEM),
                  pl.BlockSpec((1, TOP_K, bt), lambda i: (jnp.minimum(i + 1, nblk - 1), 0, 0),
                               memory_space=pltpu.SMEM),
                  pl.BlockSpec(memory_space=pl.ANY),
                  pl.BlockSpec((bt, d), lambda i: (i, 0)),
                  pl.BlockSpec((bt, TOP_K), lambda i: (i, 0)),
                  pl.BlockSpec((1, d), lambda i: (0, 0))],
        out_specs=pl.BlockSpec((bt, d), lambda i: (i, 0)),
        out_shape=jax.ShapeDtypeStruct((n, d), F32),
        scratch_shapes=[pltpu.VMEM((TOP_K, bt, d // 2), U32), pltpu.VMEM((TOP_K, bt, d // 2), U32),
                        pltpu.SemaphoreType.DMA((2,))],
        compiler_params=_cparams(("arbitrary",)),
        name="combine_final_norm",
    )(pos3, pos3, ys, h, gates_t, fg)


def _layer(x2, b, seq, norm1_g, w_in, pool_w, pool_scale, fourier_w, w_out, norm2_g,
           router_w, router_b, w_gate, b_gate, w_up, b_up, w_down, b_down):
    n, d = x2.shape
    mix = w_in.shape[1]
    pw = pool_scale.shape[0]
    n_exp = router_w.shape[1]
    f = w_gate.shape[2]
    tm = min(512, n)
    tmax = (n * TOP_K) // tm + n_exp

    up, uf = _norm_matmul(x2, norm1_g.reshape(1, d), w_in.astype(BF16), pw=pw, c=fourier_w.shape[1],
                          bm=1024, bn=1024)
    a = _pool_mixer(up.reshape(b, seq, pw), pool_w.astype(BF16), pool_scale.reshape(1, pw), ts=1024)
    yf = _fourier_mixer(uf, _fourier_prep(fourier_w), b=b, rows=1024)
    grw = jnp.zeros((d, LANES), F32).at[:, :n_exp].set(norm2_g[:, None] * router_w).astype(BF16)
    rb = jnp.zeros((1, LANES), F32).at[0, :n_exp].set(router_b)
    h, logits = _out_proj(a.reshape(n, pw), yf.reshape(n, mix - pw), w_out.astype(BF16), x2, grw, rb,
                          bm=1024, bn=1024)

    pos, gates, texp, ntl, last, nxt, nval = _route(logits, n_exp=n_exp, tm=tm, tmax=tmax)
    chunk = min(512, n)
    pos_c = pos.reshape(TOP_K, n // chunk, chunk).transpose(1, 0, 2)
    xs = _norm_dispatch(last[0, :n_exp], pos_c, h, norm2_g.reshape(1, d), n_rows=tmax * tm, tm=tm)

    bgu = jnp.concatenate([b_gate, b_up], axis=-1).reshape(n_exp, 1, 2 * f)
    ys = _expert_ffn(texp[0, :tmax], ntl[0, :1], nxt[0, :n_exp], nval[0, :tmax], xs, w_gate, w_up, w_down, bgu,
                     b_down.reshape(n_exp, 1, d), tm=tm)

    bt = min(256, n)
    pos_b = pos.reshape(TOP_K, n // bt, bt).transpose(1, 0, 2)
    return pos_b, ys, h, gates.T


def kernel(x, norm1_g, w_in, pool_w, pool_scale, fourier_w, w_out, norm2_g, router_w, router_b,
           w_gate, b_gate, w_up, b_up, w_down, b_down, final_g):
    b, seq, d = x.shape
    assert w_in.shape[0] == 1, "only a single layer is supported"
    pos_b, ys, h, gates_t = _layer(
        x.reshape(b * seq, d), b, seq, norm1_g[0], w_in[0], pool_w[0], pool_scale[0], fourier_w[0], w_out[0],
        norm2_g[0], router_w[0], router_b[0], w_gate[0], b_gate[0], w_up[0], b_up[0], w_down[0], b_down[0])
    out = _combine(pos_b, ys, h, gates_t, final_g.reshape(1, d), bt=min(256, b * seq))
    return out.reshape(b, seq, d)
```

```python
import functools
import math

import jax
import jax.numpy as jnp
from jax import lax
from jax.experimental import pallas as pl
from jax.experimental.pallas import tpu as pltpu

F32 = jnp.float32
BF16 = jnp.bfloat16
I32 = jnp.int32
U32 = jnp.uint32

RMS_EPS = 1e-5
POOL_WINDOWS = (2, 4, 8, 16)
TOP_K = 4
SWIGLU_LIMIT = 7.0
SWIGLU_ALPHA = 1.702

LANES = 128
HALO = 16
NORM_ROWS = 16
NORM_UNROLL = 8
SUBLANES = 8
VMEM_LIMIT = 56 * 1024 * 1024
VMEM_LIMIT_HIGH = 60 * 1024 * 1024
HI_MASK = 0xFFFF0000


def _cparams(sem, vmem=VMEM_LIMIT):
    return pltpu.CompilerParams(dimension_semantics=sem, vmem_limit_bytes=vmem)


def _pack_halves(lo_f32, hi_f32):
    lo = lax.bitcast_convert_type(lo_f32, U32) >> 16
    hi = lax.bitcast_convert_type(hi_f32, U32) & jnp.uint32(HI_MASK)
    return lo | hi


def _unpack_halves(w):
    lo = lax.bitcast_convert_type(w << 16, F32)
    hi = lax.bitcast_convert_type(w & jnp.uint32(HI_MASK), F32)
    return lo, hi


def _inv_rms_rows(x_ref, r_ref):
    d = x_ref.shape[1]

    def body(i, carry):
        r = pl.multiple_of(i * SUBLANES, SUBLANES)
        xc = x_ref[pl.ds(r, SUBLANES), :]
        sq = xc * xc
        acc = sq[:, :LANES]
        for c0 in range(LANES, d, LANES):
            acc = acc + sq[:, c0:c0 + LANES]
        r_ref[pl.ds(r, SUBLANES), :] = acc
        return carry
    steps = x_ref.shape[0] // SUBLANES
    lax.fori_loop(0, steps, body, 0, unroll=min(NORM_UNROLL, steps))
    ms = jnp.sum(r_ref[...], axis=-1, keepdims=True) / d
    r_ref[...] = jnp.broadcast_to(lax.rsqrt(ms + RMS_EPS), r_ref.shape)


def _norm_mm_kernel(x_hbm, g_ref, w_ref, up_ref, uf_ref, x_buf, z_ref, r_ref, xsem, *, rows, pool_blocks, c):
    i = pl.program_id(0)
    j = pl.program_id(1)
    bm = x_buf.shape[0]

    def x_copy(blk):
        return pltpu.make_async_copy(x_hbm.at[pl.ds(pl.multiple_of(blk * bm, bm), bm)], x_buf, xsem)

    @pl.when(j == 0)
    def _():
        @pl.when(i == 0)
        def _():
            x_copy(0).start()
        x_copy(i).wait()
        _inv_rms_rows(x_buf, r_ref)

        def body(t, carry):
            r = pl.multiple_of(t * rows, rows)
            inv = r_ref[pl.ds(r, rows), :]
            for c0 in range(0, x_buf.shape[1], LANES):
                z = x_buf[pl.ds(r, rows), c0:c0 + LANES] * inv * g_ref[:, c0:c0 + LANES]
                z_ref[pl.ds(r, rows), c0:c0 + LANES] = z.astype(BF16)
            return carry
        lax.fori_loop(0, bm // rows, body, 0, unroll=2)

        @pl.when(i + 1 < pl.num_programs(0))
        def _():
            x_copy(i + 1).start(priority=1)

    y = jnp.dot(z_ref[...], w_ref[...], preferred_element_type=F32)

    @pl.when(j < pool_blocks)
    def _():
        up_ref[...] = y.astype(up_ref.dtype)

    @pl.when(j >= pool_blocks)
    def _():
        yr = y.astype(BF16).astype(F32)
        ch = c // 2
        for hd in range(y.shape[1] // c):
            words = _pack_halves(yr[:, hd * c:hd * c + ch], yr[:, hd * c + ch:(hd + 1) * c])
            for s in range(ch // LANES):
                uf_ref[hd * (ch // LANES) + s] = words[:, s * LANES:(s + 1) * LANES]


def _norm_matmul(x, g, w, *, pw, c, bm, bn):
    n, d = x.shape
    m = w.shape[1]
    bm, bn = min(bm, n), min(bn, pw, m - pw)
    assert pw % bn == 0 and (m - pw) % bn == 0 and bn % c == 0 and (c // 2) % LANES == 0
    pool_blocks = pw // bn
    slabs_blk = bn // 2 // LANES
    return pl.pallas_call(
        functools.partial(_norm_mm_kernel, rows=min(NORM_ROWS, bm), pool_blocks=pool_blocks, c=c),
        grid=(n // bm, m // bn),
        in_specs=[pl.BlockSpec(memory_space=pl.ANY),
                  pl.BlockSpec((1, d), lambda i, j: (0, 0)),
                  pl.BlockSpec((d, bn), lambda i, j: (0, j))],
        out_specs=[pl.BlockSpec((bm, bn), lambda i, j: (i, jnp.minimum(j, pool_blocks - 1))),
                   pl.BlockSpec((slabs_blk, bm, LANES), lambda i, j: (jnp.maximum(j - pool_blocks, 0), i, 0))],
        out_shape=[jax.ShapeDtypeStruct((n, pw), BF16),
                   jax.ShapeDtypeStruct(((m - pw) // 2 // LANES, n, LANES), U32)],
        scratch_shapes=[pltpu.VMEM((bm, d), F32), pltpu.VMEM((bm, d), BF16), pltpu.VMEM((bm, LANES), F32),
                        pltpu.SemaphoreType.DMA(())],
        compiler_params=_cparams(("arbitrary", "arbitrary"), vmem=VMEM_LIMIT_HIGH),
        name="norm1_w_in",
    )(x, g, w)


def _pool_kernel(cur_ref, prev_ref, next_ref, pw_ref, sc_ref, o_ref, *, seq, ts, c):
    t = pl.program_id(1)
    has_prev = (t > 0).astype(F32)
    has_next = (t < pl.num_programs(1) - 1).astype(F32)
    n_ext = ts + 2 * HALO
    tok = t * ts + lax.broadcasted_iota(I32, (ts, 1), 0)
    for g, w in enumerate(POOL_WINDOWS):
        sl = slice(g * c, (g + 1) * c)
        cur = cur_ref[0, :, sl].astype(F32)
        prev = prev_ref[0, :, sl].astype(F32) * has_prev
        nxt = next_ref[0, :, sl].astype(F32) * has_next
        ext = jnp.concatenate([prev, cur, nxt], axis=0)
        s = ext + pltpu.roll(ext, 1, 0)
        h = 1
        while 2 * h < w:
            s = pltpu.roll(s, h, 0) + pltpu.roll(s, n_ext - h, 0)
            h *= 2
        win = s[HALO:HALO + ts]
        lo = jnp.maximum(tok - w // 2, 0)
        hi = jnp.minimum(tok + w // 2 - 1, seq - 1)
        cnt = (hi - lo + 1).astype(F32)
        p = win / cnt - cur
        y = jnp.dot(p.astype(BF16), pw_ref[g], preferred_element_type=F32)
        o_ref[0, :, sl] = (y * sc_ref[:, sl]).astype(o_ref.dtype)


def _pool_mixer(u3, pool_w, pool_scale, *, ts):
    b, seq, _ = u3.shape
    g, c, _ = pool_w.shape
    pw = g * c
    ts = min(ts, seq)
    nh = seq // HALO
    per = ts // HALO
    return pl.pallas_call(
        functools.partial(_pool_kernel, seq=seq, ts=ts, c=c),
        grid=(b, seq // ts),
        in_specs=[pl.BlockSpec((1, ts, pw), lambda i, t: (i, t, 0)),
                  pl.BlockSpec((1, HALO, pw), lambda i, t: (i, jnp.maximum(t * per - 1, 0), 0)),
                  pl.BlockSpec((1, HALO, pw), lambda i, t: (i, jnp.minimum((t + 1) * per, nh - 1), 0)),
                  pl.BlockSpec((g, c, c), lambda i, t: (0, 0, 0)),
                  pl.BlockSpec((1, pw), lambda i, t: (0, 0))],
        out_specs=pl.BlockSpec((1, ts, pw), lambda i, t: (i, t, 0)),
        out_shape=jax.ShapeDtypeStruct((b, seq, pw), BF16),
        compiler_params=_cparams(("parallel", "parallel")),
        name="pool_mixer",
    )(u3, u3, u3, pool_w, pool_scale)


DFT_N1 = 256


def _dft_mats(n, scale, dtype):
    j = jnp.arange(n, dtype=I32)
    ang = ((j[:, None] * j[None, :]) % n).astype(F32) * (2.0 * math.pi / n)
    return (jnp.cos(ang) * scale).astype(dtype), (jnp.sin(ang) * scale).astype(dtype)


def _fourier_prep_kernel(cc_ref, sc_ref, w_ref, o_ref, *, c):
    w = w_ref[0]
    o_ref[0, :c, :] = jnp.dot(cc_ref[...], w, preferred_element_type=F32,
                              precision=lax.Precision.HIGHEST).astype(o_ref.dtype)
    o_ref[0, c:, :] = jnp.dot(sc_ref[...], w, preferred_element_type=F32,
                              precision=lax.Precision.HIGHEST).astype(o_ref.dtype)


def _fourier_prep(fourier_w):
    h, c, _ = fourier_w.shape
    cc, sc = _dft_mats(c, c ** -0.5, F32)
    return pl.pallas_call(
        functools.partial(_fourier_prep_kernel, c=c),
        grid=(h,),
        in_specs=[pl.BlockSpec((c, c), lambda i: (0, 0)),
                  pl.BlockSpec((c, c), lambda i: (0, 0)),
                  pl.BlockSpec((1, c, c), lambda i: (i, 0, 0))],
        out_specs=pl.BlockSpec((1, 2 * c, c), lambda i: (i, 0, 0)),
        out_shape=jax.ShapeDtypeStruct((h, 2 * c, c), BF16),
        compiler_params=_cparams(("parallel",)),
        name="fourier_prep",
    )(cc, sc, fourier_w)


def _fft(xs):
    n = len(xs)
    if n == 1:
        return xs
    ev, od = _fft(xs[0::2]), _fft(xs[1::2])
    out = [None] * n
    for k in range(n // 2):
        re, im = od[k]
        if k == 0:
            tr, ti = re, im
        elif 4 * k == n:
            tr, ti = im, -re
        else:
            wr, wi = math.cos(2.0 * math.pi * k / n), -math.sin(2.0 * math.pi * k / n)
            tr, ti = re * wr - im * wi, re * wi + im * wr
        er, ei = ev[k]
        out[k] = (er + tr, ei + ti)
        out[k + n // 2] = (er - tr, ei - ti)
    return out


def _fourier_kernel(x_ref, ts_ref, ab_ref, o_ref, y_ref, *, n1, n2, c, rows):
    for j in range(n2):
        halves = [_unpack_halves(x_ref[s, pl.ds(j, n1, stride=n2), :]) for s in range(x_ref.shape[0])]
        xj = jnp.concatenate([lo for lo, _ in halves] + [hi for _, hi in halves], axis=1).astype(BF16)
        y = jnp.dot(ts_ref[j], xj, preferred_element_type=F32)
        y_ref[0, j] = y[:n1]
        y_ref[1, j] = y[n1:]

    nl = c // LANES

    def tile(i, carry):
        r = pl.multiple_of((i // nl) * SUBLANES, SUBLANES)
        l = pl.multiple_of((i % nl) * LANES, LANES)
        zs = [(y_ref[0, j, pl.ds(r, SUBLANES), pl.ds(l, LANES)], y_ref[1, j, pl.ds(r, SUBLANES), pl.ds(l, LANES)])
              for j in range(n2)]
        gs = _fft(zs)
        for k in range(n2):
            y_ref[0, k, pl.ds(r, SUBLANES), pl.ds(l, LANES)] = gs[k][0]
            y_ref[1, k, pl.ds(r, SUBLANES), pl.ds(l, LANES)] = gs[k][1]
        return carry
    lax.fori_loop(0, (n1 // SUBLANES) * nl, tile, 0, unroll=2)

    per = max(1, min(n2, rows // n1))
    for s in range(0, n2, per):
        gr = y_ref[0, s:s + per].reshape(per * n1, c).astype(BF16)
        gi = y_ref[1, s:s + per].reshape(per * n1, c).astype(BF16)
        y = jnp.dot(gr, ab_ref[0, :c, :], preferred_element_type=F32)
        y = y + jnp.dot(gi, ab_ref[0, c:, :], preferred_element_type=F32)
        o_ref[0, s * n1:(s + per) * n1, :] = y.astype(o_ref.dtype)


def _fourier_mixer(uf, ab, *, b, rows):
    slabs, n, _ = uf.shape
    seq = n // b
    h, _, c = ab.shape
    spb = slabs // h
    n1 = min(DFT_N1, seq)
    n2 = seq // n1
    assert n1 * n2 == seq and n2 & (n2 - 1) == 0, "sequence length must be N1 * 2^m"
    k1 = jnp.arange(n1, dtype=I32)
    pos = n2 * k1[None, None, :] + jnp.arange(n2, dtype=I32)[:, None, None]
    ang = ((k1[None, :, None] * pos) % seq).astype(F32) * (2.0 * math.pi / seq)
    ts = (jnp.concatenate([jnp.cos(ang), -jnp.sin(ang)], axis=1) * (seq ** -0.5)).astype(BF16)

    return pl.pallas_call(
        functools.partial(_fourier_kernel, n1=n1, n2=n2, c=c, rows=rows),
        grid=(b, h),
        in_specs=[
            pl.BlockSpec((spb, None, seq, LANES), lambda i, k: (k, i, 0, 0)),
            pl.BlockSpec((n2, 2 * n1, n1), lambda i, k: (0, 0, 0), pipeline_mode=pl.Buffered(1)),
            pl.BlockSpec((1, 2 * c, c), lambda i, k: (k, 0, 0))],
        out_specs=pl.BlockSpec((1, seq, c), lambda i, k: (i, 0, k)),
        out_shape=jax.ShapeDtypeStruct((b, seq, h * c), BF16),
        scratch_shapes=[pltpu.VMEM((2, n2, n1, c), F32)],
        compiler_params=_cparams(("parallel", "parallel")),
        name="fourier_mixer",
    )(uf.reshape(slabs, b, seq, LANES), ts, ab)


LOGIT_ROWS = 256


def _out_proj_kernel(a_ref, f_ref, w_ref, x_ref, grw_ref, rb_ref, o_ref, lg_ref, ss_acc, pl_acc, *, ka):
    j = pl.program_id(1)
    y = jnp.dot(a_ref[...], w_ref[:ka, :], preferred_element_type=F32)
    y = y + jnp.dot(f_ref[...], w_ref[ka:, :], preferred_element_type=F32)
    o_ref[...] = x_ref[...] + y

    @pl.when(j == 0)
    def _():
        ss_acc[...] = jnp.zeros_like(ss_acc)
        pl_acc[...] = jnp.zeros_like(pl_acc)

    rc = min(LOGIT_ROWS, o_ref.shape[0])
    for r in range(0, o_ref.shape[0], rc):
        hc = o_ref[r:r + rc, :]
        sq = hc * hc
        part = sq[:, :LANES]
        for c0 in range(LANES, hc.shape[1], LANES):
            part = part + sq[:, c0:c0 + LANES]
        ss_acc[r:r + rc, :] += part
        pl_acc[r:r + rc, :] += jnp.dot(hc.astype(BF16), grw_ref[...], preferred_element_type=F32)

    @pl.when(j == pl.num_programs(1) - 1)
    def _():
        d = o_ref.shape[1] * pl.num_programs(1)
        ms = jnp.sum(ss_acc[...], axis=-1, keepdims=True) / d
        lg_ref[...] = pl_acc[...] * lax.rsqrt(ms + RMS_EPS) + rb_ref[...]


def _out_proj(a, f, w, x, grw, rb, *, bm, bn):
    n, ka = a.shape
    kf = f.shape[1]
    d = w.shape[1]
    bm, bn = min(bm, n), min(bn, d)
    return pl.pallas_call(
        functools.partial(_out_proj_kernel, ka=ka),
        grid=(n // bm, d // bn),
        in_specs=[pl.BlockSpec((bm, ka), lambda i, j: (i, 0)),
                  pl.BlockSpec((bm, kf), lambda i, j: (i, 0)),
                  pl.BlockSpec((ka + kf, bn), lambda i, j: (0, j)),
                  pl.BlockSpec((bm, bn), lambda i, j: (i, j)),
                  pl.BlockSpec((bn, LANES), lambda i, j: (j, 0)),
                  pl.BlockSpec((1, LANES), lambda i, j: (0, 0))],
        out_specs=[pl.BlockSpec((bm, bn), lambda i, j: (i, j)),
                   pl.BlockSpec((bm, LANES), lambda i, j: (i, 0))],
        out_shape=[jax.ShapeDtypeStruct((n, d), F32), jax.ShapeDtypeStruct((n, LANES), F32)],
        scratch_shapes=[pltpu.VMEM((bm, LANES), F32), pltpu.VMEM((bm, LANES), F32)],
        compiler_params=_cparams(("parallel", "arbitrary"), vmem=VMEM_LIMIT_HIGH),
        name="w_out_residual",
    )(a, f, w, x, grw, rb)


def _route_kernel(lg_ref, pos_ref, gate_ref, texp_ref, ntl_ref, last_ref, nxt_ref, nval_ref, idx_s, rank_s,
                  *, n_tok, n_exp, tm, tb, tmax_pad):
    shift = tm.bit_length() - 1
    iota_e = lax.broadcasted_iota(I32, (n_exp, tb), 0)
    before = (lax.broadcasted_iota(I32, (tb, tb), 0)
              < lax.broadcasted_iota(I32, (tb, tb), 1)).astype(BF16)

    def pass1(i, counts):
        off = pl.multiple_of(i * tb, tb)
        l = lg_ref[pl.ds(off, tb), :].T[:n_exp, :]
        vals, hots = [], []
        for k in range(TOP_K):
            m = jnp.max(l, axis=0, keepdims=True)
            idx = jnp.min(jnp.where(l == m, iota_e, n_exp), axis=0, keepdims=True)
            hot = iota_e == idx
            l = jnp.where(hot, -jnp.inf, l)
            vals.append(m)
            hots.append(hot)
            idx_s[k:k + 1, pl.ds(off, tb)] = idx
        exps = [jnp.exp(v - vals[0]) for v in vals]
        tot = exps[0]
        for e in exps[1:]:
            tot = tot + e
        sel = hots[0].astype(F32)
        for hot in hots[1:]:
            sel = sel + hot.astype(F32)
        rank = jnp.dot(sel.astype(BF16), before, preferred_element_type=F32) + counts
        for k in range(TOP_K):
            gate_ref[k:k + 1, pl.ds(off, tb)] = exps[k] / tot
            rank_s[k:k + 1, pl.ds(off, tb)] = jnp.sum(jnp.where(hots[k], rank, 0.0), axis=0, keepdims=True)
        return counts + jnp.sum(sel, axis=1, keepdims=True)

    counts = lax.fori_loop(0, n_tok // tb, pass1, jnp.zeros((n_exp, 1), F32))
    ntile = (counts.astype(I32) + (tm - 1)) >> shift
    sub = lax.broadcasted_iota(I32, (n_exp, LANES), 0)
    lane = lax.broadcasted_iota(I32, (n_exp, LANES), 1)
    ntile_row = jnp.sum(jnp.where(sub == lane, ntile, 0), axis=0, keepdims=True)
    start = jnp.sum(jnp.where(lane < sub, ntile_row, 0), axis=1, keepdims=True)
    t_lane = lax.broadcasted_iota(I32, (n_exp, tmax_pad), 1)
    texp_ref[...] = jnp.sum((start <= t_lane).astype(I32), axis=0, keepdims=True) - 1
    own = (start <= t_lane) & (t_lane < start + ntile)
    left = counts.astype(I32) - ((t_lane - start) << shift)
    nval_ref[...] = jnp.sum(jnp.where(own, jnp.clip(left, 0, tm), 0), axis=0, keepdims=True)
    ntl_ref[...] = jnp.sum(ntile, axis=0, keepdims=True) + jnp.zeros((1, LANES), I32)
    last = jnp.where(ntile > 0, (start + ntile - 1) << shift, -1)
    last_ref[...] = jnp.sum(jnp.where(sub == lane, last, 0), axis=0, keepdims=True)
    nxt = jnp.min(jnp.where((lane > sub) & (ntile_row > 0), lane, n_exp), axis=1, keepdims=True)
    nxt_ref[...] = jnp.sum(jnp.where(sub == lane, nxt, 0), axis=0, keepdims=True)
    start_rows = start << shift

    def pass2(i, carry):
        off = pl.multiple_of(i * tb, tb)
        for k in range(TOP_K):
            hot = iota_e == idx_s[k:k + 1, pl.ds(off, tb)]
            base = jnp.sum(jnp.where(hot, start_rows, 0), axis=0, keepdims=True)
            pos_ref[k:k + 1, pl.ds(off, tb)] = rank_s[k:k + 1, pl.ds(off, tb)].astype(I32) + base
        return carry
    lax.fori_loop(0, n_tok // tb, pass2, 0)


def _route(logits, *, n_exp, tm, tmax):
    n_tok = logits.shape[0]
    tb = min(256, n_tok)
    tmax_pad = -(-tmax // LANES) * LANES
    return pl.pallas_call(
        functools.partial(_route_kernel, n_tok=n_tok, n_exp=n_exp, tm=tm, tb=tb, tmax_pad=tmax_pad),
        out_shape=[jax.ShapeDtypeStruct((TOP_K, n_tok), I32),
                   jax.ShapeDtypeStruct((TOP_K, n_tok), F32),
                   jax.ShapeDtypeStruct((1, tmax_pad), I32),
                   jax.ShapeDtypeStruct((1, LANES), I32),
                   jax.ShapeDtypeStruct((1, LANES), I32),
                   jax.ShapeDtypeStruct((1, LANES), I32),
                   jax.ShapeDtypeStruct((1, tmax_pad), I32)],
        scratch_shapes=[pltpu.VMEM((TOP_K, n_tok), I32), pltpu.VMEM((TOP_K, n_tok), F32)],
        compiler_params=_cparams(None),
        name="route",
    )(logits)


def _norm_dispatch_kernel(last_ref, pos_ref, h_ref, g_ref, xs_hbm, z_a, z_b, r_ref, zero_v, zsem, sem,
                          *, n_exp, tm, chunk, rows):
    i = pl.program_id(0)
    dh = z_a.shape[1]

    @pl.when(i == 0)
    def _():
        zero_v[...] = jnp.zeros_like(zero_v)

        def zero_tile(e):
            r = jnp.maximum(last_ref[e], 0)
            return pltpu.make_async_copy(zero_v, xs_hbm.at[pl.ds(pl.multiple_of(r, tm), tm)], zsem)
        for e in range(n_exp):
            @pl.when(last_ref[e] >= 0)
            def _():
                zero_tile(e).start()
        for e in range(n_exp):
            @pl.when(last_ref[e] >= 0)
            def _():
                zero_tile(e).wait()

    def wait_rows(z_buf, s):
        for k in range(TOP_K):
            pltpu.make_async_copy(z_buf, xs_hbm.at[pl.ds(0, chunk)], sem.at[s]).wait()

    _inv_rms_rows(h_ref, r_ref)

    def step(z_buf, s, other, s_other):
        @pl.when(i >= 2)
        def _():
            wait_rows(z_buf, s)

        def body(t, carry):
            r = pl.multiple_of(t * rows, rows)
            inv = r_ref[pl.ds(r, rows), :]
            for c0 in range(0, dh, LANES):
                lo = (h_ref[pl.ds(r, rows), c0:c0 + LANES] * inv * g_ref[:, c0:c0 + LANES]).astype(BF16)
                hi = (h_ref[pl.ds(r, rows), dh + c0:dh + c0 + LANES] * inv
                      * g_ref[:, dh + c0:dh + c0 + LANES]).astype(BF16)
                z_buf[pl.ds(r, rows), c0:c0 + LANES] = _pack_halves(lo.astype(F32), hi.astype(F32))
            return carry
        lax.fori_loop(0, chunk // rows, body, 0, unroll=2)

        for n in range(chunk):
            for k in range(TOP_K):
                pltpu.make_async_copy(z_buf.at[pl.ds(n, 1)], xs_hbm.at[pl.ds(pos_ref[0, k, n], 1)],
                                      sem.at[s]).start(priority=k % 2)

        @pl.when(i == pl.num_programs(0) - 1)
        def _():
            wait_rows(z_buf, s)

            @pl.when(i >= 1)
            def _():
                wait_rows(other, s_other)

    @pl.when(i % 2 == 0)
    def _():
        step(z_a, 0, z_b, 1)

    @pl.when(i % 2 == 1)
    def _():
        step(z_b, 1, z_a, 0)


def _norm_dispatch(last, pos3, h, g, *, n_rows, tm):
    n_tok, d = h.shape
    dh = d // 2
    nchunk, _, chunk = pos3.shape
    n_exp = last.shape[0]
    grid_spec = pltpu.PrefetchScalarGridSpec(
        num_scalar_prefetch=1,
        grid=(nchunk,),
        in_specs=[pl.BlockSpec((1, TOP_K, chunk), lambda c, last: (c, 0, 0), memory_space=pltpu.SMEM),
                  pl.BlockSpec((chunk, d), lambda c, last: (c, 0)),
                  pl.BlockSpec((1, d), lambda c, last: (0, 0))],
        out_specs=pl.BlockSpec(memory_space=pl.ANY),
        scratch_shapes=[pltpu.VMEM((chunk, dh), U32), pltpu.VMEM((chunk, dh), U32),
                        pltpu.VMEM((chunk, LANES), F32), pltpu.VMEM((tm, dh), U32),
                        pltpu.SemaphoreType.DMA(()), pltpu.SemaphoreType.DMA((2,))],
    )
    return pl.pallas_call(
        functools.partial(_norm_dispatch_kernel, n_exp=n_exp, tm=tm, chunk=chunk, rows=min(NORM_ROWS, chunk)),
        grid_spec=grid_spec,
        out_shape=jax.ShapeDtypeStruct((n_rows, dh), U32),
        compiler_params=_cparams(("arbitrary",)),
        name="norm2_dispatch",
    )(last, pos3, h, g)


CAST_ROWS = 256


FFN_SUB = 256


def _ffn_kernel(texp_ref, ntl_ref, nxt_ref, nval_ref, x_ref, wg_hbm, wu_hbm, wd_hbm, bgu_ref, bd_ref, o_ref,
                stage_g, stage_u, stage_d, wgu, wd, sems, *, f, n_exp, rows, sub):
    t = pl.program_id(0)
    e = texp_ref[t]
    dh = x_ref.shape[1]
    valid = t < ntl_ref[0]
    first = jnp.logical_or(t == 0, e != texp_ref[jnp.maximum(t - 1, 0)])

    def weight_copies(ex):
        return (pltpu.make_async_copy(wg_hbm.at[ex], stage_g, sems.at[0]),
                pltpu.make_async_copy(wu_hbm.at[ex], stage_u, sems.at[1]),
                pltpu.make_async_copy(wd_hbm.at[ex], stage_d, sems.at[2]))

    @pl.when(t == 0)
    def _():
        for cp in weight_copies(e):
            cp.start()

    @pl.when(jnp.logical_and(valid, first))
    def _():
        for cp in weight_copies(e):
            cp.wait()

        def cast_up(i, carry):
            r = pl.multiple_of(i * rows, rows)
            wgu[pl.ds(r, rows), :f] = stage_g[pl.ds(r, rows), :].astype(BF16)
            wgu[pl.ds(r, rows), f:] = stage_u[pl.ds(r, rows), :].astype(BF16)
            return carry
        lax.fori_loop(0, stage_g.shape[0] // rows, cast_up, 0)

        def cast_down(i, carry):
            r = pl.multiple_of(i * rows, rows)
            wd[pl.ds(r, rows), :] = stage_d[pl.ds(r, rows), :].astype(BF16)
            return carry
        lax.fori_loop(0, stage_d.shape[0] // rows, cast_down, 0)

        nx = nxt_ref[e]

        @pl.when(nx < n_exp)
        def _():
            for cp in weight_copies(nx):
                cp.start(priority=1)

    def ffn_rows(r0, m):
        lo, hi = _unpack_halves(x_ref[r0:r0 + m, :])
        hgu = jnp.dot(lo.astype(BF16), wgu[:dh, :], preferred_element_type=F32)
        hgu = hgu + jnp.dot(hi.astype(BF16), wgu[dh:, :], preferred_element_type=F32)
        hgu = hgu + bgu_ref[0]
        hg = jnp.minimum(hgu[:, :f], SWIGLU_LIMIT)
        hu = jnp.clip(hgu[:, f:], -SWIGLU_LIMIT, SWIGLU_LIMIT)
        act = (hg * (1.0 / (1.0 + jnp.exp(-SWIGLU_ALPHA * hg))) * (hu + 1.0)).astype(BF16)
        ylo = jnp.dot(act, wd[:, :dh], preferred_element_type=F32) + bd_ref[0, :, :dh]
        yhi = jnp.dot(act, wd[:, dh:], preferred_element_type=F32) + bd_ref[0, :, dh:]
        o_ref[r0:r0 + m, :] = _pack_halves(ylo.astype(BF16).astype(F32), yhi.astype(BF16).astype(F32))

    tmr = x_ref.shape[0]
    nv = nval_ref[t]

    @pl.when(nv > sub)
    def _():
        ffn_rows(0, tmr)

    @pl.when(jnp.logical_and(nv > 0, nv <= sub))
    def _():
        ffn_rows(0, sub)
        if tmr > sub:
            o_ref[sub:, :] = jnp.zeros((tmr - sub, dh), o_ref.dtype)


def _expert_ffn(texp, ntl, nxt, nval, xs, w_gate, w_up, w_down, bgu, bd, *, tm):
    n_rows, dh = xs.shape
    n_exp, d, f = w_gate.shape
    tmax = n_rows // tm
    rows = min(CAST_ROWS, f)
    assert d % rows == 0 and f % rows == 0

    def x_map(t, te, nt, nx, nv):
        return (jnp.minimum(t, nt[0] - 1), 0)

    def b_map(t, te, nt, nx, nv):
        return (te[t], 0, 0)

    grid_spec = pltpu.PrefetchScalarGridSpec(
        num_scalar_prefetch=4,
        grid=(tmax,),
        in_specs=[pl.BlockSpec((tm, dh), x_map),
                  pl.BlockSpec(memory_space=pl.ANY),
                  pl.BlockSpec(memory_space=pl.ANY),
                  pl.BlockSpec(memory_space=pl.ANY),
                  pl.BlockSpec((1, 1, 2 * f), b_map),
                  pl.BlockSpec((1, 1, d), b_map)],
        out_specs=pl.BlockSpec((tm, dh), lambda t, te, nt, nx, nv: (t, 0)),
        scratch_shapes=[pltpu.VMEM((d, f), F32), pltpu.VMEM((d, f), F32), pltpu.VMEM((f, d), F32),
                        pltpu.VMEM((d, 2 * f), BF16), pltpu.VMEM((f, d), BF16),
                        pltpu.SemaphoreType.DMA((3,))],
    )
    return pl.pallas_call(
        functools.partial(_ffn_kernel, f=f, n_exp=n_exp, rows=rows, sub=min(FFN_SUB, tm)),
        grid_spec=grid_spec,
        out_shape=jax.ShapeDtypeStruct((n_rows, dh), U32),
        compiler_params=_cparams(("arbitrary",), vmem=VMEM_LIMIT_HIGH),
        name="expert_ffn",
    )(texp, ntl, nxt, nval, xs, w_gate, w_up, w_down, bgu, bd)


COMBINE_ROWS = 16


def _combine_kernel(pos_c, pos_n, ys_hbm, h_ref, g_ref, fg_ref, o_ref, buf_a, buf_b, sem, *, bt, rows):
    i = pl.program_id(0)
    dh = buf_a.shape[-1]

    def start_rows(pos_ref, dst, s, r0):
        for r in range(rows):
            for k in range(TOP_K):
                pltpu.make_async_copy(ys_hbm.at[pl.ds(pos_ref[0, k, r0 + r], 1)],
                                      dst.at[k, pl.ds(r0 + r, 1)], sem.at[s]).start(priority=k % 2)

    def wait_block(dst, s):
        for k in range(TOP_K):
            pltpu.make_async_copy(ys_hbm.at[pl.ds(0, bt)], dst.at[k], sem.at[s]).wait()

    @pl.when(i == 0)
    def _():
        def first(c, carry):
            start_rows(pos_c, buf_a, 0, pl.multiple_of(c * rows, rows))
            return carry
        lax.fori_loop(0, bt // rows, first, 0)

    def step(cur, s_cur, nxt, s_nxt):
        wait_block(cur, s_cur)
        for r0 in range(0, bt, rows):
            start_rows(pos_n, nxt, s_nxt, r0)
            acc_lo = h_ref[r0:r0 + rows, :dh]
            acc_hi = h_ref[r0:r0 + rows, dh:]
            for k in range(TOP_K):
                lo, hi = _unpack_halves(cur[k, r0:r0 + rows, :])
                gk = g_ref[r0:r0 + rows, k:k + 1]
                acc_lo = acc_lo + gk * lo
                acc_hi = acc_hi + gk * hi
            ms = (jnp.sum(acc_lo * acc_lo, axis=-1, keepdims=True)
                  + jnp.sum(acc_hi * acc_hi, axis=-1, keepdims=True)) / (2 * dh)
            inv = lax.rsqrt(ms + RMS_EPS)
            o_ref[r0:r0 + rows, :dh] = acc_lo * inv * fg_ref[:, :dh]
            o_ref[r0:r0 + rows, dh:] = acc_hi * inv * fg_ref[:, dh:]

        @pl.when(i == pl.num_programs(0) - 1)
        def _():
            wait_block(nxt, s_nxt)

    @pl.when(i % 2 == 0)
    def _():
        step(buf_a, 0, buf_b, 1)

    @pl.when(i % 2 == 1)
    def _():
        step(buf_b, 1, buf_a, 0)


def _combine(pos3, ys, h, gates_t, fg, *, bt):
    n, d = h.shape
    nblk = n // bt
    return pl.pallas_call(
        functools.partial(_combine_kernel, bt=bt, rows=min(COMBINE_ROWS, bt)),
        grid=(nblk,),
        in_specs=[pl.BlockSpec((1, TOP_K, bt), lambda i: (i, 0, 0), memory_space=pltpu.SMEM),
                  pl.BlockSpec((1, TOP_K, bt), lambda i: (jnp.minimum(i + 1, nblk - 1), 0, 0),
                               memory_space=pltpu.SMEM),
                  pl.BlockSpec(memory_space=pl.ANY),
                  pl.BlockSpec((bt, d), lambda i: (i, 0)),
                  pl.BlockSpec((bt, TOP_K), lambda i: (i, 0)),
                  pl.BlockSpec((1, d), lambda i: (0, 0))],
        out_specs=pl.BlockSpec((bt, d), lambda i: (i, 0)),
        out_shape=jax.ShapeDtypeStruct((n, d), F32),
        scratch_shapes=[pltpu.VMEM((TOP_K, bt, d // 2), U32), pltpu.VMEM((TOP_K, bt, d // 2), U32),
                        pltpu.SemaphoreType.DMA((2,))],
        compiler_params=_cparams(("arbitrary",)),
        name="combine_final_norm",
    )(pos3, pos3, ys, h, gates_t, fg)


def _layer(x2, b, seq, norm1_g, w_in, pool_w, pool_scale, fourier_w, w_out, norm2_g,
           router_w, router_b, w_gate, b_gate, w_up, b_up, w_down, b_down):
    n, d = x2.shape
    mix = w_in.shape[1]
    pw = pool_scale.shape[0]
    n_exp = router_w.shape[1]
    f = w_gate.shape[2]
    tm = min(512, n)
    tmax = (n * TOP_K) // tm + n_exp

    up, uf = _norm_matmul(x2, norm1_g.reshape(1, d), w_in.astype(BF16), pw=pw, c=fourier_w.shape[1],
                          bm=1024, bn=1024)
    a = _pool_mixer(up.reshape(b, seq, pw), pool_w.astype(BF16), pool_scale.reshape(1, pw), ts=1024)
    yf = _fourier_mixer(uf, _fourier_prep(fourier_w), b=b, rows=1024)
    grw = jnp.zeros((d, LANES), F32).at[:, :n_exp].set(norm2_g[:, None] * router_w).astype(BF16)
    rb = jnp.zeros((1, LANES), F32).at[0, :n_exp].set(router_b)
    h, logits = _out_proj(a.reshape(n, pw), yf.reshape(n, mix - pw), w_out.astype(BF16), x2, grw, rb,
                          bm=1024, bn=1024)

    pos, gates, texp, ntl, last, nxt, nval = _route(logits, n_exp=n_exp, tm=tm, tmax=tmax)
    chunk = min(256, n)
    pos_c = pos.reshape(TOP_K, n // chunk, chunk).transpose(1, 0, 2)
    xs = _norm_dispatch(last[0, :n_exp], pos_c, h, norm2_g.reshape(1, d), n_rows=tmax * tm, tm=tm)

    bgu = jnp.concatenate([b_gate, b_up], axis=-1).reshape(n_exp, 1, 2 * f)
    ys = _expert_ffn(texp[0, :tmax], ntl[0, :1], nxt[0, :n_exp], nval[0, :tmax], xs, w_gate, w_up, w_down, bgu,
                     b_down.reshape(n_exp, 1, d), tm=tm)

    bt = min(256, n)
    pos_b = pos.reshape(TOP_K, n // bt, bt).transpose(1, 0, 2)
    return pos_b, ys, h, gates.T


def kernel(x, norm1_g, w_in, pool_w, pool_scale, fourier_w, w_out, norm2_g, router_w, router_b,
           w_gate, b_gate, w_up, b_up, w_down, b_down, final_g):
    b, seq, d = x.shape
    assert w_in.shape[0] == 1, "only a single layer is supported"
    pos_b, ys, h, gates_t = _layer(
        x.reshape(b * seq, d), b, seq, norm1_g[0], w_in[0], pool_w[0], pool_scale[0], fourier_w[0], w_out[0],
        norm2_g[0], router_w[0], router_b[0], w_gate[0], b_gate[0], w_up[0], b_up[0], w_down[0], b_down[0])
    out = _combine(pos_b, ys, h, gates_t, final_g.reshape(1, d), bt=min(256, b * seq))
    return out.reshape(b, seq, d)
```

```python
import functools
import math

import jax
import jax.numpy as jnp
from jax import lax
from jax.experimental import pallas as pl
from jax.experimental.pallas import tpu as pltpu

F32 = jnp.float32
BF16 = jnp.bfloat16
I32 = jnp.int32
U32 = jnp.uint32

RMS_EPS = 1e-5
POOL_WINDOWS = (2, 4, 8, 16)
TOP_K = 4
SWIGLU_LIMIT = 7.0
SWIGLU_ALPHA = 1.702

LANES = 128
HALO = 16
NORM_ROWS = 16
NORM_UNROLL = 8
SUBLANES = 8
VMEM_LIMIT = 56 * 1024 * 1024
VMEM_LIMIT_HIGH = 60 * 1024 * 1024
HI_MASK = 0xFFFF0000


def _cparams(sem, vmem=VMEM_LIMIT):
    return pltpu.CompilerParams(dimension_semantics=sem, vmem_limit_bytes=vmem)


def _pack_halves(lo_f32, hi_f32):
    lo = lax.bitcast_convert_type(lo_f32, U32) >> 16
    hi = lax.bitcast_convert_type(hi_f32, U32) & jnp.uint32(HI_MASK)
    return lo | hi


def _unpack_halves(w):
    lo = lax.bitcast_convert_type(w << 16, F32)
    hi = lax.bitcast_convert_type(w & jnp.uint32(HI_MASK), F32)
    return lo, hi


def _inv_rms_rows(x_ref, r_ref):
    d = x_ref.shape[1]

    def body(i, carry):
        r = pl.multiple_of(i * SUBLANES, SUBLANES)
        xc = x_ref[pl.ds(r, SUBLANES), :]
        sq = xc * xc
        acc = sq[:, :LANES]
        for c0 in range(LANES, d, LANES):
            acc = acc + sq[:, c0:c0 + LANES]
        r_ref[pl.ds(r, SUBLANES), :] = acc
        return carry
    steps = x_ref.shape[0] // SUBLANES
    lax.fori_loop(0, steps, body, 0, unroll=min(NORM_UNROLL, steps))
    ms = jnp.sum(r_ref[...], axis=-1, keepdims=True) / d
    r_ref[...] = jnp.broadcast_to(lax.rsqrt(ms + RMS_EPS), r_ref.shape)


def _norm_mm_kernel(x_hbm, g_ref, w_ref, up_ref, uf_ref, x_buf, z_ref, r_ref, xsem, *, rows, pool_blocks, c):
    i = pl.program_id(0)
    j = pl.program_id(1)
    bm = x_buf.shape[0]

    def x_copy(blk):
        return pltpu.make_async_copy(x_hbm.at[pl.ds(pl.multiple_of(blk * bm, bm), bm)], x_buf, xsem)

    @pl.when(j == 0)
    def _():
        @pl.when(i == 0)
        def _():
            x_copy(0).start()
        x_copy(i).wait()
        _inv_rms_rows(x_buf, r_ref)

        def body(t, carry):
            r = pl.multiple_of(t * rows, rows)
            inv = r_ref[pl.ds(r, rows), :]
            for c0 in range(0, x_buf.shape[1], LANES):
                z = x_buf[pl.ds(r, rows), c0:c0 + LANES] * inv * g_ref[:, c0:c0 + LANES]
                z_ref[pl.ds(r, rows), c0:c0 + LANES] = z.astype(BF16)
            return carry
        lax.fori_loop(0, bm // rows, body, 0, unroll=2)

        @pl.when(i + 1 < pl.num_programs(0))
        def _():
            x_copy(i + 1).start(priority=1)

    y = jnp.dot(z_ref[...], w_ref[...], preferred_element_type=F32)

    @pl.when(j < pool_blocks)
    def _():
        up_ref[...] = y.astype(up_ref.dtype)

    @pl.when(j >= pool_blocks)
    def _():
        yr = y.astype(BF16).astype(F32)
        ch = c // 2
        for hd in range(y.shape[1] // c):
            words = _pack_halves(yr[:, hd * c:hd * c + ch], yr[:, hd * c + ch:(hd + 1) * c])
            for s in range(ch // LANES):
                uf_ref[hd * (ch // LANES) + s] = words[:, s * LANES:(s + 1) * LANES]


def _norm_matmul(x, g, w, *, pw, c, bm, bn):
    n, d = x.shape
    m = w.shape[1]
    bm, bn = min(bm, n), min(bn, pw, m - pw)
    assert pw % bn == 0 and (m - pw) % bn == 0 and bn % c == 0 and (c // 2) % LANES == 0
    pool_blocks = pw // bn
    slabs_blk = bn // 2 // LANES
    return pl.pallas_call(
        functools.partial(_norm_mm_kernel, rows=min(NORM_ROWS, bm), pool_blocks=pool_blocks, c=c),
        grid=(n // bm, m // bn),
        in_specs=[pl.BlockSpec(memory_space=pl.ANY),
                  pl.BlockSpec((1, d), lambda i, j: (0, 0)),
                  pl.BlockSpec((d, bn), lambda i, j: (0, j))],
        out_specs=[pl.BlockSpec((bm, bn), lambda i, j: (i, jnp.minimum(j, pool_blocks - 1))),
                   pl.BlockSpec((slabs_blk, bm, LANES), lambda i, j: (jnp.maximum(j - pool_blocks, 0), i, 0))],
        out_shape=[jax.ShapeDtypeStruct((n, pw), BF16),
                   jax.ShapeDtypeStruct(((m - pw) // 2 // LANES, n, LANES), U32)],
        scratch_shapes=[pltpu.VMEM((bm, d), F32), pltpu.VMEM((bm, d), BF16), pltpu.VMEM((bm, LANES), F32),
                        pltpu.SemaphoreType.DMA(())],
        compiler_params=_cparams(("arbitrary", "arbitrary"), vmem=VMEM_LIMIT_HIGH),
        name="norm1_w_in",
    )(x, g, w)


def _pool_kernel(cur_ref, prev_ref, next_ref, pw_ref, sc_ref, o_ref, *, seq, ts, c):
    t = pl.program_id(1)
    has_prev = (t > 0).astype(F32)
    has_next = (t < pl.num_programs(1) - 1).astype(F32)
    n_ext = ts + 2 * HALO
    tok = t * ts + lax.broadcasted_iota(I32, (ts, 1), 0)
    for g, w in enumerate(POOL_WINDOWS):
        sl = slice(g * c, (g + 1) * c)
        cur = cur_ref[0, :, sl].astype(F32)
        prev = prev_ref[0, :, sl].astype(F32) * has_prev
        nxt = next_ref[0, :, sl].astype(F32) * has_next
        ext = jnp.concatenate([prev, cur, nxt], axis=0)
        s = ext + pltpu.roll(ext, 1, 0)
        h = 1
        while 2 * h < w:
            s = pltpu.roll(s, h, 0) + pltpu.roll(s, n_ext - h, 0)
            h *= 2
        win = s[HALO:HALO + ts]
        lo = jnp.maximum(tok - w // 2, 0)
        hi = jnp.minimum(tok + w // 2 - 1, seq - 1)
        cnt = (hi - lo + 1).astype(F32)
        p = win / cnt - cur
        y = jnp.dot(p.astype(BF16), pw_ref[g], preferred_element_type=F32)
        o_ref[0, :, sl] = (y * sc_ref[:, sl]).astype(o_ref.dtype)


def _pool_mixer(u3, pool_w, pool_scale, *, ts):
    b, seq, _ = u3.shape
    g, c, _ = pool_w.shape
    pw = g * c
    ts = min(ts, seq)
    nh = seq // HALO
    per = ts // HALO
    return pl.pallas_call(
        functools.partial(_pool_kernel, seq=seq, ts=ts, c=c),
        grid=(b, seq // ts),
        in_specs=[pl.BlockSpec((1, ts, pw), lambda i, t: (i, t, 0)),
                  pl.BlockSpec((1, HALO, pw), lambda i, t: (i, jnp.maximum(t * per - 1, 0), 0)),
                  pl.BlockSpec((1, HALO, pw), lambda i, t: (i, jnp.minimum((t + 1) * per, nh - 1), 0)),
                  pl.BlockSpec((g, c, c), lambda i, t: (0, 0, 0)),
                  pl.BlockSpec((1, pw), lambda i, t: (0, 0))],
        out_specs=pl.BlockSpec((1, ts, pw), lambda i, t: (i, t, 0)),
        out_shape=jax.ShapeDtypeStruct((b, seq, pw), BF16),
        compiler_params=_cparams(("parallel", "parallel")),
        name="pool_mixer",
    )(u3, u3, u3, pool_w, pool_scale)


DFT_N1 = 256


def _dft_mats(n, scale, dtype):
    j = jnp.arange(n, dtype=I32)
    ang = ((j[:, None] * j[None, :]) % n).astype(F32) * (2.0 * math.pi / n)
    return (jnp.cos(ang) * scale).astype(dtype), (jnp.sin(ang) * scale).astype(dtype)


def _fourier_prep_kernel(cc_ref, sc_ref, w_ref, o_ref, *, c):
    w = w_ref[0]
    o_ref[0, :c, :] = jnp.dot(cc_ref[...], w, preferred_element_type=F32,
                              precision=lax.Precision.HIGHEST).astype(o_ref.dtype)
    o_ref[0, c:, :] = jnp.dot(sc_ref[...], w, preferred_element_type=F32,
                              precision=lax.Precision.HIGHEST).astype(o_ref.dtype)


def _fourier_prep(fourier_w):
    h, c, _ = fourier_w.shape
    cc, sc = _dft_mats(c, c ** -0.5, F32)
    return pl.pallas_call(
        functools.partial(_fourier_prep_kernel, c=c),
        grid=(h,),
        in_specs=[pl.BlockSpec((c, c), lambda i: (0, 0)),
                  pl.BlockSpec((c, c), lambda i: (0, 0)),
                  pl.BlockSpec((1, c, c), lambda i: (i, 0, 0))],
        out_specs=pl.BlockSpec((1, 2 * c, c), lambda i: (i, 0, 0)),
        out_shape=jax.ShapeDtypeStruct((h, 2 * c, c), BF16),
        compiler_params=_cparams(("parallel",)),
        name="fourier_prep",
    )(cc, sc, fourier_w)


def _fft(xs):
    n = len(xs)
    if n == 1:
        return xs
    ev, od = _fft(xs[0::2]), _fft(xs[1::2])
    out = [None] * n
    for k in range(n // 2):
        re, im = od[k]
        if k == 0:
            tr, ti = re, im
        elif 4 * k == n:
            tr, ti = im, -re
        else:
            wr, wi = math.cos(2.0 * math.pi * k / n), -math.sin(2.0 * math.pi * k / n)
            tr, ti = re * wr - im * wi, re * wi + im * wr
        er, ei = ev[k]
        out[k] = (er + tr, ei + ti)
        out[k + n // 2] = (er - tr, ei - ti)
    return out


def _fourier_kernel(x_ref, ts_ref, ab_ref, o_ref, y_ref, *, n1, n2, c, rows):
    for j in range(n2):
        halves = [_unpack_halves(x_ref[s, pl.ds(j, n1, stride=n2), :]) for s in range(x_ref.shape[0])]
        xj = jnp.concatenate([lo for lo, _ in halves] + [hi for _, hi in halves], axis=1).astype(BF16)
        y = jnp.dot(ts_ref[j], xj, preferred_element_type=F32)
        y_ref[0, j] = y[:n1]
        y_ref[1, j] = y[n1:]

    nl = c // LANES

    def tile(i, carry):
        r = pl.multiple_of((i // nl) * SUBLANES, SUBLANES)
        l = pl.multiple_of((i % nl) * LANES, LANES)
        zs = [(y_ref[0, j, pl.ds(r, SUBLANES), pl.ds(l, LANES)], y_ref[1, j, pl.ds(r, SUBLANES), pl.ds(l, LANES)])
              for j in range(n2)]
        gs = _fft(zs)
        for k in range(n2):
            y_ref[0, k, pl.ds(r, SUBLANES), pl.ds(l, LANES)] = gs[k][0]
            y_ref[1, k, pl.ds(r, SUBLANES), pl.ds(l, LANES)] = gs[k][1]
        return carry
    lax.fori_loop(0, (n1 // SUBLANES) * nl, tile, 0, unroll=2)

    per = max(1, min(n2, rows // n1))
    for s in range(0, n2, per):
        gr = y_ref[0, s:s + per].reshape(per * n1, c).astype(BF16)
        gi = y_ref[1, s:s + per].reshape(per * n1, c).astype(BF16)
        y = jnp.dot(gr, ab_ref[0, :c, :], preferred_element_type=F32)
        y = y + jnp.dot(gi, ab_ref[0, c:, :], preferred_element_type=F32)
        o_ref[0, s * n1:(s + per) * n1, :] = y.astype(o_ref.dtype)


def _fourier_mixer(uf, ab, *, b, rows):
    slabs, n, _ = uf.shape
    seq = n // b
    h, _, c = ab.shape
    spb = slabs // h
    n1 = min(DFT_N1, seq)
    n2 = seq // n1
    assert n1 * n2 == seq and n2 & (n2 - 1) == 0, "sequence length must be N1 * 2^m"
    k1 = jnp.arange(n1, dtype=I32)
    pos = n2 * k1[None, None, :] + jnp.arange(n2, dtype=I32)[:, None, None]
    ang = ((k1[None, :, None] * pos) % seq).astype(F32) * (2.0 * math.pi / seq)
    ts = (jnp.concatenate([jnp.cos(ang), -jnp.sin(ang)], axis=1) * (seq ** -0.5)).astype(BF16)

    return pl.pallas_call(
        functools.partial(_fourier_kernel, n1=n1, n2=n2, c=c, rows=rows),
        grid=(b, h),
        in_specs=[
            pl.BlockSpec((spb, None, seq, LANES), lambda i, k: (k, i, 0, 0)),
            pl.BlockSpec((n2, 2 * n1, n1), lambda i, k: (0, 0, 0), pipeline_mode=pl.Buffered(1)),
            pl.BlockSpec((1, 2 * c, c), lambda i, k: (k, 0, 0))],
        out_specs=pl.BlockSpec((1, seq, c), lambda i, k: (i, 0, k)),
        out_shape=jax.ShapeDtypeStruct((b, seq, h * c), BF16),
        scratch_shapes=[pltpu.VMEM((2, n2, n1, c), F32)],
        compiler_params=_cparams(("parallel", "parallel")),
        name="fourier_mixer",
    )(uf.reshape(slabs, b, seq, LANES), ts, ab)


LOGIT_ROWS = 256


def _out_proj_kernel(a_ref, f_ref, w_ref, x_ref, grw_ref, rb_ref, o_ref, lg_ref, ss_acc, pl_acc, *, ka):
    j = pl.program_id(1)
    y = jnp.dot(a_ref[...], w_ref[:ka, :], preferred_element_type=F32)
    y = y + jnp.dot(f_ref[...], w_ref[ka:, :], preferred_element_type=F32)
    o_ref[...] = x_ref[...] + y

    @pl.when(j == 0)
    def _():
        ss_acc[...] = jnp.zeros_like(ss_acc)
        pl_acc[...] = jnp.zeros_like(pl_acc)

    rc = min(LOGIT_ROWS, o_ref.shape[0])
    for r in range(0, o_ref.shape[0], rc):
        hc = o_ref[r:r + rc, :]
        sq = hc * hc
        part = sq[:, :LANES]
        for c0 in range(LANES, hc.shape[1], LANES):
            part = part + sq[:, c0:c0 + LANES]
        ss_acc[r:r + rc, :] += part
        pl_acc[r:r + rc, :] += jnp.dot(hc.astype(BF16), grw_ref[...], preferred_element_type=F32)

    @pl.when(j == pl.num_programs(1) - 1)
    def _():
        d = o_ref.shape[1] * pl.num_programs(1)
        ms = jnp.sum(ss_acc[...], axis=-1, keepdims=True) / d
        lg_ref[...] = pl_acc[...] * lax.rsqrt(ms + RMS_EPS) + rb_ref[...]


def _out_proj(a, f, w, x, grw, rb, *, bm, bn):
    n, ka = a.shape
    kf = f.shape[1]
    d = w.shape[1]
    bm, bn = min(bm, n), min(bn, d)
    return pl.pallas_call(
        functools.partial(_out_proj_kernel, ka=ka),
        grid=(n // bm, d // bn),
        in_specs=[pl.BlockSpec((bm, ka), lambda i, j: (i, 0)),
                  pl.BlockSpec((bm, kf), lambda i, j: (i, 0)),
                  pl.BlockSpec((ka + kf, bn), lambda i, j: (0, j)),
                  pl.BlockSpec((bm, bn), lambda i, j: (i, j)),
                  pl.BlockSpec((bn, LANES), lambda i, j: (j, 0)),
                  pl.BlockSpec((1, LANES), lambda i, j: (0, 0))],
        out_specs=[pl.BlockSpec((bm, bn), lambda i, j: (i, j)),
                   pl.BlockSpec((bm, LANES), lambda i, j: (i, 0))],
        out_shape=[jax.ShapeDtypeStruct((n, d), F32), jax.ShapeDtypeStruct((n, LANES), F32)],
        scratch_shapes=[pltpu.VMEM((bm, LANES), F32), pltpu.VMEM((bm, LANES), F32)],
        compiler_params=_cparams(("parallel", "arbitrary"), vmem=VMEM_LIMIT_HIGH),
        name="w_out_residual",
    )(a, f, w, x, grw, rb)


def _route_kernel(lg_ref, pos_ref, gate_ref, texp_ref, ntl_ref, last_ref, nxt_ref, nval_ref, idx_s, rank_s,
                  *, n_tok, n_exp, tm, tb, tmax_pad):
    shift = tm.bit_length() - 1
    iota_e = lax.broadcasted_iota(I32, (n_exp, tb), 0)
    before = (lax.broadcasted_iota(I32, (tb, tb), 0)
              < lax.broadcasted_iota(I32, (tb, tb), 1)).astype(BF16)

    def pass1(i, counts):
        off = pl.multiple_of(i * tb, tb)
        l = lg_ref[pl.ds(off, tb), :].T[:n_exp, :]
        vals, hots = [], []
        for k in range(TOP_K):
            m = jnp.max(l, axis=0, keepdims=True)
            idx = jnp.min(jnp.where(l == m, iota_e, n_exp), axis=0, keepdims=True)
            hot = iota_e == idx
            l = jnp.where(hot, -jnp.inf, l)
            vals.append(m)
            hots.append(hot)
            idx_s[k:k + 1, pl.ds(off, tb)] = idx
        exps = [jnp.exp(v - vals[0]) for v in vals]
        tot = exps[0]
        for e in exps[1:]:
            tot = tot + e
        sel = hots[0].astype(F32)
        for hot in hots[1:]:
            sel = sel + hot.astype(F32)
        rank = jnp.dot(sel.astype(BF16), before, preferred_element_type=F32) + counts
        for k in range(TOP_K):
            gate_ref[k:k + 1, pl.ds(off, tb)] = exps[k] / tot
            rank_s[k:k + 1, pl.ds(off, tb)] = jnp.sum(jnp.where(hots[k], rank, 0.0), axis=0, keepdims=True)
        return counts + jnp.sum(sel, axis=1, keepdims=True)

    counts = lax.fori_loop(0, n_tok // tb, pass1, jnp.zeros((n_exp, 1), F32))
    ntile = (counts.astype(I32) + (tm - 1)) >> shift
    sub = lax.broadcasted_iota(I32, (n_exp, LANES), 0)
    lane = lax.broadcasted_iota(I32, (n_exp, LANES), 1)
    ntile_row = jnp.sum(jnp.where(sub == lane, ntile, 0), axis=0, keepdims=True)
    start = jnp.sum(jnp.where(lane < sub, ntile_row, 0), axis=1, keepdims=True)
    t_lane = lax.broadcasted_iota(I32, (n_exp, tmax_pad), 1)
    texp_ref[...] = jnp.sum((start <= t_lane).astype(I32), axis=0, keepdims=True) - 1
    own = (start <= t_lane) & (t_lane < start + ntile)
    left = counts.astype(I32) - ((t_lane - start) << shift)
    nval_ref[...] = jnp.sum(jnp.where(own, jnp.clip(left, 0, tm), 0), axis=0, keepdims=True)
    ntl_ref[...] = jnp.sum(ntile, axis=0, keepdims=True) + jnp.zeros((1, LANES), I32)
    last = jnp.where(ntile > 0, (start + ntile - 1) << shift, -1)
    last_ref[...] = jnp.sum(jnp.where(sub == lane, last, 0), axis=0, keepdims=True)
    nxt = jnp.min(jnp.where((lane > sub) & (ntile_row > 0), lane, n_exp), axis=1, keepdims=True)
    nxt_ref[...] = jnp.sum(jnp.where(sub == lane, nxt, 0), axis=0, keepdims=True)
    start_rows = start << shift

    def pass2(i, carry):
        off = pl.multiple_of(i * tb, tb)
        for k in range(TOP_K):
            hot = iota_e == idx_s[k:k + 1, pl.ds(off, tb)]
            base = jnp.sum(jnp.where(hot, start_rows, 0), axis=0, keepdims=True)
            pos_ref[k:k + 1, pl.ds(off, tb)] = rank_s[k:k + 1, pl.ds(off, tb)].astype(I32) + base
        return carry
    lax.fori_loop(0, n_tok // tb, pass2, 0)


def _route(logits, *, n_exp, tm, tmax):
    n_tok = logits.shape[0]
    tb = min(256, n_tok)
    tmax_pad = -(-tmax // LANES) * LANES
    return pl.pallas_call(
        functools.partial(_route_kernel, n_tok=n_tok, n_exp=n_exp, tm=tm, tb=tb, tmax_pad=tmax_pad),
        out_shape=[jax.ShapeDtypeStruct((TOP_K, n_tok), I32),
                   jax.ShapeDtypeStruct((TOP_K, n_tok), F32),
                   jax.ShapeDtypeStruct((1, tmax_pad), I32),
                   jax.ShapeDtypeStruct((1, LANES), I32),
                   jax.ShapeDtypeStruct((1, LANES), I32),
                   jax.ShapeDtypeStruct((1, LANES), I32),
                   jax.ShapeDtypeStruct((1, tmax_pad), I32)],
        scratch_shapes=[pltpu.VMEM((TOP_K, n_tok), I32), pltpu.VMEM((TOP_K, n_tok), F32)],
        compiler_params=_cparams(None),
        name="route",
    )(logits)


def _norm_dispatch_kernel(last_ref, pos_ref, h_ref, g_ref, xs_hbm, z_a, z_b, r_ref, zero_v, zsem, sem,
                          *, n_exp, tm, chunk, rows):
    i = pl.program_id(0)
    dh = z_a.shape[1]

    @pl.when(i == 0)
    def _():
        zero_v[...] = jnp.zeros_like(zero_v)

        def zero_tile(e):
            r = jnp.maximum(last_ref[e], 0)
            return pltpu.make_async_copy(zero_v, xs_hbm.at[pl.ds(pl.multiple_of(r, tm), tm)], zsem)
        for e in range(n_exp):
            @pl.when(last_ref[e] >= 0)
            def _():
                zero_tile(e).start()
        for e in range(n_exp):
            @pl.when(last_ref[e] >= 0)
            def _():
                zero_tile(e).wait()

    def wait_rows(z_buf, s):
        for k in range(TOP_K):
            pltpu.make_async_copy(z_buf, xs_hbm.at[pl.ds(0, chunk)], sem.at[s]).wait()

    _inv_rms_rows(h_ref, r_ref)

    def step(z_buf, s, other, s_other):
        @pl.when(i >= 2)
        def _():
            wait_rows(z_buf, s)

        def body(t, carry):
            r = pl.multiple_of(t * rows, rows)
            inv = r_ref[pl.ds(r, rows), :]
            for c0 in range(0, dh, LANES):
                lo = (h_ref[pl.ds(r, rows), c0:c0 + LANES] * inv * g_ref[:, c0:c0 + LANES]).astype(BF16)
                hi = (h_ref[pl.ds(r, rows), dh + c0:dh + c0 + LANES] * inv
                      * g_ref[:, dh + c0:dh + c0 + LANES]).astype(BF16)
                z_buf[pl.ds(r, rows), c0:c0 + LANES] = _pack_halves(lo.astype(F32), hi.astype(F32))
            return carry
        lax.fori_loop(0, chunk // rows, body, 0, unroll=2)

        for n in range(chunk):
            for k in range(TOP_K):
                pltpu.make_async_copy(z_buf.at[pl.ds(n, 1)], xs_hbm.at[pl.ds(pos_ref[0, k, n], 1)],
                                      sem.at[s]).start(priority=k % 2)

        @pl.when(i == pl.num_programs(0) - 1)
        def _():
            wait_rows(z_buf, s)

            @pl.when(i >= 1)
            def _():
                wait_rows(other, s_other)

    @pl.when(i % 2 == 0)
    def _():
        step(z_a, 0, z_b, 1)

    @pl.when(i % 2 == 1)
    def _():
        step(z_b, 1, z_a, 0)


def _norm_dispatch(last, pos3, h, g, *, n_rows, tm):
    n_tok, d = h.shape
    dh = d // 2
    nchunk, _, chunk = pos3.shape
    n_exp = last.shape[0]
    grid_spec = pltpu.PrefetchScalarGridSpec(
        num_scalar_prefetch=1,
        grid=(nchunk,),
        in_specs=[pl.BlockSpec((1, TOP_K, chunk), lambda c, last: (c, 0, 0), memory_space=pltpu.SMEM),
                  pl.BlockSpec((chunk, d), lambda c, last: (c, 0)),
                  pl.BlockSpec((1, d), lambda c, last: (0, 0))],
        out_specs=pl.BlockSpec(memory_space=pl.ANY),
        scratch_shapes=[pltpu.VMEM((chunk, dh), U32), pltpu.VMEM((chunk, dh), U32),
                        pltpu.VMEM((chunk, LANES), F32), pltpu.VMEM((tm, dh), U32),
                        pltpu.SemaphoreType.DMA(()), pltpu.SemaphoreType.DMA((2,))],
    )
    return pl.pallas_call(
        functools.partial(_norm_dispatch_kernel, n_exp=n_exp, tm=tm, chunk=chunk, rows=min(NORM_ROWS, chunk)),
        grid_spec=grid_spec,
        out_shape=jax.ShapeDtypeStruct((n_rows, dh), U32),
        compiler_params=_cparams(("arbitrary",)),
        name="norm2_dispatch",
    )(last, pos3, h, g)


CAST_ROWS = 256


FFN_SUB = 128


def _ffn_kernel(texp_ref, ntl_ref, nxt_ref, nval_ref, x_ref, wg_hbm, wu_hbm, wd_hbm, bgu_ref, bd_ref, o_ref,
                stage_g, stage_u, stage_d, wgu, wd, sems, *, f, n_exp, rows, sub):
    t = pl.program_id(0)
    e = texp_ref[t]
    dh = x_ref.shape[1]
    valid = t < ntl_ref[0]
    first = jnp.logical_or(t == 0, e != texp_ref[jnp.maximum(t - 1, 0)])

    def weight_copies(ex):
        return (pltpu.make_async_copy(wg_hbm.at[ex], stage_g, sems.at[0]),
                pltpu.make_async_copy(wu_hbm.at[ex], stage_u, sems.at[1]),
                pltpu.make_async_copy(wd_hbm.at[ex], stage_d, sems.at[2]))

    @pl.when(t == 0)
    def _():
        for cp in weight_copies(e):
            cp.start()

    @pl.when(jnp.logical_and(valid, first))
    def _():
        for cp in weight_copies(e):
            cp.wait()

        def cast_up(i, carry):
            r = pl.multiple_of(i * rows, rows)
            wgu[pl.ds(r, rows), :f] = stage_g[pl.ds(r, rows), :].astype(BF16)
            wgu[pl.ds(r, rows), f:] = stage_u[pl.ds(r, rows), :].astype(BF16)
            return carry
        lax.fori_loop(0, stage_g.shape[0] // rows, cast_up, 0)

        def cast_down(i, carry):
            r = pl.multiple_of(i * rows, rows)
            wd[pl.ds(r, rows), :] = stage_d[pl.ds(r, rows), :].astype(BF16)
            return carry
        lax.fori_loop(0, stage_d.shape[0] // rows, cast_down, 0)

        nx = nxt_ref[e]

        @pl.when(nx < n_exp)
        def _():
            for cp in weight_copies(nx):
                cp.start(priority=1)

    def ffn_rows(r0, m):
        lo, hi = _unpack_halves(x_ref[r0:r0 + m, :])
        hgu = jnp.dot(lo.astype(BF16), wgu[:dh, :], preferred_element_type=F32)
        hgu = hgu + jnp.dot(hi.astype(BF16), wgu[dh:, :], preferred_element_type=F32)
        hgu = hgu + bgu_ref[0]
        hg = jnp.minimum(hgu[:, :f], SWIGLU_LIMIT)
        hu = jnp.clip(hgu[:, f:], -SWIGLU_LIMIT, SWIGLU_LIMIT)
        act = (hg * (1.0 / (1.0 + jnp.exp(-SWIGLU_ALPHA * hg))) * (hu + 1.0)).astype(BF16)
        ylo = jnp.dot(act, wd[:, :dh], preferred_element_type=F32) + bd_ref[0, :, :dh]
        yhi = jnp.dot(act, wd[:, dh:], preferred_element_type=F32) + bd_ref[0, :, dh:]
        o_ref[r0:r0 + m, :] = _pack_halves(ylo.astype(BF16).astype(F32), yhi.astype(BF16).astype(F32))

    tmr = x_ref.shape[0]
    nv = nval_ref[t]
    for m in range(sub, tmr + 1, sub):
        @pl.when(jnp.logical_and(nv > m - sub, nv <= m))
        def _():
            ffn_rows(0, m)
            if m < tmr:
                o_ref[m:, :] = jnp.zeros((tmr - m, dh), o_ref.dtype)


def _expert_ffn(texp, ntl, nxt, nval, xs, w_gate, w_up, w_down, bgu, bd, *, tm):
    n_rows, dh = xs.shape
    n_exp, d, f = w_gate.shape
    tmax = n_rows // tm
    rows = min(CAST_ROWS, f)
    assert d % rows == 0 and f % rows == 0

    def x_map(t, te, nt, nx, nv):
        return (jnp.minimum(t, nt[0] - 1), 0)

    def b_map(t, te, nt, nx, nv):
        return (te[t], 0, 0)

    grid_spec = pltpu.PrefetchScalarGridSpec(
        num_scalar_prefetch=4,
        grid=(tmax,),
        in_specs=[pl.BlockSpec((tm, dh), x_map),
                  pl.BlockSpec(memory_space=pl.ANY),
                  pl.BlockSpec(memory_space=pl.ANY),
                  pl.BlockSpec(memory_space=pl.ANY),
                  pl.BlockSpec((1, 1, 2 * f), b_map),
                  pl.BlockSpec((1, 1, d), b_map)],
        out_specs=pl.BlockSpec((tm, dh), lambda t, te, nt, nx, nv: (t, 0)),
        scratch_shapes=[pltpu.VMEM((d, f), F32), pltpu.VMEM((d, f), F32), pltpu.VMEM((f, d), F32),
                        pltpu.VMEM((d, 2 * f), BF16), pltpu.VMEM((f, d), BF16),
                        pltpu.SemaphoreType.DMA((3,))],
    )
    return pl.pallas_call(
        functools.partial(_ffn_kernel, f=f, n_exp=n_exp, rows=rows, sub=min(FFN_SUB, tm)),
        grid_spec=grid_spec,
        out_shape=jax.ShapeDtypeStruct((n_rows, dh), U32),
        compiler_params=_cparams(("arbitrary",), vmem=VMEM_LIMIT_HIGH),
        name="expert_ffn",
    )(texp, ntl, nxt, nval, xs, w_gate, w_up, w_down, bgu, bd)


COMBINE_ROWS = 16


def _combine_kernel(pos_c, pos_n, ys_hbm, h_ref, g_ref, fg_ref, o_ref, buf_a, buf_b, sem, *, bt, rows):
    i = pl.program_id(0)
    dh = buf_a.shape[-1]

    def start_rows(pos_ref, dst, s, r0):
        for r in range(rows):
            for k in range(TOP_K):
                pltpu.make_async_copy(ys_hbm.at[pl.ds(pos_ref[0, k, r0 + r], 1)],
                                      dst.at[k, pl.ds(r0 + r, 1)], sem.at[s]).start(priority=k % 2)

    def wait_block(dst, s):
        for k in range(TOP_K):
            pltpu.make_async_copy(ys_hbm.at[pl.ds(0, bt)], dst.at[k], sem.at[s]).wait()

    @pl.when(i == 0)
    def _():
        def first(c, carry):
            start_rows(pos_c, buf_a, 0, pl.multiple_of(c * rows, rows))
            return carry
        lax.fori_loop(0, bt // rows, first, 0)

    def step(cur, s_cur, nxt, s_nxt):
        wait_block(cur, s_cur)
        for r0 in range(0, bt, rows):
            start_rows(pos_n, nxt, s_nxt, r0)
            acc_lo = h_ref[r0:r0 + rows, :dh]
            acc_hi = h_ref[r0:r0 + rows, dh:]
            for k in range(TOP_K):
                lo, hi = _unpack_halves(cur[k, r0:r0 + rows, :])
                gk = g_ref[r0:r0 + rows, k:k + 1]
                acc_lo = acc_lo + gk * lo
                acc_hi = acc_hi + gk * hi
            ms = (jnp.sum(acc_lo * acc_lo, axis=-1, keepdims=True)
                  + jnp.sum(acc_hi * acc_hi, axis=-1, keepdims=True)) / (2 * dh)
            inv = lax.rsqrt(ms + RMS_EPS)
            o_ref[r0:r0 + rows, :dh] = acc_lo * inv * fg_ref[:, :dh]
            o_ref[r0:r0 + rows, dh:] = acc_hi * inv * fg_ref[:, dh:]

        @pl.when(i == pl.num_programs(0) - 1)
        def _():
            wait_block(nxt, s_nxt)

    @pl.when(i % 2 == 0)
    def _():
        step(buf_a, 0, buf_b, 1)

    @pl.when(i % 2 == 1)
    def _():
        step(buf_b, 1, buf_a, 0)


def _combine(pos3, ys, h, gates_t, fg, *, bt):
    n, d = h.shape
    nblk = n // bt
    return pl.pallas_call(
        functools.partial(_combine_kernel, bt=bt, rows=min(COMBINE_ROWS, bt)),
        grid=(nblk,),
        in_specs=[pl.BlockSpec((1, TOP_K, bt), lambda i: (i, 0, 0), memory_space=pltpu.SMEM),
                  pl.BlockSpec((1, TOP_K, bt), lambda i: (jnp.minimum(i + 1, nblk - 1), 0, 0),
                               memory_space=pltpu.SMEM),
                  pl.BlockSpec(memory_space=pl.ANY),
                  pl.BlockSpec((bt, d), lambda i: (i, 0)),
                  pl.BlockSpec((bt, TOP_K), lambda i: (i, 0)),
                  pl.BlockSpec((1, d), lambda i: (0, 0))],
        out_specs=pl.BlockSpec((bt, d), lambda i: (i, 0)),
        out_shape=jax.ShapeDtypeStruct((n, d), F32),
        scratch_shapes=[pltpu.VMEM((TOP_K, bt, d // 2), U32), pltpu.VMEM((TOP_K, bt, d // 2), U32),
                        pltpu.SemaphoreType.DMA((2,))],
        compiler_params=_cparams(("arbitrary",)),
        name="combine_final_norm",
    )(pos3, pos3, ys, h, gates_t, fg)


def _layer(x2, b, seq, norm1_g, w_in, pool_w, pool_scale, fourier_w, w_out, norm2_g,
           router_w, router_b, w_gate, b_gate, w_up, b_up, w_down, b_down):
    n, d = x2.shape
    mix = w_in.shape[1]
    pw = pool_scale.shape[0]
    n_exp = router_w.shape[1]
    f = w_gate.shape[2]
    tm = min(512, n)
    tmax = (n * TOP_K) // tm + n_exp

    up, uf = _norm_matmul(x2, norm1_g.reshape(1, d), w_in.astype(BF16), pw=pw, c=fourier_w.shape[1],
                          bm=1024, bn=1024)
    a = _pool_mixer(up.reshape(b, seq, pw), pool_w.astype(BF16), pool_scale.reshape(1, pw), ts=1024)
    yf = _fourier_mixer(uf, _fourier_prep(fourier_w), b=b, rows=1024)
    grw = jnp.zeros((d, LANES), F32).at[:, :n_exp].set(norm2_g[:, None] * router_w).astype(BF16)
    rb = jnp.zeros((1, LANES), F32).at[0, :n_exp].set(router_b)
    h, logits = _out_proj(a.reshape(n, pw), yf.reshape(n, mix - pw), w_out.astype(BF16), x2, grw, rb,
                          bm=1024, bn=1024)

    pos, gates, texp, ntl, last, nxt, nval = _route(logits, n_exp=n_exp, tm=tm, tmax=tmax)
    chunk = min(256, n)
    pos_c = pos.reshape(TOP_K, n // chunk, chunk).transpose(1, 0, 2)
    xs = _norm_dispatch(last[0, :n_exp], pos_c, h, norm2_g.reshape(1, d), n_rows=tmax * tm, tm=tm)

    bgu = jnp.concatenate([b_gate, b_up], axis=-1).reshape(n_exp, 1, 2 * f)
    ys = _expert_ffn(texp[0, :tmax], ntl[0, :1], nxt[0, :n_exp], nval[0, :tmax], xs, w_gate, w_up, w_down, bgu,
                     b_down.reshape(n_exp, 1, d), tm=tm)

    bt = min(256, n)
    pos_b = pos.reshape(TOP_K, n // bt, bt).transpose(1, 0, 2)
    return pos_b, ys, h, gates.T


def kernel(x, norm1_g, w_in, pool_w, pool_scale, fourier_w, w_out, norm2_g, router_w, router_b,
           w_gate, b_gate, w_up, b_up, w_down, b_down, final_g):
    b, seq, d = x.shape
    assert w_in.shape[0] == 1, "only a single layer is supported"
    pos_b, ys, h, gates_t = _layer(
        x.reshape(b * seq, d), b, seq, norm1_g[0], w_in[0], pool_w[0], pool_scale[0], fourier_w[0], w_out[0],
        norm2_g[0], router_w[0], router_b[0], w_gate[0], b_gate[0], w_up[0], b_up[0], w_down[0], b_down[0])
    out = _combine(pos_b, ys, h, gates_t, final_g.reshape(1, d), bt=min(256, b * seq))
    return out.reshape(b, seq, d)
```

```python
import functools
import math

import jax
import jax.numpy as jnp
from jax import lax
from jax.experimental import pallas as pl
from jax.experimental.pallas import tpu as pltpu

F32 = jnp.float32
BF16 = jnp.bfloat16
I32 = jnp.int32
U32 = jnp.uint32

RMS_EPS = 1e-5
POOL_WINDOWS = (2, 4, 8, 16)
TOP_K = 4
SWIGLU_LIMIT = 7.0
SWIGLU_ALPHA = 1.702

LANES = 128
HALO = 16
NORM_ROWS = 16
NORM_UNROLL = 8
SUBLANES = 8
VMEM_LIMIT = 56 * 1024 * 1024
VMEM_LIMIT_HIGH = 60 * 1024 * 1024
HI_MASK = 0xFFFF0000


def _cparams(sem, vmem=VMEM_LIMIT):
    return pltpu.CompilerParams(dimension_semantics=sem, vmem_limit_bytes=vmem)


def _pack_halves(lo_f32, hi_f32):
    lo = lax.bitcast_convert_type(lo_f32, U32) >> 16
    hi = lax.bitcast_convert_type(hi_f32, U32) & jnp.uint32(HI_MASK)
    return lo | hi


def _unpack_halves(w):
    lo = lax.bitcast_convert_type(w << 16, F32)
    hi = lax.bitcast_convert_type(w & jnp.uint32(HI_MASK), F32)
    return lo, hi


def _inv_rms_rows(x_ref, r_ref):
    d = x_ref.shape[1]

    def body(i, carry):
        r = pl.multiple_of(i * SUBLANES, SUBLANES)
        xc = x_ref[pl.ds(r, SUBLANES), :]
        sq = xc * xc
        acc = sq[:, :LANES]
        for c0 in range(LANES, d, LANES):
            acc = acc + sq[:, c0:c0 + LANES]
        r_ref[pl.ds(r, SUBLANES), :] = acc
        return carry
    steps = x_ref.shape[0] // SUBLANES
    lax.fori_loop(0, steps, body, 0, unroll=min(NORM_UNROLL, steps))
    ms = jnp.sum(r_ref[...], axis=-1, keepdims=True) / d
    r_ref[...] = jnp.broadcast_to(lax.rsqrt(ms + RMS_EPS), r_ref.shape)


def _norm_mm_kernel(x_hbm, g_ref, w_ref, up_ref, uf_ref, x_buf, z_ref, r_ref, xsem, *, rows, pool_blocks, c):
    i = pl.program_id(0)
    j = pl.program_id(1)
    bm = x_buf.shape[0]

    def x_copy(blk):
        return pltpu.make_async_copy(x_hbm.at[pl.ds(pl.multiple_of(blk * bm, bm), bm)], x_buf, xsem)

    @pl.when(j == 0)
    def _():
        @pl.when(i == 0)
        def _():
            x_copy(0).start()
        x_copy(i).wait()
        _inv_rms_rows(x_buf, r_ref)

        def body(t, carry):
            r = pl.multiple_of(t * rows, rows)
            inv = r_ref[pl.ds(r, rows), :]
            for c0 in range(0, x_buf.shape[1], LANES):
                z = x_buf[pl.ds(r, rows), c0:c0 + LANES] * inv * g_ref[:, c0:c0 + LANES]
                z_ref[pl.ds(r, rows), c0:c0 + LANES] = z.astype(BF16)
            return carry
        lax.fori_loop(0, bm // rows, body, 0, unroll=2)

        @pl.when(i + 1 < pl.num_programs(0))
        def _():
            x_copy(i + 1).start(priority=1)

    y = jnp.dot(z_ref[...], w_ref[...], preferred_element_type=F32)

    @pl.when(j < pool_blocks)
    def _():
        up_ref[...] = y.astype(up_ref.dtype)

    @pl.when(j >= pool_blocks)
    def _():
        yr = y.astype(BF16).astype(F32)
        ch = c // 2
        for hd in range(y.shape[1] // c):
            words = _pack_halves(yr[:, hd * c:hd * c + ch], yr[:, hd * c + ch:(hd + 1) * c])
            for s in range(ch // LANES):
                uf_ref[hd * (ch // LANES) + s] = words[:, s * LANES:(s + 1) * LANES]


def _norm_matmul(x, g, w, *, pw, c, bm, bn):
    n, d = x.shape
    m = w.shape[1]
    bm, bn = min(bm, n), min(bn, pw, m - pw)
    assert pw % bn == 0 and (m - pw) % bn == 0 and bn % c == 0 and (c // 2) % LANES == 0
    pool_blocks = pw // bn
    slabs_blk = bn // 2 // LANES
    return pl.pallas_call(
        functools.partial(_norm_mm_kernel, rows=min(NORM_ROWS, bm), pool_blocks=pool_blocks, c=c),
        grid=(n // bm, m // bn),
        in_specs=[pl.BlockSpec(memory_space=pl.ANY),
                  pl.BlockSpec((1, d), lambda i, j: (0, 0)),
                  pl.BlockSpec((d, bn), lambda i, j: (0, j))],
        out_specs=[pl.BlockSpec((bm, bn), lambda i, j: (i, jnp.minimum(j, pool_blocks - 1))),
                   pl.BlockSpec((slabs_blk, bm, LANES), lambda i, j: (jnp.maximum(j - pool_blocks, 0), i, 0))],
        out_shape=[jax.ShapeDtypeStruct((n, pw), BF16),
                   jax.ShapeDtypeStruct(((m - pw) // 2 // LANES, n, LANES), U32)],
        scratch_shapes=[pltpu.VMEM((bm, d), F32), pltpu.VMEM((bm, d), BF16), pltpu.VMEM((bm, LANES), F32),
                        pltpu.SemaphoreType.DMA(())],
        compiler_params=_cparams(("arbitrary", "arbitrary"), vmem=VMEM_LIMIT_HIGH),
        name="norm1_w_in",
    )(x, g, w)


def _pool_kernel(cur_ref, prev_ref, next_ref, pw_ref, sc_ref, o_ref, *, seq, ts, c):
    t = pl.program_id(1)
    has_prev = (t > 0).astype(F32)
    has_next = (t < pl.num_programs(1) - 1).astype(F32)
    n_ext = ts + 2 * HALO
    tok = t * ts + lax.broadcasted_iota(I32, (ts, 1), 0)
    for g, w in enumerate(POOL_WINDOWS):
        sl = slice(g * c, (g + 1) * c)
        cur = cur_ref[0, :, sl].astype(F32)
        prev = prev_ref[0, :, sl].astype(F32) * has_prev
        nxt = next_ref[0, :, sl].astype(F32) * has_next
        ext = jnp.concatenate([prev, cur, nxt], axis=0)
        s = ext + pltpu.roll(ext, 1, 0)
        h = 1
        while 2 * h < w:
            s = pltpu.roll(s, h, 0) + pltpu.roll(s, n_ext - h, 0)
            h *= 2
        win = s[HALO:HALO + ts]
        lo = jnp.maximum(tok - w // 2, 0)
        hi = jnp.minimum(tok + w // 2 - 1, seq - 1)
        cnt = (hi - lo + 1).astype(F32)
        p = win / cnt - cur
        y = jnp.dot(p.astype(BF16), pw_ref[g], preferred_element_type=F32)
        o_ref[0, :, sl] = (y * sc_ref[:, sl]).astype(o_ref.dtype)


def _pool_mixer(u3, pool_w, pool_scale, *, ts):
    b, seq, _ = u3.shape
    g, c, _ = pool_w.shape
    pw = g * c
    ts = min(ts, seq)
    nh = seq // HALO
    per = ts // HALO
    return pl.pallas_call(
        functools.partial(_pool_kernel, seq=seq, ts=ts, c=c),
        grid=(b, seq // ts),
        in_specs=[pl.BlockSpec((1, ts, pw), lambda i, t: (i, t, 0)),
                  pl.BlockSpec((1, HALO, pw), lambda i, t: (i, jnp.maximum(t * per - 1, 0), 0)),
                  pl.BlockSpec((1, HALO, pw), lambda i, t: (i, jnp.minimum((t + 1) * per, nh - 1), 0)),
                  pl.BlockSpec((g, c, c), lambda i, t: (0, 0, 0)),
                  pl.BlockSpec((1, pw), lambda i, t: (0, 0))],
        out_specs=pl.BlockSpec((1, ts, pw), lambda i, t: (i, t, 0)),
        out_shape=jax.ShapeDtypeStruct((b, seq, pw), BF16),
        compiler_params=_cparams(("parallel", "parallel")),
        name="pool_mixer",
    )(u3, u3, u3, pool_w, pool_scale)


DFT_N1 = 256


def _dft_mats(n, scale, dtype):
    j = jnp.arange(n, dtype=I32)
    ang = ((j[:, None] * j[None, :]) % n).astype(F32) * (2.0 * math.pi / n)
    return (jnp.cos(ang) * scale).astype(dtype), (jnp.sin(ang) * scale).astype(dtype)


def _fourier_prep_kernel(cc_ref, sc_ref, w_ref, o_ref, *, c):
    w = w_ref[0]
    o_ref[0, :c, :] = jnp.dot(cc_ref[...], w, preferred_element_type=F32,
                              precision=lax.Precision.HIGHEST).astype(o_ref.dtype)
    o_ref[0, c:, :] = jnp.dot(sc_ref[...], w, preferred_element_type=F32,
                              precision=lax.Precision.HIGHEST).astype(o_ref.dtype)


def _fourier_prep(fourier_w):
    h, c, _ = fourier_w.shape
    cc, sc = _dft_mats(c, c ** -0.5, F32)
    return pl.pallas_call(
        functools.partial(_fourier_prep_kernel, c=c),
        grid=(h,),
        in_specs=[pl.BlockSpec((c, c), lambda i: (0, 0)),
                  pl.BlockSpec((c, c), lambda i: (0, 0)),
                  pl.BlockSpec((1, c, c), lambda i: (i, 0, 0))],
        out_specs=pl.BlockSpec((1, 2 * c, c), lambda i: (i, 0, 0)),
        out_shape=jax.ShapeDtypeStruct((h, 2 * c, c), BF16),
        compiler_params=_cparams(("parallel",)),
        name="fourier_prep",
    )(cc, sc, fourier_w)


def _fft(xs):
    n = len(xs)
    if n == 1:
        return xs
    ev, od = _fft(xs[0::2]), _fft(xs[1::2])
    out = [None] * n
    for k in range(n // 2):
        re, im = od[k]
        if k == 0:
            tr, ti = re, im
        elif 4 * k == n:
            tr, ti = im, -re
        else:
            wr, wi = math.cos(2.0 * math.pi * k / n), -math.sin(2.0 * math.pi * k / n)
            tr, ti = re * wr - im * wi, re * wi + im * wr
        er, ei = ev[k]
        out[k] = (er + tr, ei + ti)
        out[k + n // 2] = (er - tr, ei - ti)
    return out


def _fourier_kernel(x_ref, ts_ref, ab_ref, o_ref, y_ref, *, n1, n2, c, rows):
    for j in range(n2):
        halves = [_unpack_halves(x_ref[s, pl.ds(j, n1, stride=n2), :]) for s in range(x_ref.shape[0])]
        xj = jnp.concatenate([lo for lo, _ in halves] + [hi for _, hi in halves], axis=1).astype(BF16)
        y = jnp.dot(ts_ref[j], xj, preferred_element_type=F32)
        y_ref[0, j] = y[:n1]
        y_ref[1, j] = y[n1:]

    nl = c // LANES

    def tile(i, carry):
        r = pl.multiple_of((i // nl) * SUBLANES, SUBLANES)
        l = pl.multiple_of((i % nl) * LANES, LANES)
        zs = [(y_ref[0, j, pl.ds(r, SUBLANES), pl.ds(l, LANES)], y_ref[1, j, pl.ds(r, SUBLANES), pl.ds(l, LANES)])
              for j in range(n2)]
        gs = _fft(zs)
        for k in range(n2):
            y_ref[0, k, pl.ds(r, SUBLANES), pl.ds(l, LANES)] = gs[k][0]
            y_ref[1, k, pl.ds(r, SUBLANES), pl.ds(l, LANES)] = gs[k][1]
        return carry
    lax.fori_loop(0, (n1 // SUBLANES) * nl, tile, 0, unroll=2)

    per = max(1, min(n2, rows // n1))
    for s in range(0, n2, per):
        gr = y_ref[0, s:s + per].reshape(per * n1, c).astype(BF16)
        gi = y_ref[1, s:s + per].reshape(per * n1, c).astype(BF16)
        y = jnp.dot(gr, ab_ref[0, :c, :], preferred_element_type=F32)
        y = y + jnp.dot(gi, ab_ref[0, c:, :], preferred_element_type=F32)
        o_ref[0, s * n1:(s + per) * n1, :] = y.astype(o_ref.dtype)


def _fourier_mixer(uf, ab, *, b, rows):
    slabs, n, _ = uf.shape
    seq = n // b
    h, _, c = ab.shape
    spb = slabs // h
    n1 = min(DFT_N1, seq)
    n2 = seq // n1
    assert n1 * n2 == seq and n2 & (n2 - 1) == 0, "sequence length must be N1 * 2^m"
    k1 = jnp.arange(n1, dtype=I32)
    pos = n2 * k1[None, None, :] + jnp.arange(n2, dtype=I32)[:, None, None]
    ang = ((k1[None, :, None] * pos) % seq).astype(F32) * (2.0 * math.pi / seq)
    ts = (jnp.concatenate([jnp.cos(ang), -jnp.sin(ang)], axis=1) * (seq ** -0.5)).astype(BF16)

    return pl.pallas_call(
        functools.partial(_fourier_kernel, n1=n1, n2=n2, c=c, rows=rows),
        grid=(b, h),
        in_specs=[
            pl.BlockSpec((spb, None, seq, LANES), lambda i, k: (k, i, 0, 0)),
            pl.BlockSpec((n2, 2 * n1, n1), lambda i, k: (0, 0, 0), pipeline_mode=pl.Buffered(1)),
            pl.BlockSpec((1, 2 * c, c), lambda i, k: (k, 0, 0))],
        out_specs=pl.BlockSpec((1, seq, c), lambda i, k: (i, 0, k)),
        out_shape=jax.ShapeDtypeStruct((b, seq, h * c), BF16),
        scratch_shapes=[pltpu.VMEM((2, n2, n1, c), F32)],
        compiler_params=_cparams(("parallel", "parallel")),
        name="fourier_mixer",
    )(uf.reshape(slabs, b, seq, LANES), ts, ab)


LOGIT_ROWS = 256


def _out_proj_kernel(a_ref, f_ref, w_ref, x_ref, grw_ref, rb_ref, o_ref, lg_ref, ss_acc, pl_acc, *, ka):
    j = pl.program_id(1)
    y = jnp.dot(a_ref[...], w_ref[:ka, :], preferred_element_type=F32)
    y = y + jnp.dot(f_ref[...], w_ref[ka:, :], preferred_element_type=F32)
    o_ref[...] = x_ref[...] + y

    @pl.when(j == 0)
    def _():
        ss_acc[...] = jnp.zeros_like(ss_acc)
        pl_acc[...] = jnp.zeros_like(pl_acc)

    rc = min(LOGIT_ROWS, o_ref.shape[0])
    for r in range(0, o_ref.shape[0], rc):
        hc = o_ref[r:r + rc, :]
        sq = hc * hc
        part = sq[:, :LANES]
        for c0 in range(LANES, hc.shape[1], LANES):
            part = part + sq[:, c0:c0 + LANES]
        ss_acc[r:r + rc, :] += part
        pl_acc[r:r + rc, :] += jnp.dot(hc.astype(BF16), grw_ref[...], preferred_element_type=F32)

    @pl.when(j == pl.num_programs(1) - 1)
    def _():
        d = o_ref.shape[1] * pl.num_programs(1)
        ms = jnp.sum(ss_acc[...], axis=-1, keepdims=True) / d
        lg_ref[...] = pl_acc[...] * lax.rsqrt(ms + RMS_EPS) + rb_ref[...]


def _out_proj(a, f, w, x, grw, rb, *, bm, bn):
    n, ka = a.shape
    kf = f.shape[1]
    d = w.shape[1]
    bm, bn = min(bm, n), min(bn, d)
    return pl.pallas_call(
        functools.partial(_out_proj_kernel, ka=ka),
        grid=(n // bm, d // bn),
        in_specs=[pl.BlockSpec((bm, ka), lambda i, j: (i, 0)),
                  pl.BlockSpec((bm, kf), lambda i, j: (i, 0)),
                  pl.BlockSpec((ka + kf, bn), lambda i, j: (0, j)),
                  pl.BlockSpec((bm, bn), lambda i, j: (i, j)),
                  pl.BlockSpec((bn, LANES), lambda i, j: (j, 0)),
                  pl.BlockSpec((1, LANES), lambda i, j: (0, 0))],
        out_specs=[pl.BlockSpec((bm, bn), lambda i, j: (i, j)),
                   pl.BlockSpec((bm, LANES), lambda i, j: (i, 0))],
        out_shape=[jax.ShapeDtypeStruct((n, d), F32), jax.ShapeDtypeStruct((n, LANES), F32)],
        scratch_shapes=[pltpu.VMEM((bm, LANES), F32), pltpu.VMEM((bm, LANES), F32)],
        compiler_params=_cparams(("parallel", "arbitrary"), vmem=VMEM_LIMIT_HIGH),
        name="w_out_residual",
    )(a, f, w, x, grw, rb)


def _route_kernel(lg_ref, pos_ref, gate_ref, texp_ref, ntl_ref, last_ref, nxt_ref, nval_ref, idx_s, rank_s,
                  *, n_tok, n_exp, tm, tb, tmax_pad):
    shift = tm.bit_length() - 1
    iota_e = lax.broadcasted_iota(I32, (n_exp, tb), 0)
    before = (lax.broadcasted_iota(I32, (tb, tb), 0)
              < lax.broadcasted_iota(I32, (tb, tb), 1)).astype(BF16)

    def pass1(i, counts):
        off = pl.multiple_of(i * tb, tb)
        l = lg_ref[pl.ds(off, tb), :].T[:n_exp, :]
        vals, hots = [], []
        for k in range(TOP_K):
            m = jnp.max(l, axis=0, keepdims=True)
            idx = jnp.min(jnp.where(l == m, iota_e, n_exp), axis=0, keepdims=True)
            hot = iota_e == idx
            l = jnp.where(hot, -jnp.inf, l)
            vals.append(m)
            hots.append(hot)
            idx_s[k:k + 1, pl.ds(off, tb)] = idx
        exps = [jnp.exp(v - vals[0]) for v in vals]
        tot = exps[0]
        for e in exps[1:]:
            tot = tot + e
        sel = hots[0].astype(F32)
        for hot in hots[1:]:
            sel = sel + hot.astype(F32)
        rank = jnp.dot(sel.astype(BF16), before, preferred_element_type=F32) + counts
        for k in range(TOP_K):
            gate_ref[k:k + 1, pl.ds(off, tb)] = exps[k] / tot
            rank_s[k:k + 1, pl.ds(off, tb)] = jnp.sum(jnp.where(hots[k], rank, 0.0), axis=0, keepdims=True)
        return counts + jnp.sum(sel, axis=1, keepdims=True)

    counts = lax.fori_loop(0, n_tok // tb, pass1, jnp.zeros((n_exp, 1), F32))
    ntile = (counts.astype(I32) + (tm - 1)) >> shift
    sub = lax.broadcasted_iota(I32, (n_exp, LANES), 0)
    lane = lax.broadcasted_iota(I32, (n_exp, LANES), 1)
    ntile_row = jnp.sum(jnp.where(sub == lane, ntile, 0), axis=0, keepdims=True)
    start = jnp.sum(jnp.where(lane < sub, ntile_row, 0), axis=1, keepdims=True)
    t_lane = lax.broadcasted_iota(I32, (n_exp, tmax_pad), 1)
    texp_ref[...] = jnp.sum((start <= t_lane).astype(I32), axis=0, keepdims=True) - 1
    own = (start <= t_lane) & (t_lane < start + ntile)
    left = counts.astype(I32) - ((t_lane - start) << shift)
    nval_ref[...] = jnp.sum(jnp.where(own, jnp.clip(left, 0, tm), 0), axis=0, keepdims=True)
    ntl_ref[...] = jnp.sum(ntile, axis=0, keepdims=True) + jnp.zeros((1, LANES), I32)
    last = jnp.where(ntile > 0, (start + ntile - 1) << shift, -1)
    last_ref[...] = jnp.sum(jnp.where(sub == lane, last, 0), axis=0, keepdims=True)
    nxt = jnp.min(jnp.where((lane > sub) & (ntile_row > 0), lane, n_exp), axis=1, keepdims=True)
    nxt_ref[...] = jnp.sum(jnp.where(sub == lane, nxt, 0), axis=0, keepdims=True)
    start_rows = start << shift

    def pass2(i, carry):
        off = pl.multiple_of(i * tb, tb)
        for k in range(TOP_K):
            hot = iota_e == idx_s[k:k + 1, pl.ds(off, tb)]
            base = jnp.sum(jnp.where(hot, start_rows, 0), axis=0, keepdims=True)
            pos_ref[k:k + 1, pl.ds(off, tb)] = rank_s[k:k + 1, pl.ds(off, tb)].astype(I32) + base
        return carry
    lax.fori_loop(0, n_tok // tb, pass2, 0)


def _route(logits, *, n_exp, tm, tmax):
    n_tok = logits.shape[0]
    tb = min(256, n_tok)
    tmax_pad = -(-tmax // LANES) * LANES
    return pl.pallas_call(
        functools.partial(_route_kernel, n_tok=n_tok, n_exp=n_exp, tm=tm, tb=tb, tmax_pad=tmax_pad),
        out_shape=[jax.ShapeDtypeStruct((TOP_K, n_tok), I32),
                   jax.ShapeDtypeStruct((TOP_K, n_tok), F32),
                   jax.ShapeDtypeStruct((1, tmax_pad), I32),
                   jax.ShapeDtypeStruct((1, LANES), I32),
                   jax.ShapeDtypeStruct((1, LANES), I32),
                   jax.ShapeDtypeStruct((1, LANES), I32),
                   jax.ShapeDtypeStruct((1, tmax_pad), I32)],
        scratch_shapes=[pltpu.VMEM((TOP_K, n_tok), I32), pltpu.VMEM((TOP_K, n_tok), F32)],
        compiler_params=_cparams(None),
        name="route",
    )(logits)


def _norm_dispatch_kernel(last_ref, pos_ref, h_ref, g_ref, xs_hbm, z_a, z_b, r_ref, zero_v, zsem, sem,
                          *, n_exp, tm, chunk, rows):
    i = pl.program_id(0)
    dh = z_a.shape[1]

    @pl.when(i == 0)
    def _():
        zero_v[...] = jnp.zeros_like(zero_v)

        def zero_tile(e):
            r = jnp.maximum(last_ref[e], 0)
            return pltpu.make_async_copy(zero_v, xs_hbm.at[pl.ds(pl.multiple_of(r, tm), tm)], zsem)
        for e in range(n_exp):
            @pl.when(last_ref[e] >= 0)
            def _():
                zero_tile(e).start()
        for e in range(n_exp):
            @pl.when(last_ref[e] >= 0)
            def _():
                zero_tile(e).wait()

    def wait_rows(z_buf, s):
        for k in range(TOP_K):
            pltpu.make_async_copy(z_buf, xs_hbm.at[pl.ds(0, chunk)], sem.at[s]).wait()

    _inv_rms_rows(h_ref, r_ref)

    def step(z_buf, s, other, s_other):
        @pl.when(i >= 2)
        def _():
            wait_rows(z_buf, s)

        def body(t, carry):
            r = pl.multiple_of(t * rows, rows)
            inv = r_ref[pl.ds(r, rows), :]
            for c0 in range(0, dh, LANES):
                lo = (h_ref[pl.ds(r, rows), c0:c0 + LANES] * inv * g_ref[:, c0:c0 + LANES]).astype(BF16)
                hi = (h_ref[pl.ds(r, rows), dh + c0:dh + c0 + LANES] * inv
                      * g_ref[:, dh + c0:dh + c0 + LANES]).astype(BF16)
                z_buf[pl.ds(r, rows), c0:c0 + LANES] = _pack_halves(lo.astype(F32), hi.astype(F32))
            return carry
        lax.fori_loop(0, chunk // rows, body, 0, unroll=2)

        for n in range(chunk):
            for k in range(TOP_K):
                pltpu.make_async_copy(z_buf.at[pl.ds(n, 1)], xs_hbm.at[pl.ds(pos_ref[0, k, n], 1)],
                                      sem.at[s]).start(priority=k % 2)

        @pl.when(i == pl.num_programs(0) - 1)
        def _():
            wait_rows(z_buf, s)

            @pl.when(i >= 1)
            def _():
                wait_rows(other, s_other)

    @pl.when(i % 2 == 0)
    def _():
        step(z_a, 0, z_b, 1)

    @pl.when(i % 2 == 1)
    def _():
        step(z_b, 1, z_a, 0)


def _norm_dispatch(last, pos3, h, g, *, n_rows, tm):
    n_tok, d = h.shape
    dh = d // 2
    nchunk, _, chunk = pos3.shape
    n_exp = last.shape[0]
    grid_spec = pltpu.PrefetchScalarGridSpec(
        num_scalar_prefetch=1,
        grid=(nchunk,),
        in_specs=[pl.BlockSpec((1, TOP_K, chunk), lambda c, last: (c, 0, 0), memory_space=pltpu.SMEM),
                  pl.BlockSpec((chunk, d), lambda c, last: (c, 0)),
                  pl.BlockSpec((1, d), lambda c, last: (0, 0))],
        out_specs=pl.BlockSpec(memory_space=pl.ANY),
        scratch_shapes=[pltpu.VMEM((chunk, dh), U32), pltpu.VMEM((chunk, dh), U32),
                        pltpu.VMEM((chunk, LANES), F32), pltpu.VMEM((tm, dh), U32),
                        pltpu.SemaphoreType.DMA(()), pltpu.SemaphoreType.DMA((2,))],
    )
    return pl.pallas_call(
        functools.partial(_norm_dispatch_kernel, n_exp=n_exp, tm=tm, chunk=chunk, rows=min(NORM_ROWS, chunk)),
        grid_spec=grid_spec,
        out_shape=jax.ShapeDtypeStruct((n_rows, dh), U32),
        compiler_params=_cparams(("arbitrary",)),
        name="norm2_dispatch",
    )(last, pos3, h, g)


CAST_ROWS = 256


FFN_SUB = 256


def _ffn_kernel(texp_ref, ntl_ref, nxt_ref, nval_ref, x_ref, wg_hbm, wu_hbm, wd_hbm, bgu_ref, bd_ref, o_ref,
                stage_g, stage_u, stage_d, wgu, wd, sems, *, f, n_exp, rows, sub):
    t = pl.program_id(0)
    e = texp_ref[t]
    dh = x_ref.shape[1]
    valid = t < ntl_ref[0]
    first = jnp.logical_or(t == 0, e != texp_ref[jnp.maximum(t - 1, 0)])

    def weight_copies(ex):
        return (pltpu.make_async_copy(wg_hbm.at[ex], stage_g, sems.at[0]),
                pltpu.make_async_copy(wu_hbm.at[ex], stage_u, sems.at[1]),
                pltpu.make_async_copy(wd_hbm.at[ex], stage_d, sems.at[2]))

    @pl.when(t == 0)
    def _():
        for cp in weight_copies(e):
            cp.start()

    @pl.when(jnp.logical_and(valid, first))
    def _():
        for cp in weight_copies(e):
            cp.wait()

        def cast_up(i, carry):
            r = pl.multiple_of(i * rows, rows)
            wgu[pl.ds(r, rows), :f] = stage_g[pl.ds(r, rows), :].astype(BF16)
            wgu[pl.ds(r, rows), f:] = stage_u[pl.ds(r, rows), :].astype(BF16)
            return carry
        lax.fori_loop(0, stage_g.shape[0] // rows, cast_up, 0)

        def cast_down(i, carry):
            r = pl.multiple_of(i * rows, rows)
            wd[pl.ds(r, rows), :] = stage_d[pl.ds(r, rows), :].astype(BF16)
            return carry
        lax.fori_loop(0, stage_d.shape[0] // rows, cast_down, 0)

        nx = nxt_ref[e]

        @pl.when(nx < n_exp)
        def _():
            for cp in weight_copies(nx):
                cp.start(priority=1)

    def ffn_rows(r0, m):
        lo, hi = _unpack_halves(x_ref[r0:r0 + m, :])
        hgu = jnp.dot(lo.astype(BF16), wgu[:dh, :], preferred_element_type=F32)
        hgu = hgu + jnp.dot(hi.astype(BF16), wgu[dh:, :], preferred_element_type=F32)
        hgu = hgu + bgu_ref[0]
        hg = jnp.minimum(hgu[:, :f], SWIGLU_LIMIT)
        hu = jnp.clip(hgu[:, f:], -SWIGLU_LIMIT, SWIGLU_LIMIT)
        act = (hg * (1.0 / (1.0 + jnp.exp(-SWIGLU_ALPHA * hg))) * (hu + 1.0)).astype(BF16)
        ylo = jnp.dot(act, wd[:, :dh], preferred_element_type=F32) + bd_ref[0, :, :dh]
        yhi = jnp.dot(act, wd[:, dh:], preferred_element_type=F32) + bd_ref[0, :, dh:]
        o_ref[r0:r0 + m, :] = _pack_halves(ylo.astype(BF16).astype(F32), yhi.astype(BF16).astype(F32))

    tmr = x_ref.shape[0]
    nv = nval_ref[t]

    @pl.when(nv > sub)
    def _():
        ffn_rows(0, tmr)

    @pl.when(jnp.logical_and(nv > 0, nv <= sub))
    def _():
        ffn_rows(0, sub)
        if tmr > sub:
            o_ref[sub:, :] = jnp.zeros((tmr - sub, dh), o_ref.dtype)


def _expert_ffn(texp, ntl, nxt, nval, xs, w_gate, w_up, w_down, bgu, bd, *, tm):
    n_rows, dh = xs.shape
    n_exp, d, f = w_gate.shape
    tmax = n_rows // tm
    rows = min(CAST_ROWS, f)
    assert d % rows == 0 and f % rows == 0

    def x_map(t, te, nt, nx, nv):
        return (jnp.minimum(t, nt[0] - 1), 0)

    def b_map(t, te, nt, nx, nv):
        return (te[t], 0, 0)

    grid_spec = pltpu.PrefetchScalarGridSpec(
        num_scalar_prefetch=4,
        grid=(tmax,),
        in_specs=[pl.BlockSpec((tm, dh), x_map),
                  pl.BlockSpec(memory_space=pl.ANY),
                  pl.BlockSpec(memory_space=pl.ANY),
                  pl.BlockSpec(memory_space=pl.ANY),
                  pl.BlockSpec((1, 1, 2 * f), b_map),
                  pl.BlockSpec((1, 1, d), b_map)],
        out_specs=pl.BlockSpec((tm, dh), lambda t, te, nt, nx, nv: (t, 0)),
        scratch_shapes=[pltpu.VMEM((d, f), F32), pltpu.VMEM((d, f), F32), pltpu.VMEM((f, d), F32),
                        pltpu.VMEM((d, 2 * f), BF16), pltpu.VMEM((f, d), BF16),
                        pltpu.SemaphoreType.DMA((3,))],
    )
    return pl.pallas_call(
        functools.partial(_ffn_kernel, f=f, n_exp=n_exp, rows=rows, sub=min(FFN_SUB, tm)),
        grid_spec=grid_spec,
        out_shape=jax.ShapeDtypeStruct((n_rows, dh), U32),
        compiler_params=_cparams(("arbitrary",), vmem=VMEM_LIMIT_HIGH),
        name="expert_ffn",
    )(texp, ntl, nxt, nval, xs, w_gate, w_up, w_down, bgu, bd)


COMBINE_ROWS = 16


def _combine_kernel(pos_c, pos_n, ys_hbm, h_ref, g_ref, fg_ref, o_ref, buf_a, buf_b, sem, *, bt, rows):
    i = pl.program_id(0)
    dh = buf_a.shape[-1]

    def start_rows(pos_ref, dst, s, r0):
        for r in range(rows):
            for k in range(TOP_K):
                pltpu.make_async_copy(ys_hbm.at[pl.ds(pos_ref[0, k, r0 + r], 1)],
                                      dst.at[k, pl.ds(r0 + r, 1)], sem.at[s]).start(priority=1)

    def wait_block(dst, s):
        for k in range(TOP_K):
            pltpu.make_async_copy(ys_hbm.at[pl.ds(0, bt)], dst.at[k], sem.at[s]).wait()

    @pl.when(i == 0)
    def _():
        def first(c, carry):
            start_rows(pos_c, buf_a, 0, pl.multiple_of(c * rows, rows))
            return carry
        lax.fori_loop(0, bt // rows, first, 0)

    def step(cur, s_cur, nxt, s_nxt):
        wait_block(cur, s_cur)
        for r0 in range(0, bt, rows):
            start_rows(pos_n, nxt, s_nxt, r0)
            acc_lo = h_ref[r0:r0 + rows, :dh]
            acc_hi = h_ref[r0:r0 + rows, dh:]
            for k in range(TOP_K):
                lo, hi = _unpack_halves(cur[k, r0:r0 + rows, :])
                gk = g_ref[r0:r0 + rows, k:k + 1]
                acc_lo = acc_lo + gk * lo
                acc_hi = acc_hi + gk * hi
            ms = (jnp.sum(acc_lo * acc_lo, axis=-1, keepdims=True)
                  + jnp.sum(acc_hi * acc_hi, axis=-1, keepdims=True)) / (2 * dh)
            inv = lax.rsqrt(ms + RMS_EPS)
            o_ref[r0:r0 + rows, :dh] = acc_lo * inv * fg_ref[:, :dh]
            o_ref[r0:r0 + rows, dh:] = acc_hi * inv * fg_ref[:, dh:]

        @pl.when(i == pl.num_programs(0) - 1)
        def _():
            wait_block(nxt, s_nxt)

    @pl.when(i % 2 == 0)
    def _():
        step(buf_a, 0, buf_b, 1)

    @pl.when(i % 2 == 1)
    def _():
        step(buf_b, 1, buf_a, 0)


def _combine(pos3, ys, h, gates_t, fg, *, bt):
    n, d = h.shape
    nblk = n // bt
    return pl.pallas_call(
        functools.partial(_combine_kernel, bt=bt, rows=min(COMBINE_ROWS, bt)),
        grid=(nblk,),
        in_specs=[pl.BlockSpec((1, TOP_K, bt), lambda i: (i, 0, 0), memory_space=pltpu.SMEM),
                  pl.BlockSpec((1, TOP_K, bt), lambda i: (jnp.minimum(i + 1, nblk - 1), 0, 0),
                               memory_space=pltpu.SMEM),
                  pl.BlockSpec(memory_space=pl.ANY),
                  pl.BlockSpec((bt, d), lambda i: (i, 0)),
                  pl.BlockSpec((bt, TOP_K), lambda i: (i, 0)),
                  pl.BlockSpec((1, d), lambda i: (0, 0))],
        out_specs=pl.BlockSpec((bt, d), lambda i: (i, 0)),
        out_shape=jax.ShapeDtypeStruct((n, d), F32),
        scratch_shapes=[pltpu.VMEM((TOP_K, bt, d // 2), U32), pltpu.VMEM((TOP_K, bt, d // 2), U32),
                        pltpu.SemaphoreType.DMA((2,))],
        compiler_params=_cparams(("arbitrary",)),
        name="combine_final_norm",
    )(pos3, pos3, ys, h, gates_t, fg)


def _layer(x2, b, seq, norm1_g, w_in, pool_w, pool_scale, fourier_w, w_out, norm2_g,
           router_w, router_b, w_gate, b_gate, w_up, b_up, w_down, b_down):
    n, d = x2.shape
    mix = w_in.shape[1]
    pw = pool_scale.shape[0]
    n_exp = router_w.shape[1]
    f = w_gate.shape[2]
    tm = min(512, n)
    tmax = (n * TOP_K) // tm + n_exp

    up, uf = _norm_matmul(x2, norm1_g.reshape(1, d), w_in.astype(BF16), pw=pw, c=fourier_w.shape[1],
                          bm=1024, bn=1024)
    a = _pool_mixer(up.reshape(b, seq, pw), pool_w.astype(BF16), pool_scale.reshape(1, pw), ts=1024)
    yf = _fourier_mixer(uf, _fourier_prep(fourier_w), b=b, rows=1024)
    grw = jnp.zeros((d, LANES), F32).at[:, :n_exp].set(norm2_g[:, None] * router_w).astype(BF16)
    rb = jnp.zeros((1, LANES), F32).at[0, :n_exp].set(router_b)
    h, logits = _out_proj(a.reshape(n, pw), yf.reshape(n, mix - pw), w_out.astype(BF16), x2, grw, rb,
                          bm=1024, bn=1024)

    pos, gates, texp, ntl, last, nxt, nval = _route(logits, n_exp=n_exp, tm=tm, tmax=tmax)
    chunk = min(256, n)
    pos_c = pos.reshape(TOP_K, n // chunk, chunk).transpose(1, 0, 2)
    xs = _norm_dispatch(last[0, :n_exp], pos_c, h, norm2_g.reshape(1, d), n_rows=tmax * tm, tm=tm)

    bgu = jnp.concatenate([b_gate, b_up], axis=-1).reshape(n_exp, 1, 2 * f)
    ys = _expert_ffn(texp[0, :tmax], ntl[0, :1], nxt[0, :n_exp], nval[0, :tmax], xs, w_gate, w_up, w_down, bgu,
                     b_down.reshape(n_exp, 1, d), tm=tm)

    bt = min(256, n)
    pos_b = pos.reshape(TOP_K, n // bt, bt).transpose(1, 0, 2)
    return pos_b, ys, h, gates.T


def kernel(x, norm1_g, w_in, pool_w, pool_scale, fourier_w, w_out, norm2_g, router_w, router_b,
           w_gate, b_gate, w_up, b_up, w_down, b_down, final_g):
    b, seq, d = x.shape
    assert w_in.shape[0] == 1, "only a single layer is supported"
    pos_b, ys, h, gates_t = _layer(
        x.reshape(b * seq, d), b, seq, norm1_g[0], w_in[0], pool_w[0], pool_scale[0], fourier_w[0], w_out[0],
        norm2_g[0], router_w[0], router_b[0], w_gate[0], b_gate[0], w_up[0], b_up[0], w_down[0], b_down[0])
    out = _combine(pos_b, ys, h, gates_t, final_g.reshape(1, d), bt=min(256, b * seq))
    return out.reshape(b, seq, d)
```
